```python
import math
import jax
import jax.numpy as jnp
from jax import lax
import numpy as np

D_MODEL = 1024
BATCH = 2
SEQ = 8192
DEPTH = 2

CTX_LEN = 256
GRID_W = 64
EPS = 1e-6

N_HEADS = 8
HEAD_DIM = 64
V_DIM = 2 * HEAD_DIM
QK_WIDTH = N_HEADS * 2 * HEAD_DIM
V_WIDTH = N_HEADS * V_DIM
Q_BLOCK = 128
ATTN_SCALE = HEAD_DIM ** -0.5
ROPE_THETA = 10000.0
ROPE_FREQS = HEAD_DIM // 4

LRU_WIDTH = D_MODEL
LRU_BLOCKS = 16
LRU_BLOCK_DIM = LRU_WIDTH // LRU_BLOCKS
LRU_C = 8.0
CONV_W = 4
CONV_LEFT = CONV_W // 2

FOUR_GROUPS = 4
FOUR_WIDTH = D_MODEL
FOUR_GROUP_DIM = FOUR_WIDTH // FOUR_GROUPS

N_BRANCH = 3

COL_K = 0
COL_V = COL_K + QK_WIDTH
COL_LX = COL_V + V_WIDTH
COL_Q = COL_LX + LRU_WIDTH
COL_LG = COL_Q + QK_WIDTH
COL_F = COL_LG + LRU_WIDTH
COL_G = COL_F + FOUR_WIDTH
IN_WIDTH = COL_G + N_BRANCH * D_MODEL

N_EXPERTS = 256
TOP_K = 8
N_GROUPS = 8
TOPK_GROUPS = 4
EXPERT_FF = 256
SHARED_FF = 256
ROUTED_SCALE = 2.5
MOE_BLOCK = 128

kernel_name = 'hybrid_diffattn_rglru_fourier_moe_dit'


def rms_norm(x, g):
    xf = x.astype(jnp.float32)
    y = xf * lax.rsqrt(jnp.mean(xf * xf, axis=-1, keepdims=True) + EPS)
    return (y * g.astype(jnp.float32)).astype(x.dtype)


def modulate(h, shift, scale):
    return h * (1 + scale) + shift


def axial_rope_tables(n_tokens):
    rows = n_tokens // GRID_W
    row = jnp.repeat(jnp.arange(rows), GRID_W)
    col = jnp.tile(jnp.arange(GRID_W), rows)
    inv = ROPE_THETA ** (-jnp.arange(ROPE_FREQS, dtype=jnp.float32) / ROPE_FREQS)
    ang = jnp.stack([row[:, None] * inv, col[:, None] * inv], axis=1)
    return jnp.cos(ang), jnp.sin(ang)


def apply_axial_rope(x, cos, sin):
    sh = x.shape
    xr = x.reshape(sh[:-1] + (2, 2, ROPE_FREQS)).astype(jnp.float32)
    x1, x2 = xr[..., 0, :], xr[..., 1, :]
    c = cos[:, None, None]
    s = sin[:, None, None]
    out = jnp.stack([x1 * c - x2 * s, x1 * s + x2 * c], axis=-2)
    return out.reshape(sh).astype(x.dtype)


def qk_heads(u, col):
    return u[..., col:col + QK_WIDTH].reshape(u.shape[:2] + (N_HEADS, 2, HEAD_DIM))


def v_heads(u):
    return u[..., COL_V:COL_V + V_WIDTH].reshape(u.shape[:2] + (N_HEADS, V_DIM))


def diff_read(q, k, v, lam):
    s = jnp.einsum('bqhmd,bkhmd->bhmqk', q, k, preferred_element_type=jnp.float32) * ATTN_SCALE
    p = jax.nn.softmax(s, axis=-1)
    w = p[:, :, 0] - lam * p[:, :, 1]
    return jnp.einsum('bhqk,bkhe->bqhe', w.astype(v.dtype), v)


def centred_conv(x, w, b):
    t = x.shape[1]
    xp = jnp.pad(x, ((0, 0), (CONV_LEFT, CONV_W - 1 - CONV_LEFT), (0, 0)))
    y = xp[:, 0:t] * w[0]
    for j in range(1, CONV_W):
        y = y + xp[:, j:j + t] * w[j]
    return y + b


def block_diag(x, w, b):
    xs = x.reshape(x.shape[:-1] + (LRU_BLOCKS, LRU_BLOCK_DIM))
    return jnp.einsum('btnd,nde->btne', xs, w).reshape(x.shape) + b


def rglru_coeffs(xc, wa, ba, wi, bi, lam):
    r = jax.nn.sigmoid(block_diag(xc, wa, ba).astype(jnp.float32))
    i = jax.nn.sigmoid(block_diag(xc, wi, bi).astype(jnp.float32))
    log_a = -LRU_C * r * jax.nn.softplus(-lam.astype(jnp.float32))
    a = jnp.exp(log_a)
    mult = jnp.sqrt(-jnp.expm1(2.0 * log_a))
    return a, mult * i * xc.astype(jnp.float32)


def linear_scan(a, b, h0=None):
    def combine(left, right):
        return left[0] * right[0], right[0] * left[1] + right[1]
    a_cum, h = lax.associative_scan(combine, (a, b), axis=1)
    if h0 is not None:
        h = h + a_cum * h0[:, None, :]
    return h


def rglru_bidir(xl, xc, conv_w, conv_b, wa, ba, wi, bi, lam, ctx_out):
    ul = centred_conv(xl, conv_w, conv_b)
    uc = centred_conv(xc, conv_w, conv_b)
    outs_l, outs_c = [], []
    for d, rev in ((0, False), (1, True)):
        a_l, b_l = rglru_coeffs(ul, wa[d], ba[d], wi[d], bi[d], lam[d])
        a_c, b_c = rglru_coeffs(uc, wa[d], ba[d], wi[d], bi[d], lam[d])
        if rev:
            a_l, b_l, a_c, b_c = jnp.flip(a_l, 1), jnp.flip(b_l, 1), jnp.flip(a_c, 1), jnp.flip(b_c, 1)
        h_c = linear_scan(a_c, b_c)
        h_l = linear_scan(a_l, b_l, h_c[:, -1])
        if rev:
            h_l, h_c = jnp.flip(h_l, 1), jnp.flip(h_c, 1)
        outs_l.append(h_l)
        outs_c.append(h_c)
    rec_l = (outs_l[0] + outs_l[1]).astype(xl.dtype)
    if not ctx_out:
        return rec_l, None
    return rec_l, (outs_c[0] + outs_c[1]).astype(xc.dtype)


def fourier_mix(u):
    bsz, t = u.shape[:2]
    ug = u.reshape(bsz, t, FOUR_GROUPS, FOUR_GROUP_DIM).astype(jnp.float32)
    f = jnp.fft.fft2(ug, axes=(1, 3), norm='ortho')
    return jnp.real(f).reshape(bsz, t, FOUR_WIDTH).astype(u.dtype)


def token_mixer(hl, hc, w_in, q_norm_g, k_norm_g, lambda_qk, subln_g, w_attn_o,
                conv_w, conv_b, lru_wa, lru_ba, lru_wi, lru_bi, lru_lambda,
                w_lru_o, w_four_o, w_out, lam_init, cos, sin, ctx_out):
    bsz, n_lat, _ = hl.shape
    ul = hl @ w_in
    uc = hc @ (w_in if ctx_out else w_in[:, :COL_Q])

    lq = lambda_qk.astype(jnp.float32)
    lam = jnp.exp(jnp.sum(lq[0] * lq[1])) - jnp.exp(jnp.sum(lq[2] * lq[3])) + lam_init
    k_l = apply_axial_rope(rms_norm(qk_heads(ul, COL_K), k_norm_g), cos, sin)
    k_c = rms_norm(qk_heads(uc, COL_K), k_norm_g)
    v_c = v_heads(uc)
    k_all = jnp.concatenate([k_c, k_l], axis=1)
    v_all = jnp.concatenate([v_c, v_heads(ul)], axis=1)
    q_l = apply_axial_rope(rms_norm(qk_heads(ul, COL_Q), q_norm_g), cos, sin)
    n_blk = n_lat // Q_BLOCK
    q_blocks = q_l.reshape(bsz, n_blk, Q_BLOCK, N_HEADS, 2, HEAD_DIM).swapaxes(0, 1)
    o_l = lax.map(lambda qb: diff_read(qb, k_all, v_all, lam), q_blocks)
    o_l = o_l.swapaxes(0, 1).reshape(bsz, n_lat, N_HEADS, V_DIM)

    def attn_out(o):
        o = rms_norm(o, subln_g) * (1.0 - lam_init)
        return o.reshape(o.shape[:2] + (V_WIDTH,)) @ w_attn_o

    rec_l, rec_c = rglru_bidir(ul[..., COL_LX:COL_LX + LRU_WIDTH], uc[..., COL_LX:COL_LX + LRU_WIDTH],
                               conv_w, conv_b, lru_wa, lru_ba, lru_wi, lru_bi, lru_lambda, ctx_out)

    def merge(u, attn, rec):
        gates = jax.nn.sigmoid(u[..., COL_G:].reshape(u.shape[:2] + (N_BRANCH, D_MODEL)))
        rec_o = (rec * jax.nn.gelu(u[..., COL_LG:COL_LG + LRU_WIDTH])) @ w_lru_o
        four_o = fourier_mix(u[..., COL_F:COL_F + FOUR_WIDTH]) @ w_four_o
        mixed = gates[..., 0, :] * attn_out(attn) + gates[..., 1, :] * rec_o + gates[..., 2, :] * four_o
        return mixed @ w_out

    y_l = merge(ul, o_l, rec_l)
    if not ctx_out:
        return y_l, None
    q_c = rms_norm(qk_heads(uc, COL_Q), q_norm_g)
    o_c = diff_read(q_c, k_c, v_c, lam)
    return y_l, merge(uc, o_c, rec_c)


def moe_ffn(h, router_w, router_b, w_gate, w_up, w_down, ws_gate, ws_up, ws_down):
    n_tok, d = h.shape
    scores = jax.nn.sigmoid(jnp.dot(h, router_w, preferred_element_type=jnp.float32))
    biased = scores + router_b.astype(jnp.float32)
    grp_score = lax.top_k(biased.reshape(n_tok, N_GROUPS, N_EXPERTS // N_GROUPS), 2)[0].sum(-1)
    _, grp_idx = lax.top_k(grp_score, TOPK_GROUPS)
    grp_mask = jax.nn.one_hot(grp_idx, N_GROUPS, dtype=jnp.float32).sum(-2) > 0
    exp_mask = jnp.repeat(grp_mask, N_EXPERTS // N_GROUPS, axis=-1)
    _, top_idx = lax.top_k(jnp.where(exp_mask, biased, -jnp.inf), TOP_K)
    top_w = jnp.take_along_axis(scores, top_idx, axis=-1)
    top_w = top_w / jnp.sum(top_w, axis=-1, keepdims=True) * ROUTED_SCALE

    n_assign = n_tok * TOP_K
    e_flat = top_idx.reshape(-1)
    tok_flat = jnp.arange(n_assign, dtype=jnp.int32) // TOP_K
    order = jnp.argsort(e_flat)
    e_sorted = e_flat[order]
    counts = jnp.bincount(e_flat, length=N_EXPERTS)
    padded = (counts + MOE_BLOCK - 1) // MOE_BLOCK * MOE_BLOCK
    ends = jnp.cumsum(padded)
    dest = (ends - padded)[e_sorted] + jnp.arange(n_assign, dtype=jnp.int32) - (jnp.cumsum(counts) - counts)[e_sorted]
    n_blocks = -(-(n_assign + N_EXPERTS * (MOE_BLOCK - 1)) // MOE_BLOCK)
    buf_tok = jnp.full((n_blocks * MOE_BLOCK,), n_tok, jnp.int32).at[dest].set(tok_flat[order])
    buf_w = jnp.zeros((n_blocks * MOE_BLOCK,), jnp.float32).at[dest].set(top_w.reshape(-1)[order])
    block_e = jnp.minimum(jnp.searchsorted(ends, jnp.arange(n_blocks, dtype=jnp.int32) * MOE_BLOCK, side='right'),
                          N_EXPERTS - 1)
    h_pad = jnp.concatenate([h, jnp.zeros((1, d), h.dtype)], axis=0)

    def expert_block(acc, blk):
        tok, wt, e = blk
        xb = h_pad[tok]
        yb = (jax.nn.silu(xb @ w_gate[e]) * (xb @ w_up[e])) @ w_down[e]
        return acc.at[tok].add(yb.astype(jnp.float32) * wt[:, None]), None

    acc, _ = lax.scan(expert_block, jnp.zeros((n_tok + 1, d), jnp.float32),
                      (buf_tok.reshape(n_blocks, MOE_BLOCK), buf_w.reshape(n_blocks, MOE_BLOCK), block_e))
    shared = (jax.nn.silu(h @ ws_gate) * (h @ ws_up)) @ ws_down
    return (acc[:n_tok] + shared.astype(jnp.float32)).astype(h.dtype)


def setup_inputs(seed: int = 0) -> dict:
    key = jax.random.key(seed)
    ks = iter(jax.random.split(key, 48))
    L, D = DEPTH, D_MODEL

    def nrm(shape, scale):
        return jax.random.normal(next(ks), shape, jnp.float32) * scale

    u = jax.random.uniform(next(ks), (L, 2, LRU_WIDTH), jnp.float32, 0.9, 0.999)
    a0 = u ** (1.0 / LRU_C)
    lru_lambda = jnp.log(a0) - jnp.log1p(-a0)
    return {
        'x': nrm((BATCH, SEQ, D), 1.0),
        'c': nrm((BATCH, D), 1.0),
        'ctx': nrm((BATCH, CTX_LEN, D), 1.0),
        'c_ctx': nrm((D,), 1.0),
        'w_mod': nrm((L, D, 6 * D), 0.5 * D ** -0.5),
        'b_mod': nrm((L, 6 * D), 0.02),
        'norm1_g': 1.0 + nrm((L, D), 0.02),
        'w_in': nrm((L, D, IN_WIDTH), D ** -0.5),
        'q_norm_g': 1.0 + nrm((L, HEAD_DIM), 0.02),
        'k_norm_g': 1.0 + nrm((L, HEAD_DIM), 0.02),
        'lambda_qk': nrm((L, 4, HEAD_DIM), 0.1),
        'subln_g': 1.0 + nrm((L, V_DIM), 0.02),
        'w_attn_o': nrm((L, V_WIDTH, D), V_WIDTH ** -0.5),
        'conv_w': nrm((L, CONV_W, LRU_WIDTH), CONV_W ** -0.5),
        'conv_b': nrm((L, LRU_WIDTH), 0.02),
        'lru_wa': nrm((L, 2, LRU_BLOCKS, LRU_BLOCK_DIM, LRU_BLOCK_DIM), LRU_BLOCK_DIM ** -0.5),
        'lru_ba': nrm((L, 2, LRU_WIDTH), 0.02),
        'lru_wi': nrm((L, 2, LRU_BLOCKS, LRU_BLOCK_DIM, LRU_BLOCK_DIM), LRU_BLOCK_DIM ** -0.5),
        'lru_bi': nrm((L, 2, LRU_WIDTH), 0.02),
        'lru_lambda': lru_lambda,
        'w_lru_o': nrm((L, LRU_WIDTH, D), LRU_WIDTH ** -0.5),
        'w_four_o': nrm((L, FOUR_WIDTH, D), FOUR_WIDTH ** -0.5),
        'w_out': nrm((L, D, D), D ** -0.5),
        'norm2_g': 1.0 + nrm((L, D), 0.02),
        'router_w': nrm((L, D, N_EXPERTS), D ** -0.5),
        'router_b': nrm((L, N_EXPERTS), 0.01),
        'exp_w_gate': nrm((L, N_EXPERTS, D, EXPERT_FF), D ** -0.5),
        'exp_w_up': nrm((L, N_EXPERTS, D, EXPERT_FF), D ** -0.5),
        'exp_w_down': nrm((L, N_EXPERTS, EXPERT_FF, D), EXPERT_FF ** -0.5),
        'sh_w_gate': nrm((L, D, SHARED_FF), D ** -0.5),
        'sh_w_up': nrm((L, D, SHARED_FF), D ** -0.5),
        'sh_w_down': nrm((L, SHARED_FF, D), SHARED_FF ** -0.5),
    }


def reference(x, c, ctx, c_ctx, w_mod, b_mod, norm1_g, w_in, q_norm_g, k_norm_g, lambda_qk, subln_g,
              w_attn_o, conv_w, conv_b, lru_wa, lru_ba, lru_wi, lru_bi, lru_lambda, w_lru_o, w_four_o,
              w_out, norm2_g, router_w, router_b, exp_w_gate, exp_w_up, exp_w_down,
              sh_w_gate, sh_w_up, sh_w_down):
    bsz, n_lat, d = x.shape
    n_ctx = ctx.shape[1]
    cos, sin = axial_rope_tables(n_lat)
    xl, xc = x, ctx
    for l in range(DEPTH):
        last = l == DEPTH - 1
        lam_init = 0.8 - 0.6 * math.exp(-0.3 * l)
        mod_l = (jax.nn.silu(c) @ w_mod[l] + b_mod[l])[:, None, :]
        mod_c = (jax.nn.silu(c_ctx) @ w_mod[l] + b_mod[l])[None, None, :]
        sh1, sc1, g1, sh2, sc2, g2 = jnp.split(mod_l, 6, axis=-1)
        csh1, csc1, cg1, csh2, csc2, cg2 = jnp.split(mod_c, 6, axis=-1)

        hl = modulate(rms_norm(xl, norm1_g[l]), sh1, sc1)
        hc = modulate(rms_norm(xc, norm1_g[l]), csh1, csc1)
        y_l, y_c = token_mixer(hl, hc, w_in[l], q_norm_g[l], k_norm_g[l], lambda_qk[l], subln_g[l],
                               w_attn_o[l], conv_w[l], conv_b[l], lru_wa[l], lru_ba[l], lru_wi[l],
                               lru_bi[l], lru_lambda[l], w_lru_o[l], w_four_o[l], w_out[l],
                               lam_init, cos, sin, not last)
        xl = xl + g1 * y_l
        tokens = modulate(rms_norm(xl, norm2_g[l]), sh2, sc2).reshape(-1, d)
        if not last:
            xc = xc + cg1 * y_c
            tokens = jnp.concatenate(
                [tokens, modulate(rms_norm(xc, norm2_g[l]), csh2, csc2).reshape(-1, d)], axis=0)
        f = moe_ffn(tokens, router_w[l], router_b[l], exp_w_gate[l], exp_w_up[l], exp_w_down[l],
                    sh_w_gate[l], sh_w_up[l], sh_w_down[l])
        xl = xl + g2 * f[:bsz * n_lat].reshape(bsz, n_lat, d)
        if not last:
            xc = xc + cg2 * f[bsz * n_lat:].reshape(bsz, n_ctx, d)
    return xl
```

```python
import functools
import math

import jax
import jax.numpy as jnp
from jax import lax
from jax.experimental import pallas as pl
from jax.experimental.pallas import tpu as pltpu

F32 = jnp.float32
BF16 = jnp.bfloat16

D_MODEL = 1024
EPS = 1e-6
GRID_W = 64

N_HEADS = 8
HEAD_DIM = 64
V_DIM = 2 * HEAD_DIM
ATTN_SCALE = HEAD_DIM ** -0.5
ROPE_THETA = 10000.0
ROPE_FREQS = HEAD_DIM // 4

LRU_BLOCKS = 16
LRU_BLOCK_DIM = D_MODEL // LRU_BLOCKS
LRU_C = 8.0
CONV_W = 4

FOUR_GROUPS = 4
FOUR_GROUP_DIM = D_MODEL // FOUR_GROUPS

CB_K, CB_V, CB_LX, CB_Q, CB_LG, CB_F, CB_G = 0, 1, 2, 3, 4, 5, 6
IN_BLOCKS = 9

N_EXPERTS = 256
TOP_K = 8
N_GROUPS = 8
GROUP_SIZE = N_EXPERTS // N_GROUPS
TOPK_GROUPS = 4
EXPERT_FF = 256
ROUTED_SCALE = 2.5

LANES = 128
SUBLANES = 8
MXU_DIM = 256
VMEM_BYTES_V7X = 64 * 1024 * 1024

MOD_ROWS = 8
TM = 256
TM_IN = 512
TQ = 512
TK = 768
TM_ROUTE = 512
MOE_BM = 256
LOG2E = 1.4426950408889634


def _cparams(sem, vmem_mb):
    return pltpu.CompilerParams(dimension_semantics=sem, vmem_limit_bytes=int(vmem_mb * 1024 * 1024))


def _silu(x):
    return x * jax.nn.sigmoid(x)


def _mod_kernel(c_ref, w_ref, b_ref, o_ref):
    c = c_ref[...]
    s = _silu(c).astype(BF16)
    o_ref[0] = jnp.dot(s, w_ref[0].astype(BF16), preferred_element_type=F32) + b_ref[0]


def _modulation(cvec, w_mod, b_mod):
    n_layers, d, six_d = w_mod.shape
    nj = six_d // d
    return pl.pallas_call(
        _mod_kernel,
        grid=(n_layers, nj),
        in_specs=[
            pl.BlockSpec((MOD_ROWS, d), lambda l, j: (0, 0)),
            pl.BlockSpec((1, d, d), lambda l, j: (l, 0, j)),
            pl.BlockSpec((1, 1, d), lambda l, j: (l, 0, j)),
        ],
        out_specs=pl.BlockSpec((1, MOD_ROWS, d), lambda l, j: (l, 0, j)),
        out_shape=jax.ShapeDtypeStruct((n_layers, MOD_ROWS, six_d), F32),
        compiler_params=_cparams(("parallel", "parallel"), 32),
        name="modulation",
    )(cvec, w_mod, b_mod.reshape(n_layers, 1, six_d))


def _mod_spec(part, row_of_tile, n_grid_axes=1):
    if n_grid_axes == 1:
        return pl.BlockSpec((None, None, 1, D_MODEL), lambda i: (row_of_tile(i), part, 0, 0))
    return pl.BlockSpec((None, None, 1, D_MODEL), lambda i, j: (row_of_tile(i), part, 0, 0))


def _inproj_kernel(x_ref, g_ref, sh_ref, sc_ref, w_ref, o_ref, h_scr):
    @pl.when(pl.program_id(1) == 0)
    def _():
        x = x_ref[...]
        y = x * lax.rsqrt(jnp.mean(x * x, axis=-1, keepdims=True) + EPS)
        h = (y * g_ref[...]) * (1.0 + sc_ref[...]) + sh_ref[...]
        h_scr[...] = h.astype(BF16)

    o_ref[...] = jnp.dot(h_scr[...], w_ref[...], preferred_element_type=F32)


def _in_projection(x_tok, mod4, norm_g, w_in_bf16, row_of_tile, tm):
    n_tok, d = x_tok.shape
    n_cols = w_in_bf16.shape[1]
    return pl.pallas_call(
        _inproj_kernel,
        grid=(n_tok // tm, n_cols // d),
        in_specs=[
            pl.BlockSpec((tm, d), lambda i, j: (i, 0)),
            pl.BlockSpec((1, d), lambda i, j: (0, 0)),
            _mod_spec(0, row_of_tile, 2),
            _mod_spec(1, row_of_tile, 2),
            pl.BlockSpec((d, d), lambda i, j: (0, j)),
        ],
        out_specs=pl.BlockSpec((tm, d), lambda i, j: (i, j)),
        out_shape=jax.ShapeDtypeStruct((n_tok, n_cols), F32),
        scratch_shapes=[pltpu.VMEM((tm, d), BF16)],
        compiler_params=_cparams(("parallel", "arbitrary"), 40),
        name="in_projection",
    )(x_tok, norm_g.reshape(1, d), mod4, mod4, w_in_bf16)


def _prep_kernel(uk_ref, uq_ref, uv_ref, kg_ref, qg_ref, gm_ref, cos_ref, sin_ref, k_out, q_out, vt_out, *, tm):
    gm = gm_ref[...]
    lane = lax.broadcasted_iota(jnp.int32, (tm, LANES), 1)
    first_half = (lane % (2 * ROPE_FREQS)) < ROPE_FREQS

    def norm_rope(x, g):
        ss = jnp.dot((x * x).astype(BF16), gm, preferred_element_type=F32) * (1.0 / HEAD_DIM)
        y = x * lax.rsqrt(ss + EPS) * g
        c = cos_ref[...]
        s = sin_ref[...]
        outs = []
        for j in range(D_MODEL // LANES):
            yt = y[:, j * LANES:(j + 1) * LANES]
            rot = jnp.where(first_half, pltpu.roll(yt, LANES - ROPE_FREQS, 1), pltpu.roll(yt, ROPE_FREQS, 1))
            outs.append(yt * c + rot * s)
        return jnp.concatenate(outs, axis=1)

    k_out[...] = norm_rope(uk_ref[...], kg_ref[...]).astype(BF16)
    qn = norm_rope(uq_ref[...], qg_ref[...]) * (ATTN_SCALE * LOG2E)
    lo = lane < HEAD_DIM
    for h in range(N_HEADS):
        qt = qn[:, h * LANES:(h + 1) * LANES]
        q_out[:, (2 * h) * LANES:(2 * h + 1) * LANES] = jnp.where(lo, qt, 0.0).astype(BF16)
        q_out[:, (2 * h + 1) * LANES:(2 * h + 2) * LANES] = jnp.where(lo, 0.0, qt).astype(BF16)
    vt_out[...] = uv_ref[...].T.astype(BF16)


def _qkv_prepare(u, k_gain, q_gain, group_mat, cos_t, sin_t, bsz, n_lat, n_ctx):
    tm = TM
    d = D_MODEL
    n_s, n_c = n_lat // tm, n_ctx // tm
    n_rows = n_lat + n_ctx
    lat_tiles = bsz * n_s
    kg = jnp.tile(k_gain, d // HEAD_DIM).reshape(1, d)
    qg = jnp.tile(q_gain, d // HEAD_DIM).reshape(1, d)

    def tok_tile(b, j):
        return jnp.where(j < n_s, b * n_s + j, lat_tiles + b * n_c + (j - n_s))

    def u_spec(cb):
        return pl.BlockSpec((tm, d), lambda b, j: (tok_tile(b, j), cb))

    vec = pl.BlockSpec((1, d), lambda b, j: (0, 0))
    rope_spec = pl.BlockSpec((tm, LANES), lambda b, j: (j, 0))
    return pl.pallas_call(
        functools.partial(_prep_kernel, tm=tm),
        grid=(bsz, n_s + n_c),
        in_specs=[u_spec(CB_K), u_spec(CB_Q), u_spec(CB_V), vec, vec,
                  pl.BlockSpec((d, d), lambda b, j: (0, 0)), rope_spec, rope_spec],
        out_specs=[
            pl.BlockSpec((None, tm, d), lambda b, j: (b, j, 0)),
            pl.BlockSpec((None, tm, 2 * d), lambda b, j: (b, j, 0)),
            pl.BlockSpec((None, d, tm), lambda b, j: (b, 0, j)),
        ],
        out_shape=[jax.ShapeDtypeStruct((bsz, n_rows, d), BF16),
                   jax.ShapeDtypeStruct((bsz, n_rows, 2 * d), BF16),
                   jax.ShapeDtypeStruct((bsz, d, n_rows), BF16)],
        compiler_params=_cparams(("parallel", "parallel"), 40),
        name="qkv_prepare",
    )(u, u, u, kg, qg, group_mat, cos_t, sin_t)


def _attn_kernel(lq_ref, sg_ref, q_ref, k_ref, vt_ref, o_ref, s_scr, *, n_keys, tk, lam_init):
    lq = lq_ref[...]
    lam = (jnp.exp(jnp.sum(lq[0:1] * lq[1:2], axis=1, keepdims=True))
           - jnp.exp(jnp.sum(lq[2:3] * lq[3:4], axis=1, keepdims=True)) + lam_init)
    n_chunks = n_keys // tk
    tq = q_ref.shape[0]
    outs = []
    for m in range(2):
        qz = q_ref[:, m * LANES:(m + 1) * LANES]
        mx = None
        for c in range(n_chunks):
            s = lax.dot_general(k_ref[c * tk:(c + 1) * tk, :], qz, (((1,), (1,)), ((), ())),
                                preferred_element_type=F32)
            s_scr[c * tk:(c + 1) * tk, :] = s
            cm = jnp.max(s, axis=0, keepdims=True)
            mx = cm if mx is None else jnp.maximum(mx, cm)
        l = jnp.zeros((1, tq), F32)
        acc = jnp.zeros((V_DIM, tq), F32)
        for c in range(n_chunks):
            p = jnp.exp2(s_scr[c * tk:(c + 1) * tk, :] - mx)
            l = l + jnp.sum(p, axis=0, keepdims=True)
            acc = acc + jnp.dot(vt_ref[:, c * tk:(c + 1) * tk], p.astype(BF16), preferred_element_type=F32)
        outs.append(acc / l)
    o = outs[0] - lam * outs[1]
    ms = jnp.mean(o * o, axis=0, keepdims=True)
    on = (o * lax.rsqrt(ms + EPS) * sg_ref[...]) * (1.0 - lam_init)
    o_ref[...] = on.T


def _diff_attention(q_all, k_all, vt_all, lambda_qk, subln_g, lam_init, q_row0, n_q, key_row0, n_keys, tq, tk):
    bsz = q_all.shape[0]
    d = D_MODEL
    nq = n_q // tq
    qb0 = q_row0 // tq
    kb0 = key_row0 // n_keys
    s_bytes = n_keys * tq * 4
    return pl.pallas_call(
        functools.partial(_attn_kernel, n_keys=n_keys, tk=tk, lam_init=lam_init),
        grid=(bsz, N_HEADS, nq),
        in_specs=[
            pl.BlockSpec((4, HEAD_DIM), lambda b, h, i: (0, 0)),
            pl.BlockSpec((V_DIM, 1), lambda b, h, i: (0, 0)),
            pl.BlockSpec((None, tq, 2 * LANES), lambda b, h, i: (b, qb0 + i, h)),
            pl.BlockSpec((None, n_keys, LANES), lambda b, h, i: (b, kb0, h)),
            pl.BlockSpec((None, V_DIM, n_keys), lambda b, h, i: (b, h, kb0)),
        ],
        out_specs=pl.BlockSpec((tq, LANES), lambda b, h, i: (b * nq + i, h)),
        out_shape=jax.ShapeDtypeStruct((bsz * n_q, d), F32),
        scratch_shapes=[pltpu.VMEM((n_keys, tq), F32)],
        compiler_params=_cparams(("parallel", "parallel", "arbitrary"), 32 + s_bytes / 2 ** 20),
        name="diff_attention",
    )(lambda_qk, subln_g.reshape(V_DIM, 1), q_all, k_all, vt_all)


def _lru_kernel(*refs, rev, n_c, n_s, tm):
    if rev:
        (cur_ref, prev_ref, next_ref, cw_ref, cb_ref, wa_ref, ba_ref, wi_ref, bi_ref, lam_ref, fwd_ref,
         o_ref, a_scr, b_scr, h_scr) = refs
    else:
        (cur_ref, prev_ref, next_ref, cw_ref, cb_ref, wa_ref, ba_ref, wi_ref, bi_ref, lam_ref,
         o_ref, a_scr, b_scr, h_scr) = refs
    s = pl.program_id(1)
    is_ctx = s < n_c
    if rev:
        tile = jnp.where(is_ctx, n_c - 1 - s, n_s - 1 - (s - n_c))
    else:
        tile = jnp.where(is_ctx, s, s - n_c)
    n_tile = jnp.where(is_ctx, n_c, n_s)

    @pl.when(s == 0)
    def _():
        h_scr[...] = jnp.zeros_like(h_scr)

    x = cur_ref[...]
    pv = jnp.where(tile == 0, 0.0, prev_ref[...])
    nx = jnp.where(tile == n_tile - 1, 0.0, next_ref[...])
    r8 = lax.broadcasted_iota(jnp.int32, (SUBLANES, D_MODEL), 0)

    def earlier(k):
        rolled = pltpu.roll(x, k, 0)
        top = jnp.where(r8 < k, pltpu.roll(pv, k, 0), rolled[:SUBLANES])
        return jnp.concatenate([top, rolled[SUBLANES:]], axis=0)

    rolled = pltpu.roll(x, tm - 1, 0)
    bot = jnp.where(r8 >= SUBLANES - 1, pltpu.roll(nx, SUBLANES - 1, 0), rolled[tm - SUBLANES:])
    later = jnp.concatenate([rolled[:tm - SUBLANES], bot], axis=0)
    cw = cw_ref[...]
    conv = earlier(2) * cw[0:1] + earlier(1) * cw[1:2] + x * cw[2:3] + later * cw[3:4] + cb_ref[...]

    conv16 = conv.astype(BF16)

    def block_diag(w_ref, b_ref):
        parts = [jnp.dot(conv16[:, j * MXU_DIM:(j + 1) * MXU_DIM], w_ref[j], preferred_element_type=F32)
                 for j in range(D_MODEL // MXU_DIM)]
        return jnp.concatenate(parts, axis=1) + b_ref[...]

    r = jax.nn.sigmoid(block_diag(wa_ref, ba_ref))
    gate_i = jax.nn.sigmoid(block_diag(wi_ref, bi_ref))
    neg_lam = -lam_ref[...]
    softplus = jnp.maximum(neg_lam, 0.0) + jnp.log1p(jnp.exp(-jnp.abs(neg_lam)))
    log_a = -LRU_C * r * softplus
    a = jnp.exp(log_a)
    mult = jnp.sqrt(1.0 - a * a)
    a_scr[...] = a
    b_scr[...] = mult * gate_i * conv

    def body(t, h):
        row = tm - 1 - t if rev else t
        h = a_scr[pl.ds(row, 1), :] * h + b_scr[pl.ds(row, 1), :]
        if rev:
            o_ref[pl.ds(row, 1), :] = fwd_ref[pl.ds(row, 1), :] + h
        else:
            o_ref[pl.ds(row, 1), :] = h
        return h

    h_scr[...] = lax.fori_loop(0, tm, body, h_scr[...], unroll=8)


def _rglru(u, conv_w, conv_b, wa_bd, ba, wi_bd, bi, lam, bsz, n_lat, n_ctx):
    tm = TM
    d = D_MODEL
    n_tok = u.shape[0]
    n_s, n_c = n_lat // tm, n_ctx // tm
    lat_tiles = bsz * n_s
    per8 = tm // SUBLANES
    last8 = n_tok // SUBLANES - 1

    def make(rev, fwd):
        def blk(b, s):
            is_ctx = s < n_c
            if rev:
                tile = jnp.where(is_ctx, n_c - 1 - s, n_s - 1 - (s - n_c))
            else:
                tile = jnp.where(is_ctx, s, s - n_c)
            return jnp.where(is_ctx, lat_tiles + b * n_c + tile, b * n_s + tile)

        vec = pl.BlockSpec((None, 1, d), lambda b, s: (int(rev), 0, 0))
        wspec = pl.BlockSpec((None, d // MXU_DIM, MXU_DIM, MXU_DIM), lambda b, s: (int(rev), 0, 0, 0))
        row_spec = pl.BlockSpec((tm, d), lambda b, s: (blk(b, s), 0))
        in_specs = [
            pl.BlockSpec((tm, d), lambda b, s: (blk(b, s), CB_LX)),
            pl.BlockSpec((SUBLANES, d), lambda b, s: (jnp.maximum(blk(b, s) * per8 - 1, 0), CB_LX)),
            pl.BlockSpec((SUBLANES, d), lambda b, s: (jnp.minimum((blk(b, s) + 1) * per8, last8), CB_LX)),
            pl.BlockSpec((CONV_W, d), lambda b, s: (0, 0)),
            pl.BlockSpec((1, d), lambda b, s: (0, 0)),
            wspec, vec, wspec, vec, vec,
        ]
        args = [u, u, u, conv_w, conv_b.reshape(1, d), wa_bd, ba.reshape(2, 1, d), wi_bd, bi.reshape(2, 1, d),
                lam.reshape(2, 1, d)]
        if rev:
            in_specs.append(row_spec)
            args.append(fwd)
        return pl.pallas_call(
            functools.partial(_lru_kernel, rev=rev, n_c=n_c, n_s=n_s, tm=tm),
            grid=(bsz, n_c + n_s),
            in_specs=in_specs,
            out_specs=row_spec,
            out_shape=jax.ShapeDtypeStruct((n_tok, d), F32),
            scratch_shapes=[pltpu.VMEM((tm, d), F32), pltpu.VMEM((tm, d), F32), pltpu.VMEM((1, d), F32)],
            input_output_aliases={10: 0} if rev else {},
            compiler_params=_cparams(("arbitrary", "arbitrary"), 40),
            name="rglru_bwd" if rev else "rglru_fwd",
        )(*args)

    fwd = make(False, None)
    return make(True, fwd)


def _chan_dft_kernel(u_ref, cs_ref, o_ref):
    x = u_ref[...].astype(BF16)
    cs = cs_ref[...]
    gd = FOUR_GROUP_DIM
    for g in range(FOUR_GROUPS):
        y = jnp.dot(x[:, g * gd:(g + 1) * gd], cs, preferred_element_type=F32)
        o_ref[:, g * gd:(g + 1) * gd] = y[:, :gd].astype(BF16)
        o_ref[:, D_MODEL + g * gd:D_MODEL + (g + 1) * gd] = y[:, gd:].astype(BF16)


def _chan_dft(u, cs_mat):
    n_tok = u.shape[0]
    tm = TM
    d = D_MODEL
    return pl.pallas_call(
        _chan_dft_kernel,
        grid=(n_tok // tm,),
        in_specs=[pl.BlockSpec((tm, d), lambda i: (i, CB_F)),
                  pl.BlockSpec((FOUR_GROUP_DIM, 2 * FOUR_GROUP_DIM), lambda i: (0, 0))],
        out_specs=pl.BlockSpec((tm, 2 * d), lambda i: (i, 0)),
        out_shape=jax.ShapeDtypeStruct((n_tok, 2 * d), BF16),
        compiler_params=_cparams(("parallel",), 32),
        name="fourier_channels",
    )(u, cs_mat)


def _time_dft_kernel(ct_ref, st_ref, yc_ref, ys_ref, o_ref, acc, *, scale):
    kk = pl.program_id(2)

    @pl.when(kk == 0)
    def _():
        acc[...] = jnp.zeros_like(acc)

    acc[...] += (jnp.dot(ct_ref[...], yc_ref[...], preferred_element_type=F32)
                 - jnp.dot(st_ref[...], ys_ref[...], preferred_element_type=F32))

    @pl.when(kk == pl.num_programs(2) - 1)
    def _():
        o_ref[...] = acc[...] * scale


def _time_dft(y, ct, st, bsz, n_pos, row0, tmk):
    d = D_MODEL
    nt = n_pos // tmk
    rb0 = row0 // tmk
    return pl.pallas_call(
        functools.partial(_time_dft_kernel, scale=float(n_pos) ** -0.5),
        grid=(bsz, nt, nt),
        in_specs=[
            pl.BlockSpec((tmk, tmk), lambda b, i, k: (i, k)),
            pl.BlockSpec((tmk, tmk), lambda b, i, k: (i, k)),
            pl.BlockSpec((tmk, d), lambda b, i, k: (rb0 + b * nt + k, 0)),
            pl.BlockSpec((tmk, d), lambda b, i, k: (rb0 + b * nt + k, 1)),
        ],
        out_specs=pl.BlockSpec((tmk, d), lambda b, i, k: (b * nt + i, 0)),
        out_shape=jax.ShapeDtypeStruct((bsz * n_pos, d), F32),
        scratch_shapes=[pltpu.VMEM((tmk, d), F32)],
        compiler_params=_cparams(("parallel", "parallel", "arbitrary"), 48),
        name="fourier_positions",
    )(ct, st, y, y)


def _dft_tables(n):
    k = jnp.arange(n, dtype=jnp.int32)[:, None]
    step = 2.0 * math.pi / n
    if n <= 1024:
        ang = ((k * jnp.arange(n, dtype=jnp.int32)[None, :]) % n).astype(F32) * step
        return jnp.cos(ang).astype(BF16), jnp.sin(ang).astype(BF16)
    hi = jnp.arange(n // LANES, dtype=jnp.int32)[None, :] * LANES
    lo = jnp.arange(LANES, dtype=jnp.int32)[None, :]
    ang_a = ((k * hi) % n).astype(F32) * step
    ang_b = ((k * lo) % n).astype(F32) * step
    ca, sa = jnp.cos(ang_a)[:, :, None], jnp.sin(ang_a)[:, :, None]
    cb, sb = jnp.cos(ang_b)[:, None, :], jnp.sin(ang_b)[:, None, :]
    ct = (ca * cb - sa * sb).reshape(n, n).astype(BF16)
    st = (sa * cb + ca * sb).reshape(n, n).astype(BF16)
    return ct, st


def _merge_kernel(x_ref, attn_l_ref, attn_c_ref, rec_ref, four_l_ref, four_c_ref, lg_ref, g0_ref, g1_ref, g2_ref,
                  wa_ref, wl_ref, wf_ref, wo_ref, gate_ref, n2_ref, sh_ref, sc_ref,
                  xo_ref, h_ref, hp_ref, *, lat_tiles):
    def mm(a, w_ref):
        return jnp.dot(a.astype(BF16), w_ref[...], preferred_element_type=F32)

    is_lat = pl.program_id(0) < lat_tiles
    attn_o = mm(jnp.where(is_lat, attn_l_ref[...], attn_c_ref[...]), wa_ref)
    rec_o = mm(rec_ref[...] * jax.nn.gelu(lg_ref[...], approximate=True), wl_ref)
    four_o = mm(jnp.where(is_lat, four_l_ref[...], four_c_ref[...]), wf_ref)
    mixed = (jax.nn.sigmoid(g0_ref[...]) * attn_o + jax.nn.sigmoid(g1_ref[...]) * rec_o
             + jax.nn.sigmoid(g2_ref[...]) * four_o)
    x = x_ref[...] + gate_ref[...] * mm(mixed, wo_ref)
    xo_ref[...] = x
    y = x * lax.rsqrt(jnp.mean(x * x, axis=-1, keepdims=True) + EPS)
    h = (y * n2_ref[...]) * (1.0 + sc_ref[...]) + sh_ref[...]
    h16 = h.astype(BF16)
    h_ref[...] = h16
    half = D_MODEL // 2
    bits = pltpu.bitcast(h16.astype(F32), jnp.uint32)
    hp_ref[...] = (bits[:, :half] & jnp.uint32(0xFFFF0000)) | (bits[:, half:] >> 16)


def _merge(x_tok, attn_l, attn_c, rec, four_l, four_c, u, w_attn_o, w_lru_o, w_four_o, w_out, mod4, norm2_g,
           row_of_tile, n_rows):
    tm = TM
    d = D_MODEL
    lat_tiles = attn_l.shape[0] // tm
    ctx_tiles = attn_c.shape[0] // tm
    row = pl.BlockSpec((tm, d), lambda i: (i, 0))
    lat_row = pl.BlockSpec((tm, d), lambda i: (jnp.minimum(i, lat_tiles - 1), 0))
    ctx_row = pl.BlockSpec((tm, d), lambda i: (jnp.clip(i - lat_tiles, 0, ctx_tiles - 1), 0))
    wspec = pl.BlockSpec((d, d), lambda i: (0, 0))

    def ucol(cb):
        return pl.BlockSpec((tm, d), lambda i: (i, cb))

    return pl.pallas_call(
        functools.partial(_merge_kernel, lat_tiles=lat_tiles),
        grid=(n_rows // tm,),
        in_specs=[row, lat_row, ctx_row, row, lat_row, ctx_row, ucol(CB_LG), ucol(CB_G), ucol(CB_G + 1),
                  ucol(CB_G + 2), wspec, wspec, wspec, wspec,
                  _mod_spec(2, row_of_tile), pl.BlockSpec((1, d), lambda i: (0, 0)),
                  _mod_spec(3, row_of_tile), _mod_spec(4, row_of_tile)],
        out_specs=[row, row, pl.BlockSpec((tm, d // 2), lambda i: (i, 0))],
        out_shape=[jax.ShapeDtypeStruct((n_rows, d), F32), jax.ShapeDtypeStruct((n_rows, d), BF16),
                   jax.ShapeDtypeStruct((n_rows, d // 2), jnp.uint32)],
        compiler_params=_cparams(("parallel",), 56),
        name="merge_branches",
    )(x_tok, attn_l, attn_c, rec, four_l, four_c, u, u, u, u, w_attn_o, w_lru_o, w_four_o, w_out,
      mod4, norm2_g.reshape(1, d), mod4, mod4)


def _route_kernel(h_ref, rw_ref, rb_ref, tri_ref, idx_ref, wgt_ref, pos_ref, cnt_ref, base_scr):
    tm = h_ref.shape[0]

    @pl.when(pl.program_id(0) == 0)
    def _():
        base_scr[...] = jnp.zeros_like(base_scr)

    logits = lax.dot_general(rw_ref[...], h_ref[...], (((1,), (1,)), ((), ())), preferred_element_type=F32)
    scores = jax.nn.sigmoid(logits)
    biased = scores + rb_ref[...]
    neg = -jnp.inf
    e_iota = lax.broadcasted_iota(jnp.int32, (N_EXPERTS, tm), 0)
    g_iota = lax.broadcasted_iota(jnp.int32, (N_GROUPS, tm), 0)

    rows = []
    for g in range(N_GROUPS):
        xg = biased[g * GROUP_SIZE:(g + 1) * GROUP_SIZE]
        m1 = jnp.max(xg, axis=0, keepdims=True)
        is_max = xg == m1
        n_max = jnp.sum(is_max.astype(F32), axis=0, keepdims=True)
        m2 = jnp.max(jnp.where(is_max, neg, xg), axis=0, keepdims=True)
        rows.append(m1 + jnp.where(n_max >= 2.0, m1, m2))
    grp = jnp.concatenate(rows, axis=0)

    g_sel = jnp.zeros((N_GROUPS, tm), jnp.bool_)
    work = grp
    for _ in range(TOPK_GROUPS):
        m = jnp.max(work, axis=0, keepdims=True)
        first = jnp.min(jnp.where(work == m, g_iota, N_GROUPS), axis=0, keepdims=True)
        hit = g_iota == first
        g_sel = jnp.logical_or(g_sel, hit)
        work = jnp.where(hit, neg, work)
    g_sel_f = g_sel.astype(F32)
    mask_rows = [jnp.broadcast_to(g_sel_f[g:g + 1], (GROUP_SIZE, tm)) for g in range(N_GROUPS)]
    e_mask = jnp.concatenate(mask_rows, axis=0) > 0.5

    work = jnp.where(e_mask, biased, neg)
    sel = jnp.zeros((N_EXPERTS, tm), F32)
    idx_rows, w_rows, hits = [], [], []
    for _ in range(TOP_K):
        m = jnp.max(work, axis=0, keepdims=True)
        first = jnp.min(jnp.where(work == m, e_iota, N_EXPERTS), axis=0, keepdims=True)
        hit = e_iota == first
        hits.append(hit)
        idx_rows.append(first)
        w_rows.append(jnp.sum(jnp.where(hit, scores, 0.0), axis=0, keepdims=True))
        sel = jnp.where(hit, 1.0, sel)
        work = jnp.where(hit, neg, work)
    w = jnp.concatenate(w_rows, axis=0)
    wgt_ref[...] = w / jnp.sum(w, axis=0, keepdims=True) * ROUTED_SCALE
    idx_ref[...] = jnp.concatenate(idx_rows, axis=0)

    before = jnp.dot(sel.astype(BF16), tri_ref[...], preferred_element_type=F32)
    rank = base_scr[...] + before
    pos_rows = [jnp.sum(jnp.where(hit, rank, 0.0), axis=0, keepdims=True) for hit in hits]
    pos_ref[...] = jnp.concatenate(pos_rows, axis=0).astype(jnp.int32)
    base_scr[...] = base_scr[...] + jnp.sum(sel, axis=1, keepdims=True)
    cnt_ref[...] = jnp.broadcast_to(base_scr[...], cnt_ref.shape)


def _route(h16, rw_t, router_b, tm):
    n_tok, d = h16.shape
    tri = (jnp.arange(tm)[:, None] < jnp.arange(tm)[None, :]).astype(BF16)
    out_col = pl.BlockSpec((TOP_K, tm), lambda i: (0, i))
    return pl.pallas_call(
        _route_kernel,
        grid=(n_tok // tm,),
        in_specs=[pl.BlockSpec((tm, d), lambda i: (i, 0)),
                  pl.BlockSpec((N_EXPERTS, d), lambda i: (0, 0)),
                  pl.BlockSpec((N_EXPERTS, 1), lambda i: (0, 0)),
                  pl.BlockSpec((tm, tm), lambda i: (0, 0))],
        out_specs=[out_col, out_col, out_col, pl.BlockSpec((N_EXPERTS, LANES), lambda i: (0, 0))],
        out_shape=[jax.ShapeDtypeStruct((TOP_K, n_tok), jnp.int32), jax.ShapeDtypeStruct((TOP_K, n_tok), F32),
                   jax.ShapeDtypeStruct((TOP_K, n_tok), jnp.int32),
                   jax.ShapeDtypeStruct((N_EXPERTS, LANES), F32)],
        scratch_shapes=[pltpu.VMEM((N_EXPERTS, 1), F32)],
        compiler_params=_cparams(("arbitrary",), 32),
        name="moe_router",
    )(h16, rw_t, router_b.reshape(N_EXPERTS, 1), tri)


def _row_copy(src_ref, src_row, dst_ref, dst_row, sem):
    return pltpu.make_async_copy(src_ref.at[pl.ds(src_row, 1)], dst_ref.at[pl.ds(dst_row, 1)], sem)


def _dispatch_kernel(dest_ref, hp_ref, _xs_in, xs_ref, sem):
    tm = hp_ref.shape[0]

    def start(t, c):
        for r in range(TOP_K):
            _row_copy(hp_ref, t, xs_ref, dest_ref[r, t], sem).start()
        return c

    lax.fori_loop(0, tm, start, 0)

    def wait(t, c):
        for r in range(TOP_K):
            _row_copy(hp_ref, t, xs_ref, dest_ref[r, t], sem).wait()
        return c

    lax.fori_loop(0, tm, wait, 0)


def _dispatch(hp, dest3, n_slots):
    n_tok, half = hp.shape
    tm = dest3.shape[2]
    xs0 = jnp.zeros((n_slots, half), jnp.uint32)
    return pl.pallas_call(
        _dispatch_kernel,
        grid=(n_tok // tm,),
        in_specs=[pl.BlockSpec((None, TOP_K, tm), lambda i: (i, 0, 0), memory_space=pltpu.SMEM),
                  pl.BlockSpec((tm, half), lambda i: (i, 0)),
                  pl.BlockSpec(memory_space=pl.ANY)],
        out_specs=pl.BlockSpec(memory_space=pl.ANY),
        out_shape=jax.ShapeDtypeStruct((n_slots, half), jnp.uint32),
        scratch_shapes=[pltpu.SemaphoreType.DMA(())],
        input_output_aliases={2: 0},
        compiler_params=_cparams(("arbitrary",), 32),
        name="moe_dispatch",
    )(dest3, hp, xs0)


def _expert_kernel(be_ref, nu_ref, xs_ref, wg_ref, wu_ref, wd_ref, ys_ref, wg_s, wu_s, wd_s):
    i = pl.program_id(0)
    used = i < nu_ref[0]
    prev_e = be_ref[jnp.maximum(i - 1, 0)]
    fresh = jnp.logical_or(i == 0, be_ref[i] != prev_e)

    @pl.when(jnp.logical_and(used, fresh))
    def _():
        wg_s[...] = wg_ref[0].astype(BF16)
        wu_s[...] = wu_ref[0].astype(BF16)
        wd_s[...] = wd_ref[0].astype(BF16)

    @pl.when(used)
    def _():
        w = xs_ref[...]
        half = D_MODEL // 2
        xa = pltpu.bitcast(w & jnp.uint32(0xFFFF0000), F32).astype(BF16)
        xb = pltpu.bitcast(w << 16, F32).astype(BF16)

        def proj(w_s):
            return (jnp.dot(xa, w_s[:half, :], preferred_element_type=F32)
                    + jnp.dot(xb, w_s[half:, :], preferred_element_type=F32))

        g = proj(wg_s)
        act = (_silu(g) * proj(wu_s)).astype(BF16)
        ys_ref[...] = jnp.dot(act, wd_s[...], preferred_element_type=F32)

    @pl.when(jnp.logical_not(used))
    def _():
        ys_ref[...] = jnp.zeros_like(ys_ref)


def _experts(xs, block_e, n_used, w_gate, w_up, w_down, bm):
    n_slots, half = xs.shape
    d = D_MODEL
    grid_spec = pltpu.PrefetchScalarGridSpec(
        num_scalar_prefetch=2,
        grid=(n_slots // bm,),
        in_specs=[pl.BlockSpec((bm, half), lambda i, be, nu: (i, 0)),
                  pl.BlockSpec((1, d, EXPERT_FF), lambda i, be, nu: (be[i], 0, 0)),
                  pl.BlockSpec((1, d, EXPERT_FF), lambda i, be, nu: (be[i], 0, 0)),
                  pl.BlockSpec((1, EXPERT_FF, d), lambda i, be, nu: (be[i], 0, 0))],
        out_specs=pl.BlockSpec((bm, d), lambda i, be, nu: (i, 0)),
        scratch_shapes=[pltpu.VMEM((d, EXPERT_FF), BF16), pltpu.VMEM((d, EXPERT_FF), BF16),
                        pltpu.VMEM((EXPERT_FF, d), BF16)],
    )
    return pl.pallas_call(
        _expert_kernel,
        grid_spec=grid_spec,
        out_shape=jax.ShapeDtypeStruct((n_slots, d), F32),
        compiler_params=_cparams(("arbitrary",), 40),
        name="moe_experts",
    )(block_e, n_used, xs, w_gate, w_up, w_down)


def _combine_kernel(dest_ref, x_ref, h_ref, tw_ref, gate_ref, wsg_ref, wsu_ref, wsd_ref, ys_ref, o_ref, buf, sem):
    tm = x_ref.shape[0]

    def start(t, c):
        for r in range(TOP_K):
            _row_copy(ys_ref, dest_ref[r, t], buf.at[r], t, sem).start()
        return c

    lax.fori_loop(0, tm, start, 0)

    h = h_ref[...]
    act = (_silu(jnp.dot(h, wsg_ref[...], preferred_element_type=F32))
           * jnp.dot(h, wsu_ref[...], preferred_element_type=F32)).astype(BF16)
    shared = jnp.dot(act, wsd_ref[...], preferred_element_type=F32)

    def wait(t, c):
        for r in range(TOP_K):
            _row_copy(ys_ref, dest_ref[r, t], buf.at[r], t, sem).wait()
        return c

    lax.fori_loop(0, tm, wait, 0)

    tw = tw_ref[...]
    moe = buf[0] * tw[:, 0:1]
    for r in range(1, TOP_K):
        moe = moe + buf[r] * tw[:, r:r + 1]
    o_ref[...] = x_ref[...] + gate_ref[...] * (moe + shared)


def _combine(x_tok, h16, ys, dest3, top_w, mod4, ws_gate, ws_up, ws_down, row_of_tile):
    n_tok, d = x_tok.shape
    tm = dest3.shape[2]
    ff = ws_gate.shape[1]
    row = pl.BlockSpec((tm, d), lambda i: (i, 0))
    return pl.pallas_call(
        _combine_kernel,
        grid=(n_tok // tm,),
        in_specs=[pl.BlockSpec((None, TOP_K, tm), lambda i: (i, 0, 0), memory_space=pltpu.SMEM),
                  row, row,
                  pl.BlockSpec((tm, TOP_K), lambda i: (i, 0)),
                  _mod_spec(5, row_of_tile),
                  pl.BlockSpec((d, ff), lambda i: (0, 0)),
                  pl.BlockSpec((d, ff), lambda i: (0, 0)),
                  pl.BlockSpec((ff, d), lambda i: (0, 0)),
                  pl.BlockSpec(memory_space=pl.ANY)],
        out_specs=row,
        out_shape=jax.ShapeDtypeStruct((n_tok, d), F32),
        scratch_shapes=[pltpu.VMEM((TOP_K, tm, d), F32), pltpu.SemaphoreType.DMA(())],
        compiler_params=_cparams(("arbitrary",), 48),
        name="moe_combine",
    )(dest3, x_tok, h16, top_w, mod4, ws_gate, ws_up, ws_down, ys)


def _moe(x_tok, h16, hp, mod4, row_of_tile, rw_t, router_b, w_gate, w_up, w_down, ws_gate, ws_up, ws_down):
    n_tok = x_tok.shape[0]
    bm = MOE_BM
    tm_r = TM_ROUTE if n_tok % TM_ROUTE == 0 else TM
    top_idx, top_w, pos, cnt = _route(h16, rw_t, router_b, tm_r)

    counts = cnt[:, 0].astype(jnp.int32)
    padded = (counts + bm - 1) // bm * bm
    ends = jnp.cumsum(padded)
    n_blocks = -(-(n_tok * TOP_K + N_EXPERTS * (bm - 1)) // bm)
    dest = (ends - padded)[top_idx] + pos
    block_e = jnp.minimum(jnp.searchsorted(ends, jnp.arange(n_blocks, dtype=jnp.int32) * bm, side='right'),
                          N_EXPERTS - 1).astype(jnp.int32)
    n_used = (ends[-1] // bm).astype(jnp.int32).reshape(1)
    tm = TM
    dest3 = dest.reshape(TOP_K, n_tok // tm, tm).transpose(1, 0, 2)

    xs = _dispatch(hp, dest3, n_blocks * bm)
    ys = _experts(xs, block_e, n_used, w_gate, w_up, w_down, bm)
    return _combine(x_tok, h16, ys, dest3, top_w.T, mod4, ws_gate, ws_up, ws_down, row_of_tile)


def _block_diag_tiles(w):
    per = MXU_DIM // LRU_BLOCK_DIM
    w = w.reshape(2, D_MODEL // MXU_DIM, per, LRU_BLOCK_DIM, LRU_BLOCK_DIM)
    eye = jnp.eye(per, dtype=w.dtype)
    t = jnp.einsum('ztpde,pq->ztpdqe', w, eye)
    return t.reshape(2, D_MODEL // MXU_DIM, MXU_DIM, MXU_DIM).astype(BF16)


def _rope_tables(n_lat, n_ctx):
    rows = n_lat // GRID_W
    row = jnp.repeat(jnp.arange(rows), GRID_W)
    col = jnp.tile(jnp.arange(GRID_W), rows)
    inv = ROPE_THETA ** (-jnp.arange(ROPE_FREQS, dtype=F32) / ROPE_FREQS)
    ang = jnp.stack([row[:, None] * inv, col[:, None] * inv], axis=1)
    ang = jnp.concatenate([ang, ang], axis=2).reshape(n_lat, HEAD_DIM)
    ang = jnp.tile(ang, (1, LANES // HEAD_DIM))
    ang = jnp.concatenate([ang, jnp.zeros((n_ctx, LANES), F32)], axis=0)
    lane = jnp.arange(LANES)
    sign = jnp.where((lane % (2 * ROPE_FREQS)) < ROPE_FREQS, -1.0, 1.0).astype(F32)
    return jnp.cos(ang), jnp.sin(ang) * sign


def kernel(x, c, ctx, c_ctx, w_mod, b_mod, norm1_g, w_in, q_norm_g, k_norm_g, lambda_qk, subln_g, w_attn_o, conv_w, conv_b, lru_wa, lru_ba, lru_wi, lru_bi, lru_lambda, w_lru_o, w_four_o, w_out, norm2_g, router_w, router_b, exp_w_gate, exp_w_up, exp_w_down, sh_w_gate, sh_w_up, sh_w_down):
    bsz, n_lat, d = x.shape
    n_ctx = ctx.shape[1]
    depth = w_mod.shape[0]
    n_latent_rows = bsz * n_lat
    n_tok = n_latent_rows + bsz * n_ctx
    assert d == D_MODEL and bsz + 1 <= MOD_ROWS
    assert n_lat % TM == 0 and n_ctx % TM == 0 and n_lat % GRID_W == 0
    tm_in = TM_IN if (n_lat % TM_IN == 0 and (bsz * n_ctx) % TM_IN == 0) else TM

    cvec = jnp.zeros((MOD_ROWS, d), F32).at[:bsz].set(c).at[bsz].set(c_ctx)
    mods = _modulation(cvec, w_mod, b_mod)

    def row_of_tile_fn(tm):
        per_batch = n_lat // tm
        return lambda i: jnp.minimum(i // per_batch, bsz)

    assert n_lat % n_ctx == 0
    cos_t, sin_t = _rope_tables(n_lat, n_ctx)
    group_mat = jnp.kron(jnp.eye(d // HEAD_DIM, dtype=F32), jnp.ones((HEAD_DIM, HEAD_DIM), F32)).astype(BF16)
    m = jnp.arange(FOUR_GROUP_DIM, dtype=jnp.int32)
    ang_c = ((m[:, None] * m[None, :]) % FOUR_GROUP_DIM).astype(F32) * (2.0 * math.pi / FOUR_GROUP_DIM)
    inv_sqrt_c = FOUR_GROUP_DIM ** -0.5
    cs_mat = jnp.concatenate([jnp.cos(ang_c) * inv_sqrt_c, jnp.sin(ang_c) * inv_sqrt_c], axis=1).astype(BF16)
    ct_lat, st_lat = _dft_tables(n_lat)
    ct_ctx, st_ctx = _dft_tables(n_ctx)
    tmk_lat = 1024 if n_lat % 1024 == 0 else TM
    tq = TQ if n_lat % TQ == 0 else TM
    n_keys = n_ctx + n_lat
    tk = TK if n_keys % TK == 0 else TM

    x_tok = jnp.concatenate([x.reshape(n_latent_rows, d), ctx.reshape(bsz * n_ctx, d)], axis=0)
    for l in range(depth):
        last = l == depth - 1
        lam_init = 0.8 - 0.6 * math.exp(-0.3 * l)
        mod4 = mods[l].reshape(MOD_ROWS, 6, 1, d)

        u = _in_projection(x_tok, mod4, norm1_g[l], w_in[l].astype(BF16), row_of_tile_fn(tm_in), tm_in)

        k_all, q_all, vt_all = _qkv_prepare(u, k_norm_g[l], q_norm_g[l], group_mat, cos_t, sin_t,
                                            bsz, n_lat, n_ctx)
        attn_l = _diff_attention(q_all, k_all, vt_all, lambda_qk[l], subln_g[l], lam_init,
                                 0, n_lat, 0, n_keys, tq, tk)
        attn_c = attn_l
        if not last:
            attn_c = _diff_attention(q_all, k_all, vt_all, lambda_qk[l], subln_g[l], lam_init,
                                     n_lat, n_ctx, n_lat, n_ctx, TM, TM)

        rec = _rglru(u, conv_w[l], conv_b[l], _block_diag_tiles(lru_wa[l]), lru_ba[l],
                     _block_diag_tiles(lru_wi[l]), lru_bi[l], lru_lambda[l], bsz, n_lat, n_ctx)

        y_four = _chan_dft(u, cs_mat)
        four_l = _time_dft(y_four, ct_lat, st_lat, bsz, n_lat, 0, tmk_lat)
        four_c = four_l
        if not last:
            four_c = _time_dft(y_four, ct_ctx, st_ctx, bsz, n_ctx, n_latent_rows, n_ctx)

        n_rows = n_latent_rows if last else n_tok
        row_of_tile = row_of_tile_fn(TM)
        x_mid, h16, hp = _merge(x_tok, attn_l, attn_c, rec, four_l, four_c, u, w_attn_o[l].astype(BF16),
                                w_lru_o[l].astype(BF16), w_four_o[l].astype(BF16), w_out[l].astype(BF16),
                                mod4, norm2_g[l], row_of_tile, n_rows)
        x_tok = _moe(x_mid, h16, hp, mod4, row_of_tile, router_w[l].T.astype(BF16), router_b[l],
                     exp_w_gate[l], exp_w_up[l], exp_w_down[l],
                     sh_w_gate[l].astype(BF16), sh_w_up[l].astype(BF16), sh_w_down[l].astype(BF16))
    return x_tok[:n_latent_rows].reshape(bsz, n_lat, d)
```

```python
import functools
import math

import jax
import jax.numpy as jnp
from jax import lax
from jax.experimental import pallas as pl
from jax.experimental.pallas import tpu as pltpu

F32 = jnp.float32
BF16 = jnp.bfloat16

D_MODEL = 1024
EPS = 1e-6
GRID_W = 64

N_HEADS = 8
HEAD_DIM = 64
V_DIM = 2 * HEAD_DIM
ATTN_SCALE = HEAD_DIM ** -0.5
ROPE_THETA = 10000.0
ROPE_FREQS = HEAD_DIM // 4

LRU_BLOCKS = 16
LRU_BLOCK_DIM = D_MODEL // LRU_BLOCKS
LRU_C = 8.0
CONV_W = 4

FOUR_GROUPS = 4
FOUR_GROUP_DIM = D_MODEL // FOUR_GROUPS

CB_K, CB_V, CB_LX, CB_Q, CB_LG, CB_F, CB_G = 0, 1, 2, 3, 4, 5, 6
IN_BLOCKS = 9

N_EXPERTS = 256
TOP_K = 8
N_GROUPS = 8
GROUP_SIZE = N_EXPERTS // N_GROUPS
TOPK_GROUPS = 4
EXPERT_FF = 256
ROUTED_SCALE = 2.5

LANES = 128
SUBLANES = 8
MXU_DIM = 256
VMEM_BYTES_V7X = 64 * 1024 * 1024

MOD_ROWS = 8
TM = 256
TM_IN = 512
TQ = 512
TK = 768
TM_ROUTE = 512
MOE_BM = 256
LOG2E = 1.4426950408889634


def _cparams(sem, vmem_mb):
    return pltpu.CompilerParams(dimension_semantics=sem, vmem_limit_bytes=int(vmem_mb * 1024 * 1024))


def _silu(x):
    return x * jax.nn.sigmoid(x)


def _mod_kernel(c_ref, w_ref, b_ref, o_ref):
    c = c_ref[...]
    s = _silu(c).astype(BF16)
    o_ref[0] = jnp.dot(s, w_ref[0].astype(BF16), preferred_element_type=F32) + b_ref[0]


def _modulation(cvec, w_mod, b_mod):
    n_layers, d, six_d = w_mod.shape
    nj = six_d // d
    return pl.pallas_call(
        _mod_kernel,
        grid=(n_layers, nj),
        in_specs=[
            pl.BlockSpec((MOD_ROWS, d), lambda l, j: (0, 0)),
            pl.BlockSpec((1, d, d), lambda l, j: (l, 0, j)),
            pl.BlockSpec((1, 1, d), lambda l, j: (l, 0, j)),
        ],
        out_specs=pl.BlockSpec((1, MOD_ROWS, d), lambda l, j: (l, 0, j)),
        out_shape=jax.ShapeDtypeStruct((n_layers, MOD_ROWS, six_d), F32),
        compiler_params=_cparams(("parallel", "parallel"), 32),
        name="modulation",
    )(cvec, w_mod, b_mod.reshape(n_layers, 1, six_d))


def _mod_spec(part, row_of_tile, n_grid_axes=1):
    if n_grid_axes == 1:
        return pl.BlockSpec((None, None, 1, D_MODEL), lambda i: (row_of_tile(i), part, 0, 0))
    return pl.BlockSpec((None, None, 1, D_MODEL), lambda i, j: (row_of_tile(i), part, 0, 0))


def _inproj_kernel(x_ref, g_ref, sh_ref, sc_ref, w_ref, o_ref, h_scr):
    @pl.when(pl.program_id(1) == 0)
    def _():
        x = x_ref[...]
        y = x * lax.rsqrt(jnp.mean(x * x, axis=-1, keepdims=True) + EPS)
        h = (y * g_ref[...]) * (1.0 + sc_ref[...]) + sh_ref[...]
        h_scr[...] = h.astype(BF16)

    o_ref[...] = jnp.dot(h_scr[...], w_ref[...], preferred_element_type=F32)


def _in_projection(x_tok, mod4, norm_g, w_in_bf16, row_of_tile, tm):
    n_tok, d = x_tok.shape
    n_cols = w_in_bf16.shape[1]
    return pl.pallas_call(
        _inproj_kernel,
        grid=(n_tok // tm, n_cols // d),
        in_specs=[
            pl.BlockSpec((tm, d), lambda i, j: (i, 0)),
            pl.BlockSpec((1, d), lambda i, j: (0, 0)),
            _mod_spec(0, row_of_tile, 2),
            _mod_spec(1, row_of_tile, 2),
            pl.BlockSpec((d, d), lambda i, j: (0, j)),
        ],
        out_specs=pl.BlockSpec((tm, d), lambda i, j: (i, j)),
        out_shape=jax.ShapeDtypeStruct((n_tok, n_cols), F32),
        scratch_shapes=[pltpu.VMEM((tm, d), BF16)],
        compiler_params=_cparams(("parallel", "arbitrary"), 40),
        name="in_projection",
    )(x_tok, norm_g.reshape(1, d), mod4, mod4, w_in_bf16)


def _prep_kernel(uk_ref, uq_ref, uv_ref, kg_ref, qg_ref, gm_ref, cos_ref, sin_ref, k_out, q_out, vt_out, *, tm):
    gm = gm_ref[...]
    lane = lax.broadcasted_iota(jnp.int32, (tm, LANES), 1)
    first_half = (lane % (2 * ROPE_FREQS)) < ROPE_FREQS

    def norm_rope(x, g):
        ss = jnp.dot((x * x).astype(BF16), gm, preferred_element_type=F32) * (1.0 / HEAD_DIM)
        y = x * lax.rsqrt(ss + EPS) * g
        c = cos_ref[...]
        s = sin_ref[...]
        outs = []
        for j in range(D_MODEL // LANES):
            yt = y[:, j * LANES:(j + 1) * LANES]
            rot = jnp.where(first_half, pltpu.roll(yt, LANES - ROPE_FREQS, 1), pltpu.roll(yt, ROPE_FREQS, 1))
            outs.append(yt * c + rot * s)
        return jnp.concatenate(outs, axis=1)

    k_out[...] = norm_rope(uk_ref[...], kg_ref[...]).astype(BF16)
    qn = norm_rope(uq_ref[...], qg_ref[...]) * (ATTN_SCALE * LOG2E)
    lo = lane < HEAD_DIM
    for h in range(N_HEADS):
        qt = qn[:, h * LANES:(h + 1) * LANES]
        q_out[:, (2 * h) * LANES:(2 * h + 1) * LANES] = jnp.where(lo, qt, 0.0).astype(BF16)
        q_out[:, (2 * h + 1) * LANES:(2 * h + 2) * LANES] = jnp.where(lo, 0.0, qt).astype(BF16)
    vt_out[...] = uv_ref[...].T.astype(BF16)


def _qkv_prepare(u, k_gain, q_gain, group_mat, cos_t, sin_t, bsz, n_lat, n_ctx):
    tm = TM
    d = D_MODEL
    n_s, n_c = n_lat // tm, n_ctx // tm
    n_rows = n_lat + n_ctx
    lat_tiles = bsz * n_s
    kg = jnp.tile(k_gain, d // HEAD_DIM).reshape(1, d)
    qg = jnp.tile(q_gain, d // HEAD_DIM).reshape(1, d)

    def tok_tile(b, j):
        return jnp.where(j < n_s, b * n_s + j, lat_tiles + b * n_c + (j - n_s))

    def u_spec(cb):
        return pl.BlockSpec((tm, d), lambda b, j: (tok_tile(b, j), cb))

    vec = pl.BlockSpec((1, d), lambda b, j: (0, 0))
    rope_spec = pl.BlockSpec((tm, LANES), lambda b, j: (j, 0))
    return pl.pallas_call(
        functools.partial(_prep_kernel, tm=tm),
        grid=(bsz, n_s + n_c),
        in_specs=[u_spec(CB_K), u_spec(CB_Q), u_spec(CB_V), vec, vec,
                  pl.BlockSpec((d, d), lambda b, j: (0, 0)), rope_spec, rope_spec],
        out_specs=[
            pl.BlockSpec((None, tm, d), lambda b, j: (b, j, 0)),
            pl.BlockSpec((None, tm, 2 * d), lambda b, j: (b, j, 0)),
            pl.BlockSpec((None, d, tm), lambda b, j: (b, 0, j)),
        ],
        out_shape=[jax.ShapeDtypeStruct((bsz, n_rows, d), BF16),
                   jax.ShapeDtypeStruct((bsz, n_rows, 2 * d), BF16),
                   jax.ShapeDtypeStruct((bsz, d, n_rows), BF16)],
        compiler_params=_cparams(("parallel", "parallel"), 40),
        name="qkv_prepare",
    )(u, u, u, kg, qg, group_mat, cos_t, sin_t)


def _attn_kernel(lq_ref, sg_ref, q_ref, k_ref, vt_ref, o_ref, s_scr, *, n_keys, tk, lam_init):
    lq = lq_ref[...]
    lam = (jnp.exp(jnp.sum(lq[0:1] * lq[1:2], axis=1, keepdims=True))
           - jnp.exp(jnp.sum(lq[2:3] * lq[3:4], axis=1, keepdims=True)) + lam_init)
    n_chunks = n_keys // tk
    tq = q_ref.shape[0]
    outs = []
    for m in range(2):
        qz = q_ref[:, m * LANES:(m + 1) * LANES]
        mx = None
        for c in range(n_chunks):
            s = lax.dot_general(k_ref[c * tk:(c + 1) * tk, :], qz, (((1,), (1,)), ((), ())),
                                preferred_element_type=F32)
            s_scr[c * tk:(c + 1) * tk, :] = s
            cm = jnp.max(s, axis=0, keepdims=True)
            mx = cm if mx is None else jnp.maximum(mx, cm)
        l = jnp.zeros((1, tq), F32)
        acc = jnp.zeros((V_DIM, tq), F32)
        for c in range(n_chunks):
            p = jnp.exp2(s_scr[c * tk:(c + 1) * tk, :] - mx)
            l = l + jnp.sum(p, axis=0, keepdims=True)
            acc = acc + jnp.dot(vt_ref[:, c * tk:(c + 1) * tk], p.astype(BF16), preferred_element_type=F32)
        outs.append(acc / l)
    o = outs[0] - lam * outs[1]
    ms = jnp.mean(o * o, axis=0, keepdims=True)
    on = (o * lax.rsqrt(ms + EPS) * sg_ref[...]) * (1.0 - lam_init)
    o_ref[...] = on.T


def _diff_attention(q_all, k_all, vt_all, lambda_qk, subln_g, lam_init, q_row0, n_q, key_row0, n_keys, tq, tk):
    bsz = q_all.shape[0]
    d = D_MODEL
    nq = n_q // tq
    qb0 = q_row0 // tq
    kb0 = key_row0 // n_keys
    s_bytes = n_keys * tq * 4
    return pl.pallas_call(
        functools.partial(_attn_kernel, n_keys=n_keys, tk=tk, lam_init=lam_init),
        grid=(bsz, N_HEADS, nq),
        in_specs=[
            pl.BlockSpec((4, HEAD_DIM), lambda b, h, i: (0, 0)),
            pl.BlockSpec((V_DIM, 1), lambda b, h, i: (0, 0)),
            pl.BlockSpec((None, tq, 2 * LANES), lambda b, h, i: (b, qb0 + i, h)),
            pl.BlockSpec((None, n_keys, LANES), lambda b, h, i: (b, kb0, h)),
            pl.BlockSpec((None, V_DIM, n_keys), lambda b, h, i: (b, h, kb0)),
        ],
        out_specs=pl.BlockSpec((tq, LANES), lambda b, h, i: (b * nq + i, h)),
        out_shape=jax.ShapeDtypeStruct((bsz * n_q, d), F32),
        scratch_shapes=[pltpu.VMEM((n_keys, tq), F32)],
        compiler_params=_cparams(("parallel", "parallel", "arbitrary"), 32 + s_bytes / 2 ** 20),
        name="diff_attention",
    )(lambda_qk, subln_g.reshape(V_DIM, 1), q_all, k_all, vt_all)


def _lru_kernel(*refs, rev, n_c, n_s, tm):
    if rev:
        (cur_ref, prev_ref, next_ref, cw_ref, cb_ref, wa_ref, ba_ref, wi_ref, bi_ref, lam_ref, fwd_ref,
         o_ref, a_scr, b_scr, h_scr) = refs
    else:
        (cur_ref, prev_ref, next_ref, cw_ref, cb_ref, wa_ref, ba_ref, wi_ref, bi_ref, lam_ref,
         o_ref, a_scr, b_scr, h_scr) = refs
    s = pl.program_id(1)
    is_ctx = s < n_c
    if rev:
        tile = jnp.where(is_ctx, n_c - 1 - s, n_s - 1 - (s - n_c))
    else:
        tile = jnp.where(is_ctx, s, s - n_c)
    n_tile = jnp.where(is_ctx, n_c, n_s)

    @pl.when(s == 0)
    def _():
        h_scr[...] = jnp.zeros_like(h_scr)

    x = cur_ref[...]
    pv = jnp.where(tile == 0, 0.0, prev_ref[...])
    nx = jnp.where(tile == n_tile - 1, 0.0, next_ref[...])
    r8 = lax.broadcasted_iota(jnp.int32, (SUBLANES, D_MODEL), 0)

    def earlier(k):
        rolled = pltpu.roll(x, k, 0)
        top = jnp.where(r8 < k, pltpu.roll(pv, k, 0), rolled[:SUBLANES])
        return jnp.concatenate([top, rolled[SUBLANES:]], axis=0)

    rolled = pltpu.roll(x, tm - 1, 0)
    bot = jnp.where(r8 >= SUBLANES - 1, pltpu.roll(nx, SUBLANES - 1, 0), rolled[tm - SUBLANES:])
    later = jnp.concatenate([rolled[:tm - SUBLANES], bot], axis=0)
    cw = cw_ref[...]
    conv = earlier(2) * cw[0:1] + earlier(1) * cw[1:2] + x * cw[2:3] + later * cw[3:4] + cb_ref[...]

    conv16 = conv.astype(BF16)

    def block_diag(w_ref, b_ref):
        parts = [jnp.dot(conv16[:, j * MXU_DIM:(j + 1) * MXU_DIM], w_ref[j], preferred_element_type=F32)
                 for j in range(D_MODEL // MXU_DIM)]
        return jnp.concatenate(parts, axis=1) + b_ref[...]

    r = jax.nn.sigmoid(block_diag(wa_ref, ba_ref))
    gate_i = jax.nn.sigmoid(block_diag(wi_ref, bi_ref))
    neg_lam = -lam_ref[...]
    softplus = jnp.maximum(neg_lam, 0.0) + jnp.log1p(jnp.exp(-jnp.abs(neg_lam)))
    log_a = -LRU_C * r * softplus
    a = jnp.exp(log_a)
    mult = jnp.sqrt(1.0 - a * a)
    a_scr[...] = a
    b_scr[...] = mult * gate_i * conv

    def body(t, h):
        row = tm - 1 - t if rev else t
        h = a_scr[pl.ds(row, 1), :] * h + b_scr[pl.ds(row, 1), :]
        if rev:
            o_ref[pl.ds(row, 1), :] = fwd_ref[pl.ds(row, 1), :] + h
        else:
            o_ref[pl.ds(row, 1), :] = h
        return h

    h_scr[...] = lax.fori_loop(0, tm, body, h_scr[...], unroll=8)


def _rglru(u, conv_w, conv_b, wa_bd, ba, wi_bd, bi, lam, bsz, n_lat, n_ctx):
    tm = TM
    d = D_MODEL
    n_tok = u.shape[0]
    n_s, n_c = n_lat // tm, n_ctx // tm
    lat_tiles = bsz * n_s
    per8 = tm // SUBLANES
    last8 = n_tok // SUBLANES - 1

    def make(rev, fwd):
        def blk(b, s):
            is_ctx = s < n_c
            if rev:
                tile = jnp.where(is_ctx, n_c - 1 - s, n_s - 1 - (s - n_c))
            else:
                tile = jnp.where(is_ctx, s, s - n_c)
            return jnp.where(is_ctx, lat_tiles + b * n_c + tile, b * n_s + tile)

        vec = pl.BlockSpec((None, 1, d), lambda b, s: (int(rev), 0, 0))
        wspec = pl.BlockSpec((None, d // MXU_DIM, MXU_DIM, MXU_DIM), lambda b, s: (int(rev), 0, 0, 0))
        row_spec = pl.BlockSpec((tm, d), lambda b, s: (blk(b, s), 0))
        in_specs = [
            pl.BlockSpec((tm, d), lambda b, s: (blk(b, s), CB_LX)),
            pl.BlockSpec((SUBLANES, d), lambda b, s: (jnp.maximum(blk(b, s) * per8 - 1, 0), CB_LX)),
            pl.BlockSpec((SUBLANES, d), lambda b, s: (jnp.minimum((blk(b, s) + 1) * per8, last8), CB_LX)),
            pl.BlockSpec((CONV_W, d), lambda b, s: (0, 0)),
            pl.BlockSpec((1, d), lambda b, s: (0, 0)),
            wspec, vec, wspec, vec, vec,
        ]
        args = [u, u, u, conv_w, conv_b.reshape(1, d), wa_bd, ba.reshape(2, 1, d), wi_bd, bi.reshape(2, 1, d),
                lam.reshape(2, 1, d)]
        if rev:
            in_specs.append(row_spec)
            args.append(fwd)
        return pl.pallas_call(
            functools.partial(_lru_kernel, rev=rev, n_c=n_c, n_s=n_s, tm=tm),
            grid=(bsz, n_c + n_s),
            in_specs=in_specs,
            out_specs=row_spec,
            out_shape=jax.ShapeDtypeStruct((n_tok, d), F32),
            scratch_shapes=[pltpu.VMEM((tm, d), F32), pltpu.VMEM((tm, d), F32), pltpu.VMEM((1, d), F32)],
            input_output_aliases={10: 0} if rev else {},
            compiler_params=_cparams(("arbitrary", "arbitrary"), 40),
            name="rglru_bwd" if rev else "rglru_fwd",
        )(*args)

    fwd = make(False, None)
    return make(True, fwd)


def _chan_dft_kernel(u_ref, cs_ref, o_ref):
    x = u_ref[...].astype(BF16)
    cs = cs_ref[...]
    gd = FOUR_GROUP_DIM
    for g in range(FOUR_GROUPS):
        y = jnp.dot(x[:, g * gd:(g + 1) * gd], cs, preferred_element_type=F32)
        o_ref[:, g * gd:(g + 1) * gd] = y[:, :gd].astype(BF16)
        o_ref[:, D_MODEL + g * gd:D_MODEL + (g + 1) * gd] = y[:, gd:].astype(BF16)


def _chan_dft(u, cs_mat):
    n_tok = u.shape[0]
    tm = TM
    d = D_MODEL
    return pl.pallas_call(
        _chan_dft_kernel,
        grid=(n_tok // tm,),
        in_specs=[pl.BlockSpec((tm, d), lambda i: (i, CB_F)),
                  pl.BlockSpec((FOUR_GROUP_DIM, 2 * FOUR_GROUP_DIM), lambda i: (0, 0))],
        out_specs=pl.BlockSpec((tm, 2 * d), lambda i: (i, 0)),
        out_shape=jax.ShapeDtypeStruct((n_tok, 2 * d), BF16),
        compiler_params=_cparams(("parallel",), 32),
        name="fourier_channels",
    )(u, cs_mat)


def _time_dft_kernel(ct_ref, st_ref, yc_ref, ys_ref, o_ref, acc, *, scale):
    kk = pl.program_id(2)

    @pl.when(kk == 0)
    def _():
        acc[...] = jnp.zeros_like(acc)

    acc[...] += (jnp.dot(ct_ref[...], yc_ref[...], preferred_element_type=F32)
                 - jnp.dot(st_ref[...], ys_ref[...], preferred_element_type=F32))

    @pl.when(kk == pl.num_programs(2) - 1)
    def _():
        o_ref[...] = acc[...] * scale


def _time_dft(y, ct, st, bsz, n_pos, row0, tmk):
    d = D_MODEL
    nt = n_pos // tmk
    rb0 = row0 // tmk
    return pl.pallas_call(
        functools.partial(_time_dft_kernel, scale=float(n_pos) ** -0.5),
        grid=(bsz, nt, nt),
        in_specs=[
            pl.BlockSpec((tmk, tmk), lambda b, i, k: (i, k)),
            pl.BlockSpec((tmk, tmk), lambda b, i, k: (i, k)),
            pl.BlockSpec((tmk, d), lambda b, i, k: (rb0 + b * nt + k, 0)),
            pl.BlockSpec((tmk, d), lambda b, i, k: (rb0 + b * nt + k, 1)),
        ],
        out_specs=pl.BlockSpec((tmk, d), lambda b, i, k: (b * nt + i, 0)),
        out_shape=jax.ShapeDtypeStruct((bsz * n_pos, d), F32),
        scratch_shapes=[pltpu.VMEM((tmk, d), F32)],
        compiler_params=_cparams(("parallel", "parallel", "arbitrary"), 48),
        name="fourier_positions",
    )(ct, st, y, y)


def _dft_tables(n):
    k = jnp.arange(n, dtype=jnp.int32)[:, None]
    step = 2.0 * math.pi / n
    if n <= 1024:
        ang = ((k * jnp.arange(n, dtype=jnp.int32)[None, :]) % n).astype(F32) * step
        return jnp.cos(ang).astype(BF16), jnp.sin(ang).astype(BF16)
    hi = jnp.arange(n // LANES, dtype=jnp.int32)[None, :] * LANES
    lo = jnp.arange(LANES, dtype=jnp.int32)[None, :]
    ang_a = ((k * hi) % n).astype(F32) * step
    ang_b = ((k * lo) % n).astype(F32) * step
    ca, sa = jnp.cos(ang_a)[:, :, None], jnp.sin(ang_a)[:, :, None]
    cb, sb = jnp.cos(ang_b)[:, None, :], jnp.sin(ang_b)[:, None, :]
    ct = (ca * cb - sa * sb).reshape(n, n).astype(BF16)
    st = (sa * cb + ca * sb).reshape(n, n).astype(BF16)
    return ct, st


def _merge_kernel(x_ref, attn_l_ref, attn_c_ref, rec_ref, four_l_ref, four_c_ref, lg_ref, g0_ref, g1_ref, g2_ref,
                  wa_ref, wl_ref, wf_ref, wo_ref, gate_ref, n2_ref, sh_ref, sc_ref,
                  xo_ref, h_ref, hp_ref, *, lat_tiles):
    def mm(a, w_ref):
        return jnp.dot(a.astype(BF16), w_ref[...], preferred_element_type=F32)

    is_lat = pl.program_id(0) < lat_tiles
    attn_o = mm(jnp.where(is_lat, attn_l_ref[...], attn_c_ref[...]), wa_ref)
    rec_o = mm(rec_ref[...] * jax.nn.gelu(lg_ref[...], approximate=True), wl_ref)
    four_o = mm(jnp.where(is_lat, four_l_ref[...], four_c_ref[...]), wf_ref)
    mixed = (jax.nn.sigmoid(g0_ref[...]) * attn_o + jax.nn.sigmoid(g1_ref[...]) * rec_o
             + jax.nn.sigmoid(g2_ref[...]) * four_o)
    x = x_ref[...] + gate_ref[...] * mm(mixed, wo_ref)
    xo_ref[...] = x
    y = x * lax.rsqrt(jnp.mean(x * x, axis=-1, keepdims=True) + EPS)
    h = (y * n2_ref[...]) * (1.0 + sc_ref[...]) + sh_ref[...]
    h16 = h.astype(BF16)
    h_ref[...] = h16
    half = D_MODEL // 2
    bits = pltpu.bitcast(h16.astype(F32), jnp.uint32)
    hp_ref[...] = (bits[:, :half] & jnp.uint32(0xFFFF0000)) | (bits[:, half:] >> 16)


def _merge(x_tok, attn_l, attn_c, rec, four_l, four_c, u, w_attn_o, w_lru_o, w_four_o, w_out, mod4, norm2_g,
           row_of_tile, n_rows):
    tm = TM
    d = D_MODEL
    lat_tiles = attn_l.shape[0] // tm
    ctx_tiles = attn_c.shape[0] // tm
    row = pl.BlockSpec((tm, d), lambda i: (i, 0))
    lat_row = pl.BlockSpec((tm, d), lambda i: (jnp.minimum(i, lat_tiles - 1), 0))
    ctx_row = pl.BlockSpec((tm, d), lambda i: (jnp.clip(i - lat_tiles, 0, ctx_tiles - 1), 0))
    wspec = pl.BlockSpec((d, d), lambda i: (0, 0))

    def ucol(cb):
        return pl.BlockSpec((tm, d), lambda i: (i, cb))

    return pl.pallas_call(
        functools.partial(_merge_kernel, lat_tiles=lat_tiles),
        grid=(n_rows // tm,),
        in_specs=[row, lat_row, ctx_row, row, lat_row, ctx_row, ucol(CB_LG), ucol(CB_G), ucol(CB_G + 1),
                  ucol(CB_G + 2), wspec, wspec, wspec, wspec,
                  _mod_spec(2, row_of_tile), pl.BlockSpec((1, d), lambda i: (0, 0)),
                  _mod_spec(3, row_of_tile), _mod_spec(4, row_of_tile)],
        out_specs=[row, row, pl.BlockSpec((tm, d // 2), lambda i: (i, 0))],
        out_shape=[jax.ShapeDtypeStruct((n_rows, d), F32), jax.ShapeDtypeStruct((n_rows, d), BF16),
                   jax.ShapeDtypeStruct((n_rows, d // 2), jnp.uint32)],
        compiler_params=_cparams(("parallel",), 56),
        name="merge_branches",
    )(x_tok, attn_l, attn_c, rec, four_l, four_c, u, u, u, u, w_attn_o, w_lru_o, w_four_o, w_out,
      mod4, norm2_g.reshape(1, d), mod4, mod4)


def _route_kernel(h_ref, rw_ref, rb_ref, tri_ref, idx_ref, wgt_ref, pos_ref, cnt_ref, base_scr):
    tm = h_ref.shape[0]

    @pl.when(pl.program_id(0) == 0)
    def _():
        base_scr[...] = jnp.zeros_like(base_scr)

    logits = lax.dot_general(rw_ref[...], h_ref[...], (((1,), (1,)), ((), ())), preferred_element_type=F32)
    scores = jax.nn.sigmoid(logits)
    biased = scores + rb_ref[...]
    neg = -jnp.inf
    e_iota = lax.broadcasted_iota(jnp.int32, (N_EXPERTS, tm), 0)
    g_iota = lax.broadcasted_iota(jnp.int32, (N_GROUPS, tm), 0)

    rows = []
    for g in range(N_GROUPS):
        xg = biased[g * GROUP_SIZE:(g + 1) * GROUP_SIZE]
        m1 = jnp.max(xg, axis=0, keepdims=True)
        is_max = xg == m1
        n_max = jnp.sum(is_max.astype(F32), axis=0, keepdims=True)
        m2 = jnp.max(jnp.where(is_max, neg, xg), axis=0, keepdims=True)
        rows.append(m1 + jnp.where(n_max >= 2.0, m1, m2))
    grp = jnp.concatenate(rows, axis=0)

    g_sel = jnp.zeros((N_GROUPS, tm), jnp.bool_)
    work = grp
    for _ in range(TOPK_GROUPS):
        m = jnp.max(work, axis=0, keepdims=True)
        first = jnp.min(jnp.where(work == m, g_iota, N_GROUPS), axis=0, keepdims=True)
        hit = g_iota == first
        g_sel = jnp.logical_or(g_sel, hit)
        work = jnp.where(hit, neg, work)
    g_sel_f = g_sel.astype(F32)
    mask_rows = [jnp.broadcast_to(g_sel_f[g:g + 1], (GROUP_SIZE, tm)) for g in range(N_GROUPS)]
    e_mask = jnp.concatenate(mask_rows, axis=0) > 0.5

    work = jnp.where(e_mask, biased, neg)
    sel = jnp.zeros((N_EXPERTS, tm), F32)
    idx_rows, w_rows, hits = [], [], []
    for _ in range(TOP_K):
        m = jnp.max(work, axis=0, keepdims=True)
        first = jnp.min(jnp.where(work == m, e_iota, N_EXPERTS), axis=0, keepdims=True)
        hit = e_iota == first
        hits.append(hit)
        idx_rows.append(first)
        w_rows.append(jnp.sum(jnp.where(hit, scores, 0.0), axis=0, keepdims=True))
        sel = jnp.where(hit, 1.0, sel)
        work = jnp.where(hit, neg, work)
    w = jnp.concatenate(w_rows, axis=0)
    wgt_ref[...] = w / jnp.sum(w, axis=0, keepdims=True) * ROUTED_SCALE
    idx_ref[...] = jnp.concatenate(idx_rows, axis=0)

    before = jnp.dot(sel.astype(BF16), tri_ref[...], preferred_element_type=F32)
    rank = base_scr[...] + before
    pos_rows = [jnp.sum(jnp.where(hit, rank, 0.0), axis=0, keepdims=True) for hit in hits]
    pos_ref[...] = jnp.concatenate(pos_rows, axis=0).astype(jnp.int32)
    base_scr[...] = base_scr[...] + jnp.sum(sel, axis=1, keepdims=True)
    cnt_ref[...] = jnp.broadcast_to(base_scr[...], cnt_ref.shape)


def _route(h16, rw_t, router_b, tm):
    n_tok, d = h16.shape
    tri = (jnp.arange(tm)[:, None] < jnp.arange(tm)[None, :]).astype(BF16)
    out_col = pl.BlockSpec((TOP_K, tm), lambda i: (0, i))
    return pl.pallas_call(
        _route_kernel,
        grid=(n_tok // tm,),
        in_specs=[pl.BlockSpec((tm, d), lambda i: (i, 0)),
                  pl.BlockSpec((N_EXPERTS, d), lambda i: (0, 0)),
                  pl.BlockSpec((N_EXPERTS, 1), lambda i: (0, 0)),
                  pl.BlockSpec((tm, tm), lambda i: (0, 0))],
        out_specs=[out_col, out_col, out_col, pl.BlockSpec((N_EXPERTS, LANES), lambda i: (0, 0))],
        out_shape=[jax.ShapeDtypeStruct((TOP_K, n_tok), jnp.int32), jax.ShapeDtypeStruct((TOP_K, n_tok), F32),
                   jax.ShapeDtypeStruct((TOP_K, n_tok), jnp.int32),
                   jax.ShapeDtypeStruct((N_EXPERTS, LANES), F32)],
        scratch_shapes=[pltpu.VMEM((N_EXPERTS, 1), F32)],
        compiler_params=_cparams(("arbitrary",), 32),
        name="moe_router",
    )(h16, rw_t, router_b.reshape(N_EXPERTS, 1), tri)


def _slot_kernel(idx_ref, pos_ref, offs_ref, dest_ref):
    tm = idx_ref.shape[1]
    e_iota = lax.broadcasted_iota(jnp.int32, (N_EXPERTS, tm), 0)
    idx = idx_ref[...]
    offs = offs_ref[...]
    rows = [jnp.sum(jnp.where(e_iota == idx[r:r + 1], offs, 0), axis=0, keepdims=True) for r in range(TOP_K)]
    dest_ref[...] = jnp.concatenate(rows, axis=0) + pos_ref[...]


def _slots(top_idx, pos, offs, tm):
    n_tok = top_idx.shape[1]
    col = pl.BlockSpec((TOP_K, tm), lambda i: (0, i))
    return pl.pallas_call(
        _slot_kernel,
        grid=(n_tok // tm,),
        in_specs=[col, col, pl.BlockSpec((N_EXPERTS, 1), lambda i: (0, 0))],
        out_specs=col,
        out_shape=jax.ShapeDtypeStruct((TOP_K, n_tok), jnp.int32),
        compiler_params=_cparams(("parallel",), 32),
        name="moe_slots",
    )(top_idx, pos, offs.reshape(N_EXPERTS, 1))


def _row_copy(src_ref, src_row, dst_ref, dst_row, sem):
    return pltpu.make_async_copy(src_ref.at[pl.ds(src_row, 1)], dst_ref.at[pl.ds(dst_row, 1)], sem)


def _dispatch_kernel(dest_ref, hp_ref, _xs_in, xs_ref, sem):
    tm = hp_ref.shape[0]

    def start(t, c):
        for r in range(TOP_K):
            _row_copy(hp_ref, t, xs_ref, dest_ref[r, t], sem).start()
        return c

    lax.fori_loop(0, tm, start, 0)

    def wait(t, c):
        for r in range(TOP_K):
            _row_copy(hp_ref, t, xs_ref, dest_ref[r, t], sem).wait()
        return c

    lax.fori_loop(0, tm, wait, 0)


def _dispatch(hp, dest3, n_slots):
    n_tok, half = hp.shape
    tm = dest3.shape[2]
    xs0 = jnp.zeros((n_slots, half), jnp.uint32)
    return pl.pallas_call(
        _dispatch_kernel,
        grid=(n_tok // tm,),
        in_specs=[pl.BlockSpec((None, TOP_K, tm), lambda i: (i, 0, 0), memory_space=pltpu.SMEM),
                  pl.BlockSpec((tm, half), lambda i: (i, 0)),
                  pl.BlockSpec(memory_space=pl.ANY)],
        out_specs=pl.BlockSpec(memory_space=pl.ANY),
        out_shape=jax.ShapeDtypeStruct((n_slots, half), jnp.uint32),
        scratch_shapes=[pltpu.SemaphoreType.DMA(())],
        input_output_aliases={2: 0},
        compiler_params=_cparams(("arbitrary",), 32),
        name="moe_dispatch",
    )(dest3, hp, xs0)


def _expert_kernel(be_ref, nu_ref, xs_ref, wg_ref, wu_ref, wd_ref, ys_ref, wg_s, wu_s, wd_s):
    i = pl.program_id(0)
    used = i < nu_ref[0]
    prev_e = be_ref[jnp.maximum(i - 1, 0)]
    fresh = jnp.logical_or(i == 0, be_ref[i] != prev_e)

    @pl.when(jnp.logical_and(used, fresh))
    def _():
        wg_s[...] = wg_ref[0].astype(BF16)
        wu_s[...] = wu_ref[0].astype(BF16)
        wd_s[...] = wd_ref[0].astype(BF16)

    @pl.when(used)
    def _():
        w = xs_ref[...]
        half = D_MODEL // 2
        xa = pltpu.bitcast(w & jnp.uint32(0xFFFF0000), F32).astype(BF16)
        xb = pltpu.bitcast(w << 16, F32).astype(BF16)

        def proj(w_s):
            return (jnp.dot(xa, w_s[:half, :], preferred_element_type=F32)
                    + jnp.dot(xb, w_s[half:, :], preferred_element_type=F32))

        g = proj(wg_s)
        act = (_silu(g) * proj(wu_s)).astype(BF16)
        ys_ref[...] = jnp.dot(act, wd_s[...], preferred_element_type=F32)

    @pl.when(jnp.logical_not(used))
    def _():
        ys_ref[...] = jnp.zeros_like(ys_ref)


def _experts(xs, block_e, n_used, w_gate, w_up, w_down, layer, bm):
    n_slots, half = xs.shape
    d = D_MODEL
    grid_spec = pltpu.PrefetchScalarGridSpec(
        num_scalar_prefetch=2,
        grid=(n_slots // bm,),
        in_specs=[pl.BlockSpec((bm, half), lambda i, be, nu: (i, 0)),
                  pl.BlockSpec((None, 1, d, EXPERT_FF), lambda i, be, nu: (layer, be[i], 0, 0)),
                  pl.BlockSpec((None, 1, d, EXPERT_FF), lambda i, be, nu: (layer, be[i], 0, 0)),
                  pl.BlockSpec((None, 1, EXPERT_FF, d), lambda i, be, nu: (layer, be[i], 0, 0))],
        out_specs=pl.BlockSpec((bm, d), lambda i, be, nu: (i, 0)),
        scratch_shapes=[pltpu.VMEM((d, EXPERT_FF), BF16), pltpu.VMEM((d, EXPERT_FF), BF16),
                        pltpu.VMEM((EXPERT_FF, d), BF16)],
    )
    return pl.pallas_call(
        _expert_kernel,
        grid_spec=grid_spec,
        out_shape=jax.ShapeDtypeStruct((n_slots, d), F32),
        compiler_params=_cparams(("arbitrary",), 40),
        name="moe_experts",
    )(block_e, n_used, xs, w_gate, w_up, w_down)


def _combine_kernel(dest_ref, x_ref, h_ref, tw_ref, gate_ref, wsg_ref, wsu_ref, wsd_ref, ys_ref, o_ref, buf, sem):
    tm = x_ref.shape[0]

    def start(t, c):
        for r in range(TOP_K):
            _row_copy(ys_ref, dest_ref[r, t], buf.at[r], t, sem).start()
        return c

    lax.fori_loop(0, tm, start, 0)

    h = h_ref[...]
    act = (_silu(jnp.dot(h, wsg_ref[...], preferred_element_type=F32))
           * jnp.dot(h, wsu_ref[...], preferred_element_type=F32)).astype(BF16)
    shared = jnp.dot(act, wsd_ref[...], preferred_element_type=F32)

    def wait(t, c):
        for r in range(TOP_K):
            _row_copy(ys_ref, dest_ref[r, t], buf.at[r], t, sem).wait()
        return c

    lax.fori_loop(0, tm, wait, 0)

    tw = tw_ref[...]
    moe = buf[0] * tw[:, 0:1]
    for r in range(1, TOP_K):
        moe = moe + buf[r] * tw[:, r:r + 1]
    o_ref[...] = x_ref[...] + gate_ref[...] * (moe + shared)


def _combine(x_tok, h16, ys, dest3, top_w, mod4, ws_gate, ws_up, ws_down, row_of_tile):
    n_tok, d = x_tok.shape
    tm = dest3.shape[2]
    ff = ws_gate.shape[1]
    row = pl.BlockSpec((tm, d), lambda i: (i, 0))
    return pl.pallas_call(
        _combine_kernel,
        grid=(n_tok // tm,),
        in_specs=[pl.BlockSpec((None, TOP_K, tm), lambda i: (i, 0, 0), memory_space=pltpu.SMEM),
                  row, row,
                  pl.BlockSpec((tm, TOP_K), lambda i: (i, 0)),
                  _mod_spec(5, row_of_tile),
                  pl.BlockSpec((d, ff), lambda i: (0, 0)),
                  pl.BlockSpec((d, ff), lambda i: (0, 0)),
                  pl.BlockSpec((ff, d), lambda i: (0, 0)),
                  pl.BlockSpec(memory_space=pl.ANY)],
        out_specs=row,
        out_shape=jax.ShapeDtypeStruct((n_tok, d), F32),
        scratch_shapes=[pltpu.VMEM((TOP_K, tm, d), F32), pltpu.SemaphoreType.DMA(())],
        compiler_params=_cparams(("arbitrary",), 48),
        name="moe_combine",
    )(dest3, x_tok, h16, top_w, mod4, ws_gate, ws_up, ws_down, ys)


def _moe(x_tok, h16, hp, mod4, row_of_tile, rw_t, router_b, w_gate, w_up, w_down, layer, ws_gate, ws_up, ws_down):
    n_tok = x_tok.shape[0]
    bm = MOE_BM
    tm_r = TM_ROUTE if n_tok % TM_ROUTE == 0 else TM
    top_idx, top_w, pos, cnt = _route(h16, rw_t, router_b, tm_r)

    counts = cnt[:, 0].astype(jnp.int32)
    padded = (counts + bm - 1) // bm * bm
    ends = jnp.cumsum(padded)
    n_blocks = -(-(n_tok * TOP_K + N_EXPERTS * (bm - 1)) // bm)
    dest = _slots(top_idx, pos, ends - padded, tm_r)
    block_e = jnp.minimum(jnp.searchsorted(ends, jnp.arange(n_blocks, dtype=jnp.int32) * bm, side='right'),
                          N_EXPERTS - 1).astype(jnp.int32)
    n_used = (ends[-1] // bm).astype(jnp.int32).reshape(1)
    tm = TM
    dest3 = dest.reshape(TOP_K, n_tok // tm, tm).transpose(1, 0, 2)

    xs = _dispatch(hp, dest3, n_blocks * bm)
    ys = _experts(xs, block_e, n_used, w_gate, w_up, w_down, layer, bm)
    return _combine(x_tok, h16, ys, dest3, top_w.T, mod4, ws_gate, ws_up, ws_down, row_of_tile)


def _block_diag_tiles(w):
    per = MXU_DIM // LRU_BLOCK_DIM
    w = w.reshape(2, D_MODEL // MXU_DIM, per, LRU_BLOCK_DIM, LRU_BLOCK_DIM)
    eye = jnp.eye(per, dtype=w.dtype)
    t = jnp.einsum('ztpde,pq->ztpdqe', w, eye)
    return t.reshape(2, D_MODEL // MXU_DIM, MXU_DIM, MXU_DIM).astype(BF16)


def _rope_tables(n_lat, n_ctx):
    rows = n_lat // GRID_W
    row = jnp.repeat(jnp.arange(rows), GRID_W)
    col = jnp.tile(jnp.arange(GRID_W), rows)
    inv = ROPE_THETA ** (-jnp.arange(ROPE_FREQS, dtype=F32) / ROPE_FREQS)
    ang = jnp.stack([row[:, None] * inv, col[:, None] * inv], axis=1)
    ang = jnp.concatenate([ang, ang], axis=2).reshape(n_lat, HEAD_DIM)
    ang = jnp.tile(ang, (1, LANES // HEAD_DIM))
    ang = jnp.concatenate([ang, jnp.zeros((n_ctx, LANES), F32)], axis=0)
    lane = jnp.arange(LANES)
    sign = jnp.where((lane % (2 * ROPE_FREQS)) < ROPE_FREQS, -1.0, 1.0).astype(F32)
    return jnp.cos(ang), jnp.sin(ang) * sign


def kernel(x, c, ctx, c_ctx, w_mod, b_mod, norm1_g, w_in, q_norm_g, k_norm_g, lambda_qk, subln_g, w_attn_o, conv_w, conv_b, lru_wa, lru_ba, lru_wi, lru_bi, lru_lambda, w_lru_o, w_four_o, w_out, norm2_g, router_w, router_b, exp_w_gate, exp_w_up, exp_w_down, sh_w_gate, sh_w_up, sh_w_down):
    bsz, n_lat, d = x.shape
    n_ctx = ctx.shape[1]
    depth = w_mod.shape[0]
    n_latent_rows = bsz * n_lat
    n_tok = n_latent_rows + bsz * n_ctx
    assert d == D_MODEL and bsz + 1 <= MOD_ROWS
    assert n_lat % TM == 0 and n_ctx % TM == 0 and n_lat % GRID_W == 0
    tm_in = TM_IN if (n_lat % TM_IN == 0 and (bsz * n_ctx) % TM_IN == 0) else TM

    cvec = jnp.zeros((MOD_ROWS, d), F32).at[:bsz].set(c).at[bsz].set(c_ctx)
    mods = _modulation(cvec, w_mod, b_mod)

    def row_of_tile_fn(tm):
        per_batch = n_lat // tm
        return lambda i: jnp.minimum(i // per_batch, bsz)

    assert n_lat % n_ctx == 0
    cos_t, sin_t = _rope_tables(n_lat, n_ctx)
    group_mat = jnp.kron(jnp.eye(d // HEAD_DIM, dtype=F32), jnp.ones((HEAD_DIM, HEAD_DIM), F32)).astype(BF16)
    m = jnp.arange(FOUR_GROUP_DIM, dtype=jnp.int32)
    ang_c = ((m[:, None] * m[None, :]) % FOUR_GROUP_DIM).astype(F32) * (2.0 * math.pi / FOUR_GROUP_DIM)
    inv_sqrt_c = FOUR_GROUP_DIM ** -0.5
    cs_mat = jnp.concatenate([jnp.cos(ang_c) * inv_sqrt_c, jnp.sin(ang_c) * inv_sqrt_c], axis=1).astype(BF16)
    ct_lat, st_lat = _dft_tables(n_lat)
    ct_ctx, st_ctx = _dft_tables(n_ctx)
    tmk_lat = 1024 if n_lat % 1024 == 0 else TM
    tq = TQ if n_lat % TQ == 0 else TM
    n_keys = n_ctx + n_lat
    tk = TK if n_keys % TK == 0 else TM

    x_tok = jnp.concatenate([x.reshape(n_latent_rows, d), ctx.reshape(bsz * n_ctx, d)], axis=0)
    for l in range(depth):
        last = l == depth - 1
        lam_init = 0.8 - 0.6 * math.exp(-0.3 * l)
        mod4 = mods[l].reshape(MOD_ROWS, 6, 1, d)

        u = _in_projection(x_tok, mod4, norm1_g[l], w_in[l].astype(BF16), row_of_tile_fn(tm_in), tm_in)

        k_all, q_all, vt_all = _qkv_prepare(u, k_norm_g[l], q_norm_g[l], group_mat, cos_t, sin_t,
                                            bsz, n_lat, n_ctx)
        attn_l = _diff_attention(q_all, k_all, vt_all, lambda_qk[l], subln_g[l], lam_init,
                                 0, n_lat, 0, n_keys, tq, tk)
        attn_c = attn_l
        if not last:
            attn_c = _diff_attention(q_all, k_all, vt_all, lambda_qk[l], subln_g[l], lam_init,
                                     n_lat, n_ctx, n_lat, n_ctx, TM, TM)

        rec = _rglru(u, conv_w[l], conv_b[l], _block_diag_tiles(lru_wa[l]), lru_ba[l],
                     _block_diag_tiles(lru_wi[l]), lru_bi[l], lru_lambda[l], bsz, n_lat, n_ctx)

        y_four = _chan_dft(u, cs_mat)
        four_l = _time_dft(y_four, ct_lat, st_lat, bsz, n_lat, 0, tmk_lat)
        four_c = four_l
        if not last:
            four_c = _time_dft(y_four, ct_ctx, st_ctx, bsz, n_ctx, n_latent_rows, n_ctx)

        n_rows = n_latent_rows if last else n_tok
        row_of_tile = row_of_tile_fn(TM)
        x_mid, h16, hp = _merge(x_tok, attn_l, attn_c, rec, four_l, four_c, u, w_attn_o[l].astype(BF16),
                                w_lru_o[l].astype(BF16), w_four_o[l].astype(BF16), w_out[l].astype(BF16),
                                mod4, norm2_g[l], row_of_tile, n_rows)
        x_tok = _moe(x_mid, h16, hp, mod4, row_of_tile, router_w[l].T.astype(BF16), router_b[l],
                     exp_w_gate, exp_w_up, exp_w_down, l,
                     sh_w_gate[l].astype(BF16), sh_w_up[l].astype(BF16), sh_w_down[l].astype(BF16))
    return x_tok[:n_latent_rows].reshape(bsz, n_lat, d)
```

```python
import functools
import math

import jax
import jax.numpy as jnp
from jax import lax
from jax.experimental import pallas as pl
from jax.experimental.pallas import tpu as pltpu

F32 = jnp.float32
BF16 = jnp.bfloat16

D_MODEL = 1024
EPS = 1e-6
GRID_W = 64

N_HEADS = 8
HEAD_DIM = 64
V_DIM = 2 * HEAD_DIM
ATTN_SCALE = HEAD_DIM ** -0.5
ROPE_THETA = 10000.0
ROPE_FREQS = HEAD_DIM // 4

LRU_BLOCKS = 16
LRU_BLOCK_DIM = D_MODEL // LRU_BLOCKS
LRU_C = 8.0
CONV_W = 4

FOUR_GROUPS = 4
FOUR_GROUP_DIM = D_MODEL // FOUR_GROUPS

CB_K, CB_V, CB_LX, CB_Q, CB_LG, CB_F, CB_G = 0, 1, 2, 3, 4, 5, 6
IN_BLOCKS = 9

N_EXPERTS = 256
TOP_K = 8
N_GROUPS = 8
GROUP_SIZE = N_EXPERTS // N_GROUPS
TOPK_GROUPS = 4
EXPERT_FF = 256
ROUTED_SCALE = 2.5

LANES = 128
SUBLANES = 8
MXU_DIM = 256
VMEM_BYTES_V7X = 64 * 1024 * 1024

MOD_ROWS = 8
TM = 256
TM_IN = 512
TQ = 256
Q_SUB = 32
TK = 768
TM_ROUTE = 512
MOE_BM = 256
LOG2E = 1.4426950408889634


def _cparams(sem, vmem_mb):
    return pltpu.CompilerParams(dimension_semantics=sem, vmem_limit_bytes=int(vmem_mb * 1024 * 1024))


def _silu(x):
    return x * jax.nn.sigmoid(x)


def _mod_kernel(c_ref, w_ref, b_ref, o_ref):
    c = c_ref[...]
    s = _silu(c).astype(BF16)
    o_ref[0] = jnp.dot(s, w_ref[0].astype(BF16), preferred_element_type=F32) + b_ref[0]


def _modulation(cvec, w_mod, b_mod):
    n_layers, d, six_d = w_mod.shape
    nj = six_d // d
    return pl.pallas_call(
        _mod_kernel,
        grid=(n_layers, nj),
        in_specs=[
            pl.BlockSpec((MOD_ROWS, d), lambda l, j: (0, 0)),
            pl.BlockSpec((1, d, d), lambda l, j: (l, 0, j)),
            pl.BlockSpec((1, 1, d), lambda l, j: (l, 0, j)),
        ],
        out_specs=pl.BlockSpec((1, MOD_ROWS, d), lambda l, j: (l, 0, j)),
        out_shape=jax.ShapeDtypeStruct((n_layers, MOD_ROWS, six_d), F32),
        compiler_params=_cparams(("parallel", "parallel"), 32),
        name="modulation",
    )(cvec, w_mod, b_mod.reshape(n_layers, 1, six_d))


def _mod_spec(part, row_of_tile, n_grid_axes=1):
    if n_grid_axes == 1:
        return pl.BlockSpec((None, None, 1, D_MODEL), lambda i: (row_of_tile(i), part, 0, 0))
    return pl.BlockSpec((None, None, 1, D_MODEL), lambda i, j: (row_of_tile(i), part, 0, 0))


def _inproj_kernel(x_ref, g_ref, sh_ref, sc_ref, w_ref, o_ref, h_scr):
    @pl.when(pl.program_id(1) == 0)
    def _():
        x = x_ref[...]
        y = x * lax.rsqrt(jnp.mean(x * x, axis=-1, keepdims=True) + EPS)
        h = (y * g_ref[...]) * (1.0 + sc_ref[...]) + sh_ref[...]
        h_scr[...] = h.astype(BF16)

    o_ref[...] = jnp.dot(h_scr[...], w_ref[...], preferred_element_type=F32)


def _in_projection(x_tok, mod4, norm_g, w_in_bf16, row_of_tile, tm):
    n_tok, d = x_tok.shape
    n_cols = w_in_bf16.shape[1]
    return pl.pallas_call(
        _inproj_kernel,
        grid=(n_tok // tm, n_cols // d),
        in_specs=[
            pl.BlockSpec((tm, d), lambda i, j: (i, 0)),
            pl.BlockSpec((1, d), lambda i, j: (0, 0)),
            _mod_spec(0, row_of_tile, 2),
            _mod_spec(1, row_of_tile, 2),
            pl.BlockSpec((d, d), lambda i, j: (0, j)),
        ],
        out_specs=pl.BlockSpec((tm, d), lambda i, j: (i, j)),
        out_shape=jax.ShapeDtypeStruct((n_tok, n_cols), F32),
        scratch_shapes=[pltpu.VMEM((tm, d), BF16)],
        compiler_params=_cparams(("parallel", "arbitrary"), 40),
        name="in_projection",
    )(x_tok, norm_g.reshape(1, d), mod4, mod4, w_in_bf16)


def _prep_kernel(uk_ref, uq_ref, uv_ref, kg_ref, qg_ref, gm_ref, cos_ref, sin_ref, k_out, q_out, vt_out, *, tm):
    gm = gm_ref[...]
    lane = lax.broadcasted_iota(jnp.int32, (tm, LANES), 1)
    first_half = (lane % (2 * ROPE_FREQS)) < ROPE_FREQS

    def norm_rope(x, g):
        ss = jnp.dot((x * x).astype(BF16), gm, preferred_element_type=F32) * (1.0 / HEAD_DIM)
        y = x * lax.rsqrt(ss + EPS) * g
        c = cos_ref[...]
        s = sin_ref[...]
        outs = []
        for j in range(D_MODEL // LANES):
            yt = y[:, j * LANES:(j + 1) * LANES]
            rot = jnp.where(first_half, pltpu.roll(yt, LANES - ROPE_FREQS, 1), pltpu.roll(yt, ROPE_FREQS, 1))
            outs.append(yt * c + rot * s)
        return jnp.concatenate(outs, axis=1)

    k_out[...] = norm_rope(uk_ref[...], kg_ref[...]).astype(BF16)
    qn = norm_rope(uq_ref[...], qg_ref[...]) * (ATTN_SCALE * LOG2E)
    lo = lane < HEAD_DIM
    for h in range(N_HEADS):
        qt = qn[:, h * LANES:(h + 1) * LANES]
        q_out[:, (2 * h) * LANES:(2 * h + 1) * LANES] = jnp.where(lo, qt, 0.0).astype(BF16)
        q_out[:, (2 * h + 1) * LANES:(2 * h + 2) * LANES] = jnp.where(lo, 0.0, qt).astype(BF16)
    vt_out[...] = uv_ref[...].T.astype(BF16)


def _qkv_prepare(u, k_gain, q_gain, group_mat, cos_t, sin_t, bsz, n_lat, n_ctx):
    tm = TM
    d = D_MODEL
    n_s, n_c = n_lat // tm, n_ctx // tm
    n_rows = n_lat + n_ctx
    lat_tiles = bsz * n_s
    kg = jnp.tile(k_gain, d // HEAD_DIM).reshape(1, d)
    qg = jnp.tile(q_gain, d // HEAD_DIM).reshape(1, d)

    def tok_tile(b, j):
        return jnp.where(j < n_s, b * n_s + j, lat_tiles + b * n_c + (j - n_s))

    def u_spec(cb):
        return pl.BlockSpec((tm, d), lambda b, j: (tok_tile(b, j), cb))

    vec = pl.BlockSpec((1, d), lambda b, j: (0, 0))
    rope_spec = pl.BlockSpec((tm, LANES), lambda b, j: (j, 0))
    return pl.pallas_call(
        functools.partial(_prep_kernel, tm=tm),
        grid=(bsz, n_s + n_c),
        in_specs=[u_spec(CB_K), u_spec(CB_Q), u_spec(CB_V), vec, vec,
                  pl.BlockSpec((d, d), lambda b, j: (0, 0)), rope_spec, rope_spec],
        out_specs=[
            pl.BlockSpec((None, tm, d), lambda b, j: (b, j, 0)),
            pl.BlockSpec((None, tm, 2 * d), lambda b, j: (b, j, 0)),
            pl.BlockSpec((None, d, tm), lambda b, j: (b, 0, j)),
        ],
        out_shape=[jax.ShapeDtypeStruct((bsz, n_rows, d), BF16),
                   jax.ShapeDtypeStruct((bsz, n_rows, 2 * d), BF16),
                   jax.ShapeDtypeStruct((bsz, d, n_rows), BF16)],
        compiler_params=_cparams(("parallel", "parallel"), 40),
        name="qkv_prepare",
    )(u, u, u, kg, qg, group_mat, cos_t, sin_t)


def _attn_kernel(lq_ref, sg_ref, q1_ref, q2_ref, k_ref, vt_ref, o_ref, s_buf, *, n_keys, tk, tq, lam_init):
    lq = lq_ref[...]
    lam = (jnp.exp(jnp.sum(lq[0:1] * lq[1:2], axis=1, keepdims=True))
           - jnp.exp(jnp.sum(lq[2:3] * lq[3:4], axis=1, keepdims=True)) + lam_init)
    n_chunks = n_keys // tk
    n_sub = q1_ref.shape[0] // tq

    def load_q(j):
        rows = pl.ds(pl.multiple_of(j * tq, tq), tq)
        return jnp.concatenate([q1_ref[rows, :], q2_ref[rows, :]], axis=0)

    def score_chunk(q, c, mx):
        s = lax.dot_general(k_ref[c * tk:(c + 1) * tk, :], q, (((1,), (1,)), ((), ())),
                            preferred_element_type=F32)
        s_buf[c * tk:(c + 1) * tk, :] = s
        cm = jnp.max(s, axis=0, keepdims=True)
        return cm if mx is None else jnp.maximum(mx, cm)

    q0 = load_q(0)
    mx0 = None
    for c in range(n_chunks):
        mx0 = score_chunk(q0, c, mx0)

    def sub_tile(j, mx):
        q_next = load_q(jnp.minimum(j + 1, n_sub - 1))
        mx_next = None
        l = jnp.zeros((1, 2 * tq), F32)
        acc = jnp.zeros((V_DIM, 2 * tq), F32)
        for c in range(n_chunks):
            p = jnp.exp2(s_buf[c * tk:(c + 1) * tk, :] - mx)
            l = l + jnp.sum(p, axis=0, keepdims=True)
            acc = acc + jnp.dot(vt_ref[:, c * tk:(c + 1) * tk], p.astype(BF16), preferred_element_type=F32)
            mx_next = score_chunk(q_next, c, mx_next)
        o = acc / l
        o = o[:, :tq] - lam * o[:, tq:]
        ms = jnp.mean(o * o, axis=0, keepdims=True)
        on = (o * lax.rsqrt(ms + EPS) * sg_ref[...]) * (1.0 - lam_init)
        o_ref[pl.ds(pl.multiple_of(j * tq, tq), tq), :] = on.T
        return mx_next

    lax.fori_loop(0, n_sub, sub_tile, mx0)


def _diff_attention(q_all, k_all, vt_all, lambda_qk, subln_g, lam_init, q_row0, n_q, key_row0, n_keys, tq, n_sub, tk):
    bsz = q_all.shape[0]
    d = D_MODEL
    tstep = tq * n_sub
    nq = n_q // tstep
    qb0 = q_row0 // tstep
    kb0 = key_row0 // n_keys
    s_bytes = n_keys * 2 * tq * 4
    return pl.pallas_call(
        functools.partial(_attn_kernel, n_keys=n_keys, tk=tk, tq=tq, lam_init=lam_init),
        grid=(bsz, N_HEADS, nq),
        in_specs=[
            pl.BlockSpec((4, HEAD_DIM), lambda b, h, i: (0, 0)),
            pl.BlockSpec((V_DIM, 1), lambda b, h, i: (0, 0)),
            pl.BlockSpec((None, tstep, LANES), lambda b, h, i: (b, qb0 + i, 2 * h)),
            pl.BlockSpec((None, tstep, LANES), lambda b, h, i: (b, qb0 + i, 2 * h + 1)),
            pl.BlockSpec((None, n_keys, LANES), lambda b, h, i: (b, kb0, h)),
            pl.BlockSpec((None, V_DIM, n_keys), lambda b, h, i: (b, h, kb0)),
        ],
        out_specs=pl.BlockSpec((tstep, LANES), lambda b, h, i: (b * nq + i, h)),
        out_shape=jax.ShapeDtypeStruct((bsz * n_q, d), F32),
        scratch_shapes=[pltpu.VMEM((n_keys, 2 * tq), F32)],
        compiler_params=_cparams(("parallel", "parallel", "arbitrary"), 28 + s_bytes / 2 ** 20),
        name="diff_attention",
    )(lambda_qk, subln_g.reshape(V_DIM, 1), q_all, q_all, k_all, vt_all)


def _lru_kernel(*refs, rev, n_c, n_s, tm):
    if rev:
        (cur_ref, prev_ref, next_ref, cw_ref, cb_ref, wa_ref, ba_ref, wi_ref, bi_ref, lam_ref, fwd_ref,
         o_ref, a_scr, b_scr, h_scr) = refs
    else:
        (cur_ref, prev_ref, next_ref, cw_ref, cb_ref, wa_ref, ba_ref, wi_ref, bi_ref, lam_ref,
         o_ref, a_scr, b_scr, h_scr) = refs
    s = pl.program_id(1)
    is_ctx = s < n_c
    if rev:
        tile = jnp.where(is_ctx, n_c - 1 - s, n_s - 1 - (s - n_c))
    else:
        tile = jnp.where(is_ctx, s, s - n_c)
    n_tile = jnp.where(is_ctx, n_c, n_s)

    @pl.when(s == 0)
    def _():
        h_scr[...] = jnp.zeros_like(h_scr)

    x = cur_ref[...]
    pv = jnp.where(tile == 0, 0.0, prev_ref[...])
    nx = jnp.where(tile == n_tile - 1, 0.0, next_ref[...])
    r8 = lax.broadcasted_iota(jnp.int32, (SUBLANES, D_MODEL), 0)

    def earlier(k):
        rolled = pltpu.roll(x, k, 0)
        top = jnp.where(r8 < k, pltpu.roll(pv, k, 0), rolled[:SUBLANES])
        return jnp.concatenate([top, rolled[SUBLANES:]], axis=0)

    rolled = pltpu.roll(x, tm - 1, 0)
    bot = jnp.where(r8 >= SUBLANES - 1, pltpu.roll(nx, SUBLANES - 1, 0), rolled[tm - SUBLANES:])
    later = jnp.concatenate([rolled[:tm - SUBLANES], bot], axis=0)
    cw = cw_ref[...]
    conv = earlier(2) * cw[0:1] + earlier(1) * cw[1:2] + x * cw[2:3] + later * cw[3:4] + cb_ref[...]

    conv16 = conv.astype(BF16)

    def block_diag(w_ref, b_ref):
        parts = [jnp.dot(conv16[:, j * MXU_DIM:(j + 1) * MXU_DIM], w_ref[j], preferred_element_type=F32)
                 for j in range(D_MODEL // MXU_DIM)]
        return jnp.concatenate(parts, axis=1) + b_ref[...]

    r = jax.nn.sigmoid(block_diag(wa_ref, ba_ref))
    gate_i = jax.nn.sigmoid(block_diag(wi_ref, bi_ref))
    neg_lam = -lam_ref[...]
    softplus = jnp.maximum(neg_lam, 0.0) + jnp.log1p(jnp.exp(-jnp.abs(neg_lam)))
    log_a = -LRU_C * r * softplus
    a = jnp.exp(log_a)
    mult = jnp.sqrt(1.0 - a * a)
    a_scr[...] = a
    b_scr[...] = mult * gate_i * conv

    def body(t, h):
        row = tm - 1 - t if rev else t
        h = a_scr[pl.ds(row, 1), :] * h + b_scr[pl.ds(row, 1), :]
        if rev:
            o_ref[pl.ds(row, 1), :] = fwd_ref[pl.ds(row, 1), :] + h
        else:
            o_ref[pl.ds(row, 1), :] = h
        return h

    h_scr[...] = lax.fori_loop(0, tm, body, h_scr[...], unroll=8)


def _rglru(u, conv_w, conv_b, wa_bd, ba, wi_bd, bi, lam, bsz, n_lat, n_ctx):
    tm = TM
    d = D_MODEL
    n_tok = u.shape[0]
    n_s, n_c = n_lat // tm, n_ctx // tm
    lat_tiles = bsz * n_s
    per8 = tm // SUBLANES
    last8 = n_tok // SUBLANES - 1

    def make(rev, fwd):
        def blk(b, s):
            is_ctx = s < n_c
            if rev:
                tile = jnp.where(is_ctx, n_c - 1 - s, n_s - 1 - (s - n_c))
            else:
                tile = jnp.where(is_ctx, s, s - n_c)
            return jnp.where(is_ctx, lat_tiles + b * n_c + tile, b * n_s + tile)

        vec = pl.BlockSpec((None, 1, d), lambda b, s: (int(rev), 0, 0))
        wspec = pl.BlockSpec((None, d // MXU_DIM, MXU_DIM, MXU_DIM), lambda b, s: (int(rev), 0, 0, 0))
        row_spec = pl.BlockSpec((tm, d), lambda b, s: (blk(b, s), 0))
        in_specs = [
            pl.BlockSpec((tm, d), lambda b, s: (blk(b, s), CB_LX)),
            pl.BlockSpec((SUBLANES, d), lambda b, s: (jnp.maximum(blk(b, s) * per8 - 1, 0), CB_LX)),
            pl.BlockSpec((SUBLANES, d), lambda b, s: (jnp.minimum((blk(b, s) + 1) * per8, last8), CB_LX)),
            pl.BlockSpec((CONV_W, d), lambda b, s: (0, 0)),
            pl.BlockSpec((1, d), lambda b, s: (0, 0)),
            wspec, vec, wspec, vec, vec,
        ]
        args = [u, u, u, conv_w, conv_b.reshape(1, d), wa_bd, ba.reshape(2, 1, d), wi_bd, bi.reshape(2, 1, d),
                lam.reshape(2, 1, d)]
        if rev:
            in_specs.append(row_spec)
            args.append(fwd)
        return pl.pallas_call(
            functools.partial(_lru_kernel, rev=rev, n_c=n_c, n_s=n_s, tm=tm),
            grid=(bsz, n_c + n_s),
            in_specs=in_specs,
            out_specs=row_spec,
            out_shape=jax.ShapeDtypeStruct((n_tok, d), F32),
            scratch_shapes=[pltpu.VMEM((tm, d), F32), pltpu.VMEM((tm, d), F32), pltpu.VMEM((1, d), F32)],
            input_output_aliases={10: 0} if rev else {},
            compiler_params=_cparams(("arbitrary", "arbitrary"), 40),
            name="rglru_bwd" if rev else "rglru_fwd",
        )(*args)

    fwd = make(False, None)
    return make(True, fwd)


def _chan_dft_kernel(u_ref, cs_ref, o_ref):
    x = u_ref[...].astype(BF16)
    cs = cs_ref[...]
    gd = FOUR_GROUP_DIM
    for g in range(FOUR_GROUPS):
        y = jnp.dot(x[:, g * gd:(g + 1) * gd], cs, preferred_element_type=F32)
        o_ref[:, g * gd:(g + 1) * gd] = y[:, :gd].astype(BF16)
        o_ref[:, D_MODEL + g * gd:D_MODEL + (g + 1) * gd] = y[:, gd:].astype(BF16)


def _chan_dft(u, cs_mat):
    n_tok = u.shape[0]
    tm = TM
    d = D_MODEL
    return pl.pallas_call(
        _chan_dft_kernel,
        grid=(n_tok // tm,),
        in_specs=[pl.BlockSpec((tm, d), lambda i: (i, CB_F)),
                  pl.BlockSpec((FOUR_GROUP_DIM, 2 * FOUR_GROUP_DIM), lambda i: (0, 0))],
        out_specs=pl.BlockSpec((tm, 2 * d), lambda i: (i, 0)),
        out_shape=jax.ShapeDtypeStruct((n_tok, 2 * d), BF16),
        compiler_params=_cparams(("parallel",), 32),
        name="fourier_channels",
    )(u, cs_mat)


def _time_dft_kernel(ct_ref, st_ref, yc_ref, ys_ref, o_ref, acc, *, scale):
    kk = pl.program_id(2)

    @pl.when(kk == 0)
    def _():
        acc[...] = jnp.zeros_like(acc)

    acc[...] += (jnp.dot(ct_ref[...], yc_ref[...], preferred_element_type=F32)
                 - jnp.dot(st_ref[...], ys_ref[...], preferred_element_type=F32))

    @pl.when(kk == pl.num_programs(2) - 1)
    def _():
        o_ref[...] = acc[...] * scale


def _time_dft(y, ct, st, bsz, n_pos, row0, tmk):
    d = D_MODEL
    nt = n_pos // tmk
    rb0 = row0 // tmk
    return pl.pallas_call(
        functools.partial(_time_dft_kernel, scale=float(n_pos) ** -0.5),
        grid=(bsz, nt, nt),
        in_specs=[
            pl.BlockSpec((tmk, tmk), lambda b, i, k: (i, k)),
            pl.BlockSpec((tmk, tmk), lambda b, i, k: (i, k)),
            pl.BlockSpec((tmk, d), lambda b, i, k: (rb0 + b * nt + k, 0)),
            pl.BlockSpec((tmk, d), lambda b, i, k: (rb0 + b * nt + k, 1)),
        ],
        out_specs=pl.BlockSpec((tmk, d), lambda b, i, k: (b * nt + i, 0)),
        out_shape=jax.ShapeDtypeStruct((bsz * n_pos, d), F32),
        scratch_shapes=[pltpu.VMEM((tmk, d), F32)],
        compiler_params=_cparams(("parallel", "parallel", "arbitrary"), 48),
        name="fourier_positions",
    )(ct, st, y, y)


def _dft_tables(n):
    k = jnp.arange(n, dtype=jnp.int32)[:, None]
    step = 2.0 * math.pi / n
    if n <= 1024:
        ang = ((k * jnp.arange(n, dtype=jnp.int32)[None, :]) % n).astype(F32) * step
        return jnp.cos(ang).astype(BF16), jnp.sin(ang).astype(BF16)
    hi = jnp.arange(n // LANES, dtype=jnp.int32)[None, :] * LANES
    lo = jnp.arange(LANES, dtype=jnp.int32)[None, :]
    ang_a = ((k * hi) % n).astype(F32) * step
    ang_b = ((k * lo) % n).astype(F32) * step
    ca, sa = jnp.cos(ang_a)[:, :, None], jnp.sin(ang_a)[:, :, None]
    cb, sb = jnp.cos(ang_b)[:, None, :], jnp.sin(ang_b)[:, None, :]
    ct = (ca * cb - sa * sb).reshape(n, n).astype(BF16)
    st = (sa * cb + ca * sb).reshape(n, n).astype(BF16)
    return ct, st


def _merge_kernel(x_ref, attn_l_ref, attn_c_ref, rec_ref, four_l_ref, four_c_ref, lg_ref, g0_ref, g1_ref, g2_ref,
                  wa_ref, wl_ref, wf_ref, wo_ref, gate_ref, n2_ref, sh_ref, sc_ref,
                  xo_ref, h_ref, hp_ref, *, lat_tiles):
    def mm(a, w_ref):
        return jnp.dot(a.astype(BF16), w_ref[...], preferred_element_type=F32)

    is_lat = pl.program_id(0) < lat_tiles
    attn_o = mm(jnp.where(is_lat, attn_l_ref[...], attn_c_ref[...]), wa_ref)
    rec_o = mm(rec_ref[...] * jax.nn.gelu(lg_ref[...], approximate=True), wl_ref)
    four_o = mm(jnp.where(is_lat, four_l_ref[...], four_c_ref[...]), wf_ref)
    mixed = (jax.nn.sigmoid(g0_ref[...]) * attn_o + jax.nn.sigmoid(g1_ref[...]) * rec_o
             + jax.nn.sigmoid(g2_ref[...]) * four_o)
    x = x_ref[...] + gate_ref[...] * mm(mixed, wo_ref)
    xo_ref[...] = x
    y = x * lax.rsqrt(jnp.mean(x * x, axis=-1, keepdims=True) + EPS)
    h = (y * n2_ref[...]) * (1.0 + sc_ref[...]) + sh_ref[...]
    h16 = h.astype(BF16)
    h_ref[...] = h16
    half = D_MODEL // 2
    bits = pltpu.bitcast(h16.astype(F32), jnp.uint32)
    hp_ref[...] = (bits[:, :half] & jnp.uint32(0xFFFF0000)) | (bits[:, half:] >> 16)


def _merge(x_tok, attn_l, attn_c, rec, four_l, four_c, u, w_attn_o, w_lru_o, w_four_o, w_out, mod4, norm2_g,
           row_of_tile, n_rows):
    tm = TM
    d = D_MODEL
    lat_tiles = attn_l.shape[0] // tm
    ctx_tiles = attn_c.shape[0] // tm
    row = pl.BlockSpec((tm, d), lambda i: (i, 0))
    lat_row = pl.BlockSpec((tm, d), lambda i: (jnp.minimum(i, lat_tiles - 1), 0))
    ctx_row = pl.BlockSpec((tm, d), lambda i: (jnp.clip(i - lat_tiles, 0, ctx_tiles - 1), 0))
    wspec = pl.BlockSpec((d, d), lambda i: (0, 0))

    def ucol(cb):
        return pl.BlockSpec((tm, d), lambda i: (i, cb))

    return pl.pallas_call(
        functools.partial(_merge_kernel, lat_tiles=lat_tiles),
        grid=(n_rows // tm,),
        in_specs=[row, lat_row, ctx_row, row, lat_row, ctx_row, ucol(CB_LG), ucol(CB_G), ucol(CB_G + 1),
                  ucol(CB_G + 2), wspec, wspec, wspec, wspec,
                  _mod_spec(2, row_of_tile), pl.BlockSpec((1, d), lambda i: (0, 0)),
                  _mod_spec(3, row_of_tile), _mod_spec(4, row_of_tile)],
        out_specs=[row, row, pl.BlockSpec((tm, d // 2), lambda i: (i, 0))],
        out_shape=[jax.ShapeDtypeStruct((n_rows, d), F32), jax.ShapeDtypeStruct((n_rows, d), BF16),
                   jax.ShapeDtypeStruct((n_rows, d // 2), jnp.uint32)],
        compiler_params=_cparams(("parallel",), 56),
        name="merge_branches",
    )(x_tok, attn_l, attn_c, rec, four_l, four_c, u, u, u, u, w_attn_o, w_lru_o, w_four_o, w_out,
      mod4, norm2_g.reshape(1, d), mod4, mod4)


def _route_kernel(h_ref, rw_ref, rb_ref, tri_ref, idx_ref, wgt_ref, pos_ref, cnt_ref, base_scr):
    tm = h_ref.shape[0]

    @pl.when(pl.program_id(0) == 0)
    def _():
        base_scr[...] = jnp.zeros_like(base_scr)

    logits = lax.dot_general(rw_ref[...], h_ref[...], (((1,), (1,)), ((), ())), preferred_element_type=F32)
    scores = jax.nn.sigmoid(logits)
    biased = scores + rb_ref[...]
    neg = -jnp.inf
    e_iota = lax.broadcasted_iota(jnp.int32, (N_EXPERTS, tm), 0)
    g_iota = lax.broadcasted_iota(jnp.int32, (N_GROUPS, tm), 0)

    rows = []
    for g in range(N_GROUPS):
        xg = biased[g * GROUP_SIZE:(g + 1) * GROUP_SIZE]
        m1 = jnp.max(xg, axis=0, keepdims=True)
        is_max = xg == m1
        n_max = jnp.sum(is_max.astype(F32), axis=0, keepdims=True)
        m2 = jnp.max(jnp.where(is_max, neg, xg), axis=0, keepdims=True)
        rows.append(m1 + jnp.where(n_max >= 2.0, m1, m2))
    grp = jnp.concatenate(rows, axis=0)

    g_sel = jnp.zeros((N_GROUPS, tm), jnp.bool_)
    work = grp
    for _ in range(TOPK_GROUPS):
        m = jnp.max(work, axis=0, keepdims=True)
        first = jnp.min(jnp.where(work == m, g_iota, N_GROUPS), axis=0, keepdims=True)
        hit = g_iota == first
        g_sel = jnp.logical_or(g_sel, hit)
        work = jnp.where(hit, neg, work)
    g_sel_f = g_sel.astype(F32)
    mask_rows = [jnp.broadcast_to(g_sel_f[g:g + 1], (GROUP_SIZE, tm)) for g in range(N_GROUPS)]
    e_mask = jnp.concatenate(mask_rows, axis=0) > 0.5

    work = jnp.where(e_mask, biased, neg)
    sel = jnp.zeros((N_EXPERTS, tm), F32)
    idx_rows, w_rows, hits = [], [], []
    for _ in range(TOP_K):
        m = jnp.max(work, axis=0, keepdims=True)
        first = jnp.min(jnp.where(work == m, e_iota, N_EXPERTS), axis=0, keepdims=True)
        hit = e_iota == first
        hits.append(hit)
        idx_rows.append(first)
        w_rows.append(jnp.sum(jnp.where(hit, scores, 0.0), axis=0, keepdims=True))
        sel = jnp.where(hit, 1.0, sel)
        work = jnp.where(hit, neg, work)
    w = jnp.concatenate(w_rows, axis=0)
    wgt_ref[...] = w / jnp.sum(w, axis=0, keepdims=True) * ROUTED_SCALE
    idx_ref[...] = jnp.concatenate(idx_rows, axis=0)

    before = jnp.dot(sel.astype(BF16), tri_ref[...], preferred_element_type=F32)
    rank = base_scr[...] + before
    pos_rows = [jnp.sum(jnp.where(hit, rank, 0.0), axis=0, keepdims=True) for hit in hits]
    pos_ref[...] = jnp.concatenate(pos_rows, axis=0).astype(jnp.int32)
    base_scr[...] = base_scr[...] + jnp.sum(sel, axis=1, keepdims=True)
    cnt_ref[...] = jnp.broadcast_to(base_scr[...], cnt_ref.shape)


def _route(h16, rw_t, router_b, tm):
    n_tok, d = h16.shape
    tri = (jnp.arange(tm)[:, None] < jnp.arange(tm)[None, :]).astype(BF16)
    out_col = pl.BlockSpec((TOP_K, tm), lambda i: (0, i))
    return pl.pallas_call(
        _route_kernel,
        grid=(n_tok // tm,),
        in_specs=[pl.BlockSpec((tm, d), lambda i: (i, 0)),
                  pl.BlockSpec((N_EXPERTS, d), lambda i: (0, 0)),
                  pl.BlockSpec((N_EXPERTS, 1), lambda i: (0, 0)),
                  pl.BlockSpec((tm, tm), lambda i: (0, 0))],
        out_specs=[out_col, out_col, out_col, pl.BlockSpec((N_EXPERTS, LANES), lambda i: (0, 0))],
        out_shape=[jax.ShapeDtypeStruct((TOP_K, n_tok), jnp.int32), jax.ShapeDtypeStruct((TOP_K, n_tok), F32),
                   jax.ShapeDtypeStruct((TOP_K, n_tok), jnp.int32),
                   jax.ShapeDtypeStruct((N_EXPERTS, LANES), F32)],
        scratch_shapes=[pltpu.VMEM((N_EXPERTS, 1), F32)],
        compiler_params=_cparams(("arbitrary",), 32),
        name="moe_router",
    )(h16, rw_t, router_b.reshape(N_EXPERTS, 1), tri)


def _slot_kernel(idx_ref, pos_ref, offs_ref, dest_ref):
    tm = idx_ref.shape[1]
    e_iota = lax.broadcasted_iota(jnp.int32, (N_EXPERTS, tm), 0)
    idx = idx_ref[...]
    offs = offs_ref[...]
    rows = [jnp.sum(jnp.where(e_iota == idx[r:r + 1], offs, 0), axis=0, keepdims=True) for r in range(TOP_K)]
    dest_ref[...] = jnp.concatenate(rows, axis=0) + pos_ref[...]


def _slots(top_idx, pos, offs, tm):
    n_tok = top_idx.shape[1]
    col = pl.BlockSpec((TOP_K, tm), lambda i: (0, i))
    return pl.pallas_call(
        _slot_kernel,
        grid=(n_tok // tm,),
        in_specs=[col, col, pl.BlockSpec((N_EXPERTS, 1), lambda i: (0, 0))],
        out_specs=col,
        out_shape=jax.ShapeDtypeStruct((TOP_K, n_tok), jnp.int32),
        compiler_params=_cparams(("parallel",), 32),
        name="moe_slots",
    )(top_idx, pos, offs.reshape(N_EXPERTS, 1))


def _row_copy(src_ref, src_row, dst_ref, dst_row, sem):
    return pltpu.make_async_copy(src_ref.at[pl.ds(src_row, 1)], dst_ref.at[pl.ds(dst_row, 1)], sem)


def _dispatch_kernel(dest_ref, hp_ref, _xs_in, xs_ref, sem):
    tm = hp_ref.shape[0]

    def start(t, c):
        for r in range(TOP_K):
            _row_copy(hp_ref, t, xs_ref, dest_ref[r, t], sem).start()
        return c

    lax.fori_loop(0, tm, start, 0)

    def wait(t, c):
        for r in range(TOP_K):
            _row_copy(hp_ref, t, xs_ref, dest_ref[r, t], sem).wait()
        return c

    lax.fori_loop(0, tm, wait, 0)


def _dispatch(hp, dest3, n_slots):
    n_tok, half = hp.shape
    tm = dest3.shape[2]
    xs0 = jnp.zeros((n_slots, half), jnp.uint32)
    return pl.pallas_call(
        _dispatch_kernel,
        grid=(n_tok // tm,),
        in_specs=[pl.BlockSpec((None, TOP_K, tm), lambda i: (i, 0, 0), memory_space=pltpu.SMEM),
                  pl.BlockSpec((tm, half), lambda i: (i, 0)),
                  pl.BlockSpec(memory_space=pl.ANY)],
        out_specs=pl.BlockSpec(memory_space=pl.ANY),
        out_shape=jax.ShapeDtypeStruct((n_slots, half), jnp.uint32),
        scratch_shapes=[pltpu.SemaphoreType.DMA(())],
        input_output_aliases={2: 0},
        compiler_params=_cparams(("arbitrary",), 32),
        name="moe_dispatch",
    )(dest3, hp, xs0)


def _expert_kernel(be_ref, nu_ref, xs_ref, wg_ref, wu_ref, wd_ref, ys_ref, wg_s, wu_s, wd_s):
    i = pl.program_id(0)
    used = i < nu_ref[0]
    prev_e = be_ref[jnp.maximum(i - 1, 0)]
    fresh = jnp.logical_or(i == 0, be_ref[i] != prev_e)

    @pl.when(jnp.logical_and(used, fresh))
    def _():
        wg_s[...] = wg_ref[0].astype(BF16)
        wu_s[...] = wu_ref[0].astype(BF16)
        wd_s[...] = wd_ref[0].astype(BF16)

    @pl.when(used)
    def _():
        w = xs_ref[...]
        half = D_MODEL // 2
        xa = pltpu.bitcast(w & jnp.uint32(0xFFFF0000), F32).astype(BF16)
        xb = pltpu.bitcast(w << 16, F32).astype(BF16)

        def proj(w_s):
            return (jnp.dot(xa, w_s[:half, :], preferred_element_type=F32)
                    + jnp.dot(xb, w_s[half:, :], preferred_element_type=F32))

        g = proj(wg_s)
        act = (_silu(g) * proj(wu_s)).astype(BF16)
        ys_ref[...] = jnp.dot(act, wd_s[...], preferred_element_type=F32)

    @pl.when(jnp.logical_not(used))
    def _():
        ys_ref[...] = jnp.zeros_like(ys_ref)


def _experts(xs, block_e, n_used, w_gate, w_up, w_down, layer, bm):
    n_slots, half = xs.shape
    d = D_MODEL
    grid_spec = pltpu.PrefetchScalarGridSpec(
        num_scalar_prefetch=2,
        grid=(n_slots // bm,),
        in_specs=[pl.BlockSpec((bm, half), lambda i, be, nu: (i, 0)),
                  pl.BlockSpec((None, 1, d, EXPERT_FF), lambda i, be, nu: (layer, be[i], 0, 0)),
                  pl.BlockSpec((None, 1, d, EXPERT_FF), lambda i, be, nu: (layer, be[i], 0, 0)),
                  pl.BlockSpec((None, 1, EXPERT_FF, d), lambda i, be, nu: (layer, be[i], 0, 0))],
        out_specs=pl.BlockSpec((bm, d), lambda i, be, nu: (i, 0)),
        scratch_shapes=[pltpu.VMEM((d, EXPERT_FF), BF16), pltpu.VMEM((d, EXPERT_FF), BF16),
                        pltpu.VMEM((EXPERT_FF, d), BF16)],
    )
    return pl.pallas_call(
        _expert_kernel,
        grid_spec=grid_spec,
        out_shape=jax.ShapeDtypeStruct((n_slots, d), F32),
        compiler_params=_cparams(("arbitrary",), 40),
        name="moe_experts",
    )(block_e, n_used, xs, w_gate, w_up, w_down)


def _combine_kernel(dest_ref, x_ref, h_ref, tw_ref, gate_ref, wsg_ref, wsu_ref, wsd_ref, ys_ref, o_ref, buf, sem):
    tm = x_ref.shape[0]

    def start(t, c):
        for r in range(TOP_K):
            _row_copy(ys_ref, dest_ref[r, t], buf.at[r], t, sem).start()
        return c

    lax.fori_loop(0, tm, start, 0)

    h = h_ref[...]
    act = (_silu(jnp.dot(h, wsg_ref[...], preferred_element_type=F32))
           * jnp.dot(h, wsu_ref[...], preferred_element_type=F32)).astype(BF16)
    shared = jnp.dot(act, wsd_ref[...], preferred_element_type=F32)

    def wait(t, c):
        for r in range(TOP_K):
            _row_copy(ys_ref, dest_ref[r, t], buf.at[r], t, sem).wait()
        return c

    lax.fori_loop(0, tm, wait, 0)

    tw = tw_ref[...]
    moe = buf[0] * tw[:, 0:1]
    for r in range(1, TOP_K):
        moe = moe + buf[r] * tw[:, r:r + 1]
    o_ref[...] = x_ref[...] + gate_ref[...] * (moe + shared)


def _combine(x_tok, h16, ys, dest3, top_w, mod4, ws_gate, ws_up, ws_down, row_of_tile):
    n_tok, d = x_tok.shape
    tm = dest3.shape[2]
    ff = ws_gate.shape[1]
    row = pl.BlockSpec((tm, d), lambda i: (i, 0))
    return pl.pallas_call(
        _combine_kernel,
        grid=(n_tok // tm,),
        in_specs=[pl.BlockSpec((None, TOP_K, tm), lambda i: (i, 0, 0), memory_space=pltpu.SMEM),
                  row, row,
                  pl.BlockSpec((tm, TOP_K), lambda i: (i, 0)),
                  _mod_spec(5, row_of_tile),
                  pl.BlockSpec((d, ff), lambda i: (0, 0)),
                  pl.BlockSpec((d, ff), lambda i: (0, 0)),
                  pl.BlockSpec((ff, d), lambda i: (0, 0)),
                  pl.BlockSpec(memory_space=pl.ANY)],
        out_specs=row,
        out_shape=jax.ShapeDtypeStruct((n_tok, d), F32),
        scratch_shapes=[pltpu.VMEM((TOP_K, tm, d), F32), pltpu.SemaphoreType.DMA(())],
        compiler_params=_cparams(("arbitrary",), 48),
        name="moe_combine",
    )(dest3, x_tok, h16, top_w, mod4, ws_gate, ws_up, ws_down, ys)


def _moe(x_tok, h16, hp, mod4, row_of_tile, rw_t, router_b, w_gate, w_up, w_down, layer, ws_gate, ws_up, ws_down):
    n_tok = x_tok.shape[0]
    bm = MOE_BM
    tm_r = TM_ROUTE if n_tok % TM_ROUTE == 0 else TM
    top_idx, top_w, pos, cnt = _route(h16, rw_t, router_b, tm_r)

    counts = cnt[:, 0].astype(jnp.int32)
    padded = (counts + bm - 1) // bm * bm
    ends = jnp.cumsum(padded)
    n_blocks = -(-(n_tok * TOP_K + N_EXPERTS * (bm - 1)) // bm)
    dest = _slots(top_idx, pos, ends - padded, tm_r)
    block_e = jnp.minimum(jnp.searchsorted(ends, jnp.arange(n_blocks, dtype=jnp.int32) * bm, side='right'),
                          N_EXPERTS - 1).astype(jnp.int32)
    n_used = (ends[-1] // bm).astype(jnp.int32).reshape(1)
    tm = TM
    dest3 = dest.reshape(TOP_K, n_tok // tm, tm).transpose(1, 0, 2)

    xs = _dispatch(hp, dest3, n_blocks * bm)
    ys = _experts(xs, block_e, n_used, w_gate, w_up, w_down, layer, bm)
    return _combine(x_tok, h16, ys, dest3, top_w.T, mod4, ws_gate, ws_up, ws_down, row_of_tile)


def _block_diag_tiles(w):
    per = MXU_DIM // LRU_BLOCK_DIM
    w = w.reshape(2, D_MODEL // MXU_DIM, per, LRU_BLOCK_DIM, LRU_BLOCK_DIM)
    eye = jnp.eye(per, dtype=w.dtype)
    t = jnp.einsum('ztpde,pq->ztpdqe', w, eye)
    return t.reshape(2, D_MODEL // MXU_DIM, MXU_DIM, MXU_DIM).astype(BF16)


def _rope_tables(n_lat, n_ctx):
    rows = n_lat // GRID_W
    row = jnp.repeat(jnp.arange(rows), GRID_W)
    col = jnp.tile(jnp.arange(GRID_W), rows)
    inv = ROPE_THETA ** (-jnp.arange(ROPE_FREQS, dtype=F32) / ROPE_FREQS)
    ang = jnp.stack([row[:, None] * inv, col[:, None] * inv], axis=1)
    ang = jnp.concatenate([ang, ang], axis=2).reshape(n_lat, HEAD_DIM)
    ang = jnp.tile(ang, (1, LANES // HEAD_DIM))
    ang = jnp.concatenate([ang, jnp.zeros((n_ctx, LANES), F32)], axis=0)
    lane = jnp.arange(LANES)
    sign = jnp.where((lane % (2 * ROPE_FREQS)) < ROPE_FREQS, -1.0, 1.0).astype(F32)
    return jnp.cos(ang), jnp.sin(ang) * sign


def kernel(x, c, ctx, c_ctx, w_mod, b_mod, norm1_g, w_in, q_norm_g, k_norm_g, lambda_qk, subln_g, w_attn_o, conv_w, conv_b, lru_wa, lru_ba, lru_wi, lru_bi, lru_lambda, w_lru_o, w_four_o, w_out, norm2_g, router_w, router_b, exp_w_gate, exp_w_up, exp_w_down, sh_w_gate, sh_w_up, sh_w_down):
    bsz, n_lat, d = x.shape
    n_ctx = ctx.shape[1]
    depth = w_mod.shape[0]
    n_latent_rows = bsz * n_lat
    n_tok = n_latent_rows + bsz * n_ctx
    assert d == D_MODEL and bsz + 1 <= MOD_ROWS
    assert n_lat % TM == 0 and n_ctx % TM == 0 and n_lat % GRID_W == 0
    tm_in = TM_IN if (n_lat % TM_IN == 0 and (bsz * n_ctx) % TM_IN == 0) else TM

    cvec = jnp.zeros((MOD_ROWS, d), F32).at[:bsz].set(c).at[bsz].set(c_ctx)
    mods = _modulation(cvec, w_mod, b_mod)

    def row_of_tile_fn(tm):
        per_batch = n_lat // tm
        return lambda i: jnp.minimum(i // per_batch, bsz)

    assert n_lat % n_ctx == 0
    cos_t, sin_t = _rope_tables(n_lat, n_ctx)
    group_mat = jnp.kron(jnp.eye(d // HEAD_DIM, dtype=F32), jnp.ones((HEAD_DIM, HEAD_DIM), F32)).astype(BF16)
    m = jnp.arange(FOUR_GROUP_DIM, dtype=jnp.int32)
    ang_c = ((m[:, None] * m[None, :]) % FOUR_GROUP_DIM).astype(F32) * (2.0 * math.pi / FOUR_GROUP_DIM)
    inv_sqrt_c = FOUR_GROUP_DIM ** -0.5
    cs_mat = jnp.concatenate([jnp.cos(ang_c) * inv_sqrt_c, jnp.sin(ang_c) * inv_sqrt_c], axis=1).astype(BF16)
    ct_lat, st_lat = _dft_tables(n_lat)
    ct_ctx, st_ctx = _dft_tables(n_ctx)
    tmk_lat = 1024 if n_lat % 1024 == 0 else TM
    q_sub = math.gcd(Q_SUB, n_lat // TQ)
    n_keys = n_ctx + n_lat
    tk = TK if n_keys % TK == 0 else TM

    x_tok = jnp.concatenate([x.reshape(n_latent_rows, d), ctx.reshape(bsz * n_ctx, d)], axis=0)
    for l in range(depth):
        last = l == depth - 1
        lam_init = 0.8 - 0.6 * math.exp(-0.3 * l)
        mod4 = mods[l].reshape(MOD_ROWS, 6, 1, d)

        u = _in_projection(x_tok, mod4, norm1_g[l], w_in[l].astype(BF16), row_of_tile_fn(tm_in), tm_in)

        k_all, q_all, vt_all = _qkv_prepare(u, k_norm_g[l], q_norm_g[l], group_mat, cos_t, sin_t,
                                            bsz, n_lat, n_ctx)
        attn_l = _diff_attention(q_all, k_all, vt_all, lambda_qk[l], subln_g[l], lam_init,
                                 0, n_lat, 0, n_keys, TQ, q_sub, tk)
        attn_c = attn_l
        if not last:
            attn_c = _diff_attention(q_all, k_all, vt_all, lambda_qk[l], subln_g[l], lam_init,
                                     n_lat, n_ctx, n_lat, n_ctx, TM, 1, TM)

        rec = _rglru(u, conv_w[l], conv_b[l], _block_diag_tiles(lru_wa[l]), lru_ba[l],
                     _block_diag_tiles(lru_wi[l]), lru_bi[l], lru_lambda[l], bsz, n_lat, n_ctx)

        y_four = _chan_dft(u, cs_mat)
        four_l = _time_dft(y_four, ct_lat, st_lat, bsz, n_lat, 0, tmk_lat)
        four_c = four_l
        if not last:
            four_c = _time_dft(y_four, ct_ctx, st_ctx, bsz, n_ctx, n_latent_rows, n_ctx)

        n_rows = n_latent_rows if last else n_tok
        row_of_tile = row_of_tile_fn(TM)
        x_mid, h16, hp = _merge(x_tok, attn_l, attn_c, rec, four_l, four_c, u, w_attn_o[l].astype(BF16),
                                w_lru_o[l].astype(BF16), w_four_o[l].astype(BF16), w_out[l].astype(BF16),
                                mod4, norm2_g[l], row_of_tile, n_rows)
        x_tok = _moe(x_mid, h16, hp, mod4, row_of_tile, router_w[l].T.astype(BF16), router_b[l],
                     exp_w_gate, exp_w_up, exp_w_down, l,
                     sh_w_gate[l].astype(BF16), sh_w_up[l].astype(BF16), sh_w_down[l].astype(BF16))
    return x_tok[:n_latent_rows].reshape(bsz, n_lat, d)
```

```python
import functools
import math

import jax
import jax.numpy as jnp
from jax import lax
from jax.experimental import pallas as pl
from jax.experimental.pallas import tpu as pltpu

F32 = jnp.float32
BF16 = jnp.bfloat16

D_MODEL = 1024
EPS = 1e-6
GRID_W = 64

N_HEADS = 8
HEAD_DIM = 64
V_DIM = 2 * HEAD_DIM
ATTN_SCALE = HEAD_DIM ** -0.5
ROPE_THETA = 10000.0
ROPE_FREQS = HEAD_DIM // 4

LRU_BLOCKS = 16
LRU_BLOCK_DIM = D_MODEL // LRU_BLOCKS
LRU_C = 8.0
CONV_W = 4

FOUR_GROUPS = 4
FOUR_GROUP_DIM = D_MODEL // FOUR_GROUPS

CB_K, CB_V, CB_LX, CB_Q, CB_LG, CB_F, CB_G = 0, 1, 2, 3, 4, 5, 6
IN_BLOCKS = 9

N_EXPERTS = 256
TOP_K = 8
N_GROUPS = 8
GROUP_SIZE = N_EXPERTS // N_GROUPS
TOPK_GROUPS = 4
EXPERT_FF = 256
ROUTED_SCALE = 2.5

LANES = 128
SUBLANES = 8
MXU_DIM = 256
VMEM_BYTES_V7X = 64 * 1024 * 1024

MOD_ROWS = 8
TM = 256
TM_IN = 512
TQ = 256
Q_SUB = 32
TK = 768
TM_ROUTE = 512
MOE_BM = 256
FFT_RADIX = 8
LOG2E = 1.4426950408889634


def _cparams(sem, vmem_mb):
    return pltpu.CompilerParams(dimension_semantics=sem, vmem_limit_bytes=int(vmem_mb * 1024 * 1024))


def _silu(x):
    return x * jax.nn.sigmoid(x)


def _mod_kernel(c_ref, w_ref, b_ref, o_ref):
    c = c_ref[...]
    s = _silu(c).astype(BF16)
    o_ref[0] = jnp.dot(s, w_ref[0].astype(BF16), preferred_element_type=F32) + b_ref[0]


def _modulation(cvec, w_mod, b_mod):
    n_layers, d, six_d = w_mod.shape
    nj = six_d // d
    return pl.pallas_call(
        _mod_kernel,
        grid=(n_layers, nj),
        in_specs=[
            pl.BlockSpec((MOD_ROWS, d), lambda l, j: (0, 0)),
            pl.BlockSpec((1, d, d), lambda l, j: (l, 0, j)),
            pl.BlockSpec((1, 1, d), lambda l, j: (l, 0, j)),
        ],
        out_specs=pl.BlockSpec((1, MOD_ROWS, d), lambda l, j: (l, 0, j)),
        out_shape=jax.ShapeDtypeStruct((n_layers, MOD_ROWS, six_d), F32),
        compiler_params=_cparams(("parallel", "parallel"), 32),
        name="modulation",
    )(cvec, w_mod, b_mod.reshape(n_layers, 1, six_d))


def _mod_spec(part, row_of_tile, n_grid_axes=1):
    if n_grid_axes == 1:
        return pl.BlockSpec((None, None, 1, D_MODEL), lambda i: (row_of_tile(i), part, 0, 0))
    return pl.BlockSpec((None, None, 1, D_MODEL), lambda i, j: (row_of_tile(i), part, 0, 0))


def _inproj_kernel(x_ref, g_ref, sh_ref, sc_ref, w_ref, o_ref, h_scr):
    @pl.when(pl.program_id(1) == 0)
    def _():
        x = x_ref[...]
        y = x * lax.rsqrt(jnp.mean(x * x, axis=-1, keepdims=True) + EPS)
        h = (y * g_ref[...]) * (1.0 + sc_ref[...]) + sh_ref[...]
        h_scr[...] = h.astype(BF16)

    o_ref[...] = jnp.dot(h_scr[...], w_ref[...], preferred_element_type=F32)


def _in_projection(x_tok, mod4, norm_g, w_in_bf16, row_of_tile, tm):
    n_tok, d = x_tok.shape
    n_cols = w_in_bf16.shape[1]
    return pl.pallas_call(
        _inproj_kernel,
        grid=(n_tok // tm, n_cols // d),
        in_specs=[
            pl.BlockSpec((tm, d), lambda i, j: (i, 0)),
            pl.BlockSpec((1, d), lambda i, j: (0, 0)),
            _mod_spec(0, row_of_tile, 2),
            _mod_spec(1, row_of_tile, 2),
            pl.BlockSpec((d, d), lambda i, j: (0, j)),
        ],
        out_specs=pl.BlockSpec((tm, d), lambda i, j: (i, j)),
        out_shape=jax.ShapeDtypeStruct((n_tok, n_cols), F32),
        scratch_shapes=[pltpu.VMEM((tm, d), BF16)],
        compiler_params=_cparams(("parallel", "arbitrary"), 40),
        name="in_projection",
    )(x_tok, norm_g.reshape(1, d), mod4, mod4, w_in_bf16)


def _prep_kernel(uk_ref, uq_ref, uv_ref, kg_ref, qg_ref, gm_ref, cos_ref, sin_ref, k_out, q_out, vt_out, *, tm):
    gm = gm_ref[...]
    lane = lax.broadcasted_iota(jnp.int32, (tm, LANES), 1)
    first_half = (lane % (2 * ROPE_FREQS)) < ROPE_FREQS

    def norm_rope(x, g):
        ss = jnp.dot((x * x).astype(BF16), gm, preferred_element_type=F32) * (1.0 / HEAD_DIM)
        y = x * lax.rsqrt(ss + EPS) * g
        c = cos_ref[...]
        s = sin_ref[...]
        outs = []
        for j in range(D_MODEL // LANES):
            yt = y[:, j * LANES:(j + 1) * LANES]
            rot = jnp.where(first_half, pltpu.roll(yt, LANES - ROPE_FREQS, 1), pltpu.roll(yt, ROPE_FREQS, 1))
            outs.append(yt * c + rot * s)
        return jnp.concatenate(outs, axis=1)

    k_out[...] = norm_rope(uk_ref[...], kg_ref[...]).astype(BF16)
    qn = norm_rope(uq_ref[...], qg_ref[...]) * (ATTN_SCALE * LOG2E)
    lo = lane < HEAD_DIM
    for h in range(N_HEADS):
        qt = qn[:, h * LANES:(h + 1) * LANES]
        q_out[:, (2 * h) * LANES:(2 * h + 1) * LANES] = jnp.where(lo, qt, 0.0).astype(BF16)
        q_out[:, (2 * h + 1) * LANES:(2 * h + 2) * LANES] = jnp.where(lo, 0.0, qt).astype(BF16)
    vt_out[...] = uv_ref[...].T.astype(BF16)


def _qkv_prepare(u, k_gain, q_gain, group_mat, cos_t, sin_t, bsz, n_lat, n_ctx):
    tm = TM
    d = D_MODEL
    n_s, n_c = n_lat // tm, n_ctx // tm
    n_rows = n_lat + n_ctx
    lat_tiles = bsz * n_s
    kg = jnp.tile(k_gain, d // HEAD_DIM).reshape(1, d)
    qg = jnp.tile(q_gain, d // HEAD_DIM).reshape(1, d)

    def tok_tile(b, j):
        return jnp.where(j < n_s, b * n_s + j, lat_tiles + b * n_c + (j - n_s))

    def u_spec(cb):
        return pl.BlockSpec((tm, d), lambda b, j: (tok_tile(b, j), cb))

    vec = pl.BlockSpec((1, d), lambda b, j: (0, 0))
    rope_spec = pl.BlockSpec((tm, LANES), lambda b, j: (j, 0))
    return pl.pallas_call(
        functools.partial(_prep_kernel, tm=tm),
        grid=(bsz, n_s + n_c),
        in_specs=[u_spec(CB_K), u_spec(CB_Q), u_spec(CB_V), vec, vec,
                  pl.BlockSpec((d, d), lambda b, j: (0, 0)), rope_spec, rope_spec],
        out_specs=[
            pl.BlockSpec((None, tm, d), lambda b, j: (b, j, 0)),
            pl.BlockSpec((None, tm, 2 * d), lambda b, j: (b, j, 0)),
            pl.BlockSpec((None, d, tm), lambda b, j: (b, 0, j)),
        ],
        out_shape=[jax.ShapeDtypeStruct((bsz, n_rows, d), BF16),
                   jax.ShapeDtypeStruct((bsz, n_rows, 2 * d), BF16),
                   jax.ShapeDtypeStruct((bsz, d, n_rows), BF16)],
        compiler_params=_cparams(("parallel", "parallel"), 40),
        name="qkv_prepare",
    )(u, u, u, kg, qg, group_mat, cos_t, sin_t)


def _attn_kernel(lq_ref, sg_ref, q1_ref, q2_ref, k_ref, vt_ref, o_ref, s_buf, *, n_keys, tk, tq, lam_init):
    lq = lq_ref[...]
    lam = (jnp.exp(jnp.sum(lq[0:1] * lq[1:2], axis=1, keepdims=True))
           - jnp.exp(jnp.sum(lq[2:3] * lq[3:4], axis=1, keepdims=True)) + lam_init)
    n_chunks = n_keys // tk
    n_sub = q1_ref.shape[0] // tq

    def load_q(j):
        rows = pl.ds(pl.multiple_of(j * tq, tq), tq)
        return jnp.concatenate([q1_ref[rows, :], q2_ref[rows, :]], axis=0)

    def score_chunk(q, c, mx):
        s = lax.dot_general(k_ref[c * tk:(c + 1) * tk, :], q, (((1,), (1,)), ((), ())),
                            preferred_element_type=F32)
        s_buf[c * tk:(c + 1) * tk, :] = s
        cm = jnp.max(s, axis=0, keepdims=True)
        return cm if mx is None else jnp.maximum(mx, cm)

    q0 = load_q(0)
    mx0 = None
    for c in range(n_chunks):
        mx0 = score_chunk(q0, c, mx0)

    def sub_tile(j, mx):
        q_next = load_q(jnp.minimum(j + 1, n_sub - 1))
        mx_next = None
        l = jnp.zeros((1, 2 * tq), F32)
        acc = jnp.zeros((V_DIM, 2 * tq), F32)
        for c in range(n_chunks):
            p = jnp.exp2(s_buf[c * tk:(c + 1) * tk, :] - mx)
            l = l + jnp.sum(p, axis=0, keepdims=True)
            acc = acc + jnp.dot(vt_ref[:, c * tk:(c + 1) * tk], p.astype(BF16), preferred_element_type=F32)
            mx_next = score_chunk(q_next, c, mx_next)
        o = acc / l
        o = o[:, :tq] - lam * o[:, tq:]
        ms = jnp.mean(o * o, axis=0, keepdims=True)
        on = (o * lax.rsqrt(ms + EPS) * sg_ref[...]) * (1.0 - lam_init)
        o_ref[pl.ds(pl.multiple_of(j * tq, tq), tq), :] = on.T
        return mx_next

    lax.fori_loop(0, n_sub, sub_tile, mx0)


def _diff_attention(q_all, k_all, vt_all, lambda_qk, subln_g, lam_init, q_row0, n_q, key_row0, n_keys, tq, n_sub, tk):
    bsz = q_all.shape[0]
    d = D_MODEL
    tstep = tq * n_sub
    nq = n_q // tstep
    qb0 = q_row0 // tstep
    kb0 = key_row0 // n_keys
    s_bytes = n_keys * 2 * tq * 4
    return pl.pallas_call(
        functools.partial(_attn_kernel, n_keys=n_keys, tk=tk, tq=tq, lam_init=lam_init),
        grid=(bsz, N_HEADS, nq),
        in_specs=[
            pl.BlockSpec((4, HEAD_DIM), lambda b, h, i: (0, 0)),
            pl.BlockSpec((V_DIM, 1), lambda b, h, i: (0, 0)),
            pl.BlockSpec((None, tstep, LANES), lambda b, h, i: (b, qb0 + i, 2 * h)),
            pl.BlockSpec((None, tstep, LANES), lambda b, h, i: (b, qb0 + i, 2 * h + 1)),
            pl.BlockSpec((None, n_keys, LANES), lambda b, h, i: (b, kb0, h)),
            pl.BlockSpec((None, V_DIM, n_keys), lambda b, h, i: (b, h, kb0)),
        ],
        out_specs=pl.BlockSpec((tstep, LANES), lambda b, h, i: (b * nq + i, h)),
        out_shape=jax.ShapeDtypeStruct((bsz * n_q, d), F32),
        scratch_shapes=[pltpu.VMEM((n_keys, 2 * tq), F32)],
        compiler_params=_cparams(("parallel", "parallel", "arbitrary"), 28 + s_bytes / 2 ** 20),
        name="diff_attention",
    )(lambda_qk, subln_g.reshape(V_DIM, 1), q_all, q_all, k_all, vt_all)


def _lru_kernel(*refs, rev, n_c, n_s, tm):
    if rev:
        (cur_ref, prev_ref, next_ref, cw_ref, cb_ref, wa_ref, ba_ref, wi_ref, bi_ref, lam_ref, fwd_ref,
         o_ref, a_scr, b_scr, h_scr) = refs
    else:
        (cur_ref, prev_ref, next_ref, cw_ref, cb_ref, wa_ref, ba_ref, wi_ref, bi_ref, lam_ref,
         o_ref, a_scr, b_scr, h_scr) = refs
    s = pl.program_id(1)
    is_ctx = s < n_c
    if rev:
        tile = jnp.where(is_ctx, n_c - 1 - s, n_s - 1 - (s - n_c))
    else:
        tile = jnp.where(is_ctx, s, s - n_c)
    n_tile = jnp.where(is_ctx, n_c, n_s)

    @pl.when(s == 0)
    def _():
        h_scr[...] = jnp.zeros_like(h_scr)

    x = cur_ref[...]
    pv = jnp.where(tile == 0, 0.0, prev_ref[...])
    nx = jnp.where(tile == n_tile - 1, 0.0, next_ref[...])
    r8 = lax.broadcasted_iota(jnp.int32, (SUBLANES, D_MODEL), 0)

    def earlier(k):
        rolled = pltpu.roll(x, k, 0)
        top = jnp.where(r8 < k, pltpu.roll(pv, k, 0), rolled[:SUBLANES])
        return jnp.concatenate([top, rolled[SUBLANES:]], axis=0)

    rolled = pltpu.roll(x, tm - 1, 0)
    bot = jnp.where(r8 >= SUBLANES - 1, pltpu.roll(nx, SUBLANES - 1, 0), rolled[tm - SUBLANES:])
    later = jnp.concatenate([rolled[:tm - SUBLANES], bot], axis=0)
    cw = cw_ref[...]
    conv = earlier(2) * cw[0:1] + earlier(1) * cw[1:2] + x * cw[2:3] + later * cw[3:4] + cb_ref[...]

    conv16 = conv.astype(BF16)

    def block_diag(w_ref, b_ref):
        parts = [jnp.dot(conv16[:, j * MXU_DIM:(j + 1) * MXU_DIM], w_ref[j], preferred_element_type=F32)
                 for j in range(D_MODEL // MXU_DIM)]
        return jnp.concatenate(parts, axis=1) + b_ref[...]

    r = jax.nn.sigmoid(block_diag(wa_ref, ba_ref))
    gate_i = jax.nn.sigmoid(block_diag(wi_ref, bi_ref))
    neg_lam = -lam_ref[...]
    softplus = jnp.maximum(neg_lam, 0.0) + jnp.log1p(jnp.exp(-jnp.abs(neg_lam)))
    log_a = -LRU_C * r * softplus
    a = jnp.exp(log_a)
    mult = jnp.sqrt(1.0 - a * a)
    a_scr[...] = a
    b_scr[...] = mult * gate_i * conv

    def body(t, h):
        row = tm - 1 - t if rev else t
        h = a_scr[pl.ds(row, 1), :] * h + b_scr[pl.ds(row, 1), :]
        if rev:
            o_ref[pl.ds(row, 1), :] = fwd_ref[pl.ds(row, 1), :] + h
        else:
            o_ref[pl.ds(row, 1), :] = h
        return h

    h_scr[...] = lax.fori_loop(0, tm, body, h_scr[...], unroll=8)


def _rglru(u, conv_w, conv_b, wa_bd, ba, wi_bd, bi, lam, bsz, n_lat, n_ctx):
    tm = TM
    d = D_MODEL
    n_tok = u.shape[0]
    n_s, n_c = n_lat // tm, n_ctx // tm
    lat_tiles = bsz * n_s
    per8 = tm // SUBLANES
    last8 = n_tok // SUBLANES - 1

    def make(rev, fwd):
        def blk(b, s):
            is_ctx = s < n_c
            if rev:
                tile = jnp.where(is_ctx, n_c - 1 - s, n_s - 1 - (s - n_c))
            else:
                tile = jnp.where(is_ctx, s, s - n_c)
            return jnp.where(is_ctx, lat_tiles + b * n_c + tile, b * n_s + tile)

        vec = pl.BlockSpec((None, 1, d), lambda b, s: (int(rev), 0, 0))
        wspec = pl.BlockSpec((None, d // MXU_DIM, MXU_DIM, MXU_DIM), lambda b, s: (int(rev), 0, 0, 0))
        row_spec = pl.BlockSpec((tm, d), lambda b, s: (blk(b, s), 0))
        in_specs = [
            pl.BlockSpec((tm, d), lambda b, s: (blk(b, s), CB_LX)),
            pl.BlockSpec((SUBLANES, d), lambda b, s: (jnp.maximum(blk(b, s) * per8 - 1, 0), CB_LX)),
            pl.BlockSpec((SUBLANES, d), lambda b, s: (jnp.minimum((blk(b, s) + 1) * per8, last8), CB_LX)),
            pl.BlockSpec((CONV_W, d), lambda b, s: (0, 0)),
            pl.BlockSpec((1, d), lambda b, s: (0, 0)),
            wspec, vec, wspec, vec, vec,
        ]
        args = [u, u, u, conv_w, conv_b.reshape(1, d), wa_bd, ba.reshape(2, 1, d), wi_bd, bi.reshape(2, 1, d),
                lam.reshape(2, 1, d)]
        if rev:
            in_specs.append(row_spec)
            args.append(fwd)
        return pl.pallas_call(
            functools.partial(_lru_kernel, rev=rev, n_c=n_c, n_s=n_s, tm=tm),
            grid=(bsz, n_c + n_s),
            in_specs=in_specs,
            out_specs=row_spec,
            out_shape=jax.ShapeDtypeStruct((n_tok, d), F32),
            scratch_shapes=[pltpu.VMEM((tm, d), F32), pltpu.VMEM((tm, d), F32), pltpu.VMEM((1, d), F32)],
            input_output_aliases={10: 0} if rev else {},
            compiler_params=_cparams(("arbitrary", "arbitrary"), 40),
            name="rglru_bwd" if rev else "rglru_fwd",
        )(*args)

    fwd = make(False, None)
    return make(True, fwd)


def _chan_dft_kernel(u_ref, cs_ref, o_ref):
    x = u_ref[...].astype(BF16)
    cs = cs_ref[...]
    gd = FOUR_GROUP_DIM
    for g in range(FOUR_GROUPS):
        y = jnp.dot(x[:, g * gd:(g + 1) * gd], cs, preferred_element_type=F32)
        o_ref[:, g * gd:(g + 1) * gd] = y[:, :gd].astype(BF16)
        o_ref[:, D_MODEL + g * gd:D_MODEL + (g + 1) * gd] = y[:, gd:].astype(BF16)


def _chan_dft(u, cs_mat, row0, n_rows):
    tm = TM
    d = D_MODEL
    tile0 = row0 // tm
    return pl.pallas_call(
        _chan_dft_kernel,
        grid=(n_rows // tm,),
        in_specs=[pl.BlockSpec((tm, d), lambda i: (tile0 + i, CB_F)),
                  pl.BlockSpec((FOUR_GROUP_DIM, 2 * FOUR_GROUP_DIM), lambda i: (0, 0))],
        out_specs=pl.BlockSpec((tm, 2 * d), lambda i: (i, 0)),
        out_shape=jax.ShapeDtypeStruct((n_rows, 2 * d), BF16),
        compiler_params=_cparams(("parallel",), 32),
        name="fourier_channels",
    )(u, cs_mat)


def _chan_dft_split_kernel(u_ref, cs_ref, o_ref, y_scr, *, radix):
    x = u_ref[...].astype(BF16)
    cs = cs_ref[...]
    gd = FOUR_GROUP_DIM
    per_g = gd // LANES
    n_cos = D_MODEL // LANES
    for g in range(FOUR_GROUPS):
        y = jnp.dot(x[:, g * gd:(g + 1) * gd], cs, preferred_element_type=F32)
        for j in range(per_g):
            y_scr[g * per_g + j] = y[:, j * LANES:(j + 1) * LANES]
            y_scr[n_cos + g * per_g + j] = y[:, gd + j * LANES:gd + (j + 1) * LANES]
    rows = u_ref.shape[0] // radix
    for r in range(radix):
        for cb in range(2 * n_cos):
            o_ref[r, :, cb * LANES:(cb + 1) * LANES] = y_scr[cb, pl.ds(r, rows, stride=radix), :].astype(BF16)


def _chan_dft_split(u, cs_mat, bsz, n_lat, radix):
    tm = TM
    d = D_MODEL
    n_s = n_lat // tm
    return pl.pallas_call(
        functools.partial(_chan_dft_split_kernel, radix=radix),
        grid=(bsz, n_s),
        in_specs=[pl.BlockSpec((tm, d), lambda b, i: (b * n_s + i, CB_F)),
                  pl.BlockSpec((FOUR_GROUP_DIM, 2 * FOUR_GROUP_DIM), lambda b, i: (0, 0))],
        out_specs=pl.BlockSpec((None, radix, tm // radix, 2 * d), lambda b, i: (b, 0, i, 0)),
        out_shape=jax.ShapeDtypeStruct((bsz, radix, n_lat // radix, 2 * d), BF16),
        scratch_shapes=[pltpu.VMEM((2 * d // LANES, tm, LANES), F32)],
        compiler_params=_cparams(("parallel", "parallel"), 32),
        name="fourier_channels_split",
    )(u, cs_mat)


def _time_fft_kernel(zr_ref, zi_ref, cc_ref, pc_ref, ps_ref, o_ref, *, radix):
    cc = cc_ref[...]
    tk1 = cc.shape[0] // 2
    for r in range(radix):
        pr = jnp.dot(cc, zr_ref[r], preferred_element_type=F32)
        pi = jnp.dot(cc, zi_ref[r], preferred_element_type=F32)
        a_re = pr[:tk1] - pi[tk1:]
        a_im = pr[tk1:] + pi[:tk1]
        for k2 in range(radix):
            term = pc_ref[k2][:, r:r + 1] * a_re - ps_ref[k2][:, r:r + 1] * a_im
            if r == 0:
                o_ref[k2] = term
            else:
                o_ref[k2] += term


def _time_fft(yp, cc, pc, ps, radix, tk1, tn):
    bsz, _, t1, two_d = yp.shape
    d = two_d // 2
    nj = d // tn
    return pl.pallas_call(
        functools.partial(_time_fft_kernel, radix=radix),
        grid=(bsz, nj, t1 // tk1),
        in_specs=[
            pl.BlockSpec((None, radix, t1, tn), lambda b, j, i: (b, 0, 0, j)),
            pl.BlockSpec((None, radix, t1, tn), lambda b, j, i: (b, 0, 0, nj + j)),
            pl.BlockSpec((None, 2 * tk1, t1), lambda b, j, i: (i, 0, 0)),
            pl.BlockSpec((radix, tk1, radix), lambda b, j, i: (0, i, 0)),
            pl.BlockSpec((radix, tk1, radix), lambda b, j, i: (0, i, 0)),
        ],
        out_specs=pl.BlockSpec((None, radix, tk1, tn), lambda b, j, i: (b, 0, i, j)),
        out_shape=jax.ShapeDtypeStruct((bsz, radix, t1, d), F32),
        compiler_params=_cparams(("parallel", "parallel", "arbitrary"), 48),
        name="fourier_positions_fft",
    )(yp, yp, cc, pc, ps)


def _fft_tables(n, radix, tk1):
    t1 = n // radix
    k1 = jnp.arange(t1, dtype=jnp.int32)
    ang = ((k1[:, None] * k1[None, :]) % t1).astype(F32) * (2.0 * math.pi / t1)
    c = jnp.cos(ang).reshape(t1 // tk1, tk1, t1)
    s = jnp.sin(ang).reshape(t1 // tk1, tk1, t1)
    cc = jnp.concatenate([c, s], axis=1).astype(BF16)
    k2 = jnp.arange(radix, dtype=jnp.int32)
    r = jnp.arange(radix, dtype=jnp.int32)
    k = k1[None, :, None] + t1 * k2[:, None, None]
    ph = ((k * r[None, None, :]) % n).astype(F32) * (2.0 * math.pi / n)
    scale = float(n) ** -0.5
    return cc, jnp.cos(ph) * scale, jnp.sin(ph) * scale


def _time_dft_kernel(ct_ref, st_ref, yc_ref, ys_ref, o_ref, acc, *, scale):
    kk = pl.program_id(2)

    @pl.when(kk == 0)
    def _():
        acc[...] = jnp.zeros_like(acc)

    acc[...] += (jnp.dot(ct_ref[...], yc_ref[...], preferred_element_type=F32)
                 - jnp.dot(st_ref[...], ys_ref[...], preferred_element_type=F32))

    @pl.when(kk == pl.num_programs(2) - 1)
    def _():
        o_ref[...] = acc[...] * scale


def _time_dft(y, ct, st, bsz, n_pos, row0, tmk):
    d = D_MODEL
    nt = n_pos // tmk
    rb0 = row0 // tmk
    return pl.pallas_call(
        functools.partial(_time_dft_kernel, scale=float(n_pos) ** -0.5),
        grid=(bsz, nt, nt),
        in_specs=[
            pl.BlockSpec((tmk, tmk), lambda b, i, k: (i, k)),
            pl.BlockSpec((tmk, tmk), lambda b, i, k: (i, k)),
            pl.BlockSpec((tmk, d), lambda b, i, k: (rb0 + b * nt + k, 0)),
            pl.BlockSpec((tmk, d), lambda b, i, k: (rb0 + b * nt + k, 1)),
        ],
        out_specs=pl.BlockSpec((tmk, d), lambda b, i, k: (b * nt + i, 0)),
        out_shape=jax.ShapeDtypeStruct((bsz * n_pos, d), F32),
        scratch_shapes=[pltpu.VMEM((tmk, d), F32)],
        compiler_params=_cparams(("parallel", "parallel", "arbitrary"), 48),
        name="fourier_positions",
    )(ct, st, y, y)


def _dft_tables(n):
    k = jnp.arange(n, dtype=jnp.int32)
    ang = ((k[:, None] * k[None, :]) % n).astype(F32) * (2.0 * math.pi / n)
    return jnp.cos(ang).astype(BF16), jnp.sin(ang).astype(BF16)


def _merge_kernel(x_ref, attn_l_ref, attn_c_ref, rec_ref, four_l_ref, four_c_ref, lg_ref, g0_ref, g1_ref, g2_ref,
                  wa_ref, wl_ref, wf_ref, wo_ref, gate_ref, n2_ref, sh_ref, sc_ref,
                  xo_ref, h_ref, hp_ref, *, lat_tiles):
    def mm(a, w_ref):
        return jnp.dot(a.astype(BF16), w_ref[...], preferred_element_type=F32)

    is_lat = pl.program_id(0) < lat_tiles
    attn_o = mm(jnp.where(is_lat, attn_l_ref[...], attn_c_ref[...]), wa_ref)
    rec_o = mm(rec_ref[...] * jax.nn.gelu(lg_ref[...], approximate=True), wl_ref)
    four_o = mm(jnp.where(is_lat, four_l_ref[...], four_c_ref[...]), wf_ref)
    mixed = (jax.nn.sigmoid(g0_ref[...]) * attn_o + jax.nn.sigmoid(g1_ref[...]) * rec_o
             + jax.nn.sigmoid(g2_ref[...]) * four_o)
    x = x_ref[...] + gate_ref[...] * mm(mixed, wo_ref)
    xo_ref[...] = x
    y = x * lax.rsqrt(jnp.mean(x * x, axis=-1, keepdims=True) + EPS)
    h = (y * n2_ref[...]) * (1.0 + sc_ref[...]) + sh_ref[...]
    h16 = h.astype(BF16)
    h_ref[...] = h16
    half = D_MODEL // 2
    bits = pltpu.bitcast(h16.astype(F32), jnp.uint32)
    hp_ref[...] = (bits[:, :half] & jnp.uint32(0xFFFF0000)) | (bits[:, half:] >> 16)


def _merge(x_tok, attn_l, attn_c, rec, four_l, four_c, u, w_attn_o, w_lru_o, w_four_o, w_out, mod4, norm2_g,
           row_of_tile, n_rows):
    tm = TM
    d = D_MODEL
    lat_tiles = attn_l.shape[0] // tm
    ctx_tiles = attn_c.shape[0] // tm
    row = pl.BlockSpec((tm, d), lambda i: (i, 0))
    lat_row = pl.BlockSpec((tm, d), lambda i: (jnp.minimum(i, lat_tiles - 1), 0))
    ctx_row = pl.BlockSpec((tm, d), lambda i: (jnp.clip(i - lat_tiles, 0, ctx_tiles - 1), 0))
    wspec = pl.BlockSpec((d, d), lambda i: (0, 0))

    def ucol(cb):
        return pl.BlockSpec((tm, d), lambda i: (i, cb))

    return pl.pallas_call(
        functools.partial(_merge_kernel, lat_tiles=lat_tiles),
        grid=(n_rows // tm,),
        in_specs=[row, lat_row, ctx_row, row, lat_row, ctx_row, ucol(CB_LG), ucol(CB_G), ucol(CB_G + 1),
                  ucol(CB_G + 2), wspec, wspec, wspec, wspec,
                  _mod_spec(2, row_of_tile), pl.BlockSpec((1, d), lambda i: (0, 0)),
                  _mod_spec(3, row_of_tile), _mod_spec(4, row_of_tile)],
        out_specs=[row, row, pl.BlockSpec((tm, d // 2), lambda i: (i, 0))],
        out_shape=[jax.ShapeDtypeStruct((n_rows, d), F32), jax.ShapeDtypeStruct((n_rows, d), BF16),
                   jax.ShapeDtypeStruct((n_rows, d // 2), jnp.uint32)],
        compiler_params=_cparams(("parallel",), 56),
        name="merge_branches",
    )(x_tok, attn_l, attn_c, rec, four_l, four_c, u, u, u, u, w_attn_o, w_lru_o, w_four_o, w_out,
      mod4, norm2_g.reshape(1, d), mod4, mod4)


def _route_kernel(h_ref, rw_ref, rb_ref, tri_ref, idx_ref, wgt_ref, pos_ref, cnt_ref, base_scr):
    tm = h_ref.shape[0]

    @pl.when(pl.program_id(0) == 0)
    def _():
        base_scr[...] = jnp.zeros_like(base_scr)

    logits = lax.dot_general(rw_ref[...], h_ref[...], (((1,), (1,)), ((), ())), preferred_element_type=F32)
    scores = jax.nn.sigmoid(logits)
    biased = scores + rb_ref[...]
    neg = -jnp.inf
    e_iota = lax.broadcasted_iota(jnp.int32, (N_EXPERTS, tm), 0)
    g_iota = lax.broadcasted_iota(jnp.int32, (N_GROUPS, tm), 0)

    rows = []
    for g in range(N_GROUPS):
        xg = biased[g * GROUP_SIZE:(g + 1) * GROUP_SIZE]
        m1 = jnp.max(xg, axis=0, keepdims=True)
        is_max = xg == m1
        n_max = jnp.sum(is_max.astype(F32), axis=0, keepdims=True)
        m2 = jnp.max(jnp.where(is_max, neg, xg), axis=0, keepdims=True)
        rows.append(m1 + jnp.where(n_max >= 2.0, m1, m2))
    grp = jnp.concatenate(rows, axis=0)

    g_sel = jnp.zeros((N_GROUPS, tm), jnp.bool_)
    work = grp
    for _ in range(TOPK_GROUPS):
        m = jnp.max(work, axis=0, keepdims=True)
        first = jnp.min(jnp.where(work == m, g_iota, N_GROUPS), axis=0, keepdims=True)
        hit = g_iota == first
        g_sel = jnp.logical_or(g_sel, hit)
        work = jnp.where(hit, neg, work)
    g_sel_f = g_sel.astype(F32)
    mask_rows = [jnp.broadcast_to(g_sel_f[g:g + 1], (GROUP_SIZE, tm)) for g in range(N_GROUPS)]
    e_mask = jnp.concatenate(mask_rows, axis=0) > 0.5

    work = jnp.where(e_mask, biased, neg)
    sel = jnp.zeros((N_EXPERTS, tm), F32)
    idx_rows, w_rows, hits = [], [], []
    for _ in range(TOP_K):
        m = jnp.max(work, axis=0, keepdims=True)
        first = jnp.min(jnp.where(work == m, e_iota, N_EXPERTS), axis=0, keepdims=True)
        hit = e_iota == first
        hits.append(hit)
        idx_rows.append(first)
        w_rows.append(jnp.sum(jnp.where(hit, scores, 0.0), axis=0, keepdims=True))
        sel = jnp.where(hit, 1.0, sel)
        work = jnp.where(hit, neg, work)
    w = jnp.concatenate(w_rows, axis=0)
    wgt_ref[...] = w / jnp.sum(w, axis=0, keepdims=True) * ROUTED_SCALE
    idx_ref[...] = jnp.concatenate(idx_rows, axis=0)

    before = jnp.dot(sel.astype(BF16), tri_ref[...], preferred_element_type=F32)
    rank = base_scr[...] + before
    pos_rows = [jnp.sum(jnp.where(hit, rank, 0.0), axis=0, keepdims=True) for hit in hits]
    pos_ref[...] = jnp.concatenate(pos_rows, axis=0).astype(jnp.int32)
    base_scr[...] = base_scr[...] + jnp.sum(sel, axis=1, keepdims=True)
    cnt_ref[...] = jnp.broadcast_to(base_scr[...], cnt_ref.shape)


def _route(h16, rw_t, router_b, tm):
    n_tok, d = h16.shape
    tri = (jnp.arange(tm)[:, None] < jnp.arange(tm)[None, :]).astype(BF16)
    out_col = pl.BlockSpec((TOP_K, tm), lambda i: (0, i))
    return pl.pallas_call(
        _route_kernel,
        grid=(n_tok // tm,),
        in_specs=[pl.BlockSpec((tm, d), lambda i: (i, 0)),
                  pl.BlockSpec((N_EXPERTS, d), lambda i: (0, 0)),
                  pl.BlockSpec((N_EXPERTS, 1), lambda i: (0, 0)),
                  pl.BlockSpec((tm, tm), lambda i: (0, 0))],
        out_specs=[out_col, out_col, out_col, pl.BlockSpec((N_EXPERTS, LANES), lambda i: (0, 0))],
        out_shape=[jax.ShapeDtypeStruct((TOP_K, n_tok), jnp.int32), jax.ShapeDtypeStruct((TOP_K, n_tok), F32),
                   jax.ShapeDtypeStruct((TOP_K, n_tok), jnp.int32),
                   jax.ShapeDtypeStruct((N_EXPERTS, LANES), F32)],
        scratch_shapes=[pltpu.VMEM((N_EXPERTS, 1), F32)],
        compiler_params=_cparams(("arbitrary",), 32),
        name="moe_router",
    )(h16, rw_t, router_b.reshape(N_EXPERTS, 1), tri)


def _slot_kernel(idx_ref, pos_ref, offs_ref, dest_ref):
    tm = idx_ref.shape[1]
    e_iota = lax.broadcasted_iota(jnp.int32, (N_EXPERTS, tm), 0)
    idx = idx_ref[...]
    offs = offs_ref[...]
    rows = [jnp.sum(jnp.where(e_iota == idx[r:r + 1], offs, 0), axis=0, keepdims=True) for r in range(TOP_K)]
    dest_ref[...] = jnp.concatenate(rows, axis=0) + pos_ref[...]


def _slots(top_idx, pos, offs, tm):
    n_tok = top_idx.shape[1]
    col = pl.BlockSpec((TOP_K, tm), lambda i: (0, i))
    return pl.pallas_call(
        _slot_kernel,
        grid=(n_tok // tm,),
        in_specs=[col, col, pl.BlockSpec((N_EXPERTS, 1), lambda i: (0, 0))],
        out_specs=col,
        out_shape=jax.ShapeDtypeStruct((TOP_K, n_tok), jnp.int32),
        compiler_params=_cparams(("parallel",), 32),
        name="moe_slots",
    )(top_idx, pos, offs.reshape(N_EXPERTS, 1))


def _row_copy(src_ref, src_row, dst_ref, dst_row, sem):
    return pltpu.make_async_copy(src_ref.at[pl.ds(src_row, 1)], dst_ref.at[pl.ds(dst_row, 1)], sem)


def _dispatch_kernel(dest_ref, hp_ref, _xs_in, xs_ref, sem):
    tm = hp_ref.shape[0]

    def start(t, c):
        for r in range(TOP_K):
            _row_copy(hp_ref, t, xs_ref, dest_ref[r, t], sem).start()
        return c

    lax.fori_loop(0, tm, start, 0)

    def wait(t, c):
        for r in range(TOP_K):
            _row_copy(hp_ref, t, xs_ref, dest_ref[r, t], sem).wait()
        return c

    lax.fori_loop(0, tm, wait, 0)


def _dispatch(hp, dest3, n_slots):
    n_tok, half = hp.shape
    tm = dest3.shape[2]
    xs0 = jnp.zeros((n_slots, half), jnp.uint32)
    return pl.pallas_call(
        _dispatch_kernel,
        grid=(n_tok // tm,),
        in_specs=[pl.BlockSpec((None, TOP_K, tm), lambda i: (i, 0, 0), memory_space=pltpu.SMEM),
                  pl.BlockSpec((tm, half), lambda i: (i, 0)),
                  pl.BlockSpec(memory_space=pl.ANY)],
        out_specs=pl.BlockSpec(memory_space=pl.ANY),
        out_shape=jax.ShapeDtypeStruct((n_slots, half), jnp.uint32),
        scratch_shapes=[pltpu.SemaphoreType.DMA(())],
        input_output_aliases={2: 0},
        compiler_params=_cparams(("arbitrary",), 32),
        name="moe_dispatch",
    )(dest3, hp, xs0)


def _expert_kernel(be_ref, nu_ref, xs_ref, wg_ref, wu_ref, wd_ref, ys_ref, wg_s, wu_s, wd_s):
    i = pl.program_id(0)
    used = i < nu_ref[0]
    prev_e = be_ref[jnp.maximum(i - 1, 0)]
    fresh = jnp.logical_or(i == 0, be_ref[i] != prev_e)

    @pl.when(jnp.logical_and(used, fresh))
    def _():
        wg_s[...] = wg_ref[0].astype(BF16)
        wu_s[...] = wu_ref[0].astype(BF16)
        wd_s[...] = wd_ref[0].astype(BF16)

    @pl.when(used)
    def _():
        w = xs_ref[...]
        half = D_MODEL // 2
        xa = pltpu.bitcast(w & jnp.uint32(0xFFFF0000), F32).astype(BF16)
        xb = pltpu.bitcast(w << 16, F32).astype(BF16)

        def proj(w_s):
            return (jnp.dot(xa, w_s[:half, :], preferred_element_type=F32)
                    + jnp.dot(xb, w_s[half:, :], preferred_element_type=F32))

        g = proj(wg_s)
        act = (_silu(g) * proj(wu_s)).astype(BF16)
        ys_ref[...] = jnp.dot(act, wd_s[...], preferred_element_type=F32)

    @pl.when(jnp.logical_not(used))
    def _():
        ys_ref[...] = jnp.zeros_like(ys_ref)


def _experts(xs, block_e, n_used, w_gate, w_up, w_down, layer, bm):
    n_slots, half = xs.shape
    d = D_MODEL
    grid_spec = pltpu.PrefetchScalarGridSpec(
        num_scalar_prefetch=2,
        grid=(n_slots // bm,),
        in_specs=[pl.BlockSpec((bm, half), lambda i, be, nu: (i, 0)),
                  pl.BlockSpec((None, 1, d, EXPERT_FF), lambda i, be, nu: (layer, be[i], 0, 0)),
                  pl.BlockSpec((None, 1, d, EXPERT_FF), lambda i, be, nu: (layer, be[i], 0, 0)),
                  pl.BlockSpec((None, 1, EXPERT_FF, d), lambda i, be, nu: (layer, be[i], 0, 0))],
        out_specs=pl.BlockSpec((bm, d), lambda i, be, nu: (i, 0)),
        scratch_shapes=[pltpu.VMEM((d, EXPERT_FF), BF16), pltpu.VMEM((d, EXPERT_FF), BF16),
                        pltpu.VMEM((EXPERT_FF, d), BF16)],
    )
    return pl.pallas_call(
        _expert_kernel,
        grid_spec=grid_spec,
        out_shape=jax.ShapeDtypeStruct((n_slots, d), F32),
        compiler_params=_cparams(("arbitrary",), 40),
        name="moe_experts",
    )(block_e, n_used, xs, w_gate, w_up, w_down)


def _combine_kernel(dest_ref, x_ref, h_ref, tw_ref, gate_ref, wsg_ref, wsu_ref, wsd_ref, ys_ref, o_ref, buf, sem):
    tm = x_ref.shape[0]

    def start(t, c):
        for r in range(TOP_K):
            _row_copy(ys_ref, dest_ref[r, t], buf.at[r], t, sem).start()
        return c

    lax.fori_loop(0, tm, start, 0)

    h = h_ref[...]
    act = (_silu(jnp.dot(h, wsg_ref[...], preferred_element_type=F32))
           * jnp.dot(h, wsu_ref[...], preferred_element_type=F32)).astype(BF16)
    shared = jnp.dot(act, wsd_ref[...], preferred_element_type=F32)

    def wait(t, c):
        for r in range(TOP_K):
            _row_copy(ys_ref, dest_ref[r, t], buf.at[r], t, sem).wait()
        return c

    lax.fori_loop(0, tm, wait, 0)

    tw = tw_ref[...]
    moe = buf[0] * tw[:, 0:1]
    for r in range(1, TOP_K):
        moe = moe + buf[r] * tw[:, r:r + 1]
    o_ref[...] = x_ref[...] + gate_ref[...] * (moe + shared)


def _combine(x_tok, h16, ys, dest3, top_w, mod4, ws_gate, ws_up, ws_down, row_of_tile):
    n_tok, d = x_tok.shape
    tm = dest3.shape[2]
    ff = ws_gate.shape[1]
    row = pl.BlockSpec((tm, d), lambda i: (i, 0))
    return pl.pallas_call(
        _combine_kernel,
        grid=(n_tok // tm,),
        in_specs=[pl.BlockSpec((None, TOP_K, tm), lambda i: (i, 0, 0), memory_space=pltpu.SMEM),
                  row, row,
                  pl.BlockSpec((tm, TOP_K), lambda i: (i, 0)),
                  _mod_spec(5, row_of_tile),
                  pl.BlockSpec((d, ff), lambda i: (0, 0)),
                  pl.BlockSpec((d, ff), lambda i: (0, 0)),
                  pl.BlockSpec((ff, d), lambda i: (0, 0)),
                  pl.BlockSpec(memory_space=pl.ANY)],
        out_specs=row,
        out_shape=jax.ShapeDtypeStruct((n_tok, d), F32),
        scratch_shapes=[pltpu.VMEM((TOP_K, tm, d), F32), pltpu.SemaphoreType.DMA(())],
        compiler_params=_cparams(("arbitrary",), 48),
        name="moe_combine",
    )(dest3, x_tok, h16, top_w, mod4, ws_gate, ws_up, ws_down, ys)


def _moe(x_tok, h16, hp, mod4, row_of_tile, rw_t, router_b, w_gate, w_up, w_down, layer, ws_gate, ws_up, ws_down):
    n_tok = x_tok.shape[0]
    bm = MOE_BM
    tm_r = TM_ROUTE if n_tok % TM_ROUTE == 0 else TM
    top_idx, top_w, pos, cnt = _route(h16, rw_t, router_b, tm_r)

    counts = cnt[:, 0].astype(jnp.int32)
    padded = (counts + bm - 1) // bm * bm
    ends = jnp.cumsum(padded)
    n_blocks = -(-(n_tok * TOP_K + N_EXPERTS * (bm - 1)) // bm)
    dest = _slots(top_idx, pos, ends - padded, tm_r)
    block_e = jnp.minimum(jnp.searchsorted(ends, jnp.arange(n_blocks, dtype=jnp.int32) * bm, side='right'),
                          N_EXPERTS - 1).astype(jnp.int32)
    n_used = (ends[-1] // bm).astype(jnp.int32).reshape(1)
    tm = TM
    dest3 = dest.reshape(TOP_K, n_tok // tm, tm).transpose(1, 0, 2)

    xs = _dispatch(hp, dest3, n_blocks * bm)
    ys = _experts(xs, block_e, n_used, w_gate, w_up, w_down, layer, bm)
    return _combine(x_tok, h16, ys, dest3, top_w.T, mod4, ws_gate, ws_up, ws_down, row_of_tile)


def _block_diag_tiles(w):
    per = MXU_DIM // LRU_BLOCK_DIM
    w = w.reshape(2, D_MODEL // MXU_DIM, per, LRU_BLOCK_DIM, LRU_BLOCK_DIM)
    eye = jnp.eye(per, dtype=w.dtype)
    t = jnp.einsum('ztpde,pq->ztpdqe', w, eye)
    return t.reshape(2, D_MODEL // MXU_DIM, MXU_DIM, MXU_DIM).astype(BF16)


def _rope_tables(n_lat, n_ctx):
    rows = n_lat // GRID_W
    row = jnp.repeat(jnp.arange(rows), GRID_W)
    col = jnp.tile(jnp.arange(GRID_W), rows)
    inv = ROPE_THETA ** (-jnp.arange(ROPE_FREQS, dtype=F32) / ROPE_FREQS)
    ang = jnp.stack([row[:, None] * inv, col[:, None] * inv], axis=1)
    ang = jnp.concatenate([ang, ang], axis=2).reshape(n_lat, HEAD_DIM)
    ang = jnp.tile(ang, (1, LANES // HEAD_DIM))
    ang = jnp.concatenate([ang, jnp.zeros((n_ctx, LANES), F32)], axis=0)
    lane = jnp.arange(LANES)
    sign = jnp.where((lane % (2 * ROPE_FREQS)) < ROPE_FREQS, -1.0, 1.0).astype(F32)
    return jnp.cos(ang), jnp.sin(ang) * sign


def kernel(x, c, ctx, c_ctx, w_mod, b_mod, norm1_g, w_in, q_norm_g, k_norm_g, lambda_qk, subln_g, w_attn_o, conv_w, conv_b, lru_wa, lru_ba, lru_wi, lru_bi, lru_lambda, w_lru_o, w_four_o, w_out, norm2_g, router_w, router_b, exp_w_gate, exp_w_up, exp_w_down, sh_w_gate, sh_w_up, sh_w_down):
    bsz, n_lat, d = x.shape
    n_ctx = ctx.shape[1]
    depth = w_mod.shape[0]
    n_latent_rows = bsz * n_lat
    n_tok = n_latent_rows + bsz * n_ctx
    assert d == D_MODEL and bsz + 1 <= MOD_ROWS
    assert n_lat % TM == 0 and n_ctx % TM == 0 and n_lat % GRID_W == 0
    tm_in = TM_IN if (n_lat % TM_IN == 0 and (bsz * n_ctx) % TM_IN == 0) else TM

    cvec = jnp.zeros((MOD_ROWS, d), F32).at[:bsz].set(c).at[bsz].set(c_ctx)
    mods = _modulation(cvec, w_mod, b_mod)

    def row_of_tile_fn(tm):
        per_batch = n_lat // tm
        return lambda i: jnp.minimum(i // per_batch, bsz)

    assert n_lat % n_ctx == 0
    cos_t, sin_t = _rope_tables(n_lat, n_ctx)
    group_mat = jnp.kron(jnp.eye(d // HEAD_DIM, dtype=F32), jnp.ones((HEAD_DIM, HEAD_DIM), F32)).astype(BF16)
    m = jnp.arange(FOUR_GROUP_DIM, dtype=jnp.int32)
    ang_c = ((m[:, None] * m[None, :]) % FOUR_GROUP_DIM).astype(F32) * (2.0 * math.pi / FOUR_GROUP_DIM)
    inv_sqrt_c = FOUR_GROUP_DIM ** -0.5
    cs_mat = jnp.concatenate([jnp.cos(ang_c) * inv_sqrt_c, jnp.sin(ang_c) * inv_sqrt_c], axis=1).astype(BF16)
    assert n_lat % (FFT_RADIX * MXU_DIM) == 0 or n_lat // FFT_RADIX < MXU_DIM
    fft_tk1 = min(MXU_DIM, n_lat // FFT_RADIX)
    fft_cc, fft_pc, fft_ps = _fft_tables(n_lat, FFT_RADIX, fft_tk1)
    ct_ctx, st_ctx = _dft_tables(n_ctx)
    q_sub = math.gcd(Q_SUB, n_lat // TQ)
    n_keys = n_ctx + n_lat
    tk = TK if n_keys % TK == 0 else TM

    x_tok = jnp.concatenate([x.reshape(n_latent_rows, d), ctx.reshape(bsz * n_ctx, d)], axis=0)
    for l in range(depth):
        last = l == depth - 1
        lam_init = 0.8 - 0.6 * math.exp(-0.3 * l)
        mod4 = mods[l].reshape(MOD_ROWS, 6, 1, d)

        u = _in_projection(x_tok, mod4, norm1_g[l], w_in[l].astype(BF16), row_of_tile_fn(tm_in), tm_in)

        k_all, q_all, vt_all = _qkv_prepare(u, k_norm_g[l], q_norm_g[l], group_mat, cos_t, sin_t,
                                            bsz, n_lat, n_ctx)
        attn_l = _diff_attention(q_all, k_all, vt_all, lambda_qk[l], subln_g[l], lam_init,
                                 0, n_lat, 0, n_keys, TQ, q_sub, tk)
        attn_c = attn_l
        if not last:
            attn_c = _diff_attention(q_all, k_all, vt_all, lambda_qk[l], subln_g[l], lam_init,
                                     n_lat, n_ctx, n_lat, n_ctx, TM, 1, TM)

        rec = _rglru(u, conv_w[l], conv_b[l], _block_diag_tiles(lru_wa[l]), lru_ba[l],
                     _block_diag_tiles(lru_wi[l]), lru_bi[l], lru_lambda[l], bsz, n_lat, n_ctx)

        y_lat = _chan_dft_split(u, cs_mat, bsz, n_lat, FFT_RADIX)
        four_l = _time_fft(y_lat, fft_cc, fft_pc, fft_ps, FFT_RADIX, fft_tk1, MXU_DIM).reshape(n_latent_rows, d)
        four_c = four_l
        if not last:
            y_ctx = _chan_dft(u, cs_mat, n_latent_rows, bsz * n_ctx)
            four_c = _time_dft(y_ctx, ct_ctx, st_ctx, bsz, n_ctx, 0, n_ctx)

        n_rows = n_latent_rows if last else n_tok
        row_of_tile = row_of_tile_fn(TM)
        x_mid, h16, hp = _merge(x_tok, attn_l, attn_c, rec, four_l, four_c, u, w_attn_o[l].astype(BF16),
                                w_lru_o[l].astype(BF16), w_four_o[l].astype(BF16), w_out[l].astype(BF16),
                                mod4, norm2_g[l], row_of_tile, n_rows)
        x_tok = _moe(x_mid, h16, hp, mod4, row_of_tile, router_w[l].T.astype(BF16), router_b[l],
                     exp_w_gate, exp_w_up, exp_w_down, l,
                     sh_w_gate[l].astype(BF16), sh_w_up[l].astype(BF16), sh_w_down[l].astype(BF16))
    return x_tok[:n_latent_rows].reshape(bsz, n_lat, d)
```

```python
import functools
import math

import jax
import jax.numpy as jnp
from jax import lax
from jax.experimental import pallas as pl
from jax.experimental.pallas import tpu as pltpu

F32 = jnp.float32
BF16 = jnp.bfloat16

D_MODEL = 1024
EPS = 1e-6
GRID_W = 64

N_HEADS = 8
HEAD_DIM = 64
V_DIM = 2 * HEAD_DIM
ATTN_SCALE = HEAD_DIM ** -0.5
ROPE_THETA = 10000.0
ROPE_FREQS = HEAD_DIM // 4

LRU_BLOCKS = 16
LRU_BLOCK_DIM = D_MODEL // LRU_BLOCKS
LRU_C = 8.0
CONV_W = 4

FOUR_GROUPS = 4
FOUR_GROUP_DIM = D_MODEL // FOUR_GROUPS

CB_K, CB_V, CB_LX, CB_Q, CB_LG, CB_F, CB_G = 0, 1, 2, 3, 4, 5, 6
IN_BLOCKS = 9

N_EXPERTS = 256
TOP_K = 8
N_GROUPS = 8
GROUP_SIZE = N_EXPERTS // N_GROUPS
TOPK_GROUPS = 4
EXPERT_FF = 256
ROUTED_SCALE = 2.5

LANES = 128
SUBLANES = 8
MXU_DIM = 256
VMEM_BYTES_V7X = 64 * 1024 * 1024

MOD_ROWS = 8
TM = 256
TM_IN = 512
TQ = 256
Q_SUB = 32
TK = 768
TM_ROUTE = 512
MOE_BM = 256
FFT_RADIX = 8
LOG2E = 1.4426950408889634


def _cparams(sem, vmem_mb):
    return pltpu.CompilerParams(dimension_semantics=sem, vmem_limit_bytes=int(vmem_mb * 1024 * 1024))


def _silu(x):
    return x * jax.nn.sigmoid(x)


def _pack_halves(x):
    half = x.shape[1] // 2
    bits = pltpu.bitcast(x.astype(BF16).astype(F32), jnp.uint32)
    return (bits[:, :half] & jnp.uint32(0xFFFF0000)) | (bits[:, half:] >> 16)


def _unpack_halves(w):
    return pltpu.bitcast(w & jnp.uint32(0xFFFF0000), F32), pltpu.bitcast(w << 16, F32)


def _mod_kernel(c_ref, w_ref, b_ref, o_ref):
    c = c_ref[...]
    s = _silu(c).astype(BF16)
    o_ref[0] = jnp.dot(s, w_ref[0].astype(BF16), preferred_element_type=F32) + b_ref[0]


def _modulation(cvec, w_mod, b_mod):
    n_layers, d, six_d = w_mod.shape
    nj = six_d // d
    return pl.pallas_call(
        _mod_kernel,
        grid=(n_layers, nj),
        in_specs=[
            pl.BlockSpec((MOD_ROWS, d), lambda l, j: (0, 0)),
            pl.BlockSpec((1, d, d), lambda l, j: (l, 0, j)),
            pl.BlockSpec((1, 1, d), lambda l, j: (l, 0, j)),
        ],
        out_specs=pl.BlockSpec((1, MOD_ROWS, d), lambda l, j: (l, 0, j)),
        out_shape=jax.ShapeDtypeStruct((n_layers, MOD_ROWS, six_d), F32),
        compiler_params=_cparams(("parallel", "parallel"), 32),
        name="modulation",
    )(cvec, w_mod, b_mod.reshape(n_layers, 1, six_d))


def _mod_spec(part, row_of_tile, n_grid_axes=1):
    if n_grid_axes == 1:
        return pl.BlockSpec((None, None, 1, D_MODEL), lambda i: (row_of_tile(i), part, 0, 0))
    return pl.BlockSpec((None, None, 1, D_MODEL), lambda i, j: (row_of_tile(i), part, 0, 0))


def _inproj_kernel(x_ref, g_ref, sh_ref, sc_ref, w_ref, o_ref, h_scr):
    @pl.when(pl.program_id(1) == 0)
    def _():
        x = x_ref[...]
        y = x * lax.rsqrt(jnp.mean(x * x, axis=-1, keepdims=True) + EPS)
        h = (y * g_ref[...]) * (1.0 + sc_ref[...]) + sh_ref[...]
        h_scr[...] = h.astype(BF16)

    o_ref[...] = jnp.dot(h_scr[...], w_ref[...], preferred_element_type=F32)


def _in_projection(x_tok, mod4, norm_g, w_in_bf16, row_of_tile, tm):
    n_tok, d = x_tok.shape
    n_cols = w_in_bf16.shape[1]
    return pl.pallas_call(
        _inproj_kernel,
        grid=(n_tok // tm, n_cols // d),
        in_specs=[
            pl.BlockSpec((tm, d), lambda i, j: (i, 0)),
            pl.BlockSpec((1, d), lambda i, j: (0, 0)),
            _mod_spec(0, row_of_tile, 2),
            _mod_spec(1, row_of_tile, 2),
            pl.BlockSpec((d, d), lambda i, j: (0, j)),
        ],
        out_specs=pl.BlockSpec((tm, d), lambda i, j: (i, j)),
        out_shape=jax.ShapeDtypeStruct((n_tok, n_cols), F32),
        scratch_shapes=[pltpu.VMEM((tm, d), BF16)],
        compiler_params=_cparams(("parallel", "arbitrary"), 40),
        name="in_projection",
    )(x_tok, norm_g.reshape(1, d), mod4, mod4, w_in_bf16)


def _prep_kernel(uk_ref, uq_ref, uv_ref, kg_ref, qg_ref, gm_ref, cos_ref, sin_ref, k_out, q_out, vt_out, *, tm):
    gm = gm_ref[...]
    lane = lax.broadcasted_iota(jnp.int32, (tm, LANES), 1)
    first_half = (lane % (2 * ROPE_FREQS)) < ROPE_FREQS

    def norm_rope(x, g):
        ss = jnp.dot((x * x).astype(BF16), gm, preferred_element_type=F32) * (1.0 / HEAD_DIM)
        y = x * lax.rsqrt(ss + EPS) * g
        c = cos_ref[...]
        s = sin_ref[...]
        outs = []
        for j in range(D_MODEL // LANES):
            yt = y[:, j * LANES:(j + 1) * LANES]
            rot = jnp.where(first_half, pltpu.roll(yt, LANES - ROPE_FREQS, 1), pltpu.roll(yt, ROPE_FREQS, 1))
            outs.append(yt * c + rot * s)
        return jnp.concatenate(outs, axis=1)

    k_out[...] = norm_rope(uk_ref[...], kg_ref[...]).astype(BF16)
    qn = norm_rope(uq_ref[...], qg_ref[...]) * (ATTN_SCALE * LOG2E)
    lo = lane < HEAD_DIM
    for h in range(N_HEADS):
        qt = qn[:, h * LANES:(h + 1) * LANES]
        q_out[:, (2 * h) * LANES:(2 * h + 1) * LANES] = jnp.where(lo, qt, 0.0).astype(BF16)
        q_out[:, (2 * h + 1) * LANES:(2 * h + 2) * LANES] = jnp.where(lo, 0.0, qt).astype(BF16)
    vt_out[...] = uv_ref[...].T.astype(BF16)


def _qkv_prepare(u, k_gain, q_gain, group_mat, cos_t, sin_t, bsz, n_lat, n_ctx):
    tm = TM
    d = D_MODEL
    n_s, n_c = n_lat // tm, n_ctx // tm
    n_rows = n_lat + n_ctx
    lat_tiles = bsz * n_s
    kg = jnp.tile(k_gain, d // HEAD_DIM).reshape(1, d)
    qg = jnp.tile(q_gain, d // HEAD_DIM).reshape(1, d)

    def tok_tile(b, j):
        return jnp.where(j < n_s, b * n_s + j, lat_tiles + b * n_c + (j - n_s))

    def u_spec(cb):
        return pl.BlockSpec((tm, d), lambda b, j: (tok_tile(b, j), cb))

    vec = pl.BlockSpec((1, d), lambda b, j: (0, 0))
    rope_spec = pl.BlockSpec((tm, LANES), lambda b, j: (j, 0))
    return pl.pallas_call(
        functools.partial(_prep_kernel, tm=tm),
        grid=(bsz, n_s + n_c),
        in_specs=[u_spec(CB_K), u_spec(CB_Q), u_spec(CB_V), vec, vec,
                  pl.BlockSpec((d, d), lambda b, j: (0, 0)), rope_spec, rope_spec],
        out_specs=[
            pl.BlockSpec((None, tm, d), lambda b, j: (b, j, 0)),
            pl.BlockSpec((None, tm, 2 * d), lambda b, j: (b, j, 0)),
            pl.BlockSpec((None, d, tm), lambda b, j: (b, 0, j)),
        ],
        out_shape=[jax.ShapeDtypeStruct((bsz, n_rows, d), BF16),
                   jax.ShapeDtypeStruct((bsz, n_rows, 2 * d), BF16),
                   jax.ShapeDtypeStruct((bsz, d, n_rows), BF16)],
        compiler_params=_cparams(("parallel", "parallel"), 40),
        name="qkv_prepare",
    )(u, u, u, kg, qg, group_mat, cos_t, sin_t)


def _attn_kernel(lq_ref, sg_ref, q1_ref, q2_ref, k_ref, vt_ref, o_ref, s_buf, *, n_keys, tk, tq, lam_init):
    lq = lq_ref[...]
    lam = (jnp.exp(jnp.sum(lq[0:1] * lq[1:2], axis=1, keepdims=True))
           - jnp.exp(jnp.sum(lq[2:3] * lq[3:4], axis=1, keepdims=True)) + lam_init)
    n_chunks = n_keys // tk
    n_sub = q1_ref.shape[0] // tq

    def load_q(j):
        rows = pl.ds(pl.multiple_of(j * tq, tq), tq)
        return jnp.concatenate([q1_ref[rows, :], q2_ref[rows, :]], axis=0)

    def score_chunk(q, c, mx):
        s = lax.dot_general(k_ref[c * tk:(c + 1) * tk, :], q, (((1,), (1,)), ((), ())),
                            preferred_element_type=F32)
        s_buf[c * tk:(c + 1) * tk, :] = s
        cm = jnp.max(s, axis=0, keepdims=True)
        return cm if mx is None else jnp.maximum(mx, cm)

    q0 = load_q(0)
    mx0 = None
    for c in range(n_chunks):
        mx0 = score_chunk(q0, c, mx0)

    def sub_tile(j, mx):
        q_next = load_q(jnp.minimum(j + 1, n_sub - 1))
        mx_next = None
        l = jnp.zeros((1, 2 * tq), F32)
        acc = jnp.zeros((V_DIM, 2 * tq), F32)
        for c in range(n_chunks):
            p = jnp.exp2(s_buf[c * tk:(c + 1) * tk, :] - mx)
            l = l + jnp.sum(p, axis=0, keepdims=True)
            acc = acc + jnp.dot(vt_ref[:, c * tk:(c + 1) * tk], p.astype(BF16), preferred_element_type=F32)
            mx_next = score_chunk(q_next, c, mx_next)
        o = acc / l
        o = o[:, :tq] - lam * o[:, tq:]
        ms = jnp.mean(o * o, axis=0, keepdims=True)
        on = (o * lax.rsqrt(ms + EPS) * sg_ref[...]) * (1.0 - lam_init)
        o_ref[pl.ds(pl.multiple_of(j * tq, tq), tq), :] = on.T
        return mx_next

    lax.fori_loop(0, n_sub, sub_tile, mx0)


def _diff_attention(q_all, k_all, vt_all, lambda_qk, subln_g, lam_init, q_row0, n_q, key_row0, n_keys, tq, n_sub, tk):
    bsz = q_all.shape[0]
    d = D_MODEL
    tstep = tq * n_sub
    nq = n_q // tstep
    qb0 = q_row0 // tstep
    kb0 = key_row0 // n_keys
    s_bytes = n_keys * 2 * tq * 4
    return pl.pallas_call(
        functools.partial(_attn_kernel, n_keys=n_keys, tk=tk, tq=tq, lam_init=lam_init),
        grid=(bsz, N_HEADS, nq),
        in_specs=[
            pl.BlockSpec((4, HEAD_DIM), lambda b, h, i: (0, 0)),
            pl.BlockSpec((V_DIM, 1), lambda b, h, i: (0, 0)),
            pl.BlockSpec((None, tstep, LANES), lambda b, h, i: (b, qb0 + i, 2 * h)),
            pl.BlockSpec((None, tstep, LANES), lambda b, h, i: (b, qb0 + i, 2 * h + 1)),
            pl.BlockSpec((None, n_keys, LANES), lambda b, h, i: (b, kb0, h)),
            pl.BlockSpec((None, V_DIM, n_keys), lambda b, h, i: (b, h, kb0)),
        ],
        out_specs=pl.BlockSpec((tstep, LANES), lambda b, h, i: (b * nq + i, h)),
        out_shape=jax.ShapeDtypeStruct((bsz * n_q, d), F32),
        scratch_shapes=[pltpu.VMEM((n_keys, 2 * tq), F32)],
        compiler_params=_cparams(("parallel", "parallel", "arbitrary"), 28 + s_bytes / 2 ** 20),
        name="diff_attention",
    )(lambda_qk, subln_g.reshape(V_DIM, 1), q_all, q_all, k_all, vt_all)


def _lru_kernel(*refs, rev, n_c, n_s, tm):
    if rev:
        (cur_ref, prev_ref, next_ref, cw_ref, cb_ref, wa_ref, ba_ref, wi_ref, bi_ref, lam_ref, fwd_ref,
         o_ref, a_scr, b_scr, h_scr) = refs
    else:
        (cur_ref, prev_ref, next_ref, cw_ref, cb_ref, wa_ref, ba_ref, wi_ref, bi_ref, lam_ref,
         o_ref, a_scr, b_scr, h_scr) = refs
    s = pl.program_id(1)
    is_ctx = s < n_c
    if rev:
        tile = jnp.where(is_ctx, n_c - 1 - s, n_s - 1 - (s - n_c))
    else:
        tile = jnp.where(is_ctx, s, s - n_c)
    n_tile = jnp.where(is_ctx, n_c, n_s)

    @pl.when(s == 0)
    def _():
        h_scr[...] = jnp.zeros_like(h_scr)

    x = cur_ref[...]
    pv = jnp.where(tile == 0, 0.0, prev_ref[...])
    nx = jnp.where(tile == n_tile - 1, 0.0, next_ref[...])
    r8 = lax.broadcasted_iota(jnp.int32, (SUBLANES, D_MODEL), 0)

    def earlier(k):
        rolled = pltpu.roll(x, k, 0)
        top = jnp.where(r8 < k, pltpu.roll(pv, k, 0), rolled[:SUBLANES])
        return jnp.concatenate([top, rolled[SUBLANES:]], axis=0)

    rolled = pltpu.roll(x, tm - 1, 0)
    bot = jnp.where(r8 >= SUBLANES - 1, pltpu.roll(nx, SUBLANES - 1, 0), rolled[tm - SUBLANES:])
    later = jnp.concatenate([rolled[:tm - SUBLANES], bot], axis=0)
    cw = cw_ref[...]
    conv = earlier(2) * cw[0:1] + earlier(1) * cw[1:2] + x * cw[2:3] + later * cw[3:4] + cb_ref[...]

    conv16 = conv.astype(BF16)

    def block_diag(w_ref, b_ref):
        parts = [jnp.dot(conv16[:, j * MXU_DIM:(j + 1) * MXU_DIM], w_ref[j], preferred_element_type=F32)
                 for j in range(D_MODEL // MXU_DIM)]
        return jnp.concatenate(parts, axis=1) + b_ref[...]

    r = jax.nn.sigmoid(block_diag(wa_ref, ba_ref))
    gate_i = jax.nn.sigmoid(block_diag(wi_ref, bi_ref))
    neg_lam = -lam_ref[...]
    softplus = jnp.maximum(neg_lam, 0.0) + jnp.log1p(jnp.exp(-jnp.abs(neg_lam)))
    log_a = -LRU_C * r * softplus
    a = jnp.exp(log_a)
    mult = jnp.sqrt(1.0 - a * a)
    a_scr[...] = a
    b_scr[...] = mult * gate_i * conv

    def body(t, h):
        row = tm - 1 - t if rev else t
        h = a_scr[pl.ds(row, 1), :] * h + b_scr[pl.ds(row, 1), :]
        if rev:
            o_ref[pl.ds(row, 1), :] = fwd_ref[pl.ds(row, 1), :] + h
        else:
            o_ref[pl.ds(row, 1), :] = h
        return h

    h_scr[...] = lax.fori_loop(0, tm, body, h_scr[...], unroll=8)


def _rglru(u, conv_w, conv_b, wa_bd, ba, wi_bd, bi, lam, bsz, n_lat, n_ctx):
    tm = TM
    d = D_MODEL
    n_tok = u.shape[0]
    n_s, n_c = n_lat // tm, n_ctx // tm
    lat_tiles = bsz * n_s
    per8 = tm // SUBLANES
    last8 = n_tok // SUBLANES - 1

    def make(rev, fwd):
        def blk(b, s):
            is_ctx = s < n_c
            if rev:
                tile = jnp.where(is_ctx, n_c - 1 - s, n_s - 1 - (s - n_c))
            else:
                tile = jnp.where(is_ctx, s, s - n_c)
            return jnp.where(is_ctx, lat_tiles + b * n_c + tile, b * n_s + tile)

        vec = pl.BlockSpec((None, 1, d), lambda b, s: (int(rev), 0, 0))
        wspec = pl.BlockSpec((None, d // MXU_DIM, MXU_DIM, MXU_DIM), lambda b, s: (int(rev), 0, 0, 0))
        row_spec = pl.BlockSpec((tm, d), lambda b, s: (blk(b, s), 0))
        in_specs = [
            pl.BlockSpec((tm, d), lambda b, s: (blk(b, s), CB_LX)),
            pl.BlockSpec((SUBLANES, d), lambda b, s: (jnp.maximum(blk(b, s) * per8 - 1, 0), CB_LX)),
            pl.BlockSpec((SUBLANES, d), lambda b, s: (jnp.minimum((blk(b, s) + 1) * per8, last8), CB_LX)),
            pl.BlockSpec((CONV_W, d), lambda b, s: (0, 0)),
            pl.BlockSpec((1, d), lambda b, s: (0, 0)),
            wspec, vec, wspec, vec, vec,
        ]
        args = [u, u, u, conv_w, conv_b.reshape(1, d), wa_bd, ba.reshape(2, 1, d), wi_bd, bi.reshape(2, 1, d),
                lam.reshape(2, 1, d)]
        if rev:
            in_specs.append(row_spec)
            args.append(fwd)
        return pl.pallas_call(
            functools.partial(_lru_kernel, rev=rev, n_c=n_c, n_s=n_s, tm=tm),
            grid=(bsz, n_c + n_s),
            in_specs=in_specs,
            out_specs=row_spec,
            out_shape=jax.ShapeDtypeStruct((n_tok, d), F32),
            scratch_shapes=[pltpu.VMEM((tm, d), F32), pltpu.VMEM((tm, d), F32), pltpu.VMEM((1, d), F32)],
            input_output_aliases={10: 0} if rev else {},
            compiler_params=_cparams(("arbitrary", "arbitrary"), 40),
            name="rglru_bwd" if rev else "rglru_fwd",
        )(*args)

    fwd = make(False, None)
    return make(True, fwd)


def _chan_dft_kernel(u_ref, cs_ref, o_ref):
    x = u_ref[...].astype(BF16)
    cs = cs_ref[...]
    gd = FOUR_GROUP_DIM
    for g in range(FOUR_GROUPS):
        y = jnp.dot(x[:, g * gd:(g + 1) * gd], cs, preferred_element_type=F32)
        o_ref[:, g * gd:(g + 1) * gd] = y[:, :gd].astype(BF16)
        o_ref[:, D_MODEL + g * gd:D_MODEL + (g + 1) * gd] = y[:, gd:].astype(BF16)


def _chan_dft(u, cs_mat, row0, n_rows):
    tm = TM
    d = D_MODEL
    tile0 = row0 // tm
    return pl.pallas_call(
        _chan_dft_kernel,
        grid=(n_rows // tm,),
        in_specs=[pl.BlockSpec((tm, d), lambda i: (tile0 + i, CB_F)),
                  pl.BlockSpec((FOUR_GROUP_DIM, 2 * FOUR_GROUP_DIM), lambda i: (0, 0))],
        out_specs=pl.BlockSpec((tm, 2 * d), lambda i: (i, 0)),
        out_shape=jax.ShapeDtypeStruct((n_rows, 2 * d), BF16),
        compiler_params=_cparams(("parallel",), 32),
        name="fourier_channels",
    )(u, cs_mat)


def _chan_dft_split_kernel(u_ref, cs_ref, o_ref, y_scr, *, radix):
    x = u_ref[...].astype(BF16)
    cs = cs_ref[...]
    gd = FOUR_GROUP_DIM
    per_g = gd // LANES
    n_cos = D_MODEL // LANES
    for g in range(FOUR_GROUPS):
        y = jnp.dot(x[:, g * gd:(g + 1) * gd], cs, preferred_element_type=F32)
        for j in range(per_g):
            y_scr[g * per_g + j] = y[:, j * LANES:(j + 1) * LANES]
            y_scr[n_cos + g * per_g + j] = y[:, gd + j * LANES:gd + (j + 1) * LANES]
    rows = u_ref.shape[0] // radix
    for r in range(radix):
        for cb in range(2 * n_cos):
            o_ref[r, :, cb * LANES:(cb + 1) * LANES] = y_scr[cb, pl.ds(r, rows, stride=radix), :].astype(BF16)


def _chan_dft_split(u, cs_mat, bsz, n_lat, radix):
    tm = TM
    d = D_MODEL
    n_s = n_lat // tm
    return pl.pallas_call(
        functools.partial(_chan_dft_split_kernel, radix=radix),
        grid=(bsz, n_s),
        in_specs=[pl.BlockSpec((tm, d), lambda b, i: (b * n_s + i, CB_F)),
                  pl.BlockSpec((FOUR_GROUP_DIM, 2 * FOUR_GROUP_DIM), lambda b, i: (0, 0))],
        out_specs=pl.BlockSpec((None, radix, tm // radix, 2 * d), lambda b, i: (b, 0, i, 0)),
        out_shape=jax.ShapeDtypeStruct((bsz, radix, n_lat // radix, 2 * d), BF16),
        scratch_shapes=[pltpu.VMEM((2 * d // LANES, tm, LANES), F32)],
        compiler_params=_cparams(("parallel", "parallel"), 32),
        name="fourier_channels_split",
    )(u, cs_mat)


def _time_fft_kernel(zr_ref, zi_ref, cc_ref, pc_ref, ps_ref, o_ref, *, radix):
    cc = cc_ref[...]
    tk1 = cc.shape[0] // 2
    for r in range(radix):
        pr = jnp.dot(cc, zr_ref[r], preferred_element_type=F32)
        pi = jnp.dot(cc, zi_ref[r], preferred_element_type=F32)
        a_re = pr[:tk1] - pi[tk1:]
        a_im = pr[tk1:] + pi[:tk1]
        for k2 in range(radix):
            term = pc_ref[k2][:, r:r + 1] * a_re - ps_ref[k2][:, r:r + 1] * a_im
            if r == 0:
                o_ref[k2] = term
            else:
                o_ref[k2] += term


def _time_fft(yp, cc, pc, ps, radix, tk1, tn):
    bsz, _, t1, two_d = yp.shape
    d = two_d // 2
    nj = d // tn
    return pl.pallas_call(
        functools.partial(_time_fft_kernel, radix=radix),
        grid=(bsz, nj, t1 // tk1),
        in_specs=[
            pl.BlockSpec((None, radix, t1, tn), lambda b, j, i: (b, 0, 0, j)),
            pl.BlockSpec((None, radix, t1, tn), lambda b, j, i: (b, 0, 0, nj + j)),
            pl.BlockSpec((None, 2 * tk1, t1), lambda b, j, i: (i, 0, 0)),
            pl.BlockSpec((radix, tk1, radix), lambda b, j, i: (0, i, 0)),
            pl.BlockSpec((radix, tk1, radix), lambda b, j, i: (0, i, 0)),
        ],
        out_specs=pl.BlockSpec((None, radix, tk1, tn), lambda b, j, i: (b, 0, i, j)),
        out_shape=jax.ShapeDtypeStruct((bsz, radix, t1, d), F32),
        compiler_params=_cparams(("parallel", "parallel", "arbitrary"), 48),
        name="fourier_positions_fft",
    )(yp, yp, cc, pc, ps)


def _fft_tables(n, radix, tk1):
    t1 = n // radix
    k1 = jnp.arange(t1, dtype=jnp.int32)
    ang = ((k1[:, None] * k1[None, :]) % t1).astype(F32) * (2.0 * math.pi / t1)
    c = jnp.cos(ang).reshape(t1 // tk1, tk1, t1)
    s = jnp.sin(ang).reshape(t1 // tk1, tk1, t1)
    cc = jnp.concatenate([c, s], axis=1).astype(BF16)
    k2 = jnp.arange(radix, dtype=jnp.int32)
    r = jnp.arange(radix, dtype=jnp.int32)
    k = k1[None, :, None] + t1 * k2[:, None, None]
    ph = ((k * r[None, None, :]) % n).astype(F32) * (2.0 * math.pi / n)
    scale = float(n) ** -0.5
    return cc, jnp.cos(ph) * scale, jnp.sin(ph) * scale


def _time_dft_kernel(ct_ref, st_ref, yc_ref, ys_ref, o_ref, acc, *, scale):
    kk = pl.program_id(2)

    @pl.when(kk == 0)
    def _():
        acc[...] = jnp.zeros_like(acc)

    acc[...] += (jnp.dot(ct_ref[...], yc_ref[...], preferred_element_type=F32)
                 - jnp.dot(st_ref[...], ys_ref[...], preferred_element_type=F32))

    @pl.when(kk == pl.num_programs(2) - 1)
    def _():
        o_ref[...] = acc[...] * scale


def _time_dft(y, ct, st, bsz, n_pos, row0, tmk):
    d = D_MODEL
    nt = n_pos // tmk
    rb0 = row0 // tmk
    return pl.pallas_call(
        functools.partial(_time_dft_kernel, scale=float(n_pos) ** -0.5),
        grid=(bsz, nt, nt),
        in_specs=[
            pl.BlockSpec((tmk, tmk), lambda b, i, k: (i, k)),
            pl.BlockSpec((tmk, tmk), lambda b, i, k: (i, k)),
            pl.BlockSpec((tmk, d), lambda b, i, k: (rb0 + b * nt + k, 0)),
            pl.BlockSpec((tmk, d), lambda b, i, k: (rb0 + b * nt + k, 1)),
        ],
        out_specs=pl.BlockSpec((tmk, d), lambda b, i, k: (b * nt + i, 0)),
        out_shape=jax.ShapeDtypeStruct((bsz * n_pos, d), F32),
        scratch_shapes=[pltpu.VMEM((tmk, d), F32)],
        compiler_params=_cparams(("parallel", "parallel", "arbitrary"), 48),
        name="fourier_positions",
    )(ct, st, y, y)


def _dft_tables(n):
    k = jnp.arange(n, dtype=jnp.int32)
    ang = ((k[:, None] * k[None, :]) % n).astype(F32) * (2.0 * math.pi / n)
    return jnp.cos(ang).astype(BF16), jnp.sin(ang).astype(BF16)


def _merge_kernel(x_ref, attn_l_ref, attn_c_ref, rec_ref, four_l_ref, four_c_ref, lg_ref, g0_ref, g1_ref, g2_ref,
                  wa_ref, wl_ref, wf_ref, wo_ref, gate_ref, n2_ref, sh_ref, sc_ref,
                  xo_ref, h_ref, hp_ref, *, lat_tiles):
    def mm(a, w_ref):
        return jnp.dot(a.astype(BF16), w_ref[...], preferred_element_type=F32)

    is_lat = pl.program_id(0) < lat_tiles
    attn_o = mm(jnp.where(is_lat, attn_l_ref[...], attn_c_ref[...]), wa_ref)
    rec_o = mm(rec_ref[...] * jax.nn.gelu(lg_ref[...], approximate=True), wl_ref)
    four_o = mm(jnp.where(is_lat, four_l_ref[...], four_c_ref[...]), wf_ref)
    mixed = (jax.nn.sigmoid(g0_ref[...]) * attn_o + jax.nn.sigmoid(g1_ref[...]) * rec_o
             + jax.nn.sigmoid(g2_ref[...]) * four_o)
    x = x_ref[...] + gate_ref[...] * mm(mixed, wo_ref)
    xo_ref[...] = x
    y = x * lax.rsqrt(jnp.mean(x * x, axis=-1, keepdims=True) + EPS)
    h = (y * n2_ref[...]) * (1.0 + sc_ref[...]) + sh_ref[...]
    h_ref[...] = h.astype(BF16)
    hp_ref[...] = _pack_halves(h)


def _merge(x_tok, attn_l, attn_c, rec, four_l, four_c, u, w_attn_o, w_lru_o, w_four_o, w_out, mod4, norm2_g,
           row_of_tile, n_rows):
    tm = TM
    d = D_MODEL
    lat_tiles = attn_l.shape[0] // tm
    ctx_tiles = attn_c.shape[0] // tm
    row = pl.BlockSpec((tm, d), lambda i: (i, 0))
    lat_row = pl.BlockSpec((tm, d), lambda i: (jnp.minimum(i, lat_tiles - 1), 0))
    ctx_row = pl.BlockSpec((tm, d), lambda i: (jnp.clip(i - lat_tiles, 0, ctx_tiles - 1), 0))
    wspec = pl.BlockSpec((d, d), lambda i: (0, 0))

    def ucol(cb):
        return pl.BlockSpec((tm, d), lambda i: (i, cb))

    return pl.pallas_call(
        functools.partial(_merge_kernel, lat_tiles=lat_tiles),
        grid=(n_rows // tm,),
        in_specs=[row, lat_row, ctx_row, row, lat_row, ctx_row, ucol(CB_LG), ucol(CB_G), ucol(CB_G + 1),
                  ucol(CB_G + 2), wspec, wspec, wspec, wspec,
                  _mod_spec(2, row_of_tile), pl.BlockSpec((1, d), lambda i: (0, 0)),
                  _mod_spec(3, row_of_tile), _mod_spec(4, row_of_tile)],
        out_specs=[row, row, pl.BlockSpec((tm, d // 2), lambda i: (i, 0))],
        out_shape=[jax.ShapeDtypeStruct((n_rows, d), F32), jax.ShapeDtypeStruct((n_rows, d), BF16),
                   jax.ShapeDtypeStruct((n_rows, d // 2), jnp.uint32)],
        compiler_params=_cparams(("parallel",), 56),
        name="merge_branches",
    )(x_tok, attn_l, attn_c, rec, four_l, four_c, u, u, u, u, w_attn_o, w_lru_o, w_four_o, w_out,
      mod4, norm2_g.reshape(1, d), mod4, mod4)


def _route_kernel(h_ref, rw_ref, rb_ref, tri_ref, idx_ref, wgt_ref, pos_ref, cnt_ref, base_scr):
    tm = h_ref.shape[0]

    @pl.when(pl.program_id(0) == 0)
    def _():
        base_scr[...] = jnp.zeros_like(base_scr)

    logits = lax.dot_general(rw_ref[...], h_ref[...], (((1,), (1,)), ((), ())), preferred_element_type=F32)
    scores = jax.nn.sigmoid(logits)
    biased = scores + rb_ref[...]
    neg = -jnp.inf
    e_iota = lax.broadcasted_iota(jnp.int32, (N_EXPERTS, tm), 0)
    g_iota = lax.broadcasted_iota(jnp.int32, (N_GROUPS, tm), 0)

    rows = []
    for g in range(N_GROUPS):
        xg = biased[g * GROUP_SIZE:(g + 1) * GROUP_SIZE]
        m1 = jnp.max(xg, axis=0, keepdims=True)
        is_max = xg == m1
        n_max = jnp.sum(is_max.astype(F32), axis=0, keepdims=True)
        m2 = jnp.max(jnp.where(is_max, neg, xg), axis=0, keepdims=True)
        rows.append(m1 + jnp.where(n_max >= 2.0, m1, m2))
    grp = jnp.concatenate(rows, axis=0)

    g_sel = jnp.zeros((N_GROUPS, tm), jnp.bool_)
    work = grp
    for _ in range(TOPK_GROUPS):
        m = jnp.max(work, axis=0, keepdims=True)
        first = jnp.min(jnp.where(work == m, g_iota, N_GROUPS), axis=0, keepdims=True)
        hit = g_iota == first
        g_sel = jnp.logical_or(g_sel, hit)
        work = jnp.where(hit, neg, work)
    g_sel_f = g_sel.astype(F32)
    mask_rows = [jnp.broadcast_to(g_sel_f[g:g + 1], (GROUP_SIZE, tm)) for g in range(N_GROUPS)]
    e_mask = jnp.concatenate(mask_rows, axis=0) > 0.5

    work = jnp.where(e_mask, biased, neg)
    sel = jnp.zeros((N_EXPERTS, tm), F32)
    idx_rows, w_rows, hits = [], [], []
    for _ in range(TOP_K):
        m = jnp.max(work, axis=0, keepdims=True)
        first = jnp.min(jnp.where(work == m, e_iota, N_EXPERTS), axis=0, keepdims=True)
        hit = e_iota == first
        hits.append(hit)
        idx_rows.append(first)
        w_rows.append(jnp.sum(jnp.where(hit, scores, 0.0), axis=0, keepdims=True))
        sel = jnp.where(hit, 1.0, sel)
        work = jnp.where(hit, neg, work)
    w = jnp.concatenate(w_rows, axis=0)
    wgt_ref[...] = w / jnp.sum(w, axis=0, keepdims=True) * ROUTED_SCALE
    idx_ref[...] = jnp.concatenate(idx_rows, axis=0)

    before = jnp.dot(sel.astype(BF16), tri_ref[...], preferred_element_type=F32)
    rank = base_scr[...] + before
    pos_rows = [jnp.sum(jnp.where(hit, rank, 0.0), axis=0, keepdims=True) for hit in hits]
    pos_ref[...] = jnp.concatenate(pos_rows, axis=0).astype(jnp.int32)
    base_scr[...] = base_scr[...] + jnp.sum(sel, axis=1, keepdims=True)
    cnt_ref[...] = jnp.broadcast_to(base_scr[...], cnt_ref.shape)


def _route(h16, rw_t, router_b, tm):
    n_tok, d = h16.shape
    tri = (jnp.arange(tm)[:, None] < jnp.arange(tm)[None, :]).astype(BF16)
    out_col = pl.BlockSpec((TOP_K, tm), lambda i: (0, i))
    return pl.pallas_call(
        _route_kernel,
        grid=(n_tok // tm,),
        in_specs=[pl.BlockSpec((tm, d), lambda i: (i, 0)),
                  pl.BlockSpec((N_EXPERTS, d), lambda i: (0, 0)),
                  pl.BlockSpec((N_EXPERTS, 1), lambda i: (0, 0)),
                  pl.BlockSpec((tm, tm), lambda i: (0, 0))],
        out_specs=[out_col, out_col, out_col, pl.BlockSpec((N_EXPERTS, LANES), lambda i: (0, 0))],
        out_shape=[jax.ShapeDtypeStruct((TOP_K, n_tok), jnp.int32), jax.ShapeDtypeStruct((TOP_K, n_tok), F32),
                   jax.ShapeDtypeStruct((TOP_K, n_tok), jnp.int32),
                   jax.ShapeDtypeStruct((N_EXPERTS, LANES), F32)],
        scratch_shapes=[pltpu.VMEM((N_EXPERTS, 1), F32)],
        compiler_params=_cparams(("arbitrary",), 32),
        name="moe_router",
    )(h16, rw_t, router_b.reshape(N_EXPERTS, 1), tri)


def _slot_kernel(idx_ref, pos_ref, offs_ref, dest_ref):
    tm = idx_ref.shape[1]
    e_iota = lax.broadcasted_iota(jnp.int32, (N_EXPERTS, tm), 0)
    idx = idx_ref[...]
    offs = offs_ref[...]
    rows = [jnp.sum(jnp.where(e_iota == idx[r:r + 1], offs, 0), axis=0, keepdims=True) for r in range(TOP_K)]
    dest_ref[...] = jnp.concatenate(rows, axis=0) + pos_ref[...]


def _slots(top_idx, pos, offs, tm):
    n_tok = top_idx.shape[1]
    col = pl.BlockSpec((TOP_K, tm), lambda i: (0, i))
    return pl.pallas_call(
        _slot_kernel,
        grid=(n_tok // tm,),
        in_specs=[col, col, pl.BlockSpec((N_EXPERTS, 1), lambda i: (0, 0))],
        out_specs=col,
        out_shape=jax.ShapeDtypeStruct((TOP_K, n_tok), jnp.int32),
        compiler_params=_cparams(("parallel",), 32),
        name="moe_slots",
    )(top_idx, pos, offs.reshape(N_EXPERTS, 1))


def _row_copy(src_ref, src_row, dst_ref, dst_row, sem):
    return pltpu.make_async_copy(src_ref.at[pl.ds(src_row, 1)], dst_ref.at[pl.ds(dst_row, 1)], sem)


def _dispatch_kernel(dest_ref, hp_ref, xs_ref, sem):
    tm = hp_ref.shape[0]

    def start(t, c):
        for r in range(TOP_K):
            _row_copy(hp_ref, t, xs_ref, dest_ref[r, t], sem).start()
        return c

    lax.fori_loop(0, tm, start, 0)

    def wait(t, c):
        for r in range(TOP_K):
            _row_copy(hp_ref, t, xs_ref, dest_ref[r, t], sem).wait()
        return c

    lax.fori_loop(0, tm, wait, 0)


def _dispatch(hp, dest3, n_slots):
    n_tok, half = hp.shape
    tm = dest3.shape[2]
    return pl.pallas_call(
        _dispatch_kernel,
        grid=(n_tok // tm,),
        in_specs=[pl.BlockSpec((None, TOP_K, tm), lambda i: (i, 0, 0), memory_space=pltpu.SMEM),
                  pl.BlockSpec((tm, half), lambda i: (i, 0))],
        out_specs=pl.BlockSpec(memory_space=pl.ANY),
        out_shape=jax.ShapeDtypeStruct((n_slots, half), jnp.uint32),
        scratch_shapes=[pltpu.SemaphoreType.DMA(())],
        compiler_params=_cparams(("arbitrary",), 32),
        name="moe_dispatch",
    )(dest3, hp)


def _expert_kernel(be_ref, nu_ref, nv_ref, xs_ref, wg_ref, wu_ref, wd_ref, ys_ref, wg_s, wu_s, wd_s):
    i = pl.program_id(0)
    used = i < nu_ref[0]
    prev_e = be_ref[jnp.maximum(i - 1, 0)]
    fresh = jnp.logical_or(i == 0, be_ref[i] != prev_e)

    @pl.when(jnp.logical_and(used, fresh))
    def _():
        wg_s[...] = wg_ref[0].astype(BF16)
        wu_s[...] = wu_ref[0].astype(BF16)
        wd_s[...] = wd_ref[0].astype(BF16)

    @pl.when(used)
    def _():
        w = xs_ref[...]
        row = lax.broadcasted_iota(jnp.int32, w.shape, 0)
        w = jnp.where(row < nv_ref[i], w, jnp.uint32(0))
        half = D_MODEL // 2
        xa, xb = _unpack_halves(w)
        xa = xa.astype(BF16)
        xb = xb.astype(BF16)

        def proj(w_s):
            return (jnp.dot(xa, w_s[:half, :], preferred_element_type=F32)
                    + jnp.dot(xb, w_s[half:, :], preferred_element_type=F32))

        g = proj(wg_s)
        act = (_silu(g) * proj(wu_s)).astype(BF16)
        ys_ref[...] = _pack_halves(jnp.dot(act, wd_s[...], preferred_element_type=F32))


def _experts(xs, block_e, n_used, n_valid, w_gate, w_up, w_down, layer, bm):
    n_slots, half = xs.shape
    d = D_MODEL

    def blk(i, be, nu, nv):
        return (jnp.minimum(i, nu[0] - 1), 0)

    def wsel(i, be, nu, nv):
        return (layer, be[i], 0, 0)

    grid_spec = pltpu.PrefetchScalarGridSpec(
        num_scalar_prefetch=3,
        grid=(n_slots // bm,),
        in_specs=[pl.BlockSpec((bm, half), blk),
                  pl.BlockSpec((None, 1, d, EXPERT_FF), wsel),
                  pl.BlockSpec((None, 1, d, EXPERT_FF), wsel),
                  pl.BlockSpec((None, 1, EXPERT_FF, d), wsel)],
        out_specs=pl.BlockSpec((bm, half), blk),
        scratch_shapes=[pltpu.VMEM((d, EXPERT_FF), BF16), pltpu.VMEM((d, EXPERT_FF), BF16),
                        pltpu.VMEM((EXPERT_FF, d), BF16)],
    )
    return pl.pallas_call(
        _expert_kernel,
        grid_spec=grid_spec,
        out_shape=jax.ShapeDtypeStruct((n_slots, half), jnp.uint32),
        compiler_params=_cparams(("arbitrary",), 40),
        name="moe_experts",
    )(block_e, n_used, n_valid, xs, w_gate, w_up, w_down)


def _combine_kernel(dest_ref, x_ref, h_ref, tw_ref, gate_ref, wsg_ref, wsu_ref, wsd_ref, ys_ref, o_ref, buf, sem):
    tm = x_ref.shape[0]
    half = D_MODEL // 2

    def start(t, c):
        for r in range(TOP_K):
            _row_copy(ys_ref, dest_ref[r, t], buf.at[r], t, sem).start()
        return c

    lax.fori_loop(0, tm, start, 0)

    h = h_ref[...]
    act = (_silu(jnp.dot(h, wsg_ref[...], preferred_element_type=F32))
           * jnp.dot(h, wsu_ref[...], preferred_element_type=F32)).astype(BF16)
    shared = jnp.dot(act, wsd_ref[...], preferred_element_type=F32)

    def wait(t, c):
        for r in range(TOP_K):
            _row_copy(ys_ref, dest_ref[r, t], buf.at[r], t, sem).wait()
        return c

    lax.fori_loop(0, tm, wait, 0)

    tw = tw_ref[...]
    moe_a = moe_b = None
    for r in range(TOP_K):
        ya, yb = _unpack_halves(buf[r])
        wr = tw[:, r:r + 1]
        moe_a = ya * wr if moe_a is None else moe_a + ya * wr
        moe_b = yb * wr if moe_b is None else moe_b + yb * wr
    gate = gate_ref[...]
    o_ref[:, :half] = x_ref[:, :half] + gate[:, :half] * (moe_a + shared[:, :half])
    o_ref[:, half:] = x_ref[:, half:] + gate[:, half:] * (moe_b + shared[:, half:])


def _combine(x_tok, h16, ys, dest3, top_w, mod4, ws_gate, ws_up, ws_down, row_of_tile):
    n_tok, d = x_tok.shape
    tm = dest3.shape[2]
    ff = ws_gate.shape[1]
    row = pl.BlockSpec((tm, d), lambda i: (i, 0))
    return pl.pallas_call(
        _combine_kernel,
        grid=(n_tok // tm,),
        in_specs=[pl.BlockSpec((None, TOP_K, tm), lambda i: (i, 0, 0), memory_space=pltpu.SMEM),
                  row, row,
                  pl.BlockSpec((tm, TOP_K), lambda i: (i, 0)),
                  _mod_spec(5, row_of_tile),
                  pl.BlockSpec((d, ff), lambda i: (0, 0)),
                  pl.BlockSpec((d, ff), lambda i: (0, 0)),
                  pl.BlockSpec((ff, d), lambda i: (0, 0)),
                  pl.BlockSpec(memory_space=pl.ANY)],
        out_specs=row,
        out_shape=jax.ShapeDtypeStruct((n_tok, d), F32),
        scratch_shapes=[pltpu.VMEM((TOP_K, tm, d // 2), jnp.uint32), pltpu.SemaphoreType.DMA(())],
        compiler_params=_cparams(("arbitrary",), 48),
        name="moe_combine",
    )(dest3, x_tok, h16, top_w, mod4, ws_gate, ws_up, ws_down, ys)


def _moe(x_tok, h16, hp, mod4, row_of_tile, rw_t, router_b, w_gate, w_up, w_down, layer, ws_gate, ws_up, ws_down):
    n_tok = x_tok.shape[0]
    bm = MOE_BM
    tm_r = TM_ROUTE if n_tok % TM_ROUTE == 0 else TM
    top_idx, top_w, pos, cnt = _route(h16, rw_t, router_b, tm_r)

    counts = cnt[:, 0].astype(jnp.int32)
    padded = (counts + bm - 1) // bm * bm
    ends = jnp.cumsum(padded)
    n_blocks = -(-(n_tok * TOP_K + N_EXPERTS * (bm - 1)) // bm)
    offs = ends - padded
    dest = _slots(top_idx, pos, offs, tm_r)
    block_start = jnp.arange(n_blocks, dtype=jnp.int32) * bm
    block_e = jnp.minimum(jnp.sum((ends[None, :] <= block_start[:, None]).astype(jnp.int32), axis=1),
                          N_EXPERTS - 1)
    n_valid = jnp.clip(counts[block_e] - (block_start - offs[block_e]), 0, bm).astype(jnp.int32)
    n_used = (ends[-1] // bm).astype(jnp.int32).reshape(1)
    tm = TM
    dest3 = dest.reshape(TOP_K, n_tok // tm, tm).transpose(1, 0, 2)

    xs = _dispatch(hp, dest3, n_blocks * bm)
    ys = _experts(xs, block_e, n_used, n_valid, w_gate, w_up, w_down, layer, bm)
    return _combine(x_tok, h16, ys, dest3, top_w.T, mod4, ws_gate, ws_up, ws_down, row_of_tile)


def _block_diag_tiles(w):
    per = MXU_DIM // LRU_BLOCK_DIM
    w = w.reshape(2, D_MODEL // MXU_DIM, per, LRU_BLOCK_DIM, LRU_BLOCK_DIM)
    eye = jnp.eye(per, dtype=w.dtype)
    t = jnp.einsum('ztpde,pq->ztpdqe', w, eye)
    return t.reshape(2, D_MODEL // MXU_DIM, MXU_DIM, MXU_DIM).astype(BF16)


def _rope_tables(n_lat, n_ctx):
    rows = n_lat // GRID_W
    row = jnp.repeat(jnp.arange(rows), GRID_W)
    col = jnp.tile(jnp.arange(GRID_W), rows)
    inv = ROPE_THETA ** (-jnp.arange(ROPE_FREQS, dtype=F32) / ROPE_FREQS)
    ang = jnp.stack([row[:, None] * inv, col[:, None] * inv], axis=1)
    ang = jnp.concatenate([ang, ang], axis=2).reshape(n_lat, HEAD_DIM)
    ang = jnp.tile(ang, (1, LANES // HEAD_DIM))
    ang = jnp.concatenate([ang, jnp.zeros((n_ctx, LANES), F32)], axis=0)
    lane = jnp.arange(LANES)
    sign = jnp.where((lane % (2 * ROPE_FREQS)) < ROPE_FREQS, -1.0, 1.0).astype(F32)
    return jnp.cos(ang), jnp.sin(ang) * sign


def kernel(x, c, ctx, c_ctx, w_mod, b_mod, norm1_g, w_in, q_norm_g, k_norm_g, lambda_qk, subln_g, w_attn_o, conv_w, conv_b, lru_wa, lru_ba, lru_wi, lru_bi, lru_lambda, w_lru_o, w_four_o, w_out, norm2_g, router_w, router_b, exp_w_gate, exp_w_up, exp_w_down, sh_w_gate, sh_w_up, sh_w_down):
    bsz, n_lat, d = x.shape
    n_ctx = ctx.shape[1]
    depth = w_mod.shape[0]
    n_latent_rows = bsz * n_lat
    n_tok = n_latent_rows + bsz * n_ctx
    assert d == D_MODEL and bsz + 1 <= MOD_ROWS
    assert n_lat % TM == 0 and n_ctx % TM == 0 and n_lat % GRID_W == 0
    tm_in = TM_IN if (n_lat % TM_IN == 0 and (bsz * n_ctx) % TM_IN == 0) else TM

    cvec = jnp.zeros((MOD_ROWS, d), F32).at[:bsz].set(c).at[bsz].set(c_ctx)
    mods = _modulation(cvec, w_mod, b_mod)

    def row_of_tile_fn(tm):
        per_batch = n_lat // tm
        return lambda i: jnp.minimum(i // per_batch, bsz)

    assert n_lat % n_ctx == 0
    cos_t, sin_t = _rope_tables(n_lat, n_ctx)
    group_mat = jnp.kron(jnp.eye(d // HEAD_DIM, dtype=F32), jnp.ones((HEAD_DIM, HEAD_DIM), F32)).astype(BF16)
    m = jnp.arange(FOUR_GROUP_DIM, dtype=jnp.int32)
    ang_c = ((m[:, None] * m[None, :]) % FOUR_GROUP_DIM).astype(F32) * (2.0 * math.pi / FOUR_GROUP_DIM)
    inv_sqrt_c = FOUR_GROUP_DIM ** -0.5
    cs_mat = jnp.concatenate([jnp.cos(ang_c) * inv_sqrt_c, jnp.sin(ang_c) * inv_sqrt_c], axis=1).astype(BF16)
    assert n_lat % (FFT_RADIX * MXU_DIM) == 0 or n_lat // FFT_RADIX < MXU_DIM
    fft_tk1 = min(MXU_DIM, n_lat // FFT_RADIX)
    fft_cc, fft_pc, fft_ps = _fft_tables(n_lat, FFT_RADIX, fft_tk1)
    ct_ctx, st_ctx = _dft_tables(n_ctx)
    q_sub = math.gcd(Q_SUB, n_lat // TQ)
    n_keys = n_ctx + n_lat
    tk = TK if n_keys % TK == 0 else TM

    x_tok = jnp.concatenate([x.reshape(n_latent_rows, d), ctx.reshape(bsz * n_ctx, d)], axis=0)
    for l in range(depth):
        last = l == depth - 1
        lam_init = 0.8 - 0.6 * math.exp(-0.3 * l)
        mod4 = mods[l].reshape(MOD_ROWS, 6, 1, d)

        u = _in_projection(x_tok, mod4, norm1_g[l], w_in[l].astype(BF16), row_of_tile_fn(tm_in), tm_in)

        k_all, q_all, vt_all = _qkv_prepare(u, k_norm_g[l], q_norm_g[l], group_mat, cos_t, sin_t,
                                            bsz, n_lat, n_ctx)
        attn_l = _diff_attention(q_all, k_all, vt_all, lambda_qk[l], subln_g[l], lam_init,
                                 0, n_lat, 0, n_keys, TQ, q_sub, tk)
        attn_c = attn_l
        if not last:
            attn_c = _diff_attention(q_all, k_all, vt_all, lambda_qk[l], subln_g[l], lam_init,
                                     n_lat, n_ctx, n_lat, n_ctx, TM, 1, TM)

        rec = _rglru(u, conv_w[l], conv_b[l], _block_diag_tiles(lru_wa[l]), lru_ba[l],
                     _block_diag_tiles(lru_wi[l]), lru_bi[l], lru_lambda[l], bsz, n_lat, n_ctx)

        y_lat = _chan_dft_split(u, cs_mat, bsz, n_lat, FFT_RADIX)
        four_l = _time_fft(y_lat, fft_cc, fft_pc, fft_ps, FFT_RADIX, fft_tk1, MXU_DIM).reshape(n_latent_rows, d)
        four_c = four_l
        if not last:
            y_ctx = _chan_dft(u, cs_mat, n_latent_rows, bsz * n_ctx)
            four_c = _time_dft(y_ctx, ct_ctx, st_ctx, bsz, n_ctx, 0, n_ctx)

        n_rows = n_latent_rows if last else n_tok
        row_of_tile = row_of_tile_fn(TM)
        x_mid, h16, hp = _merge(x_tok, attn_l, attn_c, rec, four_l, four_c, u, w_attn_o[l].astype(BF16),
                                w_lru_o[l].astype(BF16), w_four_o[l].astype(BF16), w_out[l].astype(BF16),
                                mod4, norm2_g[l], row_of_tile, n_rows)
        x_tok = _moe(x_mid, h16, hp, mod4, row_of_tile, router_w[l].T.astype(BF16), router_b[l],
                     exp_w_gate, exp_w_up, exp_w_down, l,
                     sh_w_gate[l].astype(BF16), sh_w_up[l].astype(BF16), sh_w_down[l].astype(BF16))
    return x_tok[:n_latent_rows].reshape(bsz, n_lat, d)
```

```python
import functools
import math

import jax
import jax.numpy as jnp
from jax import lax
from jax.experimental import pallas as pl
from jax.experimental.pallas import tpu as pltpu
from jax.experimental.pallas import tpu_sc as plsc

F32 = jnp.float32
BF16 = jnp.bfloat16

D_MODEL = 1024
EPS = 1e-6
GRID_W = 64

N_HEADS = 8
HEAD_DIM = 64
V_DIM = 2 * HEAD_DIM
ATTN_SCALE = HEAD_DIM ** -0.5
ROPE_THETA = 10000.0
ROPE_FREQS = HEAD_DIM // 4

LRU_BLOCKS = 16
LRU_BLOCK_DIM = D_MODEL // LRU_BLOCKS
LRU_C = 8.0
CONV_W = 4

FOUR_GROUPS = 4
FOUR_GROUP_DIM = D_MODEL // FOUR_GROUPS

CB_K, CB_V, CB_LX, CB_Q, CB_LG, CB_F, CB_G = 0, 1, 2, 3, 4, 5, 6
IN_BLOCKS = 9

N_EXPERTS = 256
TOP_K = 8
N_GROUPS = 8
GROUP_SIZE = N_EXPERTS // N_GROUPS
TOPK_GROUPS = 4
EXPERT_FF = 256
ROUTED_SCALE = 2.5

LANES = 128
SUBLANES = 8
MXU_DIM = 256
VMEM_BYTES_V7X = 64 * 1024 * 1024

MOD_ROWS = 8
TM = 256
TM_IN = 512
TQ = 256
Q_SUB = 32
TK = 768
TM_ROUTE = 512
MOE_BM = 512
SC_WINDOW = 128
FFT_RADIX = 8
LOG2E = 1.4426950408889634


def _cparams(sem, vmem_mb):
    return pltpu.CompilerParams(dimension_semantics=sem, vmem_limit_bytes=int(vmem_mb * 1024 * 1024))


def _silu(x):
    return x * jax.nn.sigmoid(x)


def _pack_halves(x):
    half = x.shape[1] // 2
    bits = pltpu.bitcast(x.astype(BF16).astype(F32), jnp.uint32)
    return (bits[:, :half] & jnp.uint32(0xFFFF0000)) | (bits[:, half:] >> 16)


def _unpack_halves(w):
    return pltpu.bitcast(w & jnp.uint32(0xFFFF0000), F32), pltpu.bitcast(w << 16, F32)


def _mod_kernel(c_ref, w_ref, b_ref, o_ref):
    c = c_ref[...]
    s = _silu(c).astype(BF16)
    o_ref[0] = jnp.dot(s, w_ref[0].astype(BF16), preferred_element_type=F32) + b_ref[0]


def _modulation(cvec, w_mod, b_mod):
    n_layers, d, six_d = w_mod.shape
    nj = six_d // d
    return pl.pallas_call(
        _mod_kernel,
        grid=(n_layers, nj),
        in_specs=[
            pl.BlockSpec((MOD_ROWS, d), lambda l, j: (0, 0)),
            pl.BlockSpec((1, d, d), lambda l, j: (l, 0, j)),
            pl.BlockSpec((1, 1, d), lambda l, j: (l, 0, j)),
        ],
        out_specs=pl.BlockSpec((1, MOD_ROWS, d), lambda l, j: (l, 0, j)),
        out_shape=jax.ShapeDtypeStruct((n_layers, MOD_ROWS, six_d), F32),
        compiler_params=_cparams(("parallel", "parallel"), 32),
        name="modulation",
    )(cvec, w_mod, b_mod.reshape(n_layers, 1, six_d))


def _mod_spec(part, row_of_tile, n_grid_axes=1):
    if n_grid_axes == 1:
        return pl.BlockSpec((None, None, 1, D_MODEL), lambda i: (row_of_tile(i), part, 0, 0))
    return pl.BlockSpec((None, None, 1, D_MODEL), lambda i, j: (row_of_tile(i), part, 0, 0))


def _inproj_kernel(x_ref, g_ref, sh_ref, sc_ref, w_ref, o_ref, h_scr):
    @pl.when(pl.program_id(1) == 0)
    def _():
        x = x_ref[...]
        y = x * lax.rsqrt(jnp.mean(x * x, axis=-1, keepdims=True) + EPS)
        h = (y * g_ref[...]) * (1.0 + sc_ref[...]) + sh_ref[...]
        h_scr[...] = h.astype(BF16)

    o_ref[...] = jnp.dot(h_scr[...], w_ref[...], preferred_element_type=F32)


def _in_projection(x_tok, mod4, norm_g, w_in_bf16, row_of_tile, tm):
    n_tok, d = x_tok.shape
    n_cols = w_in_bf16.shape[1]
    return pl.pallas_call(
        _inproj_kernel,
        grid=(n_tok // tm, n_cols // d),
        in_specs=[
            pl.BlockSpec((tm, d), lambda i, j: (i, 0)),
            pl.BlockSpec((1, d), lambda i, j: (0, 0)),
            _mod_spec(0, row_of_tile, 2),
            _mod_spec(1, row_of_tile, 2),
            pl.BlockSpec((d, d), lambda i, j: (0, j)),
        ],
        out_specs=pl.BlockSpec((tm, d), lambda i, j: (i, j)),
        out_shape=jax.ShapeDtypeStruct((n_tok, n_cols), F32),
        scratch_shapes=[pltpu.VMEM((tm, d), BF16)],
        compiler_params=_cparams(("parallel", "arbitrary"), 40),
        name="in_projection",
    )(x_tok, norm_g.reshape(1, d), mod4, mod4, w_in_bf16)


def _prep_kernel(uk_ref, uq_ref, uv_ref, kg_ref, qg_ref, gm_ref, cos_ref, sin_ref, k_out, q_out, vt_out, *, tm):
    gm = gm_ref[...]
    lane = lax.broadcasted_iota(jnp.int32, (tm, LANES), 1)
    first_half = (lane % (2 * ROPE_FREQS)) < ROPE_FREQS

    def norm_rope(x, g):
        ss = jnp.dot((x * x).astype(BF16), gm, preferred_element_type=F32) * (1.0 / HEAD_DIM)
        y = x * lax.rsqrt(ss + EPS) * g
        c = cos_ref[...]
        s = sin_ref[...]
        outs = []
        for j in range(D_MODEL // LANES):
            yt = y[:, j * LANES:(j + 1) * LANES]
            rot = jnp.where(first_half, pltpu.roll(yt, LANES - ROPE_FREQS, 1), pltpu.roll(yt, ROPE_FREQS, 1))
            outs.append(yt * c + rot * s)
        return jnp.concatenate(outs, axis=1)

    k_out[...] = norm_rope(uk_ref[...], kg_ref[...]).astype(BF16)
    qn = norm_rope(uq_ref[...], qg_ref[...]) * (ATTN_SCALE * LOG2E)
    lo = lane < HEAD_DIM
    for h in range(N_HEADS):
        qt = qn[:, h * LANES:(h + 1) * LANES]
        q_out[:, (2 * h) * LANES:(2 * h + 1) * LANES] = jnp.where(lo, qt, 0.0).astype(BF16)
        q_out[:, (2 * h + 1) * LANES:(2 * h + 2) * LANES] = jnp.where(lo, 0.0, qt).astype(BF16)
    vt_out[...] = uv_ref[...].T.astype(BF16)


def _qkv_prepare(u, k_gain, q_gain, group_mat, cos_t, sin_t, bsz, n_lat, n_ctx):
    tm = TM
    d = D_MODEL
    n_s, n_c = n_lat // tm, n_ctx // tm
    n_rows = n_lat + n_ctx
    lat_tiles = bsz * n_s
    kg = jnp.tile(k_gain, d // HEAD_DIM).reshape(1, d)
    qg = jnp.tile(q_gain, d // HEAD_DIM).reshape(1, d)

    def tok_tile(b, j):
        return jnp.where(j < n_s, b * n_s + j, lat_tiles + b * n_c + (j - n_s))

    def u_spec(cb):
        return pl.BlockSpec((tm, d), lambda b, j: (tok_tile(b, j), cb))

    vec = pl.BlockSpec((1, d), lambda b, j: (0, 0))
    rope_spec = pl.BlockSpec((tm, LANES), lambda b, j: (j, 0))
    return pl.pallas_call(
        functools.partial(_prep_kernel, tm=tm),
        grid=(bsz, n_s + n_c),
        in_specs=[u_spec(CB_K), u_spec(CB_Q), u_spec(CB_V), vec, vec,
                  pl.BlockSpec((d, d), lambda b, j: (0, 0)), rope_spec, rope_spec],
        out_specs=[
            pl.BlockSpec((None, tm, d), lambda b, j: (b, j, 0)),
            pl.BlockSpec((None, tm, 2 * d), lambda b, j: (b, j, 0)),
            pl.BlockSpec((None, d, tm), lambda b, j: (b, 0, j)),
        ],
        out_shape=[jax.ShapeDtypeStruct((bsz, n_rows, d), BF16),
                   jax.ShapeDtypeStruct((bsz, n_rows, 2 * d), BF16),
                   jax.ShapeDtypeStruct((bsz, d, n_rows), BF16)],
        compiler_params=_cparams(("parallel", "parallel"), 40),
        name="qkv_prepare",
    )(u, u, u, kg, qg, group_mat, cos_t, sin_t)


def _attn_kernel(lq_ref, sg_ref, q1_ref, q2_ref, k_ref, vt_ref, o_ref, s_buf, *, n_keys, tk, tq, lam_init):
    lq = lq_ref[...]
    lam = (jnp.exp(jnp.sum(lq[0:1] * lq[1:2], axis=1, keepdims=True))
           - jnp.exp(jnp.sum(lq[2:3] * lq[3:4], axis=1, keepdims=True)) + lam_init)
    n_chunks = n_keys // tk
    n_sub = q1_ref.shape[0] // tq

    def load_q(j):
        rows = pl.ds(pl.multiple_of(j * tq, tq), tq)
        return jnp.concatenate([q1_ref[rows, :], q2_ref[rows, :]], axis=0)

    def score_chunk(q, c, mx):
        s = lax.dot_general(k_ref[c * tk:(c + 1) * tk, :], q, (((1,), (1,)), ((), ())),
                            preferred_element_type=F32)
        s_buf[c * tk:(c + 1) * tk, :] = s
        cm = jnp.max(s, axis=0, keepdims=True)
        return cm if mx is None else jnp.maximum(mx, cm)

    q0 = load_q(0)
    mx0 = None
    for c in range(n_chunks):
        mx0 = score_chunk(q0, c, mx0)

    def sub_tile(j, mx):
        q_next = load_q(jnp.minimum(j + 1, n_sub - 1))
        mx_next = None
        l = jnp.zeros((1, 2 * tq), F32)
        acc = jnp.zeros((V_DIM, 2 * tq), F32)
        for c in range(n_chunks):
            p = jnp.exp2(s_buf[c * tk:(c + 1) * tk, :] - mx)
            l = l + jnp.sum(p, axis=0, keepdims=True)
            acc = acc + jnp.dot(vt_ref[:, c * tk:(c + 1) * tk], p.astype(BF16), preferred_element_type=F32)
            mx_next = score_chunk(q_next, c, mx_next)
        o = acc / l
        o = o[:, :tq] - lam * o[:, tq:]
        ms = jnp.mean(o * o, axis=0, keepdims=True)
        on = (o * lax.rsqrt(ms + EPS) * sg_ref[...]) * (1.0 - lam_init)
        o_ref[pl.ds(pl.multiple_of(j * tq, tq), tq), :] = on.T
        return mx_next

    lax.fori_loop(0, n_sub, sub_tile, mx0)


def _diff_attention(q_all, k_all, vt_all, lambda_qk, subln_g, lam_init, q_row0, n_q, key_row0, n_keys, tq, n_sub, tk):
    bsz = q_all.shape[0]
    d = D_MODEL
    tstep = tq * n_sub
    nq = n_q // tstep
    qb0 = q_row0 // tstep
    kb0 = key_row0 // n_keys
    s_bytes = n_keys * 2 * tq * 4
    return pl.pallas_call(
        functools.partial(_attn_kernel, n_keys=n_keys, tk=tk, tq=tq, lam_init=lam_init),
        grid=(bsz, N_HEADS, nq),
        in_specs=[
            pl.BlockSpec((4, HEAD_DIM), lambda b, h, i: (0, 0)),
            pl.BlockSpec((V_DIM, 1), lambda b, h, i: (0, 0)),
            pl.BlockSpec((None, tstep, LANES), lambda b, h, i: (b, qb0 + i, 2 * h)),
            pl.BlockSpec((None, tstep, LANES), lambda b, h, i: (b, qb0 + i, 2 * h + 1)),
            pl.BlockSpec((None, n_keys, LANES), lambda b, h, i: (b, kb0, h)),
            pl.BlockSpec((None, V_DIM, n_keys), lambda b, h, i: (b, h, kb0)),
        ],
        out_specs=pl.BlockSpec((tstep, LANES), lambda b, h, i: (b * nq + i, h)),
        out_shape=jax.ShapeDtypeStruct((bsz * n_q, d), F32),
        scratch_shapes=[pltpu.VMEM((n_keys, 2 * tq), F32)],
        compiler_params=_cparams(("parallel", "parallel", "arbitrary"), 28 + s_bytes / 2 ** 20),
        name="diff_attention",
    )(lambda_qk, subln_g.reshape(V_DIM, 1), q_all, q_all, k_all, vt_all)


def _lru_kernel(*refs, rev, n_c, n_s, tm):
    if rev:
        (cur_ref, prev_ref, next_ref, cw_ref, cb_ref, wa_ref, ba_ref, wi_ref, bi_ref, lam_ref, fwd_ref,
         o_ref, a_scr, b_scr, h_scr) = refs
    else:
        (cur_ref, prev_ref, next_ref, cw_ref, cb_ref, wa_ref, ba_ref, wi_ref, bi_ref, lam_ref,
         o_ref, a_scr, b_scr, h_scr) = refs
    s = pl.program_id(1)
    is_ctx = s < n_c
    if rev:
        tile = jnp.where(is_ctx, n_c - 1 - s, n_s - 1 - (s - n_c))
    else:
        tile = jnp.where(is_ctx, s, s - n_c)
    n_tile = jnp.where(is_ctx, n_c, n_s)

    @pl.when(s == 0)
    def _():
        h_scr[...] = jnp.zeros_like(h_scr)

    x = cur_ref[...]
    pv = jnp.where(tile == 0, 0.0, prev_ref[...])
    nx = jnp.where(tile == n_tile - 1, 0.0, next_ref[...])
    r8 = lax.broadcasted_iota(jnp.int32, (SUBLANES, D_MODEL), 0)

    def earlier(k):
        rolled = pltpu.roll(x, k, 0)
        top = jnp.where(r8 < k, pltpu.roll(pv, k, 0), rolled[:SUBLANES])
        return jnp.concatenate([top, rolled[SUBLANES:]], axis=0)

    rolled = pltpu.roll(x, tm - 1, 0)
    bot = jnp.where(r8 >= SUBLANES - 1, pltpu.roll(nx, SUBLANES - 1, 0), rolled[tm - SUBLANES:])
    later = jnp.concatenate([rolled[:tm - SUBLANES], bot], axis=0)
    cw = cw_ref[...]
    conv = earlier(2) * cw[0:1] + earlier(1) * cw[1:2] + x * cw[2:3] + later * cw[3:4] + cb_ref[...]

    conv16 = conv.astype(BF16)

    def block_diag(w_ref, b_ref):
        parts = [jnp.dot(conv16[:, j * MXU_DIM:(j + 1) * MXU_DIM], w_ref[j], preferred_element_type=F32)
                 for j in range(D_MODEL // MXU_DIM)]
        return jnp.concatenate(parts, axis=1) + b_ref[...]

    r = jax.nn.sigmoid(block_diag(wa_ref, ba_ref))
    gate_i = jax.nn.sigmoid(block_diag(wi_ref, bi_ref))
    neg_lam = -lam_ref[...]
    softplus = jnp.maximum(neg_lam, 0.0) + jnp.log1p(jnp.exp(-jnp.abs(neg_lam)))
    log_a = -LRU_C * r * softplus
    a = jnp.exp(log_a)
    mult = jnp.sqrt(1.0 - a * a)
    a_scr[...] = a
    b_scr[...] = mult * gate_i * conv

    def body(t, h):
        row = tm - 1 - t if rev else t
        h = a_scr[pl.ds(row, 1), :] * h + b_scr[pl.ds(row, 1), :]
        if rev:
            o_ref[pl.ds(row, 1), :] = fwd_ref[pl.ds(row, 1), :] + h
        else:
            o_ref[pl.ds(row, 1), :] = h
        return h

    h_scr[...] = lax.fori_loop(0, tm, body, h_scr[...], unroll=8)


def _rglru(u, conv_w, conv_b, wa_bd, ba, wi_bd, bi, lam, bsz, n_lat, n_ctx):
    tm = TM
    d = D_MODEL
    n_tok = u.shape[0]
    n_s, n_c = n_lat // tm, n_ctx // tm
    lat_tiles = bsz * n_s
    per8 = tm // SUBLANES
    last8 = n_tok // SUBLANES - 1

    def make(rev, fwd):
        def blk(b, s):
            is_ctx = s < n_c
            if rev:
                tile = jnp.where(is_ctx, n_c - 1 - s, n_s - 1 - (s - n_c))
            else:
                tile = jnp.where(is_ctx, s, s - n_c)
            return jnp.where(is_ctx, lat_tiles + b * n_c + tile, b * n_s + tile)

        vec = pl.BlockSpec((None, 1, d), lambda b, s: (int(rev), 0, 0))
        wspec = pl.BlockSpec((None, d // MXU_DIM, MXU_DIM, MXU_DIM), lambda b, s: (int(rev), 0, 0, 0))
        row_spec = pl.BlockSpec((tm, d), lambda b, s: (blk(b, s), 0))
        in_specs = [
            pl.BlockSpec((tm, d), lambda b, s: (blk(b, s), CB_LX)),
            pl.BlockSpec((SUBLANES, d), lambda b, s: (jnp.maximum(blk(b, s) * per8 - 1, 0), CB_LX)),
            pl.BlockSpec((SUBLANES, d), lambda b, s: (jnp.minimum((blk(b, s) + 1) * per8, last8), CB_LX)),
            pl.BlockSpec((CONV_W, d), lambda b, s: (0, 0)),
            pl.BlockSpec((1, d), lambda b, s: (0, 0)),
            wspec, vec, wspec, vec, vec,
        ]
        args = [u, u, u, conv_w, conv_b.reshape(1, d), wa_bd, ba.reshape(2, 1, d), wi_bd, bi.reshape(2, 1, d),
                lam.reshape(2, 1, d)]
        if rev:
            in_specs.append(row_spec)
            args.append(fwd)
        return pl.pallas_call(
            functools.partial(_lru_kernel, rev=rev, n_c=n_c, n_s=n_s, tm=tm),
            grid=(bsz, n_c + n_s),
            in_specs=in_specs,
            out_specs=row_spec,
            out_shape=jax.ShapeDtypeStruct((n_tok, d), F32),
            scratch_shapes=[pltpu.VMEM((tm, d), F32), pltpu.VMEM((tm, d), F32), pltpu.VMEM((1, d), F32)],
            input_output_aliases={10: 0} if rev else {},
            compiler_params=_cparams(("arbitrary", "arbitrary"), 40),
            name="rglru_bwd" if rev else "rglru_fwd",
        )(*args)

    fwd = make(False, None)
    return make(True, fwd)


def _chan_dft_kernel(u_ref, cs_ref, o_ref):
    x = u_ref[...].astype(BF16)
    cs = cs_ref[...]
    gd = FOUR_GROUP_DIM
    for g in range(FOUR_GROUPS):
        y = jnp.dot(x[:, g * gd:(g + 1) * gd], cs, preferred_element_type=F32)
        o_ref[:, g * gd:(g + 1) * gd] = y[:, :gd].astype(BF16)
        o_ref[:, D_MODEL + g * gd:D_MODEL + (g + 1) * gd] = y[:, gd:].astype(BF16)


def _chan_dft(u, cs_mat, row0, n_rows):
    tm = TM
    d = D_MODEL
    tile0 = row0 // tm
    return pl.pallas_call(
        _chan_dft_kernel,
        grid=(n_rows // tm,),
        in_specs=[pl.BlockSpec((tm, d), lambda i: (tile0 + i, CB_F)),
                  pl.BlockSpec((FOUR_GROUP_DIM, 2 * FOUR_GROUP_DIM), lambda i: (0, 0))],
        out_specs=pl.BlockSpec((tm, 2 * d), lambda i: (i, 0)),
        out_shape=jax.ShapeDtypeStruct((n_rows, 2 * d), BF16),
        compiler_params=_cparams(("parallel",), 32),
        name="fourier_channels",
    )(u, cs_mat)


def _chan_dft_split_kernel(u_ref, cs_ref, o_ref, y_scr, *, radix):
    x = u_ref[...].astype(BF16)
    cs = cs_ref[...]
    gd = FOUR_GROUP_DIM
    per_g = gd // LANES
    n_cos = D_MODEL // LANES
    for g in range(FOUR_GROUPS):
        y = jnp.dot(x[:, g * gd:(g + 1) * gd], cs, preferred_element_type=F32)
        for j in range(per_g):
            y_scr[g * per_g + j] = y[:, j * LANES:(j + 1) * LANES]
            y_scr[n_cos + g * per_g + j] = y[:, gd + j * LANES:gd + (j + 1) * LANES]
    rows = u_ref.shape[0] // radix
    for r in range(radix):
        for cb in range(2 * n_cos):
            o_ref[r, :, cb * LANES:(cb + 1) * LANES] = y_scr[cb, pl.ds(r, rows, stride=radix), :].astype(BF16)


def _chan_dft_split(u, cs_mat, bsz, n_lat, radix):
    tm = TM
    d = D_MODEL
    n_s = n_lat // tm
    return pl.pallas_call(
        functools.partial(_chan_dft_split_kernel, radix=radix),
        grid=(bsz, n_s),
        in_specs=[pl.BlockSpec((tm, d), lambda b, i: (b * n_s + i, CB_F)),
                  pl.BlockSpec((FOUR_GROUP_DIM, 2 * FOUR_GROUP_DIM), lambda b, i: (0, 0))],
        out_specs=pl.BlockSpec((None, radix, tm // radix, 2 * d), lambda b, i: (b, 0, i, 0)),
        out_shape=jax.ShapeDtypeStruct((bsz, radix, n_lat // radix, 2 * d), BF16),
        scratch_shapes=[pltpu.VMEM((2 * d // LANES, tm, LANES), F32)],
        compiler_params=_cparams(("parallel", "parallel"), 32),
        name="fourier_channels_split",
    )(u, cs_mat)


def _time_fft_kernel(zr_ref, zi_ref, cc_ref, pc_ref, ps_ref, o_ref, *, radix):
    cc = cc_ref[...]
    tk1 = cc.shape[0] // 2
    for r in range(radix):
        pr = jnp.dot(cc, zr_ref[r], preferred_element_type=F32)
        pi = jnp.dot(cc, zi_ref[r], preferred_element_type=F32)
        a_re = pr[:tk1] - pi[tk1:]
        a_im = pr[tk1:] + pi[:tk1]
        for k2 in range(radix):
            term = pc_ref[k2][:, r:r + 1] * a_re - ps_ref[k2][:, r:r + 1] * a_im
            if r == 0:
                o_ref[k2] = term
            else:
                o_ref[k2] += term


def _time_fft(yp, cc, pc, ps, radix, tk1, tn):
    bsz, _, t1, two_d = yp.shape
    d = two_d // 2
    nj = d // tn
    return pl.pallas_call(
        functools.partial(_time_fft_kernel, radix=radix),
        grid=(bsz, nj, t1 // tk1),
        in_specs=[
            pl.BlockSpec((None, radix, t1, tn), lambda b, j, i: (b, 0, 0, j)),
            pl.BlockSpec((None, radix, t1, tn), lambda b, j, i: (b, 0, 0, nj + j)),
            pl.BlockSpec((None, 2 * tk1, t1), lambda b, j, i: (i, 0, 0)),
            pl.BlockSpec((radix, tk1, radix), lambda b, j, i: (0, i, 0)),
            pl.BlockSpec((radix, tk1, radix), lambda b, j, i: (0, i, 0)),
        ],
        out_specs=pl.BlockSpec((None, radix, tk1, tn), lambda b, j, i: (b, 0, i, j)),
        out_shape=jax.ShapeDtypeStruct((bsz, radix, t1, d), F32),
        compiler_params=_cparams(("parallel", "parallel", "arbitrary"), 48),
        name="fourier_positions_fft",
    )(yp, yp, cc, pc, ps)


def _fft_tables(n, radix, tk1):
    t1 = n // radix
    k1 = jnp.arange(t1, dtype=jnp.int32)
    ang = ((k1[:, None] * k1[None, :]) % t1).astype(F32) * (2.0 * math.pi / t1)
    c = jnp.cos(ang).reshape(t1 // tk1, tk1, t1)
    s = jnp.sin(ang).reshape(t1 // tk1, tk1, t1)
    cc = jnp.concatenate([c, s], axis=1).astype(BF16)
    k2 = jnp.arange(radix, dtype=jnp.int32)
    r = jnp.arange(radix, dtype=jnp.int32)
    k = k1[None, :, None] + t1 * k2[:, None, None]
    ph = ((k * r[None, None, :]) % n).astype(F32) * (2.0 * math.pi / n)
    scale = float(n) ** -0.5
    return cc, jnp.cos(ph) * scale, jnp.sin(ph) * scale


def _time_dft_kernel(ct_ref, st_ref, yc_ref, ys_ref, o_ref, acc, *, scale):
    kk = pl.program_id(2)

    @pl.when(kk == 0)
    def _():
        acc[...] = jnp.zeros_like(acc)

    acc[...] += (jnp.dot(ct_ref[...], yc_ref[...], preferred_element_type=F32)
                 - jnp.dot(st_ref[...], ys_ref[...], preferred_element_type=F32))

    @pl.when(kk == pl.num_programs(2) - 1)
    def _():
        o_ref[...] = acc[...] * scale


def _time_dft(y, ct, st, bsz, n_pos, row0, tmk):
    d = D_MODEL
    nt = n_pos // tmk
    rb0 = row0 // tmk
    return pl.pallas_call(
        functools.partial(_time_dft_kernel, scale=float(n_pos) ** -0.5),
        grid=(bsz, nt, nt),
        in_specs=[
            pl.BlockSpec((tmk, tmk), lambda b, i, k: (i, k)),
            pl.BlockSpec((tmk, tmk), lambda b, i, k: (i, k)),
            pl.BlockSpec((tmk, d), lambda b, i, k: (rb0 + b * nt + k, 0)),
            pl.BlockSpec((tmk, d), lambda b, i, k: (rb0 + b * nt + k, 1)),
        ],
        out_specs=pl.BlockSpec((tmk, d), lambda b, i, k: (b * nt + i, 0)),
        out_shape=jax.ShapeDtypeStruct((bsz * n_pos, d), F32),
        scratch_shapes=[pltpu.VMEM((tmk, d), F32)],
        compiler_params=_cparams(("parallel", "parallel", "arbitrary"), 48),
        name="fourier_positions",
    )(ct, st, y, y)


def _dft_tables(n):
    k = jnp.arange(n, dtype=jnp.int32)
    ang = ((k[:, None] * k[None, :]) % n).astype(F32) * (2.0 * math.pi / n)
    return jnp.cos(ang).astype(BF16), jnp.sin(ang).astype(BF16)


def _merge_kernel(x_ref, attn_l_ref, attn_c_ref, rec_ref, four_l_ref, four_c_ref, lg_ref, g0_ref, g1_ref, g2_ref,
                  wa_ref, wl_ref, wf_ref, wo_ref, gate_ref, n2_ref, sh_ref, sc_ref,
                  xo_ref, h_ref, hp_ref, *, lat_tiles):
    def mm(a, w_ref):
        return jnp.dot(a.astype(BF16), w_ref[...], preferred_element_type=F32)

    is_lat = pl.program_id(0) < lat_tiles
    attn_o = mm(jnp.where(is_lat, attn_l_ref[...], attn_c_ref[...]), wa_ref)
    rec_o = mm(rec_ref[...] * jax.nn.gelu(lg_ref[...], approximate=True), wl_ref)
    four_o = mm(jnp.where(is_lat, four_l_ref[...], four_c_ref[...]), wf_ref)
    mixed = (jax.nn.sigmoid(g0_ref[...]) * attn_o + jax.nn.sigmoid(g1_ref[...]) * rec_o
             + jax.nn.sigmoid(g2_ref[...]) * four_o)
    x = x_ref[...] + gate_ref[...] * mm(mixed, wo_ref)
    xo_ref[...] = x
    y = x * lax.rsqrt(jnp.mean(x * x, axis=-1, keepdims=True) + EPS)
    h = (y * n2_ref[...]) * (1.0 + sc_ref[...]) + sh_ref[...]
    h_ref[...] = h.astype(BF16)
    hp_ref[...] = _pack_halves(h)


def _merge(x_tok, attn_l, attn_c, rec, four_l, four_c, u, w_attn_o, w_lru_o, w_four_o, w_out, mod4, norm2_g,
           row_of_tile, n_rows):
    tm = TM
    d = D_MODEL
    lat_tiles = attn_l.shape[0] // tm
    ctx_tiles = attn_c.shape[0] // tm
    row = pl.BlockSpec((tm, d), lambda i: (i, 0))
    lat_row = pl.BlockSpec((tm, d), lambda i: (jnp.minimum(i, lat_tiles - 1), 0))
    ctx_row = pl.BlockSpec((tm, d), lambda i: (jnp.clip(i - lat_tiles, 0, ctx_tiles - 1), 0))
    wspec = pl.BlockSpec((d, d), lambda i: (0, 0))

    def ucol(cb):
        return pl.BlockSpec((tm, d), lambda i: (i, cb))

    return pl.pallas_call(
        functools.partial(_merge_kernel, lat_tiles=lat_tiles),
        grid=(n_rows // tm,),
        in_specs=[row, lat_row, ctx_row, row, lat_row, ctx_row, ucol(CB_LG), ucol(CB_G), ucol(CB_G + 1),
                  ucol(CB_G + 2), wspec, wspec, wspec, wspec,
                  _mod_spec(2, row_of_tile), pl.BlockSpec((1, d), lambda i: (0, 0)),
                  _mod_spec(3, row_of_tile), _mod_spec(4, row_of_tile)],
        out_specs=[row, row, pl.BlockSpec((tm, d // 2), lambda i: (i, 0))],
        out_shape=[jax.ShapeDtypeStruct((n_rows, d), F32), jax.ShapeDtypeStruct((n_rows, d), BF16),
                   jax.ShapeDtypeStruct((n_rows, d // 2), jnp.uint32)],
        compiler_params=_cparams(("parallel",), 56),
        name="merge_branches",
    )(x_tok, attn_l, attn_c, rec, four_l, four_c, u, u, u, u, w_attn_o, w_lru_o, w_four_o, w_out,
      mod4, norm2_g.reshape(1, d), mod4, mod4)


def _route_kernel(h_ref, rw_ref, rb_ref, tri_ref, idx_ref, wgt_ref, pos_ref, cnt_ref, base_scr):
    tm = h_ref.shape[0]

    @pl.when(pl.program_id(0) == 0)
    def _():
        base_scr[...] = jnp.zeros_like(base_scr)

    logits = lax.dot_general(rw_ref[...], h_ref[...], (((1,), (1,)), ((), ())), preferred_element_type=F32)
    scores = jax.nn.sigmoid(logits)
    biased = scores + rb_ref[...]
    neg = -jnp.inf
    e_iota = lax.broadcasted_iota(jnp.int32, (N_EXPERTS, tm), 0)
    g_iota = lax.broadcasted_iota(jnp.int32, (N_GROUPS, tm), 0)

    rows = []
    for g in range(N_GROUPS):
        xg = biased[g * GROUP_SIZE:(g + 1) * GROUP_SIZE]
        m1 = jnp.max(xg, axis=0, keepdims=True)
        is_max = xg == m1
        n_max = jnp.sum(is_max.astype(F32), axis=0, keepdims=True)
        m2 = jnp.max(jnp.where(is_max, neg, xg), axis=0, keepdims=True)
        rows.append(m1 + jnp.where(n_max >= 2.0, m1, m2))
    grp = jnp.concatenate(rows, axis=0)

    g_sel = jnp.zeros((N_GROUPS, tm), jnp.bool_)
    work = grp
    for _ in range(TOPK_GROUPS):
        m = jnp.max(work, axis=0, keepdims=True)
        first = jnp.min(jnp.where(work == m, g_iota, N_GROUPS), axis=0, keepdims=True)
        hit = g_iota == first
        g_sel = jnp.logical_or(g_sel, hit)
        work = jnp.where(hit, neg, work)
    g_sel_f = g_sel.astype(F32)
    mask_rows = [jnp.broadcast_to(g_sel_f[g:g + 1], (GROUP_SIZE, tm)) for g in range(N_GROUPS)]
    e_mask = jnp.concatenate(mask_rows, axis=0) > 0.5

    work = jnp.where(e_mask, biased, neg)
    sel = jnp.zeros((N_EXPERTS, tm), F32)
    idx_rows, w_rows, hits = [], [], []
    for _ in range(TOP_K):
        m = jnp.max(work, axis=0, keepdims=True)
        first = jnp.min(jnp.where(work == m, e_iota, N_EXPERTS), axis=0, keepdims=True)
        hit = e_iota == first
        hits.append(hit)
        idx_rows.append(first)
        w_rows.append(jnp.sum(jnp.where(hit, scores, 0.0), axis=0, keepdims=True))
        sel = jnp.where(hit, 1.0, sel)
        work = jnp.where(hit, neg, work)
    w = jnp.concatenate(w_rows, axis=0)
    wgt_ref[...] = w / jnp.sum(w, axis=0, keepdims=True) * ROUTED_SCALE
    idx_ref[...] = jnp.concatenate(idx_rows, axis=0)

    before = jnp.dot(sel.astype(BF16), tri_ref[...], preferred_element_type=F32)
    rank = base_scr[...] + before
    pos_rows = [jnp.sum(jnp.where(hit, rank, 0.0), axis=0, keepdims=True) for hit in hits]
    pos_ref[...] = jnp.concatenate(pos_rows, axis=0).astype(jnp.int32)
    base_scr[...] = base_scr[...] + jnp.sum(sel, axis=1, keepdims=True)
    cnt_ref[...] = jnp.broadcast_to(base_scr[...], cnt_ref.shape)


def _route(h16, rw_t, router_b, tm):
    n_tok, d = h16.shape
    tri = (jnp.arange(tm)[:, None] < jnp.arange(tm)[None, :]).astype(BF16)
    out_col = pl.BlockSpec((TOP_K, tm), lambda i: (0, i))
    return pl.pallas_call(
        _route_kernel,
        grid=(n_tok // tm,),
        in_specs=[pl.BlockSpec((tm, d), lambda i: (i, 0)),
                  pl.BlockSpec((N_EXPERTS, d), lambda i: (0, 0)),
                  pl.BlockSpec((N_EXPERTS, 1), lambda i: (0, 0)),
                  pl.BlockSpec((tm, tm), lambda i: (0, 0))],
        out_specs=[out_col, out_col, out_col, pl.BlockSpec((N_EXPERTS, LANES), lambda i: (0, 0))],
        out_shape=[jax.ShapeDtypeStruct((TOP_K, n_tok), jnp.int32), jax.ShapeDtypeStruct((TOP_K, n_tok), F32),
                   jax.ShapeDtypeStruct((TOP_K, n_tok), jnp.int32),
                   jax.ShapeDtypeStruct((N_EXPERTS, LANES), F32)],
        scratch_shapes=[pltpu.VMEM((N_EXPERTS, 1), F32)],
        compiler_params=_cparams(("arbitrary",), 32),
        name="moe_router",
    )(h16, rw_t, router_b.reshape(N_EXPERTS, 1), tri)


def _slot_kernel(idx_ref, pos_ref, offs_ref, dest_ref):
    tm = idx_ref.shape[1]
    e_iota = lax.broadcasted_iota(jnp.int32, (N_EXPERTS, tm), 0)
    idx = idx_ref[...]
    offs = offs_ref[...]
    rows = [jnp.sum(jnp.where(e_iota == idx[r:r + 1], offs, 0), axis=0, keepdims=True) for r in range(TOP_K)]
    dest_ref[...] = jnp.concatenate(rows, axis=0) + pos_ref[...]


def _slots(top_idx, pos, offs, tm):
    n_tok = top_idx.shape[1]
    col = pl.BlockSpec((TOP_K, tm), lambda i: (0, i))
    return pl.pallas_call(
        _slot_kernel,
        grid=(n_tok // tm,),
        in_specs=[col, col, pl.BlockSpec((N_EXPERTS, 1), lambda i: (0, 0))],
        out_specs=col,
        out_shape=jax.ShapeDtypeStruct((TOP_K, n_tok), jnp.int32),
        compiler_params=_cparams(("parallel",), 32),
        name="moe_slots",
    )(top_idx, pos, offs.reshape(N_EXPERTS, 1))


def _row_copy(src_ref, src_row, dst_ref, dst_row, sem):
    return pltpu.make_async_copy(src_ref.at[pl.ds(src_row, 1)], dst_ref.at[pl.ds(dst_row, 1)], sem)


def _dispatch_kernel(dest_ref, hp_ref, xs_ref, sem):
    tm = hp_ref.shape[0]

    def start(t, c):
        for r in range(TOP_K):
            _row_copy(hp_ref, t, xs_ref, dest_ref[r, t], sem).start()
        return c

    lax.fori_loop(0, tm, start, 0)

    def wait(t, c):
        for r in range(TOP_K):
            _row_copy(hp_ref, t, xs_ref, dest_ref[r, t], sem).wait()
        return c

    lax.fori_loop(0, tm, wait, 0)


def _dispatch(hp, dest3, n_slots):
    n_tok, half = hp.shape
    tm = dest3.shape[2]
    return pl.pallas_call(
        _dispatch_kernel,
        grid=(n_tok // tm,),
        in_specs=[pl.BlockSpec((None, TOP_K, tm), lambda i: (i, 0, 0), memory_space=pltpu.SMEM),
                  pl.BlockSpec((tm, half), lambda i: (i, 0))],
        out_specs=pl.BlockSpec(memory_space=pl.ANY),
        out_shape=jax.ShapeDtypeStruct((n_slots, half), jnp.uint32),
        scratch_shapes=[pltpu.SemaphoreType.DMA(())],
        compiler_params=_cparams(("arbitrary",), 32),
        name="moe_dispatch",
    )(dest3, hp)


def _expert_kernel(be_ref, nu_ref, nv_ref, xs_ref, wg_ref, wu_ref, wd_ref, ylo_ref, yhi_ref, wg_s, wu_s, wd_s):
    i = pl.program_id(0)
    used = i < nu_ref[0]
    prev_e = be_ref[jnp.maximum(i - 1, 0)]
    fresh = jnp.logical_or(i == 0, be_ref[i] != prev_e)

    @pl.when(jnp.logical_and(used, fresh))
    def _():
        wg_s[...] = wg_ref[0].astype(BF16)
        wu_s[...] = wu_ref[0].astype(BF16)
        wd_s[...] = wd_ref[0].astype(BF16)

    @pl.when(used)
    def _():
        w = xs_ref[...]
        row = lax.broadcasted_iota(jnp.int32, w.shape, 0)
        w = jnp.where(row < nv_ref[i], w, jnp.uint32(0))
        half = D_MODEL // 2
        xa, xb = _unpack_halves(w)
        xa = xa.astype(BF16)
        xb = xb.astype(BF16)

        def proj(w_s):
            return (jnp.dot(xa, w_s[:half, :], preferred_element_type=F32)
                    + jnp.dot(xb, w_s[half:, :], preferred_element_type=F32))

        g = proj(wg_s)
        act = (_silu(g) * proj(wu_s)).astype(BF16)
        y = jnp.dot(act, wd_s[...], preferred_element_type=F32)
        ylo_ref[...] = _pack_halves(y[:, :half])
        yhi_ref[...] = _pack_halves(y[:, half:])


def _experts(xs, block_e, n_used, n_valid, w_gate, w_up, w_down, layer, bm):
    n_slots, half = xs.shape
    d = D_MODEL

    def blk(i, be, nu, nv):
        return (jnp.minimum(i, nu[0] - 1), 0)

    def wsel(i, be, nu, nv):
        return (layer, be[i], 0, 0)

    grid_spec = pltpu.PrefetchScalarGridSpec(
        num_scalar_prefetch=3,
        grid=(n_slots // bm,),
        in_specs=[pl.BlockSpec((bm, half), blk),
                  pl.BlockSpec((None, 1, d, EXPERT_FF), wsel),
                  pl.BlockSpec((None, 1, d, EXPERT_FF), wsel),
                  pl.BlockSpec((None, 1, EXPERT_FF, d), wsel)],
        out_specs=[pl.BlockSpec((bm, half // 2), blk), pl.BlockSpec((bm, half // 2), blk)],
        scratch_shapes=[pltpu.VMEM((d, EXPERT_FF), BF16), pltpu.VMEM((d, EXPERT_FF), BF16),
                        pltpu.VMEM((EXPERT_FF, d), BF16)],
    )
    y_shape = jax.ShapeDtypeStruct((n_slots, half // 2), jnp.uint32)
    return pl.pallas_call(
        _expert_kernel,
        grid_spec=grid_spec,
        out_shape=[y_shape, y_shape],
        compiler_params=_cparams(("arbitrary",), 40),
        name="moe_experts",
    )(block_e, n_used, n_valid, xs, w_gate, w_up, w_down)


def _sc_gather_rows(table, idx):
    n_idx = idx.shape[0]
    width = table.shape[1]
    mesh = plsc.VectorSubcoreMesh(core_axis_name="core", subcore_axis_name="subcore")

    @pl.kernel(out_type=jax.ShapeDtypeStruct((n_idx, width), table.dtype), mesh=mesh, scratch_types=[],
               name="moe_gather_sc")
    def gather(table_hbm, idx_hbm, out_hbm):
        def body(idx_vmem, out_vmem):
            pltpu.sync_copy(table_hbm.at[idx_vmem.at[0]], out_vmem)

        pltpu.emit_pipeline(
            body,
            grid=(n_idx // SC_WINDOW,),
            in_specs=[pl.BlockSpec((1, SC_WINDOW), index_map=lambda i: (0, i))],
            out_specs=[pl.BlockSpec((SC_WINDOW, width), index_map=lambda i: (i, 0))],
            core_axis_name=("core", "subcore"),
            dimension_semantics=(pltpu.PARALLEL,),
        )(idx_hbm, out_hbm)

    return gather(table, idx.reshape(1, n_idx))


def _combine_kernel(x_ref, h_ref, tw_ref, gate_ref, wsg_ref, wsu_ref, wsd_ref, lo_ref, hi_ref, o_ref):
    quarter = D_MODEL // 4
    h = h_ref[...]
    act = (_silu(jnp.dot(h, wsg_ref[...], preferred_element_type=F32))
           * jnp.dot(h, wsu_ref[...], preferred_element_type=F32)).astype(BF16)
    shared = jnp.dot(act, wsd_ref[...], preferred_element_type=F32)

    tw = tw_ref[...]
    gate = gate_ref[...]
    for q, buf in enumerate((lo_ref, hi_ref)):
        moe_a = moe_b = None
        for r in range(TOP_K):
            ya, yb = _unpack_halves(buf[r])
            wr = tw[:, r:r + 1]
            moe_a = ya * wr if moe_a is None else moe_a + ya * wr
            moe_b = yb * wr if moe_b is None else moe_b + yb * wr
        for part, moe in ((2 * q, moe_a), (2 * q + 1, moe_b)):
            cols = slice(part * quarter, (part + 1) * quarter)
            o_ref[:, cols] = x_ref[:, cols] + gate[:, cols] * (moe + shared[:, cols])


def _combine(x_tok, h16, g_lo, g_hi, top_w, mod4, ws_gate, ws_up, ws_down, row_of_tile):
    n_tok, d = x_tok.shape
    tm = TM
    ff = ws_gate.shape[1]
    row = pl.BlockSpec((tm, d), lambda i: (i, 0))
    gat = pl.BlockSpec((TOP_K, tm, d // 4), lambda i: (0, i, 0))
    return pl.pallas_call(
        _combine_kernel,
        grid=(n_tok // tm,),
        in_specs=[row, row,
                  pl.BlockSpec((tm, TOP_K), lambda i: (i, 0)),
                  _mod_spec(5, row_of_tile),
                  pl.BlockSpec((d, ff), lambda i: (0, 0)),
                  pl.BlockSpec((d, ff), lambda i: (0, 0)),
                  pl.BlockSpec((ff, d), lambda i: (0, 0)),
                  gat, gat],
        out_specs=row,
        out_shape=jax.ShapeDtypeStruct((n_tok, d), F32),
        compiler_params=_cparams(("parallel",), 48),
        name="moe_combine",
    )(x_tok, h16, top_w, mod4, ws_gate, ws_up, ws_down, g_lo, g_hi)


def _moe(x_tok, h16, hp, mod4, row_of_tile, rw_t, router_b, w_gate, w_up, w_down, layer, ws_gate, ws_up, ws_down):
    n_tok = x_tok.shape[0]
    bm = MOE_BM
    tm_r = TM_ROUTE if n_tok % TM_ROUTE == 0 else TM
    top_idx, top_w, pos, cnt = _route(h16, rw_t, router_b, tm_r)

    counts = cnt[:, 0].astype(jnp.int32)
    padded = (counts + bm - 1) // bm * bm
    ends = jnp.cumsum(padded)
    n_blocks = -(-(n_tok * TOP_K + N_EXPERTS * (bm - 1)) // bm)
    offs = ends - padded
    dest = _slots(top_idx, pos, offs, tm_r)
    block_start = jnp.arange(n_blocks, dtype=jnp.int32) * bm
    block_e = jnp.minimum(jnp.sum((ends[None, :] <= block_start[:, None]).astype(jnp.int32), axis=1),
                          N_EXPERTS - 1)
    n_valid = jnp.clip(counts[block_e] - (block_start - offs[block_e]), 0, bm).astype(jnp.int32)
    n_used = (ends[-1] // bm).astype(jnp.int32).reshape(1)
    tm = TM
    dest3 = dest.reshape(TOP_K, n_tok // tm, tm).transpose(1, 0, 2)

    xs = _dispatch(hp, dest3, n_blocks * bm)
    ys_lo, ys_hi = _experts(xs, block_e, n_used, n_valid, w_gate, w_up, w_down, layer, bm)
    dest_flat = dest.reshape(TOP_K * n_tok)
    g_lo = _sc_gather_rows(ys_lo, dest_flat).reshape(TOP_K, n_tok, D_MODEL // 4)
    g_hi = _sc_gather_rows(ys_hi, dest_flat).reshape(TOP_K, n_tok, D_MODEL // 4)
    return _combine(x_tok, h16, g_lo, g_hi, top_w.T, mod4, ws_gate, ws_up, ws_down, row_of_tile)


def _block_diag_tiles(w):
    per = MXU_DIM // LRU_BLOCK_DIM
    w = w.reshape(2, D_MODEL // MXU_DIM, per, LRU_BLOCK_DIM, LRU_BLOCK_DIM)
    eye = jnp.eye(per, dtype=w.dtype)
    t = jnp.einsum('ztpde,pq->ztpdqe', w, eye)
    return t.reshape(2, D_MODEL // MXU_DIM, MXU_DIM, MXU_DIM).astype(BF16)


def _rope_tables(n_lat, n_ctx):
    rows = n_lat // GRID_W
    row = jnp.repeat(jnp.arange(rows), GRID_W)
    col = jnp.tile(jnp.arange(GRID_W), rows)
    inv = ROPE_THETA ** (-jnp.arange(ROPE_FREQS, dtype=F32) / ROPE_FREQS)
    ang = jnp.stack([row[:, None] * inv, col[:, None] * inv], axis=1)
    ang = jnp.concatenate([ang, ang], axis=2).reshape(n_lat, HEAD_DIM)
    ang = jnp.tile(ang, (1, LANES // HEAD_DIM))
    ang = jnp.concatenate([ang, jnp.zeros((n_ctx, LANES), F32)], axis=0)
    lane = jnp.arange(LANES)
    sign = jnp.where((lane % (2 * ROPE_FREQS)) < ROPE_FREQS, -1.0, 1.0).astype(F32)
    return jnp.cos(ang), jnp.sin(ang) * sign


def kernel(x, c, ctx, c_ctx, w_mod, b_mod, norm1_g, w_in, q_norm_g, k_norm_g, lambda_qk, subln_g, w_attn_o, conv_w, conv_b, lru_wa, lru_ba, lru_wi, lru_bi, lru_lambda, w_lru_o, w_four_o, w_out, norm2_g, router_w, router_b, exp_w_gate, exp_w_up, exp_w_down, sh_w_gate, sh_w_up, sh_w_down):
    bsz, n_lat, d = x.shape
    n_ctx = ctx.shape[1]
    depth = w_mod.shape[0]
    n_latent_rows = bsz * n_lat
    n_tok = n_latent_rows + bsz * n_ctx
    assert d == D_MODEL and bsz + 1 <= MOD_ROWS
    assert n_lat % TM == 0 and n_ctx % TM == 0 and n_lat % GRID_W == 0
    tm_in = TM_IN if (n_lat % TM_IN == 0 and (bsz * n_ctx) % TM_IN == 0) else TM

    cvec = jnp.zeros((MOD_ROWS, d), F32).at[:bsz].set(c).at[bsz].set(c_ctx)
    mods = _modulation(cvec, w_mod, b_mod)

    def row_of_tile_fn(tm):
        per_batch = n_lat // tm
        return lambda i: jnp.minimum(i // per_batch, bsz)

    assert n_lat % n_ctx == 0
    cos_t, sin_t = _rope_tables(n_lat, n_ctx)
    group_mat = jnp.kron(jnp.eye(d // HEAD_DIM, dtype=F32), jnp.ones((HEAD_DIM, HEAD_DIM), F32)).astype(BF16)
    m = jnp.arange(FOUR_GROUP_DIM, dtype=jnp.int32)
    ang_c = ((m[:, None] * m[None, :]) % FOUR_GROUP_DIM).astype(F32) * (2.0 * math.pi / FOUR_GROUP_DIM)
    inv_sqrt_c = FOUR_GROUP_DIM ** -0.5
    cs_mat = jnp.concatenate([jnp.cos(ang_c) * inv_sqrt_c, jnp.sin(ang_c) * inv_sqrt_c], axis=1).astype(BF16)
    assert n_lat % (FFT_RADIX * MXU_DIM) == 0 or n_lat // FFT_RADIX < MXU_DIM
    fft_tk1 = min(MXU_DIM, n_lat // FFT_RADIX)
    fft_cc, fft_pc, fft_ps = _fft_tables(n_lat, FFT_RADIX, fft_tk1)
    ct_ctx, st_ctx = _dft_tables(n_ctx)
    q_sub = math.gcd(Q_SUB, n_lat // TQ)
    n_keys = n_ctx + n_lat
    tk = TK if n_keys % TK == 0 else TM

    x_tok = jnp.concatenate([x.reshape(n_latent_rows, d), ctx.reshape(bsz * n_ctx, d)], axis=0)
    for l in range(depth):
        last = l == depth - 1
        lam_init = 0.8 - 0.6 * math.exp(-0.3 * l)
        mod4 = mods[l].reshape(MOD_ROWS, 6, 1, d)

        u = _in_projection(x_tok, mod4, norm1_g[l], w_in[l].astype(BF16), row_of_tile_fn(tm_in), tm_in)

        k_all, q_all, vt_all = _qkv_prepare(u, k_norm_g[l], q_norm_g[l], group_mat, cos_t, sin_t,
                                            bsz, n_lat, n_ctx)
        attn_l = _diff_attention(q_all, k_all, vt_all, lambda_qk[l], subln_g[l], lam_init,
                                 0, n_lat, 0, n_keys, TQ, q_sub, tk)
        attn_c = attn_l
        if not last:
            attn_c = _diff_attention(q_all, k_all, vt_all, lambda_qk[l], subln_g[l], lam_init,
                                     n_lat, n_ctx, n_lat, n_ctx, TM, 1, TM)

        rec = _rglru(u, conv_w[l], conv_b[l], _block_diag_tiles(lru_wa[l]), lru_ba[l],
                     _block_diag_tiles(lru_wi[l]), lru_bi[l], lru_lambda[l], bsz, n_lat, n_ctx)

        y_lat = _chan_dft_split(u, cs_mat, bsz, n_lat, FFT_RADIX)
        four_l = _time_fft(y_lat, fft_cc, fft_pc, fft_ps, FFT_RADIX, fft_tk1, MXU_DIM).reshape(n_latent_rows, d)
        four_c = four_l
        if not last:
            y_ctx = _chan_dft(u, cs_mat, n_latent_rows, bsz * n_ctx)
            four_c = _time_dft(y_ctx, ct_ctx, st_ctx, bsz, n_ctx, 0, n_ctx)

        n_rows = n_latent_rows if last else n_tok
        row_of_tile = row_of_tile_fn(TM)
        x_mid, h16, hp = _merge(x_tok, attn_l, attn_c, rec, four_l, four_c, u, w_attn_o[l].astype(BF16),
                                w_lru_o[l].astype(BF16), w_four_o[l].astype(BF16), w_out[l].astype(BF16),
                                mod4, norm2_g[l], row_of_tile, n_rows)
        x_tok = _moe(x_mid, h16, hp, mod4, row_of_tile, router_w[l].T.astype(BF16), router_b[l],
                     exp_w_gate, exp_w_up, exp_w_down, l,
                     sh_w_gate[l].astype(BF16), sh_w_up[l].astype(BF16), sh_w_down[l].astype(BF16))
    return x_tok[:n_latent_rows].reshape(bsz, n_lat, d)
```

```python
import functools
import math

import jax
import jax.numpy as jnp
from jax import lax
from jax.experimental import pallas as pl
from jax.experimental.pallas import tpu as pltpu
from jax.experimental.pallas import tpu_sc as plsc

F32 = jnp.float32
BF16 = jnp.bfloat16

D_MODEL = 1024
EPS = 1e-6
GRID_W = 64

N_HEADS = 8
HEAD_DIM = 64
V_DIM = 2 * HEAD_DIM
ATTN_SCALE = HEAD_DIM ** -0.5
ROPE_THETA = 10000.0
ROPE_FREQS = HEAD_DIM // 4

LRU_BLOCKS = 16
LRU_BLOCK_DIM = D_MODEL // LRU_BLOCKS
LRU_C = 8.0
CONV_W = 4

FOUR_GROUPS = 4
FOUR_GROUP_DIM = D_MODEL // FOUR_GROUPS

CB_K, CB_V, CB_LX, CB_Q, CB_LG, CB_F, CB_G = 0, 1, 2, 3, 4, 5, 6
IN_BLOCKS = 9

N_EXPERTS = 256
TOP_K = 8
N_GROUPS = 8
GROUP_SIZE = N_EXPERTS // N_GROUPS
TOPK_GROUPS = 4
EXPERT_FF = 256
ROUTED_SCALE = 2.5

LANES = 128
SUBLANES = 8
MXU_DIM = 256
VMEM_BYTES_V7X = 64 * 1024 * 1024

MOD_ROWS = 8
TM = 256
TM_IN = 512
TQ = 256
Q_SUB = 32
TK = 768
TM_ROUTE = 512
MOE_BM = 512
SC_WINDOW = 128
FFT_RADIX = 8
LOG2E = 1.4426950408889634


def _cparams(sem, vmem_mb):
    return pltpu.CompilerParams(dimension_semantics=sem, vmem_limit_bytes=int(vmem_mb * 1024 * 1024))


def _silu(x):
    return x * jax.nn.sigmoid(x)


def _pack_halves(x):
    half = x.shape[1] // 2
    bits = pltpu.bitcast(x.astype(BF16).astype(F32), jnp.uint32)
    return (bits[:, :half] & jnp.uint32(0xFFFF0000)) | (bits[:, half:] >> 16)


def _unpack_halves(w):
    return pltpu.bitcast(w & jnp.uint32(0xFFFF0000), F32), pltpu.bitcast(w << 16, F32)


def _mod_kernel(c_ref, w_ref, b_ref, o_ref):
    c = c_ref[...]
    s = _silu(c).astype(BF16)
    o_ref[0] = jnp.dot(s, w_ref[0].astype(BF16), preferred_element_type=F32) + b_ref[0]


def _modulation(cvec, w_mod, b_mod):
    n_layers, d, six_d = w_mod.shape
    nj = six_d // d
    return pl.pallas_call(
        _mod_kernel,
        grid=(n_layers, nj),
        in_specs=[
            pl.BlockSpec((MOD_ROWS, d), lambda l, j: (0, 0)),
            pl.BlockSpec((1, d, d), lambda l, j: (l, 0, j)),
            pl.BlockSpec((1, 1, d), lambda l, j: (l, 0, j)),
        ],
        out_specs=pl.BlockSpec((1, MOD_ROWS, d), lambda l, j: (l, 0, j)),
        out_shape=jax.ShapeDtypeStruct((n_layers, MOD_ROWS, six_d), F32),
        compiler_params=_cparams(("parallel", "parallel"), 32),
        name="modulation",
    )(cvec, w_mod, b_mod.reshape(n_layers, 1, six_d))


def _mod_spec(part, row_of_tile, n_grid_axes=1):
    if n_grid_axes == 1:
        return pl.BlockSpec((None, None, 1, D_MODEL), lambda i: (row_of_tile(i), part, 0, 0))
    return pl.BlockSpec((None, None, 1, D_MODEL), lambda i, j: (row_of_tile(i), part, 0, 0))


def _inproj_kernel(x_ref, g_ref, sh_ref, sc_ref, w_ref, o_ref, h_scr):
    @pl.when(pl.program_id(1) == 0)
    def _():
        x = x_ref[...]
        y = x * lax.rsqrt(jnp.mean(x * x, axis=-1, keepdims=True) + EPS)
        h = (y * g_ref[...]) * (1.0 + sc_ref[...]) + sh_ref[...]
        h_scr[...] = h.astype(BF16)

    o_ref[...] = jnp.dot(h_scr[...], w_ref[...], preferred_element_type=F32)


def _in_projection(x_tok, mod4, norm_g, w_in_bf16, row_of_tile, tm):
    n_tok, d = x_tok.shape
    n_cols = w_in_bf16.shape[1]
    return pl.pallas_call(
        _inproj_kernel,
        grid=(n_tok // tm, n_cols // d),
        in_specs=[
            pl.BlockSpec((tm, d), lambda i, j: (i, 0)),
            pl.BlockSpec((1, d), lambda i, j: (0, 0)),
            _mod_spec(0, row_of_tile, 2),
            _mod_spec(1, row_of_tile, 2),
            pl.BlockSpec((d, d), lambda i, j: (0, j)),
        ],
        out_specs=pl.BlockSpec((tm, d), lambda i, j: (i, j)),
        out_shape=jax.ShapeDtypeStruct((n_tok, n_cols), F32),
        scratch_shapes=[pltpu.VMEM((tm, d), BF16)],
        compiler_params=_cparams(("parallel", "arbitrary"), 40),
        name="in_projection",
    )(x_tok, norm_g.reshape(1, d), mod4, mod4, w_in_bf16)


def _prep_kernel(uk_ref, uq_ref, uv_ref, kg_ref, qg_ref, gm_ref, cos_ref, sin_ref, k_out, q_out, vt_out, *, tm):
    gm = gm_ref[...]
    lane = lax.broadcasted_iota(jnp.int32, (tm, LANES), 1)
    first_half = (lane % (2 * ROPE_FREQS)) < ROPE_FREQS

    def norm_rope(x, g):
        ss = jnp.dot((x * x).astype(BF16), gm, preferred_element_type=F32) * (1.0 / HEAD_DIM)
        y = x * lax.rsqrt(ss + EPS) * g
        c = cos_ref[...]
        s = sin_ref[...]
        outs = []
        for j in range(D_MODEL // LANES):
            yt = y[:, j * LANES:(j + 1) * LANES]
            rot = jnp.where(first_half, pltpu.roll(yt, LANES - ROPE_FREQS, 1), pltpu.roll(yt, ROPE_FREQS, 1))
            outs.append(yt * c + rot * s)
        return jnp.concatenate(outs, axis=1)

    k_out[...] = norm_rope(uk_ref[...], kg_ref[...]).astype(BF16)
    qn = norm_rope(uq_ref[...], qg_ref[...]) * (ATTN_SCALE * LOG2E)
    lo = lane < HEAD_DIM
    for h in range(N_HEADS):
        qt = qn[:, h * LANES:(h + 1) * LANES]
        q_out[:, (2 * h) * LANES:(2 * h + 1) * LANES] = jnp.where(lo, qt, 0.0).astype(BF16)
        q_out[:, (2 * h + 1) * LANES:(2 * h + 2) * LANES] = jnp.where(lo, 0.0, qt).astype(BF16)
    vt_out[...] = uv_ref[...].T.astype(BF16)


def _qkv_prepare(u, k_gain, q_gain, group_mat, cos_t, sin_t, bsz, n_lat, n_ctx):
    tm = TM
    d = D_MODEL
    n_s, n_c = n_lat // tm, n_ctx // tm
    n_rows = n_lat + n_ctx
    lat_tiles = bsz * n_s
    kg = jnp.tile(k_gain, d // HEAD_DIM).reshape(1, d)
    qg = jnp.tile(q_gain, d // HEAD_DIM).reshape(1, d)

    def tok_tile(b, j):
        return jnp.where(j < n_s, b * n_s + j, lat_tiles + b * n_c + (j - n_s))

    def u_spec(cb):
        return pl.BlockSpec((tm, d), lambda b, j: (tok_tile(b, j), cb))

    vec = pl.BlockSpec((1, d), lambda b, j: (0, 0))
    rope_spec = pl.BlockSpec((tm, LANES), lambda b, j: (j, 0))
    return pl.pallas_call(
        functools.partial(_prep_kernel, tm=tm),
        grid=(bsz, n_s + n_c),
        in_specs=[u_spec(CB_K), u_spec(CB_Q), u_spec(CB_V), vec, vec,
                  pl.BlockSpec((d, d), lambda b, j: (0, 0)), rope_spec, rope_spec],
        out_specs=[
            pl.BlockSpec((None, tm, d), lambda b, j: (b, j, 0)),
            pl.BlockSpec((None, tm, 2 * d), lambda b, j: (b, j, 0)),
            pl.BlockSpec((None, d, tm), lambda b, j: (b, 0, j)),
        ],
        out_shape=[jax.ShapeDtypeStruct((bsz, n_rows, d), BF16),
                   jax.ShapeDtypeStruct((bsz, n_rows, 2 * d), BF16),
                   jax.ShapeDtypeStruct((bsz, d, n_rows), BF16)],
        compiler_params=_cparams(("parallel", "parallel"), 40),
        name="qkv_prepare",
    )(u, u, u, kg, qg, group_mat, cos_t, sin_t)


def _attn_kernel(lq_ref, sg_ref, q1_ref, q2_ref, k_ref, vt_ref, o_ref, s_buf, *, n_keys, tk, tq, lam_init):
    lq = lq_ref[...]
    lam = (jnp.exp(jnp.sum(lq[0:1] * lq[1:2], axis=1, keepdims=True))
           - jnp.exp(jnp.sum(lq[2:3] * lq[3:4], axis=1, keepdims=True)) + lam_init)
    n_chunks = n_keys // tk
    n_sub = q1_ref.shape[0] // tq

    def load_q(j):
        rows = pl.ds(pl.multiple_of(j * tq, tq), tq)
        return jnp.concatenate([q1_ref[rows, :], q2_ref[rows, :]], axis=0)

    def score_chunk(q, c, mx):
        s = lax.dot_general(k_ref[c * tk:(c + 1) * tk, :], q, (((1,), (1,)), ((), ())),
                            preferred_element_type=F32)
        s_buf[c * tk:(c + 1) * tk, :] = s
        cm = jnp.max(s, axis=0, keepdims=True)
        return cm if mx is None else jnp.maximum(mx, cm)

    q0 = load_q(0)
    mx0 = None
    for c in range(n_chunks):
        mx0 = score_chunk(q0, c, mx0)

    def sub_tile(j, mx):
        q_next = load_q(jnp.minimum(j + 1, n_sub - 1))
        mx_next = None
        l = jnp.zeros((1, 2 * tq), F32)
        acc = jnp.zeros((V_DIM, 2 * tq), F32)
        for c in range(n_chunks):
            p = jnp.exp2(s_buf[c * tk:(c + 1) * tk, :] - mx)
            l = l + jnp.sum(p, axis=0, keepdims=True)
            acc = acc + jnp.dot(vt_ref[:, c * tk:(c + 1) * tk], p.astype(BF16), preferred_element_type=F32)
            mx_next = score_chunk(q_next, c, mx_next)
        o = acc / l
        o = o[:, :tq] - lam * o[:, tq:]
        ms = jnp.mean(o * o, axis=0, keepdims=True)
        on = (o * lax.rsqrt(ms + EPS) * sg_ref[...]) * (1.0 - lam_init)
        o_ref[pl.ds(pl.multiple_of(j * tq, tq), tq), :] = on.T
        return mx_next

    lax.fori_loop(0, n_sub, sub_tile, mx0)


def _diff_attention(q_all, k_all, vt_all, lambda_qk, subln_g, lam_init, q_row0, n_q, key_row0, n_keys, tq, n_sub, tk):
    bsz = q_all.shape[0]
    d = D_MODEL
    tstep = tq * n_sub
    nq = n_q // tstep
    qb0 = q_row0 // tstep
    kb0 = key_row0 // n_keys
    s_bytes = n_keys * 2 * tq * 4
    return pl.pallas_call(
        functools.partial(_attn_kernel, n_keys=n_keys, tk=tk, tq=tq, lam_init=lam_init),
        grid=(bsz, N_HEADS, nq),
        in_specs=[
            pl.BlockSpec((4, HEAD_DIM), lambda b, h, i: (0, 0)),
            pl.BlockSpec((V_DIM, 1), lambda b, h, i: (0, 0)),
            pl.BlockSpec((None, tstep, LANES), lambda b, h, i: (b, qb0 + i, 2 * h)),
            pl.BlockSpec((None, tstep, LANES), lambda b, h, i: (b, qb0 + i, 2 * h + 1)),
            pl.BlockSpec((None, n_keys, LANES), lambda b, h, i: (b, kb0, h)),
            pl.BlockSpec((None, V_DIM, n_keys), lambda b, h, i: (b, h, kb0)),
        ],
        out_specs=pl.BlockSpec((tstep, LANES), lambda b, h, i: (b * nq + i, h)),
        out_shape=jax.ShapeDtypeStruct((bsz * n_q, d), F32),
        scratch_shapes=[pltpu.VMEM((n_keys, 2 * tq), F32)],
        compiler_params=_cparams(("parallel", "parallel", "arbitrary"), 28 + s_bytes / 2 ** 20),
        name="diff_attention",
    )(lambda_qk, subln_g.reshape(V_DIM, 1), q_all, q_all, k_all, vt_all)


def _lru_kernel(*refs, rev, n_c, n_s, tm):
    if rev:
        (cur_ref, prev_ref, next_ref, cw_ref, cb_ref, wa_ref, ba_ref, wi_ref, bi_ref, lam_ref, fwd_ref,
         o_ref, a_scr, b_scr, h_scr) = refs
    else:
        (cur_ref, prev_ref, next_ref, cw_ref, cb_ref, wa_ref, ba_ref, wi_ref, bi_ref, lam_ref,
         o_ref, a_scr, b_scr, h_scr) = refs
    s = pl.program_id(1)
    is_ctx = s < n_c
    if rev:
        tile = jnp.where(is_ctx, n_c - 1 - s, n_s - 1 - (s - n_c))
    else:
        tile = jnp.where(is_ctx, s, s - n_c)
    n_tile = jnp.where(is_ctx, n_c, n_s)

    @pl.when(s == 0)
    def _():
        h_scr[...] = jnp.zeros_like(h_scr)

    x = cur_ref[...]
    pv = jnp.where(tile == 0, 0.0, prev_ref[...])
    nx = jnp.where(tile == n_tile - 1, 0.0, next_ref[...])
    r8 = lax.broadcasted_iota(jnp.int32, (SUBLANES, D_MODEL), 0)

    def earlier(k):
        rolled = pltpu.roll(x, k, 0)
        top = jnp.where(r8 < k, pltpu.roll(pv, k, 0), rolled[:SUBLANES])
        return jnp.concatenate([top, rolled[SUBLANES:]], axis=0)

    rolled = pltpu.roll(x, tm - 1, 0)
    bot = jnp.where(r8 >= SUBLANES - 1, pltpu.roll(nx, SUBLANES - 1, 0), rolled[tm - SUBLANES:])
    later = jnp.concatenate([rolled[:tm - SUBLANES], bot], axis=0)
    cw = cw_ref[...]
    conv = earlier(2) * cw[0:1] + earlier(1) * cw[1:2] + x * cw[2:3] + later * cw[3:4] + cb_ref[...]

    conv16 = conv.astype(BF16)

    def block_diag(w_ref, b_ref):
        parts = [jnp.dot(conv16[:, j * MXU_DIM:(j + 1) * MXU_DIM], w_ref[j], preferred_element_type=F32)
                 for j in range(D_MODEL // MXU_DIM)]
        return jnp.concatenate(parts, axis=1) + b_ref[...]

    r = jax.nn.sigmoid(block_diag(wa_ref, ba_ref))
    gate_i = jax.nn.sigmoid(block_diag(wi_ref, bi_ref))
    neg_lam = -lam_ref[...]
    softplus = jnp.maximum(neg_lam, 0.0) + jnp.log1p(jnp.exp(-jnp.abs(neg_lam)))
    log_a = -LRU_C * r * softplus
    a = jnp.exp(log_a)
    mult = jnp.sqrt(1.0 - a * a)
    a_scr[...] = a
    b_scr[...] = mult * gate_i * conv

    def body(t, h):
        row = tm - 1 - t if rev else t
        h = a_scr[pl.ds(row, 1), :] * h + b_scr[pl.ds(row, 1), :]
        if rev:
            o_ref[pl.ds(row, 1), :] = fwd_ref[pl.ds(row, 1), :] + h
        else:
            o_ref[pl.ds(row, 1), :] = h
        return h

    h_scr[...] = lax.fori_loop(0, tm, body, h_scr[...], unroll=8)


def _rglru(u, conv_w, conv_b, wa_bd, ba, wi_bd, bi, lam, bsz, n_lat, n_ctx):
    tm = TM
    d = D_MODEL
    n_tok = u.shape[0]
    n_s, n_c = n_lat // tm, n_ctx // tm
    lat_tiles = bsz * n_s
    per8 = tm // SUBLANES
    last8 = n_tok // SUBLANES - 1

    def make(rev, fwd):
        def blk(b, s):
            is_ctx = s < n_c
            if rev:
                tile = jnp.where(is_ctx, n_c - 1 - s, n_s - 1 - (s - n_c))
            else:
                tile = jnp.where(is_ctx, s, s - n_c)
            return jnp.where(is_ctx, lat_tiles + b * n_c + tile, b * n_s + tile)

        vec = pl.BlockSpec((None, 1, d), lambda b, s: (int(rev), 0, 0))
        wspec = pl.BlockSpec((None, d // MXU_DIM, MXU_DIM, MXU_DIM), lambda b, s: (int(rev), 0, 0, 0))
        row_spec = pl.BlockSpec((tm, d), lambda b, s: (blk(b, s), 0))
        in_specs = [
            pl.BlockSpec((tm, d), lambda b, s: (blk(b, s), CB_LX)),
            pl.BlockSpec((SUBLANES, d), lambda b, s: (jnp.maximum(blk(b, s) * per8 - 1, 0), CB_LX)),
            pl.BlockSpec((SUBLANES, d), lambda b, s: (jnp.minimum((blk(b, s) + 1) * per8, last8), CB_LX)),
            pl.BlockSpec((CONV_W, d), lambda b, s: (0, 0)),
            pl.BlockSpec((1, d), lambda b, s: (0, 0)),
            wspec, vec, wspec, vec, vec,
        ]
        args = [u, u, u, conv_w, conv_b.reshape(1, d), wa_bd, ba.reshape(2, 1, d), wi_bd, bi.reshape(2, 1, d),
                lam.reshape(2, 1, d)]
        if rev:
            in_specs.append(row_spec)
            args.append(fwd)
        return pl.pallas_call(
            functools.partial(_lru_kernel, rev=rev, n_c=n_c, n_s=n_s, tm=tm),
            grid=(bsz, n_c + n_s),
            in_specs=in_specs,
            out_specs=row_spec,
            out_shape=jax.ShapeDtypeStruct((n_tok, d), F32),
            scratch_shapes=[pltpu.VMEM((tm, d), F32), pltpu.VMEM((tm, d), F32), pltpu.VMEM((1, d), F32)],
            input_output_aliases={10: 0} if rev else {},
            compiler_params=_cparams(("arbitrary", "arbitrary"), 40),
            name="rglru_bwd" if rev else "rglru_fwd",
        )(*args)

    fwd = make(False, None)
    return make(True, fwd)


def _chan_dft_kernel(u_ref, cs_ref, o_ref):
    x = u_ref[...].astype(BF16)
    cs = cs_ref[...]
    gd = FOUR_GROUP_DIM
    for g in range(FOUR_GROUPS):
        y = jnp.dot(x[:, g * gd:(g + 1) * gd], cs, preferred_element_type=F32)
        o_ref[:, g * gd:(g + 1) * gd] = y[:, :gd].astype(BF16)
        o_ref[:, D_MODEL + g * gd:D_MODEL + (g + 1) * gd] = y[:, gd:].astype(BF16)


def _chan_dft(u, cs_mat, row0, n_rows):
    tm = TM
    d = D_MODEL
    tile0 = row0 // tm
    return pl.pallas_call(
        _chan_dft_kernel,
        grid=(n_rows // tm,),
        in_specs=[pl.BlockSpec((tm, d), lambda i: (tile0 + i, CB_F)),
                  pl.BlockSpec((FOUR_GROUP_DIM, 2 * FOUR_GROUP_DIM), lambda i: (0, 0))],
        out_specs=pl.BlockSpec((tm, 2 * d), lambda i: (i, 0)),
        out_shape=jax.ShapeDtypeStruct((n_rows, 2 * d), BF16),
        compiler_params=_cparams(("parallel",), 32),
        name="fourier_channels",
    )(u, cs_mat)


def _chan_dft_split_kernel(u_ref, cs_ref, o_ref, y_scr, *, radix):
    x = u_ref[...].astype(BF16)
    cs = cs_ref[...]
    gd = FOUR_GROUP_DIM
    per_g = gd // LANES
    n_cos = D_MODEL // LANES
    for g in range(FOUR_GROUPS):
        y = jnp.dot(x[:, g * gd:(g + 1) * gd], cs, preferred_element_type=F32)
        for j in range(per_g):
            y_scr[g * per_g + j] = y[:, j * LANES:(j + 1) * LANES]
            y_scr[n_cos + g * per_g + j] = y[:, gd + j * LANES:gd + (j + 1) * LANES]
    rows = u_ref.shape[0] // radix
    for r in range(radix):
        for cb in range(2 * n_cos):
            o_ref[r, :, cb * LANES:(cb + 1) * LANES] = y_scr[cb, pl.ds(r, rows, stride=radix), :].astype(BF16)


def _chan_dft_split(u, cs_mat, bsz, n_lat, radix):
    tm = TM
    d = D_MODEL
    n_s = n_lat // tm
    return pl.pallas_call(
        functools.partial(_chan_dft_split_kernel, radix=radix),
        grid=(bsz, n_s),
        in_specs=[pl.BlockSpec((tm, d), lambda b, i: (b * n_s + i, CB_F)),
                  pl.BlockSpec((FOUR_GROUP_DIM, 2 * FOUR_GROUP_DIM), lambda b, i: (0, 0))],
        out_specs=pl.BlockSpec((None, radix, tm // radix, 2 * d), lambda b, i: (b, 0, i, 0)),
        out_shape=jax.ShapeDtypeStruct((bsz, radix, n_lat // radix, 2 * d), BF16),
        scratch_shapes=[pltpu.VMEM((2 * d // LANES, tm, LANES), F32)],
        compiler_params=_cparams(("parallel", "parallel"), 32),
        name="fourier_channels_split",
    )(u, cs_mat)


def _time_fft_kernel(zr_ref, zi_ref, cc_ref, pc_ref, ps_ref, o_ref, *, radix):
    cc = cc_ref[...]
    tk1 = cc.shape[0] // 2
    for r in range(radix):
        pr = jnp.dot(cc, zr_ref[r], preferred_element_type=F32)
        pi = jnp.dot(cc, zi_ref[r], preferred_element_type=F32)
        a_re = pr[:tk1] - pi[tk1:]
        a_im = pr[tk1:] + pi[:tk1]
        for k2 in range(radix):
            term = pc_ref[k2][:, r:r + 1] * a_re - ps_ref[k2][:, r:r + 1] * a_im
            if r == 0:
                o_ref[k2] = term
            else:
                o_ref[k2] += term


def _time_fft(yp, cc, pc, ps, radix, tk1, tn):
    bsz, _, t1, two_d = yp.shape
    d = two_d // 2
    nj = d // tn
    return pl.pallas_call(
        functools.partial(_time_fft_kernel, radix=radix),
        grid=(bsz, nj, t1 // tk1),
        in_specs=[
            pl.BlockSpec((None, radix, t1, tn), lambda b, j, i: (b, 0, 0, j)),
            pl.BlockSpec((None, radix, t1, tn), lambda b, j, i: (b, 0, 0, nj + j)),
            pl.BlockSpec((None, 2 * tk1, t1), lambda b, j, i: (i, 0, 0)),
            pl.BlockSpec((radix, tk1, radix), lambda b, j, i: (0, i, 0)),
            pl.BlockSpec((radix, tk1, radix), lambda b, j, i: (0, i, 0)),
        ],
        out_specs=pl.BlockSpec((None, radix, tk1, tn), lambda b, j, i: (b, 0, i, j)),
        out_shape=jax.ShapeDtypeStruct((bsz, radix, t1, d), F32),
        compiler_params=_cparams(("parallel", "parallel", "arbitrary"), 48),
        name="fourier_positions_fft",
    )(yp, yp, cc, pc, ps)


def _fft_tables(n, radix, tk1):
    t1 = n // radix
    k1 = jnp.arange(t1, dtype=jnp.int32)
    ang = ((k1[:, None] * k1[None, :]) % t1).astype(F32) * (2.0 * math.pi / t1)
    c = jnp.cos(ang).reshape(t1 // tk1, tk1, t1)
    s = jnp.sin(ang).reshape(t1 // tk1, tk1, t1)
    cc = jnp.concatenate([c, s], axis=1).astype(BF16)
    k2 = jnp.arange(radix, dtype=jnp.int32)
    r = jnp.arange(radix, dtype=jnp.int32)
    k = k1[None, :, None] + t1 * k2[:, None, None]
    ph = ((k * r[None, None, :]) % n).astype(F32) * (2.0 * math.pi / n)
    scale = float(n) ** -0.5
    return cc, jnp.cos(ph) * scale, jnp.sin(ph) * scale


def _time_dft_kernel(ct_ref, st_ref, yc_ref, ys_ref, o_ref, acc, *, scale):
    kk = pl.program_id(2)

    @pl.when(kk == 0)
    def _():
        acc[...] = jnp.zeros_like(acc)

    acc[...] += (jnp.dot(ct_ref[...], yc_ref[...], preferred_element_type=F32)
                 - jnp.dot(st_ref[...], ys_ref[...], preferred_element_type=F32))

    @pl.when(kk == pl.num_programs(2) - 1)
    def _():
        o_ref[...] = acc[...] * scale


def _time_dft(y, ct, st, bsz, n_pos, row0, tmk):
    d = D_MODEL
    nt = n_pos // tmk
    rb0 = row0 // tmk
    return pl.pallas_call(
        functools.partial(_time_dft_kernel, scale=float(n_pos) ** -0.5),
        grid=(bsz, nt, nt),
        in_specs=[
            pl.BlockSpec((tmk, tmk), lambda b, i, k: (i, k)),
            pl.BlockSpec((tmk, tmk), lambda b, i, k: (i, k)),
            pl.BlockSpec((tmk, d), lambda b, i, k: (rb0 + b * nt + k, 0)),
            pl.BlockSpec((tmk, d), lambda b, i, k: (rb0 + b * nt + k, 1)),
        ],
        out_specs=pl.BlockSpec((tmk, d), lambda b, i, k: (b * nt + i, 0)),
        out_shape=jax.ShapeDtypeStruct((bsz * n_pos, d), F32),
        scratch_shapes=[pltpu.VMEM((tmk, d), F32)],
        compiler_params=_cparams(("parallel", "parallel", "arbitrary"), 48),
        name="fourier_positions",
    )(ct, st, y, y)


def _dft_tables(n):
    k = jnp.arange(n, dtype=jnp.int32)
    ang = ((k[:, None] * k[None, :]) % n).astype(F32) * (2.0 * math.pi / n)
    return jnp.cos(ang).astype(BF16), jnp.sin(ang).astype(BF16)


def _merge_kernel(x_ref, attn_l_ref, attn_c_ref, rec_ref, four_l_ref, four_c_ref, lg_ref, g0_ref, g1_ref, g2_ref,
                  wa_ref, wl_ref, wf_ref, wo_ref, gate_ref, n2_ref, sh_ref, sc_ref,
                  xo_ref, h_ref, hlo_ref, hhi_ref, *, lat_tiles):
    def mm(a, w_ref):
        return jnp.dot(a.astype(BF16), w_ref[...], preferred_element_type=F32)

    is_lat = pl.program_id(0) < lat_tiles
    attn_o = mm(jnp.where(is_lat, attn_l_ref[...], attn_c_ref[...]), wa_ref)
    rec_o = mm(rec_ref[...] * jax.nn.gelu(lg_ref[...], approximate=True), wl_ref)
    four_o = mm(jnp.where(is_lat, four_l_ref[...], four_c_ref[...]), wf_ref)
    mixed = (jax.nn.sigmoid(g0_ref[...]) * attn_o + jax.nn.sigmoid(g1_ref[...]) * rec_o
             + jax.nn.sigmoid(g2_ref[...]) * four_o)
    x = x_ref[...] + gate_ref[...] * mm(mixed, wo_ref)
    xo_ref[...] = x
    y = x * lax.rsqrt(jnp.mean(x * x, axis=-1, keepdims=True) + EPS)
    h = (y * n2_ref[...]) * (1.0 + sc_ref[...]) + sh_ref[...]
    h_ref[...] = h.astype(BF16)
    hlo_ref[...] = _pack_halves(h[:, :D_MODEL // 2])
    hhi_ref[...] = _pack_halves(h[:, D_MODEL // 2:])


def _merge(x_tok, attn_l, attn_c, rec, four_l, four_c, u, w_attn_o, w_lru_o, w_four_o, w_out, mod4, norm2_g,
           row_of_tile, n_rows):
    tm = TM
    d = D_MODEL
    lat_tiles = attn_l.shape[0] // tm
    ctx_tiles = attn_c.shape[0] // tm
    row = pl.BlockSpec((tm, d), lambda i: (i, 0))
    lat_row = pl.BlockSpec((tm, d), lambda i: (jnp.minimum(i, lat_tiles - 1), 0))
    ctx_row = pl.BlockSpec((tm, d), lambda i: (jnp.clip(i - lat_tiles, 0, ctx_tiles - 1), 0))
    wspec = pl.BlockSpec((d, d), lambda i: (0, 0))

    def ucol(cb):
        return pl.BlockSpec((tm, d), lambda i: (i, cb))

    return pl.pallas_call(
        functools.partial(_merge_kernel, lat_tiles=lat_tiles),
        grid=(n_rows // tm,),
        in_specs=[row, lat_row, ctx_row, row, lat_row, ctx_row, ucol(CB_LG), ucol(CB_G), ucol(CB_G + 1),
                  ucol(CB_G + 2), wspec, wspec, wspec, wspec,
                  _mod_spec(2, row_of_tile), pl.BlockSpec((1, d), lambda i: (0, 0)),
                  _mod_spec(3, row_of_tile), _mod_spec(4, row_of_tile)],
        out_specs=[row, row, pl.BlockSpec((tm, d // 4), lambda i: (i, 0)),
                   pl.BlockSpec((tm, d // 4), lambda i: (i, 0))],
        out_shape=[jax.ShapeDtypeStruct((n_rows, d), F32), jax.ShapeDtypeStruct((n_rows, d), BF16),
                   jax.ShapeDtypeStruct((n_rows, d // 4), jnp.uint32),
                   jax.ShapeDtypeStruct((n_rows, d // 4), jnp.uint32)],
        compiler_params=_cparams(("parallel",), 56),
        name="merge_branches",
    )(x_tok, attn_l, attn_c, rec, four_l, four_c, u, u, u, u, w_attn_o, w_lru_o, w_four_o, w_out,
      mod4, norm2_g.reshape(1, d), mod4, mod4)


def _route_kernel(h_ref, rw_ref, rb_ref, tri_ref, idx_ref, wgt_ref, pos_ref, cnt_ref, base_scr):
    tm = h_ref.shape[0]

    @pl.when(pl.program_id(0) == 0)
    def _():
        base_scr[...] = jnp.zeros_like(base_scr)

    logits = lax.dot_general(rw_ref[...], h_ref[...], (((1,), (1,)), ((), ())), preferred_element_type=F32)
    scores = jax.nn.sigmoid(logits)
    biased = scores + rb_ref[...]
    neg = -jnp.inf
    e_iota = lax.broadcasted_iota(jnp.int32, (N_EXPERTS, tm), 0)
    g_iota = lax.broadcasted_iota(jnp.int32, (N_GROUPS, tm), 0)

    rows = []
    for g in range(N_GROUPS):
        xg = biased[g * GROUP_SIZE:(g + 1) * GROUP_SIZE]
        m1 = jnp.max(xg, axis=0, keepdims=True)
        is_max = xg == m1
        n_max = jnp.sum(is_max.astype(F32), axis=0, keepdims=True)
        m2 = jnp.max(jnp.where(is_max, neg, xg), axis=0, keepdims=True)
        rows.append(m1 + jnp.where(n_max >= 2.0, m1, m2))
    grp = jnp.concatenate(rows, axis=0)

    g_sel = jnp.zeros((N_GROUPS, tm), jnp.bool_)
    work = grp
    for _ in range(TOPK_GROUPS):
        m = jnp.max(work, axis=0, keepdims=True)
        first = jnp.min(jnp.where(work == m, g_iota, N_GROUPS), axis=0, keepdims=True)
        hit = g_iota == first
        g_sel = jnp.logical_or(g_sel, hit)
        work = jnp.where(hit, neg, work)
    g_sel_f = g_sel.astype(F32)
    mask_rows = [jnp.broadcast_to(g_sel_f[g:g + 1], (GROUP_SIZE, tm)) for g in range(N_GROUPS)]
    e_mask = jnp.concatenate(mask_rows, axis=0) > 0.5

    work = jnp.where(e_mask, biased, neg)
    sel = jnp.zeros((N_EXPERTS, tm), F32)
    idx_rows, w_rows, hits = [], [], []
    for _ in range(TOP_K):
        m = jnp.max(work, axis=0, keepdims=True)
        first = jnp.min(jnp.where(work == m, e_iota, N_EXPERTS), axis=0, keepdims=True)
        hit = e_iota == first
        hits.append(hit)
        idx_rows.append(first)
        w_rows.append(jnp.sum(jnp.where(hit, scores, 0.0), axis=0, keepdims=True))
        sel = jnp.where(hit, 1.0, sel)
        work = jnp.where(hit, neg, work)
    w = jnp.concatenate(w_rows, axis=0)
    wgt_ref[...] = w / jnp.sum(w, axis=0, keepdims=True) * ROUTED_SCALE
    idx_ref[...] = jnp.concatenate(idx_rows, axis=0)

    before = jnp.dot(sel.astype(BF16), tri_ref[...], preferred_element_type=F32)
    rank = base_scr[...] + before
    pos_rows = [jnp.sum(jnp.where(hit, rank, 0.0), axis=0, keepdims=True) for hit in hits]
    pos_ref[...] = jnp.concatenate(pos_rows, axis=0).astype(jnp.int32)
    base_scr[...] = base_scr[...] + jnp.sum(sel, axis=1, keepdims=True)
    cnt_ref[...] = jnp.broadcast_to(base_scr[...], cnt_ref.shape)


def _route(h16, rw_t, router_b, tm):
    n_tok, d = h16.shape
    tri = (jnp.arange(tm)[:, None] < jnp.arange(tm)[None, :]).astype(BF16)
    out_col = pl.BlockSpec((TOP_K, tm), lambda i: (0, i))
    return pl.pallas_call(
        _route_kernel,
        grid=(n_tok // tm,),
        in_specs=[pl.BlockSpec((tm, d), lambda i: (i, 0)),
                  pl.BlockSpec((N_EXPERTS, d), lambda i: (0, 0)),
                  pl.BlockSpec((N_EXPERTS, 1), lambda i: (0, 0)),
                  pl.BlockSpec((tm, tm), lambda i: (0, 0))],
        out_specs=[out_col, out_col, out_col, pl.BlockSpec((N_EXPERTS, LANES), lambda i: (0, 0))],
        out_shape=[jax.ShapeDtypeStruct((TOP_K, n_tok), jnp.int32), jax.ShapeDtypeStruct((TOP_K, n_tok), F32),
                   jax.ShapeDtypeStruct((TOP_K, n_tok), jnp.int32),
                   jax.ShapeDtypeStruct((N_EXPERTS, LANES), F32)],
        scratch_shapes=[pltpu.VMEM((N_EXPERTS, 1), F32)],
        compiler_params=_cparams(("arbitrary",), 32),
        name="moe_router",
    )(h16, rw_t, router_b.reshape(N_EXPERTS, 1), tri)


def _slot_kernel(idx_ref, pos_ref, offs_ref, dest_ref):
    tm = idx_ref.shape[1]
    e_iota = lax.broadcasted_iota(jnp.int32, (N_EXPERTS, tm), 0)
    idx = idx_ref[...]
    offs = offs_ref[...]
    rows = [jnp.sum(jnp.where(e_iota == idx[r:r + 1], offs, 0), axis=0, keepdims=True) for r in range(TOP_K)]
    dest_ref[...] = jnp.concatenate(rows, axis=0) + pos_ref[...]


def _slots(top_idx, pos, offs, tm):
    n_tok = top_idx.shape[1]
    col = pl.BlockSpec((TOP_K, tm), lambda i: (0, i))
    return pl.pallas_call(
        _slot_kernel,
        grid=(n_tok // tm,),
        in_specs=[col, col, pl.BlockSpec((N_EXPERTS, 1), lambda i: (0, 0))],
        out_specs=col,
        out_shape=jax.ShapeDtypeStruct((TOP_K, n_tok), jnp.int32),
        compiler_params=_cparams(("parallel",), 32),
        name="moe_slots",
    )(top_idx, pos, offs.reshape(N_EXPERTS, 1))


def _sc_scatter_rows(src, idx, n_out):
    n_src, width = src.shape
    n_idx = idx.shape[0]
    n_win = n_src // SC_WINDOW
    mesh = plsc.VectorSubcoreMesh(core_axis_name="core", subcore_axis_name="subcore")

    @pl.kernel(out_type=jax.ShapeDtypeStruct((n_out, width), src.dtype), mesh=mesh, scratch_types=[],
               name="moe_scatter_sc")
    def scatter(src_hbm, idx_hbm, out_hbm):
        def body(src_vmem, idx_vmem):
            pltpu.sync_copy(src_vmem, out_hbm.at[idx_vmem.at[0]])

        pltpu.emit_pipeline(
            body,
            grid=(n_idx // SC_WINDOW,),
            in_specs=[pl.BlockSpec((SC_WINDOW, width), index_map=lambda g: (g % n_win, 0)),
                      pl.BlockSpec((1, SC_WINDOW), index_map=lambda g: (0, g))],
            out_specs=[],
            core_axis_name=("core", "subcore"),
            dimension_semantics=(pltpu.PARALLEL,),
        )(src_hbm, idx_hbm)

    return scatter(src, idx.reshape(1, n_idx))


def _expert_kernel(be_ref, nu_ref, nv_ref, xlo_ref, xhi_ref, wg_ref, wu_ref, wd_ref, ylo_ref, yhi_ref,
                   wg_s, wu_s, wd_s):
    i = pl.program_id(0)
    used = i < nu_ref[0]
    prev_e = be_ref[jnp.maximum(i - 1, 0)]
    fresh = jnp.logical_or(i == 0, be_ref[i] != prev_e)

    @pl.when(jnp.logical_and(used, fresh))
    def _():
        wg_s[...] = wg_ref[0].astype(BF16)
        wu_s[...] = wu_ref[0].astype(BF16)
        wd_s[...] = wd_ref[0].astype(BF16)

    @pl.when(used)
    def _():
        row = lax.broadcasted_iota(jnp.int32, xlo_ref.shape, 0)
        live = row < nv_ref[i]
        half = D_MODEL // 2
        quarter = D_MODEL // 4
        parts = []
        for ref in (xlo_ref, xhi_ref):
            parts.extend(p.astype(BF16) for p in _unpack_halves(jnp.where(live, ref[...], jnp.uint32(0))))

        def proj(w_s):
            acc = jnp.dot(parts[0], w_s[:quarter, :], preferred_element_type=F32)
            for k in range(1, 4):
                acc = acc + jnp.dot(parts[k], w_s[k * quarter:(k + 1) * quarter, :], preferred_element_type=F32)
            return acc

        g = proj(wg_s)
        act = (_silu(g) * proj(wu_s)).astype(BF16)
        y = jnp.dot(act, wd_s[...], preferred_element_type=F32)
        ylo_ref[...] = _pack_halves(y[:, :half])
        yhi_ref[...] = _pack_halves(y[:, half:])


def _experts(xs_lo, xs_hi, block_e, n_used, n_valid, w_gate, w_up, w_down, layer, bm):
    n_slots, quarter = xs_lo.shape
    half = 2 * quarter
    d = D_MODEL

    def blk(i, be, nu, nv):
        return (jnp.minimum(i, nu[0] - 1), 0)

    def wsel(i, be, nu, nv):
        return (layer, be[i], 0, 0)

    grid_spec = pltpu.PrefetchScalarGridSpec(
        num_scalar_prefetch=3,
        grid=(n_slots // bm,),
        in_specs=[pl.BlockSpec((bm, quarter), blk),
                  pl.BlockSpec((bm, quarter), blk),
                  pl.BlockSpec((None, 1, d, EXPERT_FF), wsel),
                  pl.BlockSpec((None, 1, d, EXPERT_FF), wsel),
                  pl.BlockSpec((None, 1, EXPERT_FF, d), wsel)],
        out_specs=[pl.BlockSpec((bm, half // 2), blk), pl.BlockSpec((bm, half // 2), blk)],
        scratch_shapes=[pltpu.VMEM((d, EXPERT_FF), BF16), pltpu.VMEM((d, EXPERT_FF), BF16),
                        pltpu.VMEM((EXPERT_FF, d), BF16)],
    )
    y_shape = jax.ShapeDtypeStruct((n_slots, half // 2), jnp.uint32)
    return pl.pallas_call(
        _expert_kernel,
        grid_spec=grid_spec,
        out_shape=[y_shape, y_shape],
        compiler_params=_cparams(("arbitrary",), 40),
        name="moe_experts",
    )(block_e, n_used, n_valid, xs_lo, xs_hi, w_gate, w_up, w_down)


def _sc_gather_rows(table, idx):
    n_idx = idx.shape[0]
    width = table.shape[1]
    mesh = plsc.VectorSubcoreMesh(core_axis_name="core", subcore_axis_name="subcore")

    @pl.kernel(out_type=jax.ShapeDtypeStruct((n_idx, width), table.dtype), mesh=mesh, scratch_types=[],
               name="moe_gather_sc")
    def gather(table_hbm, idx_hbm, out_hbm):
        def body(idx_vmem, out_vmem):
            pltpu.sync_copy(table_hbm.at[idx_vmem.at[0]], out_vmem)

        pltpu.emit_pipeline(
            body,
            grid=(n_idx // SC_WINDOW,),
            in_specs=[pl.BlockSpec((1, SC_WINDOW), index_map=lambda i: (0, i))],
            out_specs=[pl.BlockSpec((SC_WINDOW, width), index_map=lambda i: (i, 0))],
            core_axis_name=("core", "subcore"),
            dimension_semantics=(pltpu.PARALLEL,),
        )(idx_hbm, out_hbm)

    return gather(table, idx.reshape(1, n_idx))


def _combine_kernel(x_ref, h_ref, tw_ref, gate_ref, wsg_ref, wsu_ref, wsd_ref, lo_ref, hi_ref, o_ref):
    quarter = D_MODEL // 4
    h = h_ref[...]
    act = (_silu(jnp.dot(h, wsg_ref[...], preferred_element_type=F32))
           * jnp.dot(h, wsu_ref[...], preferred_element_type=F32)).astype(BF16)
    shared = jnp.dot(act, wsd_ref[...], preferred_element_type=F32)

    tw = tw_ref[...]
    gate = gate_ref[...]
    for q, buf in enumerate((lo_ref, hi_ref)):
        moe_a = moe_b = None
        for r in range(TOP_K):
            ya, yb = _unpack_halves(buf[r])
            wr = tw[:, r:r + 1]
            moe_a = ya * wr if moe_a is None else moe_a + ya * wr
            moe_b = yb * wr if moe_b is None else moe_b + yb * wr
        for part, moe in ((2 * q, moe_a), (2 * q + 1, moe_b)):
            cols = slice(part * quarter, (part + 1) * quarter)
            o_ref[:, cols] = x_ref[:, cols] + gate[:, cols] * (moe + shared[:, cols])


def _combine(x_tok, h16, g_lo, g_hi, top_w, mod4, ws_gate, ws_up, ws_down, row_of_tile):
    n_tok, d = x_tok.shape
    tm = TM
    ff = ws_gate.shape[1]
    row = pl.BlockSpec((tm, d), lambda i: (i, 0))
    gat = pl.BlockSpec((TOP_K, tm, d // 4), lambda i: (0, i, 0))
    return pl.pallas_call(
        _combine_kernel,
        grid=(n_tok // tm,),
        in_specs=[row, row,
                  pl.BlockSpec((tm, TOP_K), lambda i: (i, 0)),
                  _mod_spec(5, row_of_tile),
                  pl.BlockSpec((d, ff), lambda i: (0, 0)),
                  pl.BlockSpec((d, ff), lambda i: (0, 0)),
                  pl.BlockSpec((ff, d), lambda i: (0, 0)),
                  gat, gat],
        out_specs=row,
        out_shape=jax.ShapeDtypeStruct((n_tok, d), F32),
        compiler_params=_cparams(("parallel",), 48),
        name="moe_combine",
    )(x_tok, h16, top_w, mod4, ws_gate, ws_up, ws_down, g_lo, g_hi)


def _moe(x_tok, h16, h_lo, h_hi, mod4, row_of_tile, rw_t, router_b, w_gate, w_up, w_down, layer, ws_gate, ws_up, ws_down):
    n_tok = x_tok.shape[0]
    bm = MOE_BM
    tm_r = TM_ROUTE if n_tok % TM_ROUTE == 0 else TM
    top_idx, top_w, pos, cnt = _route(h16, rw_t, router_b, tm_r)

    counts = cnt[:, 0].astype(jnp.int32)
    padded = (counts + bm - 1) // bm * bm
    ends = jnp.cumsum(padded)
    n_blocks = -(-(n_tok * TOP_K + N_EXPERTS * (bm - 1)) // bm)
    offs = ends - padded
    dest = _slots(top_idx, pos, offs, tm_r)
    block_start = jnp.arange(n_blocks, dtype=jnp.int32) * bm
    block_e = jnp.minimum(jnp.sum((ends[None, :] <= block_start[:, None]).astype(jnp.int32), axis=1),
                          N_EXPERTS - 1)
    n_valid = jnp.clip(counts[block_e] - (block_start - offs[block_e]), 0, bm).astype(jnp.int32)
    n_used = (ends[-1] // bm).astype(jnp.int32).reshape(1)
    dest_flat = dest.reshape(TOP_K * n_tok)
    xs_lo = _sc_scatter_rows(h_lo, dest_flat, n_blocks * bm)
    xs_hi = _sc_scatter_rows(h_hi, dest_flat, n_blocks * bm)
    ys_lo, ys_hi = _experts(xs_lo, xs_hi, block_e, n_used, n_valid, w_gate, w_up, w_down, layer, bm)
    g_lo = _sc_gather_rows(ys_lo, dest_flat).reshape(TOP_K, n_tok, D_MODEL // 4)
    g_hi = _sc_gather_rows(ys_hi, dest_flat).reshape(TOP_K, n_tok, D_MODEL // 4)
    return _combine(x_tok, h16, g_lo, g_hi, top_w.T, mod4, ws_gate, ws_up, ws_down, row_of_tile)


def _block_diag_tiles(w):
    per = MXU_DIM // LRU_BLOCK_DIM
    w = w.reshape(2, D_MODEL // MXU_DIM, per, LRU_BLOCK_DIM, LRU_BLOCK_DIM)
    eye = jnp.eye(per, dtype=w.dtype)
    t = jnp.einsum('ztpde,pq->ztpdqe', w, eye)
    return t.reshape(2, D_MODEL // MXU_DIM, MXU_DIM, MXU_DIM).astype(BF16)


def _rope_tables(n_lat, n_ctx):
    rows = n_lat // GRID_W
    row = jnp.repeat(jnp.arange(rows), GRID_W)
    col = jnp.tile(jnp.arange(GRID_W), rows)
    inv = ROPE_THETA ** (-jnp.arange(ROPE_FREQS, dtype=F32) / ROPE_FREQS)
    ang = jnp.stack([row[:, None] * inv, col[:, None] * inv], axis=1)
    ang = jnp.concatenate([ang, ang], axis=2).reshape(n_lat, HEAD_DIM)
    ang = jnp.tile(ang, (1, LANES // HEAD_DIM))
    ang = jnp.concatenate([ang, jnp.zeros((n_ctx, LANES), F32)], axis=0)
    lane = jnp.arange(LANES)
    sign = jnp.where((lane % (2 * ROPE_FREQS)) < ROPE_FREQS, -1.0, 1.0).astype(F32)
    return jnp.cos(ang), jnp.sin(ang) * sign


def kernel(x, c, ctx, c_ctx, w_mod, b_mod, norm1_g, w_in, q_norm_g, k_norm_g, lambda_qk, subln_g, w_attn_o, conv_w, conv_b, lru_wa, lru_ba, lru_wi, lru_bi, lru_lambda, w_lru_o, w_four_o, w_out, norm2_g, router_w, router_b, exp_w_gate, exp_w_up, exp_w_down, sh_w_gate, sh_w_up, sh_w_down):
    bsz, n_lat, d = x.shape
    n_ctx = ctx.shape[1]
    depth = w_mod.shape[0]
    n_latent_rows = bsz * n_lat
    n_tok = n_latent_rows + bsz * n_ctx
    assert d == D_MODEL and bsz + 1 <= MOD_ROWS
    assert n_lat % TM == 0 and n_ctx % TM == 0 and n_lat % GRID_W == 0
    tm_in = TM_IN if (n_lat % TM_IN == 0 and (bsz * n_ctx) % TM_IN == 0) else TM

    cvec = jnp.zeros((MOD_ROWS, d), F32).at[:bsz].set(c).at[bsz].set(c_ctx)
    mods = _modulation(cvec, w_mod, b_mod)

    def row_of_tile_fn(tm):
        per_batch = n_lat // tm
        return lambda i: jnp.minimum(i // per_batch, bsz)

    assert n_lat % n_ctx == 0
    cos_t, sin_t = _rope_tables(n_lat, n_ctx)
    group_mat = jnp.kron(jnp.eye(d // HEAD_DIM, dtype=F32), jnp.ones((HEAD_DIM, HEAD_DIM), F32)).astype(BF16)
    m = jnp.arange(FOUR_GROUP_DIM, dtype=jnp.int32)
    ang_c = ((m[:, None] * m[None, :]) % FOUR_GROUP_DIM).astype(F32) * (2.0 * math.pi / FOUR_GROUP_DIM)
    inv_sqrt_c = FOUR_GROUP_DIM ** -0.5
    cs_mat = jnp.concatenate([jnp.cos(ang_c) * inv_sqrt_c, jnp.sin(ang_c) * inv_sqrt_c], axis=1).astype(BF16)
    assert n_lat % (FFT_RADIX * MXU_DIM) == 0 or n_lat // FFT_RADIX < MXU_DIM
    fft_tk1 = min(MXU_DIM, n_lat // FFT_RADIX)
    fft_cc, fft_pc, fft_ps = _fft_tables(n_lat, FFT_RADIX, fft_tk1)
    ct_ctx, st_ctx = _dft_tables(n_ctx)
    q_sub = math.gcd(Q_SUB, n_lat // TQ)
    n_keys = n_ctx + n_lat
    tk = TK if n_keys % TK == 0 else TM

    x_tok = jnp.concatenate([x.reshape(n_latent_rows, d), ctx.reshape(bsz * n_ctx, d)], axis=0)
    for l in range(depth):
        last = l == depth - 1
        lam_init = 0.8 - 0.6 * math.exp(-0.3 * l)
        mod4 = mods[l].reshape(MOD_ROWS, 6, 1, d)

        u = _in_projection(x_tok, mod4, norm1_g[l], w_in[l].astype(BF16), row_of_tile_fn(tm_in), tm_in)

        k_all, q_all, vt_all = _qkv_prepare(u, k_norm_g[l], q_norm_g[l], group_mat, cos_t, sin_t,
                                            bsz, n_lat, n_ctx)
        attn_l = _diff_attention(q_all, k_all, vt_all, lambda_qk[l], subln_g[l], lam_init,
                                 0, n_lat, 0, n_keys, TQ, q_sub, tk)
        attn_c = attn_l
        if not last:
            attn_c = _diff_attention(q_all, k_all, vt_all, lambda_qk[l], subln_g[l], lam_init,
                                     n_lat, n_ctx, n_lat, n_ctx, TM, 1, TM)

        rec = _rglru(u, conv_w[l], conv_b[l], _block_diag_tiles(lru_wa[l]), lru_ba[l],
                     _block_diag_tiles(lru_wi[l]), lru_bi[l], lru_lambda[l], bsz, n_lat, n_ctx)

        y_lat = _chan_dft_split(u, cs_mat, bsz, n_lat, FFT_RADIX)
        four_l = _time_fft(y_lat, fft_cc, fft_pc, fft_ps, FFT_RADIX, fft_tk1, MXU_DIM).reshape(n_latent_rows, d)
        four_c = four_l
        if not last:
            y_ctx = _chan_dft(u, cs_mat, n_latent_rows, bsz * n_ctx)
            four_c = _time_dft(y_ctx, ct_ctx, st_ctx, bsz, n_ctx, 0, n_ctx)

        n_rows = n_latent_rows if last else n_tok
        row_of_tile = row_of_tile_fn(TM)
        x_mid, h16, h_lo, h_hi = _merge(x_tok, attn_l, attn_c, rec, four_l, four_c, u, w_attn_o[l].astype(BF16),
                                w_lru_o[l].astype(BF16), w_four_o[l].astype(BF16), w_out[l].astype(BF16),
                                mod4, norm2_g[l], row_of_tile, n_rows)
        x_tok = _moe(x_mid, h16, h_lo, h_hi, mod4, row_of_tile, router_w[l].T.astype(BF16), router_b[l],
                     exp_w_gate, exp_w_up, exp_w_down, l,
                     sh_w_gate[l].astype(BF16), sh_w_up[l].astype(BF16), sh_w_down[l].astype(BF16))
    return x_tok[:n_latent_rows].reshape(bsz, n_lat, d)
```

```python
import functools
import math

import jax
import jax.numpy as jnp
from jax import lax
from jax.experimental import pallas as pl
from jax.experimental.pallas import tpu as pltpu
from jax.experimental.pallas import tpu_sc as plsc

F32 = jnp.float32
BF16 = jnp.bfloat16

D_MODEL = 1024
EPS = 1e-6
GRID_W = 64

N_HEADS = 8
HEAD_DIM = 64
V_DIM = 2 * HEAD_DIM
ATTN_SCALE = HEAD_DIM ** -0.5
ROPE_THETA = 10000.0
ROPE_FREQS = HEAD_DIM // 4

LRU_BLOCKS = 16
LRU_BLOCK_DIM = D_MODEL // LRU_BLOCKS
LRU_C = 8.0
CONV_W = 4

FOUR_GROUPS = 4
FOUR_GROUP_DIM = D_MODEL // FOUR_GROUPS

CB_K, CB_V, CB_LX, CB_Q, CB_LG, CB_F, CB_G = 0, 1, 2, 3, 4, 5, 6
IN_BLOCKS = 9

N_EXPERTS = 256
TOP_K = 8
N_GROUPS = 8
GROUP_SIZE = N_EXPERTS // N_GROUPS
TOPK_GROUPS = 4
EXPERT_FF = 256
ROUTED_SCALE = 2.5

LANES = 128
SUBLANES = 8
MXU_DIM = 256
VMEM_BYTES_V7X = 64 * 1024 * 1024

MOD_ROWS = 8
TM = 256
TM_IN = 512
TQ = 256
Q_SUB = 32
TK = 768
TM_ROUTE = 512
MOE_BM = 640
SC_WINDOW = 128
FFT_RADIX = 8
LOG2E = 1.4426950408889634


def _cparams(sem, vmem_mb):
    return pltpu.CompilerParams(dimension_semantics=sem, vmem_limit_bytes=int(vmem_mb * 1024 * 1024))


def _silu(x):
    return x * jax.nn.sigmoid(x)


def _pack_halves(x):
    half = x.shape[1] // 2
    bits = pltpu.bitcast(x.astype(BF16).astype(F32), jnp.uint32)
    return (bits[:, :half] & jnp.uint32(0xFFFF0000)) | (bits[:, half:] >> 16)


def _unpack_halves(w):
    return pltpu.bitcast(w & jnp.uint32(0xFFFF0000), F32), pltpu.bitcast(w << 16, F32)


def _mod_kernel(c_ref, w_ref, b_ref, o_ref):
    c = c_ref[...]
    s = _silu(c).astype(BF16)
    o_ref[0] = jnp.dot(s, w_ref[0].astype(BF16), preferred_element_type=F32) + b_ref[0]


def _modulation(cvec, w_mod, b_mod):
    n_layers, d, six_d = w_mod.shape
    nj = six_d // d
    return pl.pallas_call(
        _mod_kernel,
        grid=(n_layers, nj),
        in_specs=[
            pl.BlockSpec((MOD_ROWS, d), lambda l, j: (0, 0)),
            pl.BlockSpec((1, d, d), lambda l, j: (l, 0, j)),
            pl.BlockSpec((1, 1, d), lambda l, j: (l, 0, j)),
        ],
        out_specs=pl.BlockSpec((1, MOD_ROWS, d), lambda l, j: (l, 0, j)),
        out_shape=jax.ShapeDtypeStruct((n_layers, MOD_ROWS, six_d), F32),
        compiler_params=_cparams(("parallel", "parallel"), 32),
        name="modulation",
    )(cvec, w_mod, b_mod.reshape(n_layers, 1, six_d))


def _mod_spec(part, row_of_tile, n_grid_axes=1):
    if n_grid_axes == 1:
        return pl.BlockSpec((None, None, 1, D_MODEL), lambda i: (row_of_tile(i), part, 0, 0))
    return pl.BlockSpec((None, None, 1, D_MODEL), lambda i, j: (row_of_tile(i), part, 0, 0))


def _inproj_kernel(x_ref, g_ref, sh_ref, sc_ref, w_ref, o_ref, h_scr):
    @pl.when(pl.program_id(1) == 0)
    def _():
        x = x_ref[...]
        y = x * lax.rsqrt(jnp.mean(x * x, axis=-1, keepdims=True) + EPS)
        h = (y * g_ref[...]) * (1.0 + sc_ref[...]) + sh_ref[...]
        h_scr[...] = h.astype(BF16)

    o_ref[...] = jnp.dot(h_scr[...], w_ref[...], preferred_element_type=F32)


def _in_projection(x_tok, mod4, norm_g, w_in_bf16, row_of_tile, tm):
    n_tok, d = x_tok.shape
    n_cols = w_in_bf16.shape[1]
    return pl.pallas_call(
        _inproj_kernel,
        grid=(n_tok // tm, n_cols // d),
        in_specs=[
            pl.BlockSpec((tm, d), lambda i, j: (i, 0)),
            pl.BlockSpec((1, d), lambda i, j: (0, 0)),
            _mod_spec(0, row_of_tile, 2),
            _mod_spec(1, row_of_tile, 2),
            pl.BlockSpec((d, d), lambda i, j: (0, j)),
        ],
        out_specs=pl.BlockSpec((tm, d), lambda i, j: (i, j)),
        out_shape=jax.ShapeDtypeStruct((n_tok, n_cols), F32),
        scratch_shapes=[pltpu.VMEM((tm, d), BF16)],
        compiler_params=_cparams(("parallel", "arbitrary"), 40),
        name="in_projection",
    )(x_tok, norm_g.reshape(1, d), mod4, mod4, w_in_bf16)


def _prep_kernel(uk_ref, uq_ref, uv_ref, kg_ref, qg_ref, gm_ref, cos_ref, sin_ref, k_out, qt_out, vt_out, *, tm):
    gm = gm_ref[...]
    lane = lax.broadcasted_iota(jnp.int32, (tm, LANES), 1)
    first_half = (lane % (2 * ROPE_FREQS)) < ROPE_FREQS

    def norm_rope(x, g):
        ss = jnp.dot((x * x).astype(BF16), gm, preferred_element_type=F32) * (1.0 / HEAD_DIM)
        y = x * lax.rsqrt(ss + EPS) * g
        c = cos_ref[...]
        s = sin_ref[...]
        outs = []
        for j in range(D_MODEL // LANES):
            yt = y[:, j * LANES:(j + 1) * LANES]
            rot = jnp.where(first_half, pltpu.roll(yt, LANES - ROPE_FREQS, 1), pltpu.roll(yt, ROPE_FREQS, 1))
            outs.append(yt * c + rot * s)
        return jnp.concatenate(outs, axis=1)

    k_out[...] = norm_rope(uk_ref[...], kg_ref[...]).astype(BF16)
    qn = norm_rope(uq_ref[...], qg_ref[...]) * (ATTN_SCALE * LOG2E)
    lo = lane < HEAD_DIM
    for h in range(N_HEADS):
        qt = qn[:, h * LANES:(h + 1) * LANES]
        qt_out[(2 * h) * LANES:(2 * h + 1) * LANES, :] = jnp.where(lo, qt, 0.0).T.astype(BF16)
        qt_out[(2 * h + 1) * LANES:(2 * h + 2) * LANES, :] = jnp.where(lo, 0.0, qt).T.astype(BF16)
    vt_out[...] = uv_ref[...].T.astype(BF16)


def _qkv_prepare(u, k_gain, q_gain, group_mat, cos_t, sin_t, bsz, n_lat, n_ctx):
    tm = TM
    d = D_MODEL
    n_s, n_c = n_lat // tm, n_ctx // tm
    n_rows = n_lat + n_ctx
    lat_tiles = bsz * n_s
    kg = jnp.tile(k_gain, d // HEAD_DIM).reshape(1, d)
    qg = jnp.tile(q_gain, d // HEAD_DIM).reshape(1, d)

    def tok_tile(b, j):
        return jnp.where(j < n_s, b * n_s + j, lat_tiles + b * n_c + (j - n_s))

    def u_spec(cb):
        return pl.BlockSpec((tm, d), lambda b, j: (tok_tile(b, j), cb))

    vec = pl.BlockSpec((1, d), lambda b, j: (0, 0))
    rope_spec = pl.BlockSpec((tm, LANES), lambda b, j: (j, 0))
    return pl.pallas_call(
        functools.partial(_prep_kernel, tm=tm),
        grid=(bsz, n_s + n_c),
        in_specs=[u_spec(CB_K), u_spec(CB_Q), u_spec(CB_V), vec, vec,
                  pl.BlockSpec((d, d), lambda b, j: (0, 0)), rope_spec, rope_spec],
        out_specs=[
            pl.BlockSpec((None, tm, d), lambda b, j: (b, j, 0)),
            pl.BlockSpec((None, None, 2 * d, tm), lambda b, j: (b, j, 0, 0)),
            pl.BlockSpec((None, d, tm), lambda b, j: (b, 0, j)),
        ],
        out_shape=[jax.ShapeDtypeStruct((bsz, n_rows, d), BF16),
                   jax.ShapeDtypeStruct((bsz, n_rows // tm, 2 * d, tm), BF16),
                   jax.ShapeDtypeStruct((bsz, d, n_rows), BF16)],
        compiler_params=_cparams(("parallel", "parallel"), 40),
        name="qkv_prepare",
    )(u, u, u, kg, qg, group_mat, cos_t, sin_t)


def _attn_kernel(lq_ref, sg_ref, q1_ref, q2_ref, k_ref, vt_ref, o_ref, s_buf, *, n_keys, tk, tq, lam_init):
    lq = lq_ref[...]
    lam = (jnp.exp(jnp.sum(lq[0:1] * lq[1:2], axis=1, keepdims=True))
           - jnp.exp(jnp.sum(lq[2:3] * lq[3:4], axis=1, keepdims=True)) + lam_init)
    n_chunks = n_keys // tk
    n_sub = q1_ref.shape[0]

    def load_q(j):
        return jnp.concatenate([q1_ref[j], q2_ref[j]], axis=1)

    def score_chunk(q, c, mx):
        s = jnp.dot(k_ref[c * tk:(c + 1) * tk, :], q, preferred_element_type=F32)
        s_buf[c * tk:(c + 1) * tk, :] = s
        cm = jnp.max(s, axis=0, keepdims=True)
        return cm if mx is None else jnp.maximum(mx, cm)

    q0 = load_q(0)
    mx0 = None
    for c in range(n_chunks):
        mx0 = score_chunk(q0, c, mx0)

    def sub_tile(j, mx):
        q_next = load_q(jnp.minimum(j + 1, n_sub - 1))
        mx_next = None
        l = jnp.zeros((1, 2 * tq), F32)
        acc = jnp.zeros((V_DIM, 2 * tq), F32)
        for c in range(n_chunks):
            p = jnp.exp2(s_buf[c * tk:(c + 1) * tk, :] - mx)
            l = l + jnp.sum(p, axis=0, keepdims=True)
            acc = acc + jnp.dot(vt_ref[:, c * tk:(c + 1) * tk], p.astype(BF16), preferred_element_type=F32)
            mx_next = score_chunk(q_next, c, mx_next)
        o = acc / l
        o = o[:, :tq] - lam * o[:, tq:]
        ms = jnp.mean(o * o, axis=0, keepdims=True)
        on = (o * lax.rsqrt(ms + EPS) * sg_ref[...]) * (1.0 - lam_init)
        o_ref[pl.ds(pl.multiple_of(j * tq, tq), tq), :] = on.T.astype(o_ref.dtype)
        return mx_next

    lax.fori_loop(0, n_sub, sub_tile, mx0)


def _diff_attention(q_all, k_all, vt_all, lambda_qk, subln_g, lam_init, q_row0, n_q, key_row0, n_keys, tq, n_sub, tk):
    bsz = q_all.shape[0]
    d = D_MODEL
    tstep = tq * n_sub
    nq = n_q // tstep
    qb0 = q_row0 // tstep
    kb0 = key_row0 // n_keys
    s_bytes = n_keys * 2 * tq * 4
    return pl.pallas_call(
        functools.partial(_attn_kernel, n_keys=n_keys, tk=tk, tq=tq, lam_init=lam_init),
        grid=(bsz, N_HEADS, nq),
        in_specs=[
            pl.BlockSpec((4, HEAD_DIM), lambda b, h, i: (0, 0)),
            pl.BlockSpec((V_DIM, 1), lambda b, h, i: (0, 0)),
            pl.BlockSpec((None, n_sub, LANES, tq), lambda b, h, i: (b, qb0 + i, 2 * h, 0)),
            pl.BlockSpec((None, n_sub, LANES, tq), lambda b, h, i: (b, qb0 + i, 2 * h + 1, 0)),
            pl.BlockSpec((None, n_keys, LANES), lambda b, h, i: (b, kb0, h)),
            pl.BlockSpec((None, V_DIM, n_keys), lambda b, h, i: (b, h, kb0)),
        ],
        out_specs=pl.BlockSpec((tstep, LANES), lambda b, h, i: (b * nq + i, h)),
        out_shape=jax.ShapeDtypeStruct((bsz * n_q, d), BF16),
        scratch_shapes=[pltpu.VMEM((n_keys, 2 * tq), F32)],
        compiler_params=_cparams(("parallel", "parallel", "arbitrary"), 28 + s_bytes / 2 ** 20),
        name="diff_attention",
    )(lambda_qk, subln_g.reshape(V_DIM, 1), q_all, q_all, k_all, vt_all)


def _lru_kernel(*refs, rev, n_c, n_s, tm):
    if rev:
        (cur_ref, prev_ref, next_ref, cw_ref, cb_ref, wa_ref, ba_ref, wi_ref, bi_ref, lam_ref, fwd_ref,
         o_ref, a_scr, b_scr, h_scr) = refs
    else:
        (cur_ref, prev_ref, next_ref, cw_ref, cb_ref, wa_ref, ba_ref, wi_ref, bi_ref, lam_ref,
         o_ref, a_scr, b_scr, h_scr) = refs
    s = pl.program_id(1)
    is_ctx = s < n_c
    if rev:
        tile = jnp.where(is_ctx, n_c - 1 - s, n_s - 1 - (s - n_c))
    else:
        tile = jnp.where(is_ctx, s, s - n_c)
    n_tile = jnp.where(is_ctx, n_c, n_s)

    @pl.when(s == 0)
    def _():
        h_scr[...] = jnp.zeros_like(h_scr)

    x = cur_ref[...]
    pv = jnp.where(tile == 0, 0.0, prev_ref[...])
    nx = jnp.where(tile == n_tile - 1, 0.0, next_ref[...])
    r8 = lax.broadcasted_iota(jnp.int32, (SUBLANES, D_MODEL), 0)

    def earlier(k):
        rolled = pltpu.roll(x, k, 0)
        top = jnp.where(r8 < k, pltpu.roll(pv, k, 0), rolled[:SUBLANES])
        return jnp.concatenate([top, rolled[SUBLANES:]], axis=0)

    rolled = pltpu.roll(x, tm - 1, 0)
    bot = jnp.where(r8 >= SUBLANES - 1, pltpu.roll(nx, SUBLANES - 1, 0), rolled[tm - SUBLANES:])
    later = jnp.concatenate([rolled[:tm - SUBLANES], bot], axis=0)
    cw = cw_ref[...]
    conv = earlier(2) * cw[0:1] + earlier(1) * cw[1:2] + x * cw[2:3] + later * cw[3:4] + cb_ref[...]

    conv16 = conv.astype(BF16)

    def block_diag(w_ref, b_ref):
        parts = [jnp.dot(conv16[:, j * MXU_DIM:(j + 1) * MXU_DIM], w_ref[j], preferred_element_type=F32)
                 for j in range(D_MODEL // MXU_DIM)]
        return jnp.concatenate(parts, axis=1) + b_ref[...]

    r = jax.nn.sigmoid(block_diag(wa_ref, ba_ref))
    gate_i = jax.nn.sigmoid(block_diag(wi_ref, bi_ref))
    neg_lam = -lam_ref[...]
    softplus = jnp.maximum(neg_lam, 0.0) + jnp.log1p(jnp.exp(-jnp.abs(neg_lam)))
    log_a = -LRU_C * r * softplus
    a = jnp.exp(log_a)
    mult = jnp.sqrt(1.0 - a * a)
    a_scr[...] = a
    b_scr[...] = mult * gate_i * conv

    def body(t, h):
        row = tm - 1 - t if rev else t
        h = a_scr[pl.ds(row, 1), :] * h + b_scr[pl.ds(row, 1), :]
        if rev:
            o_ref[pl.ds(row, 1), :] = fwd_ref[pl.ds(row, 1), :] + h
        else:
            o_ref[pl.ds(row, 1), :] = h
        return h

    h_scr[...] = lax.fori_loop(0, tm, body, h_scr[...], unroll=8)


def _rglru(u, conv_w, conv_b, wa_bd, ba, wi_bd, bi, lam, bsz, n_lat, n_ctx):
    tm = TM
    d = D_MODEL
    n_tok = u.shape[0]
    n_s, n_c = n_lat // tm, n_ctx // tm
    lat_tiles = bsz * n_s
    per8 = tm // SUBLANES
    last8 = n_tok // SUBLANES - 1

    def make(rev, fwd):
        def blk(b, s):
            is_ctx = s < n_c
            if rev:
                tile = jnp.where(is_ctx, n_c - 1 - s, n_s - 1 - (s - n_c))
            else:
                tile = jnp.where(is_ctx, s, s - n_c)
            return jnp.where(is_ctx, lat_tiles + b * n_c + tile, b * n_s + tile)

        vec = pl.BlockSpec((None, 1, d), lambda b, s: (int(rev), 0, 0))
        wspec = pl.BlockSpec((None, d // MXU_DIM, MXU_DIM, MXU_DIM), lambda b, s: (int(rev), 0, 0, 0))
        row_spec = pl.BlockSpec((tm, d), lambda b, s: (blk(b, s), 0))
        in_specs = [
            pl.BlockSpec((tm, d), lambda b, s: (blk(b, s), CB_LX)),
            pl.BlockSpec((SUBLANES, d), lambda b, s: (jnp.maximum(blk(b, s) * per8 - 1, 0), CB_LX)),
            pl.BlockSpec((SUBLANES, d), lambda b, s: (jnp.minimum((blk(b, s) + 1) * per8, last8), CB_LX)),
            pl.BlockSpec((CONV_W, d), lambda b, s: (0, 0)),
            pl.BlockSpec((1, d), lambda b, s: (0, 0)),
            wspec, vec, wspec, vec, vec,
        ]
        args = [u, u, u, conv_w, conv_b.reshape(1, d), wa_bd, ba.reshape(2, 1, d), wi_bd, bi.reshape(2, 1, d),
                lam.reshape(2, 1, d)]
        if rev:
            in_specs.append(row_spec)
            args.append(fwd)
        return pl.pallas_call(
            functools.partial(_lru_kernel, rev=rev, n_c=n_c, n_s=n_s, tm=tm),
            grid=(bsz, n_c + n_s),
            in_specs=in_specs,
            out_specs=row_spec,
            out_shape=jax.ShapeDtypeStruct((n_tok, d), F32),
            scratch_shapes=[pltpu.VMEM((tm, d), F32), pltpu.VMEM((tm, d), F32), pltpu.VMEM((1, d), F32)],
            input_output_aliases={10: 0} if rev else {},
            compiler_params=_cparams(("arbitrary", "arbitrary"), 40),
            name="rglru_bwd" if rev else "rglru_fwd",
        )(*args)

    fwd = make(False, None)
    return make(True, fwd)


def _chan_dft_kernel(u_ref, cs_ref, o_ref):
    x = u_ref[...].astype(BF16)
    cs = cs_ref[...]
    gd = FOUR_GROUP_DIM
    for g in range(FOUR_GROUPS):
        y = jnp.dot(x[:, g * gd:(g + 1) * gd], cs, preferred_element_type=F32)
        o_ref[:, g * gd:(g + 1) * gd] = y[:, :gd].astype(BF16)
        o_ref[:, D_MODEL + g * gd:D_MODEL + (g + 1) * gd] = y[:, gd:].astype(BF16)


def _chan_dft(u, cs_mat, row0, n_rows):
    tm = TM
    d = D_MODEL
    tile0 = row0 // tm
    return pl.pallas_call(
        _chan_dft_kernel,
        grid=(n_rows // tm,),
        in_specs=[pl.BlockSpec((tm, d), lambda i: (tile0 + i, CB_F)),
                  pl.BlockSpec((FOUR_GROUP_DIM, 2 * FOUR_GROUP_DIM), lambda i: (0, 0))],
        out_specs=pl.BlockSpec((tm, 2 * d), lambda i: (i, 0)),
        out_shape=jax.ShapeDtypeStruct((n_rows, 2 * d), BF16),
        compiler_params=_cparams(("parallel",), 32),
        name="fourier_channels",
    )(u, cs_mat)


def _chan_dft_split_kernel(u_ref, cs_ref, o_ref, y_scr, *, radix):
    x = u_ref[...].astype(BF16)
    cs = cs_ref[...]
    gd = FOUR_GROUP_DIM
    per_g = gd // LANES
    n_cos = D_MODEL // LANES
    for g in range(FOUR_GROUPS):
        y = jnp.dot(x[:, g * gd:(g + 1) * gd], cs, preferred_element_type=F32)
        for j in range(per_g):
            y_scr[g * per_g + j] = y[:, j * LANES:(j + 1) * LANES]
            y_scr[n_cos + g * per_g + j] = y[:, gd + j * LANES:gd + (j + 1) * LANES]
    rows = u_ref.shape[0] // radix
    for r in range(radix):
        for cb in range(2 * n_cos):
            o_ref[r, :, cb * LANES:(cb + 1) * LANES] = y_scr[cb, pl.ds(r, rows, stride=radix), :].astype(BF16)


def _chan_dft_split(u, cs_mat, bsz, n_lat, radix):
    tm = TM
    d = D_MODEL
    n_s = n_lat // tm
    return pl.pallas_call(
        functools.partial(_chan_dft_split_kernel, radix=radix),
        grid=(bsz, n_s),
        in_specs=[pl.BlockSpec((tm, d), lambda b, i: (b * n_s + i, CB_F)),
                  pl.BlockSpec((FOUR_GROUP_DIM, 2 * FOUR_GROUP_DIM), lambda b, i: (0, 0))],
        out_specs=pl.BlockSpec((None, radix, tm // radix, 2 * d), lambda b, i: (b, 0, i, 0)),
        out_shape=jax.ShapeDtypeStruct((bsz, radix, n_lat // radix, 2 * d), BF16),
        scratch_shapes=[pltpu.VMEM((2 * d // LANES, tm, LANES), F32)],
        compiler_params=_cparams(("parallel", "parallel"), 32),
        name="fourier_channels_split",
    )(u, cs_mat)


def _time_fft_kernel(zr_ref, zi_ref, cc_ref, pc_ref, ps_ref, o_ref, *, radix):
    cc = cc_ref[...]
    tk1 = cc.shape[0] // 2
    for r in range(radix):
        pr = jnp.dot(cc, zr_ref[r], preferred_element_type=F32)
        pi = jnp.dot(cc, zi_ref[r], preferred_element_type=F32)
        a_re = pr[:tk1] - pi[tk1:]
        a_im = pr[tk1:] + pi[:tk1]
        for k2 in range(radix):
            term = pc_ref[k2][:, r:r + 1] * a_re - ps_ref[k2][:, r:r + 1] * a_im
            if r == 0:
                o_ref[k2] = term
            else:
                o_ref[k2] += term


def _time_fft(yp, cc, pc, ps, radix, tk1, tn):
    bsz, _, t1, two_d = yp.shape
    d = two_d // 2
    nj = d // tn
    return pl.pallas_call(
        functools.partial(_time_fft_kernel, radix=radix),
        grid=(bsz, nj, t1 // tk1),
        in_specs=[
            pl.BlockSpec((None, radix, t1, tn), lambda b, j, i: (b, 0, 0, j)),
            pl.BlockSpec((None, radix, t1, tn), lambda b, j, i: (b, 0, 0, nj + j)),
            pl.BlockSpec((None, 2 * tk1, t1), lambda b, j, i: (i, 0, 0)),
            pl.BlockSpec((radix, tk1, radix), lambda b, j, i: (0, i, 0)),
            pl.BlockSpec((radix, tk1, radix), lambda b, j, i: (0, i, 0)),
        ],
        out_specs=pl.BlockSpec((None, radix, tk1, tn), lambda b, j, i: (b, 0, i, j)),
        out_shape=jax.ShapeDtypeStruct((bsz, radix, t1, d), F32),
        compiler_params=_cparams(("parallel", "parallel", "arbitrary"), 48),
        name="fourier_positions_fft",
    )(yp, yp, cc, pc, ps)


def _fft_tables(n, radix, tk1):
    t1 = n // radix
    k1 = jnp.arange(t1, dtype=jnp.int32)
    ang = ((k1[:, None] * k1[None, :]) % t1).astype(F32) * (2.0 * math.pi / t1)
    c = jnp.cos(ang).reshape(t1 // tk1, tk1, t1)
    s = jnp.sin(ang).reshape(t1 // tk1, tk1, t1)
    cc = jnp.concatenate([c, s], axis=1).astype(BF16)
    k2 = jnp.arange(radix, dtype=jnp.int32)
    r = jnp.arange(radix, dtype=jnp.int32)
    k = k1[None, :, None] + t1 * k2[:, None, None]
    ph = ((k * r[None, None, :]) % n).astype(F32) * (2.0 * math.pi / n)
    scale = float(n) ** -0.5
    return cc, jnp.cos(ph) * scale, jnp.sin(ph) * scale


def _time_dft_kernel(ct_ref, st_ref, yc_ref, ys_ref, o_ref, acc, *, scale):
    kk = pl.program_id(2)

    @pl.when(kk == 0)
    def _():
        acc[...] = jnp.zeros_like(acc)

    acc[...] += (jnp.dot(ct_ref[...], yc_ref[...], preferred_element_type=F32)
                 - jnp.dot(st_ref[...], ys_ref[...], preferred_element_type=F32))

    @pl.when(kk == pl.num_programs(2) - 1)
    def _():
        o_ref[...] = acc[...] * scale


def _time_dft(y, ct, st, bsz, n_pos, row0, tmk):
    d = D_MODEL
    nt = n_pos // tmk
    rb0 = row0 // tmk
    return pl.pallas_call(
        functools.partial(_time_dft_kernel, scale=float(n_pos) ** -0.5),
        grid=(bsz, nt, nt),
        in_specs=[
            pl.BlockSpec((tmk, tmk), lambda b, i, k: (i, k)),
            pl.BlockSpec((tmk, tmk), lambda b, i, k: (i, k)),
            pl.BlockSpec((tmk, d), lambda b, i, k: (rb0 + b * nt + k, 0)),
            pl.BlockSpec((tmk, d), lambda b, i, k: (rb0 + b * nt + k, 1)),
        ],
        out_specs=pl.BlockSpec((tmk, d), lambda b, i, k: (b * nt + i, 0)),
        out_shape=jax.ShapeDtypeStruct((bsz * n_pos, d), F32),
        scratch_shapes=[pltpu.VMEM((tmk, d), F32)],
        compiler_params=_cparams(("parallel", "parallel", "arbitrary"), 48),
        name="fourier_positions",
    )(ct, st, y, y)


def _dft_tables(n):
    k = jnp.arange(n, dtype=jnp.int32)
    ang = ((k[:, None] * k[None, :]) % n).astype(F32) * (2.0 * math.pi / n)
    return jnp.cos(ang).astype(BF16), jnp.sin(ang).astype(BF16)


def _merge_kernel(x_ref, attn_l_ref, attn_c_ref, rec_ref, four_l_ref, four_c_ref, lg_ref, g0_ref, g1_ref, g2_ref,
                  wa_ref, wl_ref, wf_ref, wo_ref, gate_ref, n2_ref, sh_ref, sc_ref,
                  xo_ref, h_ref, hlo_ref, hhi_ref, *, lat_tiles):
    def mm(a, w_ref):
        return jnp.dot(a.astype(BF16), w_ref[...], preferred_element_type=F32)

    is_lat = pl.program_id(0) < lat_tiles
    attn_o = mm(jnp.where(is_lat, attn_l_ref[...], attn_c_ref[...]), wa_ref)
    rec_o = mm(rec_ref[...] * jax.nn.gelu(lg_ref[...], approximate=True), wl_ref)
    four_o = mm(jnp.where(is_lat, four_l_ref[...], four_c_ref[...]), wf_ref)
    mixed = (jax.nn.sigmoid(g0_ref[...]) * attn_o + jax.nn.sigmoid(g1_ref[...]) * rec_o
             + jax.nn.sigmoid(g2_ref[...]) * four_o)
    x = x_ref[...] + gate_ref[...] * mm(mixed, wo_ref)
    xo_ref[...] = x
    y = x * lax.rsqrt(jnp.mean(x * x, axis=-1, keepdims=True) + EPS)
    h = (y * n2_ref[...]) * (1.0 + sc_ref[...]) + sh_ref[...]
    h_ref[...] = h.astype(BF16)
    hlo_ref[...] = _pack_halves(h[:, :D_MODEL // 2])
    hhi_ref[...] = _pack_halves(h[:, D_MODEL // 2:])


def _merge(x_tok, attn_l, attn_c, rec, four_l, four_c, u, w_attn_o, w_lru_o, w_four_o, w_out, mod4, norm2_g,
           row_of_tile, n_rows):
    tm = TM
    d = D_MODEL
    lat_tiles = attn_l.shape[0] // tm
    ctx_tiles = attn_c.shape[0] // tm
    row = pl.BlockSpec((tm, d), lambda i: (i, 0))
    lat_row = pl.BlockSpec((tm, d), lambda i: (jnp.minimum(i, lat_tiles - 1), 0))
    ctx_row = pl.BlockSpec((tm, d), lambda i: (jnp.clip(i - lat_tiles, 0, ctx_tiles - 1), 0))
    wspec = pl.BlockSpec((d, d), lambda i: (0, 0))

    def ucol(cb):
        return pl.BlockSpec((tm, d), lambda i: (i, cb))

    return pl.pallas_call(
        functools.partial(_merge_kernel, lat_tiles=lat_tiles),
        grid=(n_rows // tm,),
        in_specs=[row, lat_row, ctx_row, row, lat_row, ctx_row, ucol(CB_LG), ucol(CB_G), ucol(CB_G + 1),
                  ucol(CB_G + 2), wspec, wspec, wspec, wspec,
                  _mod_spec(2, row_of_tile), pl.BlockSpec((1, d), lambda i: (0, 0)),
                  _mod_spec(3, row_of_tile), _mod_spec(4, row_of_tile)],
        out_specs=[row, row, pl.BlockSpec((tm, d // 4), lambda i: (i, 0)),
                   pl.BlockSpec((tm, d // 4), lambda i: (i, 0))],
        out_shape=[jax.ShapeDtypeStruct((n_rows, d), F32), jax.ShapeDtypeStruct((n_rows, d), BF16),
                   jax.ShapeDtypeStruct((n_rows, d // 4), jnp.uint32),
                   jax.ShapeDtypeStruct((n_rows, d // 4), jnp.uint32)],
        compiler_params=_cparams(("parallel",), 56),
        name="merge_branches",
    )(x_tok, attn_l, attn_c, rec, four_l, four_c, u, u, u, u, w_attn_o, w_lru_o, w_four_o, w_out,
      mod4, norm2_g.reshape(1, d), mod4, mod4)


def _route_kernel(h_ref, rw_ref, rb_ref, tri_ref, idx_ref, wgt_ref, pos_ref, cnt_ref, base_scr):
    tm = h_ref.shape[0]

    @pl.when(pl.program_id(0) == 0)
    def _():
        base_scr[...] = jnp.zeros_like(base_scr)

    logits = lax.dot_general(rw_ref[...], h_ref[...], (((1,), (1,)), ((), ())), preferred_element_type=F32)
    scores = jax.nn.sigmoid(logits)
    biased = scores + rb_ref[...]
    neg = -jnp.inf
    e_iota = lax.broadcasted_iota(jnp.int32, (N_EXPERTS, tm), 0)
    g_iota = lax.broadcasted_iota(jnp.int32, (N_GROUPS, tm), 0)

    rows = []
    for g in range(N_GROUPS):
        xg = biased[g * GROUP_SIZE:(g + 1) * GROUP_SIZE]
        m1 = jnp.max(xg, axis=0, keepdims=True)
        is_max = xg == m1
        n_max = jnp.sum(is_max.astype(F32), axis=0, keepdims=True)
        m2 = jnp.max(jnp.where(is_max, neg, xg), axis=0, keepdims=True)
        rows.append(m1 + jnp.where(n_max >= 2.0, m1, m2))
    grp = jnp.concatenate(rows, axis=0)

    g_sel = jnp.zeros((N_GROUPS, tm), jnp.bool_)
    work = grp
    for _ in range(TOPK_GROUPS):
        m = jnp.max(work, axis=0, keepdims=True)
        first = jnp.min(jnp.where(work == m, g_iota, N_GROUPS), axis=0, keepdims=True)
        hit = g_iota == first
        g_sel = jnp.logical_or(g_sel, hit)
        work = jnp.where(hit, neg, work)
    g_sel_f = g_sel.astype(F32)
    mask_rows = [jnp.broadcast_to(g_sel_f[g:g + 1], (GROUP_SIZE, tm)) for g in range(N_GROUPS)]
    e_mask = jnp.concatenate(mask_rows, axis=0) > 0.5

    work = jnp.where(e_mask, biased, neg)
    sel = jnp.zeros((N_EXPERTS, tm), F32)
    idx_rows, w_rows, hits = [], [], []
    for _ in range(TOP_K):
        m = jnp.max(work, axis=0, keepdims=True)
        first = jnp.min(jnp.where(work == m, e_iota, N_EXPERTS), axis=0, keepdims=True)
        hit = e_iota == first
        hits.append(hit)
        idx_rows.append(first)
        w_rows.append(jnp.sum(jnp.where(hit, scores, 0.0), axis=0, keepdims=True))
        sel = jnp.where(hit, 1.0, sel)
        work = jnp.where(hit, neg, work)
    w = jnp.concatenate(w_rows, axis=0)
    wgt_ref[...] = w / jnp.sum(w, axis=0, keepdims=True) * ROUTED_SCALE
    idx_ref[...] = jnp.concatenate(idx_rows, axis=0)

    before = jnp.dot(sel.astype(BF16), tri_ref[...], preferred_element_type=F32)
    rank = base_scr[...] + before
    pos_rows = [jnp.sum(jnp.where(hit, rank, 0.0), axis=0, keepdims=True) for hit in hits]
    pos_ref[...] = jnp.concatenate(pos_rows, axis=0).astype(jnp.int32)
    base_scr[...] = base_scr[...] + jnp.sum(sel, axis=1, keepdims=True)
    cnt_ref[...] = jnp.broadcast_to(base_scr[...], cnt_ref.shape)


def _route(h16, rw_t, router_b, tm):
    n_tok, d = h16.shape
    tri = (jnp.arange(tm)[:, None] < jnp.arange(tm)[None, :]).astype(BF16)
    out_col = pl.BlockSpec((TOP_K, tm), lambda i: (0, i))
    return pl.pallas_call(
        _route_kernel,
        grid=(n_tok // tm,),
        in_specs=[pl.BlockSpec((tm, d), lambda i: (i, 0)),
                  pl.BlockSpec((N_EXPERTS, d), lambda i: (0, 0)),
                  pl.BlockSpec((N_EXPERTS, 1), lambda i: (0, 0)),
                  pl.BlockSpec((tm, tm), lambda i: (0, 0))],
        out_specs=[out_col, out_col, out_col, pl.BlockSpec((N_EXPERTS, LANES), lambda i: (0, 0))],
        out_shape=[jax.ShapeDtypeStruct((TOP_K, n_tok), jnp.int32), jax.ShapeDtypeStruct((TOP_K, n_tok), F32),
                   jax.ShapeDtypeStruct((TOP_K, n_tok), jnp.int32),
                   jax.ShapeDtypeStruct((N_EXPERTS, LANES), F32)],
        scratch_shapes=[pltpu.VMEM((N_EXPERTS, 1), F32)],
        compiler_params=_cparams(("arbitrary",), 32),
        name="moe_router",
    )(h16, rw_t, router_b.reshape(N_EXPERTS, 1), tri)


def _slot_kernel(idx_ref, pos_ref, offs_ref, dest_ref):
    tm = idx_ref.shape[1]
    e_iota = lax.broadcasted_iota(jnp.int32, (N_EXPERTS, tm), 0)
    idx = idx_ref[...]
    offs = offs_ref[...]
    rows = [jnp.sum(jnp.where(e_iota == idx[r:r + 1], offs, 0), axis=0, keepdims=True) for r in range(TOP_K)]
    dest_ref[...] = jnp.concatenate(rows, axis=0) + pos_ref[...]


def _slots(top_idx, pos, offs, tm):
    n_tok = top_idx.shape[1]
    col = pl.BlockSpec((TOP_K, tm), lambda i: (0, i))
    return pl.pallas_call(
        _slot_kernel,
        grid=(n_tok // tm,),
        in_specs=[col, col, pl.BlockSpec((N_EXPERTS, 1), lambda i: (0, 0))],
        out_specs=col,
        out_shape=jax.ShapeDtypeStruct((TOP_K, n_tok), jnp.int32),
        compiler_params=_cparams(("parallel",), 32),
        name="moe_slots",
    )(top_idx, pos, offs.reshape(N_EXPERTS, 1))


def _sc_scatter_rows(src, idx, n_out):
    n_src, width = src.shape
    n_idx = idx.shape[0]
    n_win = n_src // SC_WINDOW
    mesh = plsc.VectorSubcoreMesh(core_axis_name="core", subcore_axis_name="subcore")

    @pl.kernel(out_type=jax.ShapeDtypeStruct((n_out, width), src.dtype), mesh=mesh, scratch_types=[],
               name="moe_scatter_sc")
    def scatter(src_hbm, idx_hbm, out_hbm):
        def body(src_vmem, idx_vmem):
            pltpu.sync_copy(src_vmem, out_hbm.at[idx_vmem.at[0]])

        pltpu.emit_pipeline(
            body,
            grid=(n_idx // SC_WINDOW,),
            in_specs=[pl.BlockSpec((SC_WINDOW, width), index_map=lambda g: (g % n_win, 0)),
                      pl.BlockSpec((1, SC_WINDOW), index_map=lambda g: (0, g))],
            out_specs=[],
            core_axis_name=("core", "subcore"),
            dimension_semantics=(pltpu.PARALLEL,),
        )(src_hbm, idx_hbm)

    return scatter(src, idx.reshape(1, n_idx))


def _expert_kernel(be_ref, nu_ref, nv_ref, xlo_ref, xhi_ref, wg_ref, wu_ref, wd_ref, ylo_ref, yhi_ref,
                   wg_s, wu_s, wd_s):
    i = pl.program_id(0)
    used = i < nu_ref[0]
    prev_e = be_ref[jnp.maximum(i - 1, 0)]
    fresh = jnp.logical_or(i == 0, be_ref[i] != prev_e)

    @pl.when(jnp.logical_and(used, fresh))
    def _():
        wg_s[...] = wg_ref[0].astype(BF16)
        wu_s[...] = wu_ref[0].astype(BF16)
        wd_s[...] = wd_ref[0].astype(BF16)

    @pl.when(used)
    def _():
        row = lax.broadcasted_iota(jnp.int32, xlo_ref.shape, 0)
        live = row < nv_ref[i]
        half = D_MODEL // 2
        quarter = D_MODEL // 4
        parts = []
        for ref in (xlo_ref, xhi_ref):
            parts.extend(p.astype(BF16) for p in _unpack_halves(jnp.where(live, ref[...], jnp.uint32(0))))

        def proj(w_s):
            acc = jnp.dot(parts[0], w_s[:quarter, :], preferred_element_type=F32)
            for k in range(1, 4):
                acc = acc + jnp.dot(parts[k], w_s[k * quarter:(k + 1) * quarter, :], preferred_element_type=F32)
            return acc

        g = proj(wg_s)
        act = (_silu(g) * proj(wu_s)).astype(BF16)
        y = jnp.dot(act, wd_s[...], preferred_element_type=F32)
        ylo_ref[...] = _pack_halves(y[:, :half])
        yhi_ref[...] = _pack_halves(y[:, half:])


def _experts(xs_lo, xs_hi, block_e, n_used, n_valid, w_gate, w_up, w_down, layer, bm):
    n_slots, quarter = xs_lo.shape
    half = 2 * quarter
    d = D_MODEL

    def blk(i, be, nu, nv):
        return (jnp.minimum(i, nu[0] - 1), 0)

    def wsel(i, be, nu, nv):
        return (layer, be[i], 0, 0)

    grid_spec = pltpu.PrefetchScalarGridSpec(
        num_scalar_prefetch=3,
        grid=(n_slots // bm,),
        in_specs=[pl.BlockSpec((bm, quarter), blk),
                  pl.BlockSpec((bm, quarter), blk),
                  pl.BlockSpec((None, 1, d, EXPERT_FF), wsel),
                  pl.BlockSpec((None, 1, d, EXPERT_FF), wsel),
                  pl.BlockSpec((None, 1, EXPERT_FF, d), wsel)],
        out_specs=[pl.BlockSpec((bm, half // 2), blk), pl.BlockSpec((bm, half // 2), blk)],
        scratch_shapes=[pltpu.VMEM((d, EXPERT_FF), BF16), pltpu.VMEM((d, EXPERT_FF), BF16),
                        pltpu.VMEM((EXPERT_FF, d), BF16)],
    )
    y_shape = jax.ShapeDtypeStruct((n_slots, half // 2), jnp.uint32)
    return pl.pallas_call(
        _expert_kernel,
        grid_spec=grid_spec,
        out_shape=[y_shape, y_shape],
        compiler_params=_cparams(("arbitrary",), 40),
        name="moe_experts",
    )(block_e, n_used, n_valid, xs_lo, xs_hi, w_gate, w_up, w_down)


def _sc_gather_rows(table, idx):
    n_idx = idx.shape[0]
    width = table.shape[1]
    mesh = plsc.VectorSubcoreMesh(core_axis_name="core", subcore_axis_name="subcore")

    @pl.kernel(out_type=jax.ShapeDtypeStruct((n_idx, width), table.dtype), mesh=mesh, scratch_types=[],
               name="moe_gather_sc")
    def gather(table_hbm, idx_hbm, out_hbm):
        def body(idx_vmem, out_vmem):
            pltpu.sync_copy(table_hbm.at[idx_vmem.at[0]], out_vmem)

        pltpu.emit_pipeline(
            body,
            grid=(n_idx // SC_WINDOW,),
            in_specs=[pl.BlockSpec((1, SC_WINDOW), index_map=lambda i: (0, i))],
            out_specs=[pl.BlockSpec((SC_WINDOW, width), index_map=lambda i: (i, 0))],
            core_axis_name=("core", "subcore"),
            dimension_semantics=(pltpu.PARALLEL,),
        )(idx_hbm, out_hbm)

    return gather(table, idx.reshape(1, n_idx))


def _combine_kernel(x_ref, h_ref, tw_ref, gate_ref, wsg_ref, wsu_ref, wsd_ref, lo_ref, hi_ref, o_ref):
    quarter = D_MODEL // 4
    h = h_ref[...]
    act = (_silu(jnp.dot(h, wsg_ref[...], preferred_element_type=F32))
           * jnp.dot(h, wsu_ref[...], preferred_element_type=F32)).astype(BF16)
    shared = jnp.dot(act, wsd_ref[...], preferred_element_type=F32)

    tw = tw_ref[...]
    gate = gate_ref[...]
    for q, buf in enumerate((lo_ref, hi_ref)):
        moe_a = moe_b = None
        for r in range(TOP_K):
            ya, yb = _unpack_halves(buf[r])
            wr = tw[:, r:r + 1]
            moe_a = ya * wr if moe_a is None else moe_a + ya * wr
            moe_b = yb * wr if moe_b is None else moe_b + yb * wr
        for part, moe in ((2 * q, moe_a), (2 * q + 1, moe_b)):
            cols = slice(part * quarter, (part + 1) * quarter)
            o_ref[:, cols] = x_ref[:, cols] + gate[:, cols] * (moe + shared[:, cols])


def _combine(x_tok, h16, g_lo, g_hi, top_w, mod4, ws_gate, ws_up, ws_down, row_of_tile):
    n_tok, d = x_tok.shape
    tm = TM
    ff = ws_gate.shape[1]
    row = pl.BlockSpec((tm, d), lambda i: (i, 0))
    gat = pl.BlockSpec((TOP_K, tm, d // 4), lambda i: (0, i, 0))
    return pl.pallas_call(
        _combine_kernel,
        grid=(n_tok // tm,),
        in_specs=[row, row,
                  pl.BlockSpec((tm, TOP_K), lambda i: (i, 0)),
                  _mod_spec(5, row_of_tile),
                  pl.BlockSpec((d, ff), lambda i: (0, 0)),
                  pl.BlockSpec((d, ff), lambda i: (0, 0)),
                  pl.BlockSpec((ff, d), lambda i: (0, 0)),
                  gat, gat],
        out_specs=row,
        out_shape=jax.ShapeDtypeStruct((n_tok, d), F32),
        compiler_params=_cparams(("parallel",), 48),
        name="moe_combine",
    )(x_tok, h16, top_w, mod4, ws_gate, ws_up, ws_down, g_lo, g_hi)


def _moe(x_tok, h16, h_lo, h_hi, mod4, row_of_tile, rw_t, router_b, w_gate, w_up, w_down, layer, ws_gate, ws_up, ws_down):
    n_tok = x_tok.shape[0]
    bm = MOE_BM
    tm_r = TM_ROUTE if n_tok % TM_ROUTE == 0 else TM
    top_idx, top_w, pos, cnt = _route(h16, rw_t, router_b, tm_r)

    counts = cnt[:, 0].astype(jnp.int32)
    padded = (counts + bm - 1) // bm * bm
    ends = jnp.cumsum(padded)
    n_blocks = -(-(n_tok * TOP_K + N_EXPERTS * (bm - 1)) // bm)
    offs = ends - padded
    dest = _slots(top_idx, pos, offs, tm_r)
    block_start = jnp.arange(n_blocks, dtype=jnp.int32) * bm
    block_e = jnp.minimum(jnp.sum((ends[None, :] <= block_start[:, None]).astype(jnp.int32), axis=1),
                          N_EXPERTS - 1)
    n_valid = jnp.clip(counts[block_e] - (block_start - offs[block_e]), 0, bm).astype(jnp.int32)
    n_used = (ends[-1] // bm).astype(jnp.int32).reshape(1)
    dest_flat = dest.reshape(TOP_K * n_tok)
    xs_lo = _sc_scatter_rows(h_lo, dest_flat, n_blocks * bm)
    xs_hi = _sc_scatter_rows(h_hi, dest_flat, n_blocks * bm)
    ys_lo, ys_hi = _experts(xs_lo, xs_hi, block_e, n_used, n_valid, w_gate, w_up, w_down, layer, bm)
    g_lo = _sc_gather_rows(ys_lo, dest_flat).reshape(TOP_K, n_tok, D_MODEL // 4)
    g_hi = _sc_gather_rows(ys_hi, dest_flat).reshape(TOP_K, n_tok, D_MODEL // 4)
    return _combine(x_tok, h16, g_lo, g_hi, top_w.T, mod4, ws_gate, ws_up, ws_down, row_of_tile)


def _block_diag_tiles(w):
    per = MXU_DIM // LRU_BLOCK_DIM
    w = w.reshape(2, D_MODEL // MXU_DIM, per, LRU_BLOCK_DIM, LRU_BLOCK_DIM)
    eye = jnp.eye(per, dtype=w.dtype)
    t = jnp.einsum('ztpde,pq->ztpdqe', w, eye)
    return t.reshape(2, D_MODEL // MXU_DIM, MXU_DIM, MXU_DIM).astype(BF16)


def _rope_tables(n_lat, n_ctx):
    rows = n_lat // GRID_W
    row = jnp.repeat(jnp.arange(rows), GRID_W)
    col = jnp.tile(jnp.arange(GRID_W), rows)
    inv = ROPE_THETA ** (-jnp.arange(ROPE_FREQS, dtype=F32) / ROPE_FREQS)
    ang = jnp.stack([row[:, None] * inv, col[:, None] * inv], axis=1)
    ang = jnp.concatenate([ang, ang], axis=2).reshape(n_lat, HEAD_DIM)
    ang = jnp.tile(ang, (1, LANES // HEAD_DIM))
    ang = jnp.concatenate([ang, jnp.zeros((n_ctx, LANES), F32)], axis=0)
    lane = jnp.arange(LANES)
    sign = jnp.where((lane % (2 * ROPE_FREQS)) < ROPE_FREQS, -1.0, 1.0).astype(F32)
    return jnp.cos(ang), jnp.sin(ang) * sign


def kernel(x, c, ctx, c_ctx, w_mod, b_mod, norm1_g, w_in, q_norm_g, k_norm_g, lambda_qk, subln_g, w_attn_o, conv_w, conv_b, lru_wa, lru_ba, lru_wi, lru_bi, lru_lambda, w_lru_o, w_four_o, w_out, norm2_g, router_w, router_b, exp_w_gate, exp_w_up, exp_w_down, sh_w_gate, sh_w_up, sh_w_down):
    bsz, n_lat, d = x.shape
    n_ctx = ctx.shape[1]
    depth = w_mod.shape[0]
    n_latent_rows = bsz * n_lat
    n_tok = n_latent_rows + bsz * n_ctx
    assert d == D_MODEL and bsz + 1 <= MOD_ROWS
    assert n_lat % TM == 0 and n_ctx % TM == 0 and n_lat % GRID_W == 0
    tm_in = TM_IN if (n_lat % TM_IN == 0 and (bsz * n_ctx) % TM_IN == 0) else TM

    cvec = jnp.zeros((MOD_ROWS, d), F32).at[:bsz].set(c).at[bsz].set(c_ctx)
    mods = _modulation(cvec, w_mod, b_mod)

    def row_of_tile_fn(tm):
        per_batch = n_lat // tm
        return lambda i: jnp.minimum(i // per_batch, bsz)

    assert n_lat % n_ctx == 0
    cos_t, sin_t = _rope_tables(n_lat, n_ctx)
    group_mat = jnp.kron(jnp.eye(d // HEAD_DIM, dtype=F32), jnp.ones((HEAD_DIM, HEAD_DIM), F32)).astype(BF16)
    m = jnp.arange(FOUR_GROUP_DIM, dtype=jnp.int32)
    ang_c = ((m[:, None] * m[None, :]) % FOUR_GROUP_DIM).astype(F32) * (2.0 * math.pi / FOUR_GROUP_DIM)
    inv_sqrt_c = FOUR_GROUP_DIM ** -0.5
    cs_mat = jnp.concatenate([jnp.cos(ang_c) * inv_sqrt_c, jnp.sin(ang_c) * inv_sqrt_c], axis=1).astype(BF16)
    assert n_lat % (FFT_RADIX * MXU_DIM) == 0 or n_lat // FFT_RADIX < MXU_DIM
    fft_tk1 = min(MXU_DIM, n_lat // FFT_RADIX)
    fft_cc, fft_pc, fft_ps = _fft_tables(n_lat, FFT_RADIX, fft_tk1)
    ct_ctx, st_ctx = _dft_tables(n_ctx)
    q_sub = math.gcd(Q_SUB, n_lat // TQ)
    n_keys = n_ctx + n_lat
    tk = TK if n_keys % TK == 0 else TM

    x_tok = jnp.concatenate([x.reshape(n_latent_rows, d), ctx.reshape(bsz * n_ctx, d)], axis=0)
    for l in range(depth):
        last = l == depth - 1
        lam_init = 0.8 - 0.6 * math.exp(-0.3 * l)
        mod4 = mods[l].reshape(MOD_ROWS, 6, 1, d)

        u = _in_projection(x_tok, mod4, norm1_g[l], w_in[l].astype(BF16), row_of_tile_fn(tm_in), tm_in)

        k_all, q_all, vt_all = _qkv_prepare(u, k_norm_g[l], q_norm_g[l], group_mat, cos_t, sin_t,
                                            bsz, n_lat, n_ctx)
        attn_l = _diff_attention(q_all, k_all, vt_all, lambda_qk[l], subln_g[l], lam_init,
                                 0, n_lat, 0, n_keys, TQ, q_sub, tk)
        attn_c = attn_l
        if not last:
            attn_c = _diff_attention(q_all, k_all, vt_all, lambda_qk[l], subln_g[l], lam_init,
                                     n_lat, n_ctx, n_lat, n_ctx, TM, 1, TM)

        rec = _rglru(u, conv_w[l], conv_b[l], _block_diag_tiles(lru_wa[l]), lru_ba[l],
                     _block_diag_tiles(lru_wi[l]), lru_bi[l], lru_lambda[l], bsz, n_lat, n_ctx)

        y_lat = _chan_dft_split(u, cs_mat, bsz, n_lat, FFT_RADIX)
        four_l = _time_fft(y_lat, fft_cc, fft_pc, fft_ps, FFT_RADIX, fft_tk1, MXU_DIM).reshape(n_latent_rows, d)
        four_c = four_l
        if not last:
            y_ctx = _chan_dft(u, cs_mat, n_latent_rows, bsz * n_ctx)
            four_c = _time_dft(y_ctx, ct_ctx, st_ctx, bsz, n_ctx, 0, n_ctx)

        n_rows = n_latent_rows if last else n_tok
        row_of_tile = row_of_tile_fn(TM)
        x_mid, h16, h_lo, h_hi = _merge(x_tok, attn_l, attn_c, rec, four_l, four_c, u, w_attn_o[l].astype(BF16),
                                w_lru_o[l].astype(BF16), w_four_o[l].astype(BF16), w_out[l].astype(BF16),
                                mod4, norm2_g[l], row_of_tile, n_rows)
        x_tok = _moe(x_mid, h16, h_lo, h_hi, mod4, row_of_tile, router_w[l].T.astype(BF16), router_b[l],
                     exp_w_gate, exp_w_up, exp_w_down, l,
                     sh_w_gate[l].astype(BF16), sh_w_up[l].astype(BF16), sh_w_down[l].astype(BF16))
    return x_tok[:n_latent_rows].reshape(bsz, n_lat, d)
```

```python
import functools
import math

import jax
import jax.numpy as jnp
from jax import lax
from jax.experimental import pallas as pl
from jax.experimental.pallas import tpu as pltpu
from jax.experimental.pallas import tpu_sc as plsc

F32 = jnp.float32
BF16 = jnp.bfloat16

D_MODEL = 1024
EPS = 1e-6
GRID_W = 64

N_HEADS = 8
HEAD_DIM = 64
V_DIM = 2 * HEAD_DIM
ATTN_SCALE = HEAD_DIM ** -0.5
ROPE_THETA = 10000.0
ROPE_FREQS = HEAD_DIM // 4

LRU_BLOCKS = 16
LRU_BLOCK_DIM = D_MODEL // LRU_BLOCKS
LRU_C = 8.0
CONV_W = 4

FOUR_GROUPS = 4
FOUR_GROUP_DIM = D_MODEL // FOUR_GROUPS

CB_K, CB_V, CB_LX, CB_Q, CB_LG, CB_F, CB_G = 0, 1, 2, 3, 4, 5, 6
IN_BLOCKS = 9

N_EXPERTS = 256
TOP_K = 8
N_GROUPS = 8
GROUP_SIZE = N_EXPERTS // N_GROUPS
TOPK_GROUPS = 4
EXPERT_FF = 256
ROUTED_SCALE = 2.5

LANES = 128
SUBLANES = 8
MXU_DIM = 256
VMEM_BYTES_V7X = 64 * 1024 * 1024

MOD_ROWS = 8
TM = 256
TM_IN = 512
TQ = 256
Q_SUB = 32
TK = 768
TM_ROUTE = 512
MOE_BM = 512
SC_WINDOW = 128
FFT_RADIX = 8
LOG2E = 1.4426950408889634


def _cparams(sem, vmem_mb):
    return pltpu.CompilerParams(dimension_semantics=sem, vmem_limit_bytes=int(vmem_mb * 1024 * 1024))


def _silu(x):
    return x * jax.nn.sigmoid(x)


def _pack_halves(x):
    half = x.shape[1] // 2
    bits = pltpu.bitcast(x.astype(BF16).astype(F32), jnp.uint32)
    return (bits[:, :half] & jnp.uint32(0xFFFF0000)) | (bits[:, half:] >> 16)


def _unpack_halves(w):
    return pltpu.bitcast(w & jnp.uint32(0xFFFF0000), F32), pltpu.bitcast(w << 16, F32)


def _mod_kernel(c_ref, w_ref, b_ref, o_ref):
    c = c_ref[...]
    s = _silu(c).astype(BF16)
    o_ref[0] = jnp.dot(s, w_ref[0].astype(BF16), preferred_element_type=F32) + b_ref[0]


def _modulation(cvec, w_mod, b_mod):
    n_layers, d, six_d = w_mod.shape
    nj = six_d // d
    return pl.pallas_call(
        _mod_kernel,
        grid=(n_layers, nj),
        in_specs=[
            pl.BlockSpec((MOD_ROWS, d), lambda l, j: (0, 0)),
            pl.BlockSpec((1, d, d), lambda l, j: (l, 0, j)),
            pl.BlockSpec((1, 1, d), lambda l, j: (l, 0, j)),
        ],
        out_specs=pl.BlockSpec((1, MOD_ROWS, d), lambda l, j: (l, 0, j)),
        out_shape=jax.ShapeDtypeStruct((n_layers, MOD_ROWS, six_d), F32),
        compiler_params=_cparams(("parallel", "parallel"), 32),
        name="modulation",
    )(cvec, w_mod, b_mod.reshape(n_layers, 1, six_d))


def _mod_spec(part, row_of_tile, n_grid_axes=1):
    if n_grid_axes == 1:
        return pl.BlockSpec((None, None, 1, D_MODEL), lambda i: (row_of_tile(i), part, 0, 0))
    return pl.BlockSpec((None, None, 1, D_MODEL), lambda i, j: (row_of_tile(i), part, 0, 0))


def _inproj_kernel(x_ref, g_ref, sh_ref, sc_ref, w_ref, o_ref, h_scr):
    @pl.when(pl.program_id(1) == 0)
    def _():
        x = x_ref[...]
        y = x * lax.rsqrt(jnp.mean(x * x, axis=-1, keepdims=True) + EPS)
        h = (y * g_ref[...]) * (1.0 + sc_ref[...]) + sh_ref[...]
        h_scr[...] = h.astype(BF16)

    o_ref[...] = jnp.dot(h_scr[...], w_ref[...], preferred_element_type=F32).astype(o_ref.dtype)


def _in_projection(x_tok, mod4, norm_g, w_in_bf16, row_of_tile, tm):
    n_tok, d = x_tok.shape
    n_cols = w_in_bf16.shape[1]
    return pl.pallas_call(
        _inproj_kernel,
        grid=(n_tok // tm, n_cols // d),
        in_specs=[
            pl.BlockSpec((tm, d), lambda i, j: (i, 0)),
            pl.BlockSpec((1, d), lambda i, j: (0, 0)),
            _mod_spec(0, row_of_tile, 2),
            _mod_spec(1, row_of_tile, 2),
            pl.BlockSpec((d, d), lambda i, j: (0, j)),
        ],
        out_specs=pl.BlockSpec((tm, d), lambda i, j: (i, j)),
        out_shape=jax.ShapeDtypeStruct((n_tok, n_cols), BF16),
        scratch_shapes=[pltpu.VMEM((tm, d), BF16)],
        compiler_params=_cparams(("parallel", "arbitrary"), 40),
        name="in_projection",
    )(x_tok, norm_g.reshape(1, d), mod4, mod4, w_in_bf16)


def _prep_kernel(uk_ref, uq_ref, uv_ref, kg_ref, qg_ref, gm_ref, cos_ref, sin_ref, k_out, q_out, vt_out, *, tm):
    gm = gm_ref[...]
    lane = lax.broadcasted_iota(jnp.int32, (tm, LANES), 1)
    first_half = (lane % (2 * ROPE_FREQS)) < ROPE_FREQS

    def norm_rope(x, g):
        ss = jnp.dot((x * x).astype(BF16), gm, preferred_element_type=F32) * (1.0 / HEAD_DIM)
        y = x * lax.rsqrt(ss + EPS) * g
        c = cos_ref[...]
        s = sin_ref[...]
        outs = []
        for j in range(D_MODEL // LANES):
            yt = y[:, j * LANES:(j + 1) * LANES]
            rot = jnp.where(first_half, pltpu.roll(yt, LANES - ROPE_FREQS, 1), pltpu.roll(yt, ROPE_FREQS, 1))
            outs.append(yt * c + rot * s)
        return jnp.concatenate(outs, axis=1)

    k_out[...] = norm_rope(uk_ref[...].astype(F32), kg_ref[...]).astype(BF16)
    qn = norm_rope(uq_ref[...].astype(F32), qg_ref[...]) * (ATTN_SCALE * LOG2E)
    lo = lane < HEAD_DIM
    for h in range(N_HEADS):
        qt = qn[:, h * LANES:(h + 1) * LANES]
        q_out[:, (2 * h) * LANES:(2 * h + 1) * LANES] = jnp.where(lo, qt, 0.0).astype(BF16)
        q_out[:, (2 * h + 1) * LANES:(2 * h + 2) * LANES] = jnp.where(lo, 0.0, qt).astype(BF16)
    vt_out[...] = uv_ref[...].astype(F32).T.astype(BF16)


def _qkv_prepare(u, k_gain, q_gain, group_mat, cos_t, sin_t, bsz, n_lat, n_ctx):
    tm = TM
    d = D_MODEL
    n_s, n_c = n_lat // tm, n_ctx // tm
    n_rows = n_lat + n_ctx
    lat_tiles = bsz * n_s
    kg = jnp.tile(k_gain, d // HEAD_DIM).reshape(1, d)
    qg = jnp.tile(q_gain, d // HEAD_DIM).reshape(1, d)

    def tok_tile(b, j):
        return jnp.where(j < n_s, b * n_s + j, lat_tiles + b * n_c + (j - n_s))

    def u_spec(cb):
        return pl.BlockSpec((tm, d), lambda b, j: (tok_tile(b, j), cb))

    vec = pl.BlockSpec((1, d), lambda b, j: (0, 0))
    rope_spec = pl.BlockSpec((tm, LANES), lambda b, j: (j, 0))
    return pl.pallas_call(
        functools.partial(_prep_kernel, tm=tm),
        grid=(bsz, n_s + n_c),
        in_specs=[u_spec(CB_K), u_spec(CB_Q), u_spec(CB_V), vec, vec,
                  pl.BlockSpec((d, d), lambda b, j: (0, 0)), rope_spec, rope_spec],
        out_specs=[
            pl.BlockSpec((None, tm, d), lambda b, j: (b, j, 0)),
            pl.BlockSpec((None, tm, 2 * d), lambda b, j: (b, j, 0)),
            pl.BlockSpec((None, d, tm), lambda b, j: (b, 0, j)),
        ],
        out_shape=[jax.ShapeDtypeStruct((bsz, n_rows, d), BF16),
                   jax.ShapeDtypeStruct((bsz, n_rows, 2 * d), BF16),
                   jax.ShapeDtypeStruct((bsz, d, n_rows), BF16)],
        compiler_params=_cparams(("parallel", "parallel"), 40),
        name="qkv_prepare",
    )(u, u, u, kg, qg, group_mat, cos_t, sin_t)


def _attn_kernel(lq_ref, sg_ref, q1_ref, q2_ref, k_ref, vt_ref, o_ref, s_buf, *, n_keys, tk, tq, lam_init):
    lq = lq_ref[...]
    lam = (jnp.exp(jnp.sum(lq[0:1] * lq[1:2], axis=1, keepdims=True))
           - jnp.exp(jnp.sum(lq[2:3] * lq[3:4], axis=1, keepdims=True)) + lam_init)
    n_chunks = n_keys // tk
    n_sub = q1_ref.shape[0] // tq

    def load_q(j):
        rows = pl.ds(pl.multiple_of(j * tq, tq), tq)
        return jnp.concatenate([q1_ref[rows, :], q2_ref[rows, :]], axis=0)

    def score_chunk(q, c, mx):
        s = lax.dot_general(k_ref[c * tk:(c + 1) * tk, :], q, (((1,), (1,)), ((), ())),
                            preferred_element_type=F32)
        s_buf[c * tk:(c + 1) * tk, :] = s
        cm = jnp.max(s, axis=0, keepdims=True)
        return cm if mx is None else jnp.maximum(mx, cm)

    q0 = load_q(0)
    mx0 = None
    for c in range(n_chunks):
        mx0 = score_chunk(q0, c, mx0)

    def sub_tile(j, mx):
        q_next = load_q(jnp.minimum(j + 1, n_sub - 1))
        mx_next = None
        l = jnp.zeros((1, 2 * tq), F32)
        acc = jnp.zeros((V_DIM, 2 * tq), F32)
        for c in range(n_chunks):
            p = jnp.exp2(s_buf[c * tk:(c + 1) * tk, :] - mx)
            l = l + jnp.sum(p, axis=0, keepdims=True)
            acc = acc + jnp.dot(vt_ref[:, c * tk:(c + 1) * tk], p.astype(BF16), preferred_element_type=F32)
            mx_next = score_chunk(q_next, c, mx_next)
        o = acc / l
        o = o[:, :tq] - lam * o[:, tq:]
        ms = jnp.mean(o * o, axis=0, keepdims=True)
        on = (o * lax.rsqrt(ms + EPS) * sg_ref[...]) * (1.0 - lam_init)
        o_ref[pl.ds(pl.multiple_of(j * tq, tq), tq), :] = on.T.astype(o_ref.dtype)
        return mx_next

    lax.fori_loop(0, n_sub, sub_tile, mx0)


def _diff_attention(q_all, k_all, vt_all, lambda_qk, subln_g, lam_init, q_row0, n_q, key_row0, n_keys, tq, n_sub, tk):
    bsz = q_all.shape[0]
    d = D_MODEL
    tstep = tq * n_sub
    nq = n_q // tstep
    qb0 = q_row0 // tstep
    kb0 = key_row0 // n_keys
    s_bytes = n_keys * 2 * tq * 4
    return pl.pallas_call(
        functools.partial(_attn_kernel, n_keys=n_keys, tk=tk, tq=tq, lam_init=lam_init),
        grid=(bsz, N_HEADS, nq),
        in_specs=[
            pl.BlockSpec((4, HEAD_DIM), lambda b, h, i: (0, 0)),
            pl.BlockSpec((V_DIM, 1), lambda b, h, i: (0, 0)),
            pl.BlockSpec((None, tstep, LANES), lambda b, h, i: (b, qb0 + i, 2 * h)),
            pl.BlockSpec((None, tstep, LANES), lambda b, h, i: (b, qb0 + i, 2 * h + 1)),
            pl.BlockSpec((None, n_keys, LANES), lambda b, h, i: (b, kb0, h)),
            pl.BlockSpec((None, V_DIM, n_keys), lambda b, h, i: (b, h, kb0)),
        ],
        out_specs=pl.BlockSpec((tstep, LANES), lambda b, h, i: (b * nq + i, h)),
        out_shape=jax.ShapeDtypeStruct((bsz * n_q, d), BF16),
        scratch_shapes=[pltpu.VMEM((n_keys, 2 * tq), F32)],
        compiler_params=_cparams(("parallel", "parallel", "arbitrary"), 28 + s_bytes / 2 ** 20),
        name="diff_attention",
    )(lambda_qk, subln_g.reshape(V_DIM, 1), q_all, q_all, k_all, vt_all)


def _lru_kernel(*refs, rev, n_c, n_s, tm):
    if rev:
        (cur_ref, prev_ref, next_ref, cw_ref, cb_ref, wa_ref, ba_ref, wi_ref, bi_ref, lam_ref, fwd_ref,
         o_ref, a_scr, b_scr, h_scr) = refs
    else:
        (cur_ref, prev_ref, next_ref, cw_ref, cb_ref, wa_ref, ba_ref, wi_ref, bi_ref, lam_ref,
         o_ref, a_scr, b_scr, h_scr) = refs
    s = pl.program_id(1)
    is_ctx = s < n_c
    if rev:
        tile = jnp.where(is_ctx, n_c - 1 - s, n_s - 1 - (s - n_c))
    else:
        tile = jnp.where(is_ctx, s, s - n_c)
    n_tile = jnp.where(is_ctx, n_c, n_s)

    @pl.when(s == 0)
    def _():
        h_scr[...] = jnp.zeros_like(h_scr)

    x = cur_ref[...].astype(F32)
    pv = jnp.where(tile == 0, 0.0, prev_ref[...].astype(F32)[SUBLANES:])
    nx = jnp.where(tile == n_tile - 1, 0.0, next_ref[...].astype(F32)[:SUBLANES])
    r8 = lax.broadcasted_iota(jnp.int32, (SUBLANES, D_MODEL), 0)

    def earlier(k):
        rolled = pltpu.roll(x, k, 0)
        top = jnp.where(r8 < k, pltpu.roll(pv, k, 0), rolled[:SUBLANES])
        return jnp.concatenate([top, rolled[SUBLANES:]], axis=0)

    rolled = pltpu.roll(x, tm - 1, 0)
    bot = jnp.where(r8 >= SUBLANES - 1, pltpu.roll(nx, SUBLANES - 1, 0), rolled[tm - SUBLANES:])
    later = jnp.concatenate([rolled[:tm - SUBLANES], bot], axis=0)
    cw = cw_ref[...]
    conv = earlier(2) * cw[0:1] + earlier(1) * cw[1:2] + x * cw[2:3] + later * cw[3:4] + cb_ref[...]

    conv16 = conv.astype(BF16)

    def block_diag(w_ref, b_ref):
        parts = [jnp.dot(conv16[:, j * MXU_DIM:(j + 1) * MXU_DIM], w_ref[j], preferred_element_type=F32)
                 for j in range(D_MODEL // MXU_DIM)]
        return jnp.concatenate(parts, axis=1) + b_ref[...]

    r = jax.nn.sigmoid(block_diag(wa_ref, ba_ref))
    gate_i = jax.nn.sigmoid(block_diag(wi_ref, bi_ref))
    neg_lam = -lam_ref[...]
    softplus = jnp.maximum(neg_lam, 0.0) + jnp.log1p(jnp.exp(-jnp.abs(neg_lam)))
    log_a = -LRU_C * r * softplus
    a = jnp.exp(log_a)
    mult = jnp.sqrt(1.0 - a * a)
    a_scr[...] = a
    b_scr[...] = mult * gate_i * conv

    def body(t, h):
        row = tm - 1 - t if rev else t
        h = a_scr[pl.ds(row, 1), :] * h + b_scr[pl.ds(row, 1), :]
        if rev:
            o_ref[pl.ds(row, 1), :] = fwd_ref[pl.ds(row, 1), :] + h
        else:
            o_ref[pl.ds(row, 1), :] = h
        return h

    h_scr[...] = lax.fori_loop(0, tm, body, h_scr[...], unroll=8)


def _rglru(u, conv_w, conv_b, wa_bd, ba, wi_bd, bi, lam, bsz, n_lat, n_ctx):
    tm = TM
    d = D_MODEL
    n_tok = u.shape[0]
    n_s, n_c = n_lat // tm, n_ctx // tm
    lat_tiles = bsz * n_s
    halo = 2 * SUBLANES
    per_halo = tm // halo
    last_halo = n_tok // halo - 1

    def make(rev, fwd):
        def blk(b, s):
            is_ctx = s < n_c
            if rev:
                tile = jnp.where(is_ctx, n_c - 1 - s, n_s - 1 - (s - n_c))
            else:
                tile = jnp.where(is_ctx, s, s - n_c)
            return jnp.where(is_ctx, lat_tiles + b * n_c + tile, b * n_s + tile)

        vec = pl.BlockSpec((None, 1, d), lambda b, s: (int(rev), 0, 0))
        wspec = pl.BlockSpec((None, d // MXU_DIM, MXU_DIM, MXU_DIM), lambda b, s: (int(rev), 0, 0, 0))
        row_spec = pl.BlockSpec((tm, d), lambda b, s: (blk(b, s), 0))
        in_specs = [
            pl.BlockSpec((tm, d), lambda b, s: (blk(b, s), CB_LX)),
            pl.BlockSpec((halo, d), lambda b, s: (jnp.maximum(blk(b, s) * per_halo - 1, 0), CB_LX)),
            pl.BlockSpec((halo, d), lambda b, s: (jnp.minimum((blk(b, s) + 1) * per_halo, last_halo), CB_LX)),
            pl.BlockSpec((CONV_W, d), lambda b, s: (0, 0)),
            pl.BlockSpec((1, d), lambda b, s: (0, 0)),
            wspec, vec, wspec, vec, vec,
        ]
        args = [u, u, u, conv_w, conv_b.reshape(1, d), wa_bd, ba.reshape(2, 1, d), wi_bd, bi.reshape(2, 1, d),
                lam.reshape(2, 1, d)]
        if rev:
            in_specs.append(row_spec)
            args.append(fwd)
        return pl.pallas_call(
            functools.partial(_lru_kernel, rev=rev, n_c=n_c, n_s=n_s, tm=tm),
            grid=(bsz, n_c + n_s),
            in_specs=in_specs,
            out_specs=row_spec,
            out_shape=jax.ShapeDtypeStruct((n_tok, d), F32),
            scratch_shapes=[pltpu.VMEM((tm, d), F32), pltpu.VMEM((tm, d), F32), pltpu.VMEM((1, d), F32)],
            input_output_aliases={10: 0} if rev else {},
            compiler_params=_cparams(("arbitrary", "arbitrary"), 40),
            name="rglru_bwd" if rev else "rglru_fwd",
        )(*args)

    fwd = make(False, None)
    return make(True, fwd)


def _chan_dft_kernel(u_ref, cs_ref, o_ref):
    x = u_ref[...].astype(BF16)
    cs = cs_ref[...]
    gd = FOUR_GROUP_DIM
    for g in range(FOUR_GROUPS):
        y = jnp.dot(x[:, g * gd:(g + 1) * gd], cs, preferred_element_type=F32)
        o_ref[:, g * gd:(g + 1) * gd] = y[:, :gd].astype(BF16)
        o_ref[:, D_MODEL + g * gd:D_MODEL + (g + 1) * gd] = y[:, gd:].astype(BF16)


def _chan_dft(u, cs_mat, row0, n_rows):
    tm = TM
    d = D_MODEL
    tile0 = row0 // tm
    return pl.pallas_call(
        _chan_dft_kernel,
        grid=(n_rows // tm,),
        in_specs=[pl.BlockSpec((tm, d), lambda i: (tile0 + i, CB_F)),
                  pl.BlockSpec((FOUR_GROUP_DIM, 2 * FOUR_GROUP_DIM), lambda i: (0, 0))],
        out_specs=pl.BlockSpec((tm, 2 * d), lambda i: (i, 0)),
        out_shape=jax.ShapeDtypeStruct((n_rows, 2 * d), BF16),
        compiler_params=_cparams(("parallel",), 32),
        name="fourier_channels",
    )(u, cs_mat)


def _chan_dft_split_kernel(u_ref, cs_ref, o_ref, y_scr, *, radix):
    x = u_ref[...].astype(BF16)
    cs = cs_ref[...]
    gd = FOUR_GROUP_DIM
    per_g = gd // LANES
    n_cos = D_MODEL // LANES
    for g in range(FOUR_GROUPS):
        y = jnp.dot(x[:, g * gd:(g + 1) * gd], cs, preferred_element_type=F32)
        for j in range(per_g):
            y_scr[g * per_g + j] = y[:, j * LANES:(j + 1) * LANES]
            y_scr[n_cos + g * per_g + j] = y[:, gd + j * LANES:gd + (j + 1) * LANES]
    rows = u_ref.shape[0] // radix
    for r in range(radix):
        for cb in range(2 * n_cos):
            o_ref[r, :, cb * LANES:(cb + 1) * LANES] = y_scr[cb, pl.ds(r, rows, stride=radix), :].astype(BF16)


def _chan_dft_split(u, cs_mat, bsz, n_lat, radix):
    tm = TM
    d = D_MODEL
    n_s = n_lat // tm
    return pl.pallas_call(
        functools.partial(_chan_dft_split_kernel, radix=radix),
        grid=(bsz, n_s),
        in_specs=[pl.BlockSpec((tm, d), lambda b, i: (b * n_s + i, CB_F)),
                  pl.BlockSpec((FOUR_GROUP_DIM, 2 * FOUR_GROUP_DIM), lambda b, i: (0, 0))],
        out_specs=pl.BlockSpec((None, radix, tm // radix, 2 * d), lambda b, i: (b, 0, i, 0)),
        out_shape=jax.ShapeDtypeStruct((bsz, radix, n_lat // radix, 2 * d), BF16),
        scratch_shapes=[pltpu.VMEM((2 * d // LANES, tm, LANES), F32)],
        compiler_params=_cparams(("parallel", "parallel"), 32),
        name="fourier_channels_split",
    )(u, cs_mat)


def _time_fft_kernel(zr_ref, zi_ref, cc_ref, pc_ref, ps_ref, o_ref, *, radix):
    cc = cc_ref[...]
    tk1 = cc.shape[0] // 2
    for r in range(radix):
        pr = jnp.dot(cc, zr_ref[r], preferred_element_type=F32)
        pi = jnp.dot(cc, zi_ref[r], preferred_element_type=F32)
        a_re = pr[:tk1] - pi[tk1:]
        a_im = pr[tk1:] + pi[:tk1]
        for k2 in range(radix):
            term = pc_ref[k2][:, r:r + 1] * a_re - ps_ref[k2][:, r:r + 1] * a_im
            if r == 0:
                o_ref[k2] = term
            else:
                o_ref[k2] += term


def _time_fft(yp, cc, pc, ps, radix, tk1, tn):
    bsz, _, t1, two_d = yp.shape
    d = two_d // 2
    nj = d // tn
    return pl.pallas_call(
        functools.partial(_time_fft_kernel, radix=radix),
        grid=(bsz, nj, t1 // tk1),
        in_specs=[
            pl.BlockSpec((None, radix, t1, tn), lambda b, j, i: (b, 0, 0, j)),
            pl.BlockSpec((None, radix, t1, tn), lambda b, j, i: (b, 0, 0, nj + j)),
            pl.BlockSpec((None, 2 * tk1, t1), lambda b, j, i: (i, 0, 0)),
            pl.BlockSpec((radix, tk1, radix), lambda b, j, i: (0, i, 0)),
            pl.BlockSpec((radix, tk1, radix), lambda b, j, i: (0, i, 0)),
        ],
        out_specs=pl.BlockSpec((None, radix, tk1, tn), lambda b, j, i: (b, 0, i, j)),
        out_shape=jax.ShapeDtypeStruct((bsz, radix, t1, d), F32),
        compiler_params=_cparams(("parallel", "parallel", "arbitrary"), 48),
        name="fourier_positions_fft",
    )(yp, yp, cc, pc, ps)


def _fft_tables(n, radix, tk1):
    t1 = n // radix
    k1 = jnp.arange(t1, dtype=jnp.int32)
    ang = ((k1[:, None] * k1[None, :]) % t1).astype(F32) * (2.0 * math.pi / t1)
    c = jnp.cos(ang).reshape(t1 // tk1, tk1, t1)
    s = jnp.sin(ang).reshape(t1 // tk1, tk1, t1)
    cc = jnp.concatenate([c, s], axis=1).astype(BF16)
    k2 = jnp.arange(radix, dtype=jnp.int32)
    r = jnp.arange(radix, dtype=jnp.int32)
    k = k1[None, :, None] + t1 * k2[:, None, None]
    ph = ((k * r[None, None, :]) % n).astype(F32) * (2.0 * math.pi / n)
    scale = float(n) ** -0.5
    return cc, jnp.cos(ph) * scale, jnp.sin(ph) * scale


def _time_dft_kernel(ct_ref, st_ref, yc_ref, ys_ref, o_ref, acc, *, scale):
    kk = pl.program_id(2)

    @pl.when(kk == 0)
    def _():
        acc[...] = jnp.zeros_like(acc)

    acc[...] += (jnp.dot(ct_ref[...], yc_ref[...], preferred_element_type=F32)
                 - jnp.dot(st_ref[...], ys_ref[...], preferred_element_type=F32))

    @pl.when(kk == pl.num_programs(2) - 1)
    def _():
        o_ref[...] = acc[...] * scale


def _time_dft(y, ct, st, bsz, n_pos, row0, tmk):
    d = D_MODEL
    nt = n_pos // tmk
    rb0 = row0 // tmk
    return pl.pallas_call(
        functools.partial(_time_dft_kernel, scale=float(n_pos) ** -0.5),
        grid=(bsz, nt, nt),
        in_specs=[
            pl.BlockSpec((tmk, tmk), lambda b, i, k: (i, k)),
            pl.BlockSpec((tmk, tmk), lambda b, i, k: (i, k)),
            pl.BlockSpec((tmk, d), lambda b, i, k: (rb0 + b * nt + k, 0)),
            pl.BlockSpec((tmk, d), lambda b, i, k: (rb0 + b * nt + k, 1)),
        ],
        out_specs=pl.BlockSpec((tmk, d), lambda b, i, k: (b * nt + i, 0)),
        out_shape=jax.ShapeDtypeStruct((bsz * n_pos, d), F32),
        scratch_shapes=[pltpu.VMEM((tmk, d), F32)],
        compiler_params=_cparams(("parallel", "parallel", "arbitrary"), 48),
        name="fourier_positions",
    )(ct, st, y, y)


def _dft_tables(n):
    k = jnp.arange(n, dtype=jnp.int32)
    ang = ((k[:, None] * k[None, :]) % n).astype(F32) * (2.0 * math.pi / n)
    return jnp.cos(ang).astype(BF16), jnp.sin(ang).astype(BF16)


def _merge_kernel(x_ref, attn_l_ref, attn_c_ref, rec_ref, four_l_ref, four_c_ref, lg_ref, g0_ref, g1_ref, g2_ref,
                  wa_ref, wl_ref, wf_ref, wo_ref, gate_ref, n2_ref, sh_ref, sc_ref,
                  xo_ref, h_ref, hlo_ref, hhi_ref, *, lat_tiles):
    def mm(a, w_ref):
        return jnp.dot(a.astype(BF16), w_ref[...], preferred_element_type=F32)

    is_lat = pl.program_id(0) < lat_tiles
    attn_o = mm(jnp.where(is_lat, attn_l_ref[...], attn_c_ref[...]), wa_ref)
    rec_o = mm(rec_ref[...] * jax.nn.gelu(lg_ref[...].astype(F32), approximate=True), wl_ref)
    four_o = mm(jnp.where(is_lat, four_l_ref[...], four_c_ref[...]), wf_ref)
    mixed = (jax.nn.sigmoid(g0_ref[...].astype(F32)) * attn_o + jax.nn.sigmoid(g1_ref[...].astype(F32)) * rec_o
             + jax.nn.sigmoid(g2_ref[...].astype(F32)) * four_o)
    x = x_ref[...] + gate_ref[...] * mm(mixed, wo_ref)
    xo_ref[...] = x
    y = x * lax.rsqrt(jnp.mean(x * x, axis=-1, keepdims=True) + EPS)
    h = (y * n2_ref[...]) * (1.0 + sc_ref[...]) + sh_ref[...]
    h_ref[...] = h.astype(BF16)
    hlo_ref[...] = _pack_halves(h[:, :D_MODEL // 2])
    hhi_ref[...] = _pack_halves(h[:, D_MODEL // 2:])


def _merge(x_tok, attn_l, attn_c, rec, four_l, four_c, u, w_attn_o, w_lru_o, w_four_o, w_out, mod4, norm2_g,
           row_of_tile, n_rows):
    tm = TM
    d = D_MODEL
    lat_tiles = attn_l.shape[0] // tm
    ctx_tiles = attn_c.shape[0] // tm
    row = pl.BlockSpec((tm, d), lambda i: (i, 0))
    lat_row = pl.BlockSpec((tm, d), lambda i: (jnp.minimum(i, lat_tiles - 1), 0))
    ctx_row = pl.BlockSpec((tm, d), lambda i: (jnp.clip(i - lat_tiles, 0, ctx_tiles - 1), 0))
    wspec = pl.BlockSpec((d, d), lambda i: (0, 0))

    def ucol(cb):
        return pl.BlockSpec((tm, d), lambda i: (i, cb))

    return pl.pallas_call(
        functools.partial(_merge_kernel, lat_tiles=lat_tiles),
        grid=(n_rows // tm,),
        in_specs=[row, lat_row, ctx_row, row, lat_row, ctx_row, ucol(CB_LG), ucol(CB_G), ucol(CB_G + 1),
                  ucol(CB_G + 2), wspec, wspec, wspec, wspec,
                  _mod_spec(2, row_of_tile), pl.BlockSpec((1, d), lambda i: (0, 0)),
                  _mod_spec(3, row_of_tile), _mod_spec(4, row_of_tile)],
        out_specs=[row, row, pl.BlockSpec((tm, d // 4), lambda i: (i, 0)),
                   pl.BlockSpec((tm, d // 4), lambda i: (i, 0))],
        out_shape=[jax.ShapeDtypeStruct((n_rows, d), F32), jax.ShapeDtypeStruct((n_rows, d), BF16),
                   jax.ShapeDtypeStruct((n_rows, d // 4), jnp.uint32),
                   jax.ShapeDtypeStruct((n_rows, d // 4), jnp.uint32)],
        compiler_params=_cparams(("parallel",), 56),
        name="merge_branches",
    )(x_tok, attn_l, attn_c, rec, four_l, four_c, u, u, u, u, w_attn_o, w_lru_o, w_four_o, w_out,
      mod4, norm2_g.reshape(1, d), mod4, mod4)


def _route_kernel(h_ref, rw_ref, rb_ref, tri_ref, idx_ref, wgt_ref, pos_ref, cnt_ref, base_scr):
    tm = h_ref.shape[0]

    @pl.when(pl.program_id(0) == 0)
    def _():
        base_scr[...] = jnp.zeros_like(base_scr)

    logits = lax.dot_general(rw_ref[...], h_ref[...], (((1,), (1,)), ((), ())), preferred_element_type=F32)
    scores = jax.nn.sigmoid(logits)
    biased = scores + rb_ref[...]
    neg = -jnp.inf
    e_iota = lax.broadcasted_iota(jnp.int32, (N_EXPERTS, tm), 0)
    g_iota = lax.broadcasted_iota(jnp.int32, (N_GROUPS, tm), 0)

    rows = []
    for g in range(N_GROUPS):
        xg = biased[g * GROUP_SIZE:(g + 1) * GROUP_SIZE]
        m1 = jnp.max(xg, axis=0, keepdims=True)
        is_max = xg == m1
        n_max = jnp.sum(is_max.astype(F32), axis=0, keepdims=True)
        m2 = jnp.max(jnp.where(is_max, neg, xg), axis=0, keepdims=True)
        rows.append(m1 + jnp.where(n_max >= 2.0, m1, m2))
    grp = jnp.concatenate(rows, axis=0)

    g_sel = jnp.zeros((N_GROUPS, tm), jnp.bool_)
    work = grp
    for _ in range(TOPK_GROUPS):
        m = jnp.max(work, axis=0, keepdims=True)
        first = jnp.min(jnp.where(work == m, g_iota, N_GROUPS), axis=0, keepdims=True)
        hit = g_iota == first
        g_sel = jnp.logical_or(g_sel, hit)
        work = jnp.where(hit, neg, work)
    g_sel_f = g_sel.astype(F32)
    mask_rows = [jnp.broadcast_to(g_sel_f[g:g + 1], (GROUP_SIZE, tm)) for g in range(N_GROUPS)]
    e_mask = jnp.concatenate(mask_rows, axis=0) > 0.5

    work = jnp.where(e_mask, biased, neg)
    sel = jnp.zeros((N_EXPERTS, tm), F32)
    idx_rows, w_rows, hits = [], [], []
    for _ in range(TOP_K):
        m = jnp.max(work, axis=0, keepdims=True)
        first = jnp.min(jnp.where(work == m, e_iota, N_EXPERTS), axis=0, keepdims=True)
        hit = e_iota == first
        hits.append(hit)
        idx_rows.append(first)
        w_rows.append(jnp.sum(jnp.where(hit, scores, 0.0), axis=0, keepdims=True))
        sel = jnp.where(hit, 1.0, sel)
        work = jnp.where(hit, neg, work)
    w = jnp.concatenate(w_rows, axis=0)
    wgt_ref[...] = w / jnp.sum(w, axis=0, keepdims=True) * ROUTED_SCALE
    idx_ref[...] = jnp.concatenate(idx_rows, axis=0)

    before = jnp.dot(sel.astype(BF16), tri_ref[...], preferred_element_type=F32)
    rank = base_scr[...] + before
    pos_rows = [jnp.sum(jnp.where(hit, rank, 0.0), axis=0, keepdims=True) for hit in hits]
    pos_ref[...] = jnp.concatenate(pos_rows, axis=0).astype(jnp.int32)
    base_scr[...] = base_scr[...] + jnp.sum(sel, axis=1, keepdims=True)
    cnt_ref[...] = jnp.broadcast_to(base_scr[...], cnt_ref.shape)


def _route(h16, rw_t, router_b, tm):
    n_tok, d = h16.shape
    tri = (jnp.arange(tm)[:, None] < jnp.arange(tm)[None, :]).astype(BF16)
    out_col = pl.BlockSpec((TOP_K, tm), lambda i: (0, i))
    return pl.pallas_call(
        _route_kernel,
        grid=(n_tok // tm,),
        in_specs=[pl.BlockSpec((tm, d), lambda i: (i, 0)),
                  pl.BlockSpec((N_EXPERTS, d), lambda i: (0, 0)),
                  pl.BlockSpec((N_EXPERTS, 1), lambda i: (0, 0)),
                  pl.BlockSpec((tm, tm), lambda i: (0, 0))],
        out_specs=[out_col, out_col, out_col, pl.BlockSpec((N_EXPERTS, LANES), lambda i: (0, 0))],
        out_shape=[jax.ShapeDtypeStruct((TOP_K, n_tok), jnp.int32), jax.ShapeDtypeStruct((TOP_K, n_tok), F32),
                   jax.ShapeDtypeStruct((TOP_K, n_tok), jnp.int32),
                   jax.ShapeDtypeStruct((N_EXPERTS, LANES), F32)],
        scratch_shapes=[pltpu.VMEM((N_EXPERTS, 1), F32)],
        compiler_params=_cparams(("arbitrary",), 32),
        name="moe_router",
    )(h16, rw_t, router_b.reshape(N_EXPERTS, 1), tri)


def _slot_kernel(idx_ref, pos_ref, offs_ref, dest_ref):
    tm = idx_ref.shape[1]
    e_iota = lax.broadcasted_iota(jnp.int32, (N_EXPERTS, tm), 0)
    idx = idx_ref[...]
    offs = offs_ref[...]
    rows = [jnp.sum(jnp.where(e_iota == idx[r:r + 1], offs, 0), axis=0, keepdims=True) for r in range(TOP_K)]
    dest_ref[...] = jnp.concatenate(rows, axis=0) + pos_ref[...]


def _slots(top_idx, pos, offs, tm):
    n_tok = top_idx.shape[1]
    col = pl.BlockSpec((TOP_K, tm), lambda i: (0, i))
    return pl.pallas_call(
        _slot_kernel,
        grid=(n_tok // tm,),
        in_specs=[col, col, pl.BlockSpec((N_EXPERTS, 1), lambda i: (0, 0))],
        out_specs=col,
        out_shape=jax.ShapeDtypeStruct((TOP_K, n_tok), jnp.int32),
        compiler_params=_cparams(("parallel",), 32),
        name="moe_slots",
    )(top_idx, pos, offs.reshape(N_EXPERTS, 1))


def _sc_scatter_rows(src, idx, n_out):
    n_src, width = src.shape
    n_idx = idx.shape[0]
    n_win = n_src // SC_WINDOW
    mesh = plsc.VectorSubcoreMesh(core_axis_name="core", subcore_axis_name="subcore")

    @pl.kernel(out_type=jax.ShapeDtypeStruct((n_out, width), src.dtype), mesh=mesh, scratch_types=[],
               name="moe_scatter_sc")
    def scatter(src_hbm, idx_hbm, out_hbm):
        def body(src_vmem, idx_vmem):
            pltpu.sync_copy(src_vmem, out_hbm.at[idx_vmem.at[0]])

        pltpu.emit_pipeline(
            body,
            grid=(n_idx // SC_WINDOW,),
            in_specs=[pl.BlockSpec((SC_WINDOW, width), index_map=lambda g: (g % n_win, 0)),
                      pl.BlockSpec((1, SC_WINDOW), index_map=lambda g: (0, g))],
            out_specs=[],
            core_axis_name=("core", "subcore"),
            dimension_semantics=(pltpu.PARALLEL,),
        )(src_hbm, idx_hbm)

    return scatter(src, idx.reshape(1, n_idx))


def _expert_kernel(be_ref, nu_ref, nv_ref, xlo_ref, xhi_ref, wg_ref, wu_ref, wd_ref, ylo_ref, yhi_ref,
                   wg_s, wu_s, wd_s):
    i = pl.program_id(0)
    used = i < nu_ref[0]
    prev_e = be_ref[jnp.maximum(i - 1, 0)]
    fresh = jnp.logical_or(i == 0, be_ref[i] != prev_e)

    @pl.when(jnp.logical_and(used, fresh))
    def _():
        wg_s[...] = wg_ref[0].astype(BF16)
        wu_s[...] = wu_ref[0].astype(BF16)
        wd_s[...] = wd_ref[0].astype(BF16)

    @pl.when(used)
    def _():
        row = lax.broadcasted_iota(jnp.int32, xlo_ref.shape, 0)
        live = row < nv_ref[i]
        half = D_MODEL // 2
        quarter = D_MODEL // 4
        parts = []
        for ref in (xlo_ref, xhi_ref):
            parts.extend(p.astype(BF16) for p in _unpack_halves(jnp.where(live, ref[...], jnp.uint32(0))))

        def proj(w_s):
            acc = jnp.dot(parts[0], w_s[:quarter, :], preferred_element_type=F32)
            for k in range(1, 4):
                acc = acc + jnp.dot(parts[k], w_s[k * quarter:(k + 1) * quarter, :], preferred_element_type=F32)
            return acc

        g = proj(wg_s)
        act = (_silu(g) * proj(wu_s)).astype(BF16)
        y = jnp.dot(act, wd_s[...], preferred_element_type=F32)
        ylo_ref[...] = _pack_halves(y[:, :half])
        yhi_ref[...] = _pack_halves(y[:, half:])


def _experts(xs_lo, xs_hi, block_e, n_used, n_valid, w_gate, w_up, w_down, layer, bm):
    n_slots, quarter = xs_lo.shape
    half = 2 * quarter
    d = D_MODEL

    def blk(i, be, nu, nv):
        return (jnp.minimum(i, nu[0] - 1), 0)

    def wsel(i, be, nu, nv):
        return (layer, be[i], 0, 0)

    grid_spec = pltpu.PrefetchScalarGridSpec(
        num_scalar_prefetch=3,
        grid=(n_slots // bm,),
        in_specs=[pl.BlockSpec((bm, quarter), blk),
                  pl.BlockSpec((bm, quarter), blk),
                  pl.BlockSpec((None, 1, d, EXPERT_FF), wsel),
                  pl.BlockSpec((None, 1, d, EXPERT_FF), wsel),
                  pl.BlockSpec((None, 1, EXPERT_FF, d), wsel)],
        out_specs=[pl.BlockSpec((bm, half // 2), blk), pl.BlockSpec((bm, half // 2), blk)],
        scratch_shapes=[pltpu.VMEM((d, EXPERT_FF), BF16), pltpu.VMEM((d, EXPERT_FF), BF16),
                        pltpu.VMEM((EXPERT_FF, d), BF16)],
    )
    y_shape = jax.ShapeDtypeStruct((n_slots, half // 2), jnp.uint32)
    return pl.pallas_call(
        _expert_kernel,
        grid_spec=grid_spec,
        out_shape=[y_shape, y_shape],
        compiler_params=_cparams(("arbitrary",), 40),
        name="moe_experts",
    )(block_e, n_used, n_valid, xs_lo, xs_hi, w_gate, w_up, w_down)


def _sc_gather_rows(table, idx):
    n_idx = idx.shape[0]
    width = table.shape[1]
    mesh = plsc.VectorSubcoreMesh(core_axis_name="core", subcore_axis_name="subcore")

    @pl.kernel(out_type=jax.ShapeDtypeStruct((n_idx, width), table.dtype), mesh=mesh, scratch_types=[],
               name="moe_gather_sc")
    def gather(table_hbm, idx_hbm, out_hbm):
        def body(idx_vmem, out_vmem):
            pltpu.sync_copy(table_hbm.at[idx_vmem.at[0]], out_vmem)

        pltpu.emit_pipeline(
            body,
            grid=(n_idx // SC_WINDOW,),
            in_specs=[pl.BlockSpec((1, SC_WINDOW), index_map=lambda i: (0, i))],
            out_specs=[pl.BlockSpec((SC_WINDOW, width), index_map=lambda i: (i, 0))],
            core_axis_name=("core", "subcore"),
            dimension_semantics=(pltpu.PARALLEL,),
        )(idx_hbm, out_hbm)

    return gather(table, idx.reshape(1, n_idx))


def _combine_kernel(x_ref, h_ref, tw_ref, gate_ref, wsg_ref, wsu_ref, wsd_ref, lo_ref, hi_ref, o_ref):
    quarter = D_MODEL // 4
    h = h_ref[...]
    act = (_silu(jnp.dot(h, wsg_ref[...], preferred_element_type=F32))
           * jnp.dot(h, wsu_ref[...], preferred_element_type=F32)).astype(BF16)
    shared = jnp.dot(act, wsd_ref[...], preferred_element_type=F32)

    tw = tw_ref[...]
    gate = gate_ref[...]
    for q, buf in enumerate((lo_ref, hi_ref)):
        moe_a = moe_b = None
        for r in range(TOP_K):
            ya, yb = _unpack_halves(buf[r])
            wr = tw[:, r:r + 1]
            moe_a = ya * wr if moe_a is None else moe_a + ya * wr
            moe_b = yb * wr if moe_b is None else moe_b + yb * wr
        for part, moe in ((2 * q, moe_a), (2 * q + 1, moe_b)):
            cols = slice(part * quarter, (part + 1) * quarter)
            o_ref[:, cols] = x_ref[:, cols] + gate[:, cols] * (moe + shared[:, cols])


def _combine(x_tok, h16, g_lo, g_hi, top_w, mod4, ws_gate, ws_up, ws_down, row_of_tile):
    n_tok, d = x_tok.shape
    tm = TM
    ff = ws_gate.shape[1]
    row = pl.BlockSpec((tm, d), lambda i: (i, 0))
    gat = pl.BlockSpec((TOP_K, tm, d // 4), lambda i: (0, i, 0))
    return pl.pallas_call(
        _combine_kernel,
        grid=(n_tok // tm,),
        in_specs=[row, row,
                  pl.BlockSpec((tm, TOP_K), lambda i: (i, 0)),
                  _mod_spec(5, row_of_tile),
                  pl.BlockSpec((d, ff), lambda i: (0, 0)),
                  pl.BlockSpec((d, ff), lambda i: (0, 0)),
                  pl.BlockSpec((ff, d), lambda i: (0, 0)),
                  gat, gat],
        out_specs=row,
        out_shape=jax.ShapeDtypeStruct((n_tok, d), F32),
        compiler_params=_cparams(("parallel",), 48),
        name="moe_combine",
    )(x_tok, h16, top_w, mod4, ws_gate, ws_up, ws_down, g_lo, g_hi)


def _moe(x_tok, h16, h_lo, h_hi, mod4, row_of_tile, rw_t, router_b, w_gate, w_up, w_down, layer, ws_gate, ws_up, ws_down):
    n_tok = x_tok.shape[0]
    bm = MOE_BM
    tm_r = TM_ROUTE if n_tok % TM_ROUTE == 0 else TM
    top_idx, top_w, pos, cnt = _route(h16, rw_t, router_b, tm_r)

    counts = cnt[:, 0].astype(jnp.int32)
    padded = (counts + bm - 1) // bm * bm
    ends = jnp.cumsum(padded)
    n_blocks = -(-(n_tok * TOP_K + N_EXPERTS * (bm - 1)) // bm)
    offs = ends - padded
    dest = _slots(top_idx, pos, offs, tm_r)
    block_start = jnp.arange(n_blocks, dtype=jnp.int32) * bm
    block_e = jnp.minimum(jnp.sum((ends[None, :] <= block_start[:, None]).astype(jnp.int32), axis=1),
                          N_EXPERTS - 1)
    n_valid = jnp.clip(counts[block_e] - (block_start - offs[block_e]), 0, bm).astype(jnp.int32)
    n_used = (ends[-1] // bm).astype(jnp.int32).reshape(1)
    dest_flat = dest.reshape(TOP_K * n_tok)
    xs_lo = _sc_scatter_rows(h_lo, dest_flat, n_blocks * bm)
    xs_hi = _sc_scatter_rows(h_hi, dest_flat, n_blocks * bm)
    ys_lo, ys_hi = _experts(xs_lo, xs_hi, block_e, n_used, n_valid, w_gate, w_up, w_down, layer, bm)
    g_lo = _sc_gather_rows(ys_lo, dest_flat).reshape(TOP_K, n_tok, D_MODEL // 4)
    g_hi = _sc_gather_rows(ys_hi, dest_flat).reshape(TOP_K, n_tok, D_MODEL // 4)
    return _combine(x_tok, h16, g_lo, g_hi, top_w.T, mod4, ws_gate, ws_up, ws_down, row_of_tile)


def _block_diag_tiles(w):
    per = MXU_DIM // LRU_BLOCK_DIM
    w = w.reshape(2, D_MODEL // MXU_DIM, per, LRU_BLOCK_DIM, LRU_BLOCK_DIM)
    eye = jnp.eye(per, dtype=w.dtype)
    t = jnp.einsum('ztpde,pq->ztpdqe', w, eye)
    return t.reshape(2, D_MODEL // MXU_DIM, MXU_DIM, MXU_DIM).astype(BF16)


def _rope_tables(n_lat, n_ctx):
    rows = n_lat // GRID_W
    row = jnp.repeat(jnp.arange(rows), GRID_W)
    col = jnp.tile(jnp.arange(GRID_W), rows)
    inv = ROPE_THETA ** (-jnp.arange(ROPE_FREQS, dtype=F32) / ROPE_FREQS)
    ang = jnp.stack([row[:, None] * inv, col[:, None] * inv], axis=1)
    ang = jnp.concatenate([ang, ang], axis=2).reshape(n_lat, HEAD_DIM)
    ang = jnp.tile(ang, (1, LANES // HEAD_DIM))
    ang = jnp.concatenate([ang, jnp.zeros((n_ctx, LANES), F32)], axis=0)
    lane = jnp.arange(LANES)
    sign = jnp.where((lane % (2 * ROPE_FREQS)) < ROPE_FREQS, -1.0, 1.0).astype(F32)
    return jnp.cos(ang), jnp.sin(ang) * sign


def kernel(x, c, ctx, c_ctx, w_mod, b_mod, norm1_g, w_in, q_norm_g, k_norm_g, lambda_qk, subln_g, w_attn_o, conv_w, conv_b, lru_wa, lru_ba, lru_wi, lru_bi, lru_lambda, w_lru_o, w_four_o, w_out, norm2_g, router_w, router_b, exp_w_gate, exp_w_up, exp_w_down, sh_w_gate, sh_w_up, sh_w_down):
    bsz, n_lat, d = x.shape
    n_ctx = ctx.shape[1]
    depth = w_mod.shape[0]
    n_latent_rows = bsz * n_lat
    n_tok = n_latent_rows + bsz * n_ctx
    assert d == D_MODEL and bsz + 1 <= MOD_ROWS
    assert n_lat % TM == 0 and n_ctx % TM == 0 and n_lat % GRID_W == 0
    tm_in = TM_IN if (n_lat % TM_IN == 0 and (bsz * n_ctx) % TM_IN == 0) else TM

    cvec = jnp.zeros((MOD_ROWS, d), F32).at[:bsz].set(c).at[bsz].set(c_ctx)
    mods = _modulation(cvec, w_mod, b_mod)

    def row_of_tile_fn(tm):
        per_batch = n_lat // tm
        return lambda i: jnp.minimum(i // per_batch, bsz)

    assert n_lat % n_ctx == 0
    cos_t, sin_t = _rope_tables(n_lat, n_ctx)
    group_mat = jnp.kron(jnp.eye(d // HEAD_DIM, dtype=F32), jnp.ones((HEAD_DIM, HEAD_DIM), F32)).astype(BF16)
    m = jnp.arange(FOUR_GROUP_DIM, dtype=jnp.int32)
    ang_c = ((m[:, None] * m[None, :]) % FOUR_GROUP_DIM).astype(F32) * (2.0 * math.pi / FOUR_GROUP_DIM)
    inv_sqrt_c = FOUR_GROUP_DIM ** -0.5
    cs_mat = jnp.concatenate([jnp.cos(ang_c) * inv_sqrt_c, jnp.sin(ang_c) * inv_sqrt_c], axis=1).astype(BF16)
    assert n_lat % (FFT_RADIX * MXU_DIM) == 0 or n_lat // FFT_RADIX < MXU_DIM
    fft_tk1 = min(MXU_DIM, n_lat // FFT_RADIX)
    fft_cc, fft_pc, fft_ps = _fft_tables(n_lat, FFT_RADIX, fft_tk1)
    ct_ctx, st_ctx = _dft_tables(n_ctx)
    q_sub = math.gcd(Q_SUB, n_lat // TQ)
    n_keys = n_ctx + n_lat
    tk = TK if n_keys % TK == 0 else TM

    x_tok = jnp.concatenate([x.reshape(n_latent_rows, d), ctx.reshape(bsz * n_ctx, d)], axis=0)
    for l in range(depth):
        last = l == depth - 1
        lam_init = 0.8 - 0.6 * math.exp(-0.3 * l)
        mod4 = mods[l].reshape(MOD_ROWS, 6, 1, d)

        u = _in_projection(x_tok, mod4, norm1_g[l], w_in[l].astype(BF16), row_of_tile_fn(tm_in), tm_in)

        k_all, q_all, vt_all = _qkv_prepare(u, k_norm_g[l], q_norm_g[l], group_mat, cos_t, sin_t,
                                            bsz, n_lat, n_ctx)
        attn_l = _diff_attention(q_all, k_all, vt_all, lambda_qk[l], subln_g[l], lam_init,
                                 0, n_lat, 0, n_keys, TQ, q_sub, tk)
        attn_c = attn_l
        if not last:
            attn_c = _diff_attention(q_all, k_all, vt_all, lambda_qk[l], subln_g[l], lam_init,
                                     n_lat, n_ctx, n_lat, n_ctx, TM, 1, TM)

        rec = _rglru(u, conv_w[l], conv_b[l], _block_diag_tiles(lru_wa[l]), lru_ba[l],
                     _block_diag_tiles(lru_wi[l]), lru_bi[l], lru_lambda[l], bsz, n_lat, n_ctx)

        y_lat = _chan_dft_split(u, cs_mat, bsz, n_lat, FFT_RADIX)
        four_l = _time_fft(y_lat, fft_cc, fft_pc, fft_ps, FFT_RADIX, fft_tk1, MXU_DIM).reshape(n_latent_rows, d)
        four_c = four_l
        if not last:
            y_ctx = _chan_dft(u, cs_mat, n_latent_rows, bsz * n_ctx)
            four_c = _time_dft(y_ctx, ct_ctx, st_ctx, bsz, n_ctx, 0, n_ctx)

        n_rows = n_latent_rows if last else n_tok
        row_of_tile = row_of_tile_fn(TM)
        x_mid, h16, h_lo, h_hi = _merge(x_tok, attn_l, attn_c, rec, four_l, four_c, u, w_attn_o[l].astype(BF16),
                                w_lru_o[l].astype(BF16), w_four_o[l].astype(BF16), w_out[l].astype(BF16),
                                mod4, norm2_g[l], row_of_tile, n_rows)
        x_tok = _moe(x_mid, h16, h_lo, h_hi, mod4, row_of_tile, router_w[l].T.astype(BF16), router_b[l],
                     exp_w_gate, exp_w_up, exp_w_down, l,
                     sh_w_gate[l].astype(BF16), sh_w_up[l].astype(BF16), sh_w_down[l].astype(BF16))
    return x_tok[:n_latent_rows].reshape(bsz, n_lat, d)
```

```python
import functools
import math

import jax
import jax.numpy as jnp
from jax import lax
from jax.experimental import pallas as pl
from jax.experimental.pallas import tpu as pltpu
from jax.experimental.pallas import tpu_sc as plsc

F32 = jnp.float32
BF16 = jnp.bfloat16

D_MODEL = 1024
EPS = 1e-6
GRID_W = 64

N_HEADS = 8
HEAD_DIM = 64
V_DIM = 2 * HEAD_DIM
ATTN_SCALE = HEAD_DIM ** -0.5
ROPE_THETA = 10000.0
ROPE_FREQS = HEAD_DIM // 4

LRU_BLOCKS = 16
LRU_BLOCK_DIM = D_MODEL // LRU_BLOCKS
LRU_C = 8.0
CONV_W = 4

FOUR_GROUPS = 4
FOUR_GROUP_DIM = D_MODEL // FOUR_GROUPS

CB_K, CB_V, CB_LX, CB_Q, CB_LG, CB_F, CB_G = 0, 1, 2, 3, 4, 5, 6
IN_BLOCKS = 9

N_EXPERTS = 256
TOP_K = 8
N_GROUPS = 8
GROUP_SIZE = N_EXPERTS // N_GROUPS
TOPK_GROUPS = 4
EXPERT_FF = 256
ROUTED_SCALE = 2.5

LANES = 128
SUBLANES = 8
MXU_DIM = 256
VMEM_BYTES_V7X = 64 * 1024 * 1024

MOD_ROWS = 8
TM = 256
TM_IN = 512
IN_PANEL = 3
TQ = 256
Q_SUB = 32
TK = 768
TM_ROUTE = 512
MOE_BM = 512
SC_WINDOW = 128
FFT_RADIX = 8
LOG2E = 1.4426950408889634


def _cparams(sem, vmem_mb):
    return pltpu.CompilerParams(dimension_semantics=sem, vmem_limit_bytes=int(vmem_mb * 1024 * 1024))


def _silu(x):
    return x * jax.nn.sigmoid(x)


def _pack_halves(x):
    half = x.shape[1] // 2
    bits = pltpu.bitcast(x.astype(BF16).astype(F32), jnp.uint32)
    return (bits[:, :half] & jnp.uint32(0xFFFF0000)) | (bits[:, half:] >> 16)


def _unpack_halves(w):
    return pltpu.bitcast(w & jnp.uint32(0xFFFF0000), F32), pltpu.bitcast(w << 16, F32)


def _mod_kernel(c_ref, w_ref, b_ref, o_ref):
    c = c_ref[...]
    s = _silu(c).astype(BF16)
    o_ref[0] = jnp.dot(s, w_ref[0].astype(BF16), preferred_element_type=F32) + b_ref[0]


def _modulation(cvec, w_mod, b_mod):
    n_layers, d, six_d = w_mod.shape
    nj = six_d // d
    return pl.pallas_call(
        _mod_kernel,
        grid=(n_layers, nj),
        in_specs=[
            pl.BlockSpec((MOD_ROWS, d), lambda l, j: (0, 0)),
            pl.BlockSpec((1, d, d), lambda l, j: (l, 0, j)),
            pl.BlockSpec((1, 1, d), lambda l, j: (l, 0, j)),
        ],
        out_specs=pl.BlockSpec((1, MOD_ROWS, d), lambda l, j: (l, 0, j)),
        out_shape=jax.ShapeDtypeStruct((n_layers, MOD_ROWS, six_d), F32),
        compiler_params=_cparams(("parallel", "parallel"), 32),
        name="modulation",
    )(cvec, w_mod, b_mod.reshape(n_layers, 1, six_d))


def _mod_spec(part, row_of_tile):
    return pl.BlockSpec((None, None, 1, D_MODEL), lambda i: (row_of_tile(i), part, 0, 0))


def _inproj_kernel(x_ref, g_ref, sh_ref, sc_ref, w_ref, o_ref):
    x = x_ref[...]
    y = x * lax.rsqrt(jnp.mean(x * x, axis=-1, keepdims=True) + EPS)
    h = (y * g_ref[...]) * (1.0 + sc_ref[...]) + sh_ref[...]
    o_ref[...] = jnp.dot(h.astype(BF16), w_ref[...], preferred_element_type=F32).astype(o_ref.dtype)


def _in_projection(x_tok, mod4, norm_g, w_in_bf16, row_of_tile, tm):
    n_tok, d = x_tok.shape
    n_cols = w_in_bf16.shape[1]
    tn = IN_PANEL * d
    row_of = lambda j, i: row_of_tile(i)
    return pl.pallas_call(
        _inproj_kernel,
        grid=(n_cols // tn, n_tok // tm),
        in_specs=[
            pl.BlockSpec((tm, d), lambda j, i: (i, 0)),
            pl.BlockSpec((1, d), lambda j, i: (0, 0)),
            pl.BlockSpec((None, None, 1, d), lambda j, i: (row_of(j, i), 0, 0, 0)),
            pl.BlockSpec((None, None, 1, d), lambda j, i: (row_of(j, i), 1, 0, 0)),
            pl.BlockSpec((d, tn), lambda j, i: (0, j)),
        ],
        out_specs=pl.BlockSpec((tm, tn), lambda j, i: (i, j)),
        out_shape=jax.ShapeDtypeStruct((n_tok, n_cols), BF16),
        compiler_params=_cparams(("parallel", "parallel"), 48),
        name="in_projection",
    )(x_tok, norm_g.reshape(1, d), mod4, mod4, w_in_bf16)


def _prep_kernel(uk_ref, uq_ref, uv_ref, kg_ref, qg_ref, gm_ref, cos_ref, sin_ref, k_out, q_out, vt_out, *, tm):
    gm = gm_ref[...]
    lane = lax.broadcasted_iota(jnp.int32, (tm, LANES), 1)
    first_half = (lane % (2 * ROPE_FREQS)) < ROPE_FREQS

    def norm_rope(x, g):
        ss = jnp.dot((x * x).astype(BF16), gm, preferred_element_type=F32) * (1.0 / HEAD_DIM)
        y = x * lax.rsqrt(ss + EPS) * g
        c = cos_ref[...]
        s = sin_ref[...]
        outs = []
        for j in range(D_MODEL // LANES):
            yt = y[:, j * LANES:(j + 1) * LANES]
            rot = jnp.where(first_half, pltpu.roll(yt, LANES - ROPE_FREQS, 1), pltpu.roll(yt, ROPE_FREQS, 1))
            outs.append(yt * c + rot * s)
        return jnp.concatenate(outs, axis=1)

    k_out[...] = norm_rope(uk_ref[...].astype(F32), kg_ref[...]).astype(BF16)
    qn = norm_rope(uq_ref[...].astype(F32), qg_ref[...]) * (ATTN_SCALE * LOG2E)
    lo = lane < HEAD_DIM
    for h in range(N_HEADS):
        qt = qn[:, h * LANES:(h + 1) * LANES]
        q_out[:, (2 * h) * LANES:(2 * h + 1) * LANES] = jnp.where(lo, qt, 0.0).astype(BF16)
        q_out[:, (2 * h + 1) * LANES:(2 * h + 2) * LANES] = jnp.where(lo, 0.0, qt).astype(BF16)
    vt_out[...] = uv_ref[...].astype(F32).T.astype(BF16)


def _qkv_prepare(u, k_gain, q_gain, group_mat, cos_t, sin_t, bsz, n_lat, n_ctx):
    tm = TM
    d = D_MODEL
    n_s, n_c = n_lat // tm, n_ctx // tm
    n_rows = n_lat + n_ctx
    lat_tiles = bsz * n_s
    kg = jnp.tile(k_gain, d // HEAD_DIM).reshape(1, d)
    qg = jnp.tile(q_gain, d // HEAD_DIM).reshape(1, d)

    def tok_tile(b, j):
        return jnp.where(j < n_s, b * n_s + j, lat_tiles + b * n_c + (j - n_s))

    def u_spec(cb):
        return pl.BlockSpec((tm, d), lambda b, j: (tok_tile(b, j), cb))

    vec = pl.BlockSpec((1, d), lambda b, j: (0, 0))
    rope_spec = pl.BlockSpec((tm, LANES), lambda b, j: (j, 0))
    return pl.pallas_call(
        functools.partial(_prep_kernel, tm=tm),
        grid=(bsz, n_s + n_c),
        in_specs=[u_spec(CB_K), u_spec(CB_Q), u_spec(CB_V), vec, vec,
                  pl.BlockSpec((d, d), lambda b, j: (0, 0)), rope_spec, rope_spec],
        out_specs=[
            pl.BlockSpec((None, tm, d), lambda b, j: (b, j, 0)),
            pl.BlockSpec((None, tm, 2 * d), lambda b, j: (b, j, 0)),
            pl.BlockSpec((None, d, tm), lambda b, j: (b, 0, j)),
        ],
        out_shape=[jax.ShapeDtypeStruct((bsz, n_rows, d), BF16),
                   jax.ShapeDtypeStruct((bsz, n_rows, 2 * d), BF16),
                   jax.ShapeDtypeStruct((bsz, d, n_rows), BF16)],
        compiler_params=_cparams(("parallel", "parallel"), 40),
        name="qkv_prepare",
    )(u, u, u, kg, qg, group_mat, cos_t, sin_t)


def _attn_kernel(lq_ref, sg_ref, q1_ref, q2_ref, k_ref, vt_ref, o_ref, s_buf, *, n_keys, tk, tq, lam_init):
    lq = lq_ref[...]
    lam = (jnp.exp(jnp.sum(lq[0:1] * lq[1:2], axis=1, keepdims=True))
           - jnp.exp(jnp.sum(lq[2:3] * lq[3:4], axis=1, keepdims=True)) + lam_init)
    n_chunks = n_keys // tk
    n_sub = q1_ref.shape[0] // tq

    def load_q(j):
        rows = pl.ds(pl.multiple_of(j * tq, tq), tq)
        return jnp.concatenate([q1_ref[rows, :], q2_ref[rows, :]], axis=0)

    def score_chunk(q, c, mx):
        s = lax.dot_general(k_ref[c * tk:(c + 1) * tk, :], q, (((1,), (1,)), ((), ())),
                            preferred_element_type=F32)
        s_buf[c * tk:(c + 1) * tk, :] = s
        cm = jnp.max(s, axis=0, keepdims=True)
        return cm if mx is None else jnp.maximum(mx, cm)

    q0 = load_q(0)
    mx0 = None
    for c in range(n_chunks):
        mx0 = score_chunk(q0, c, mx0)

    def sub_tile(j, mx):
        q_next = load_q(jnp.minimum(j + 1, n_sub - 1))
        mx_next = None
        l = jnp.zeros((1, 2 * tq), F32)
        acc = jnp.zeros((V_DIM, 2 * tq), F32)
        for c in range(n_chunks):
            p = jnp.exp2(s_buf[c * tk:(c + 1) * tk, :] - mx)
            l = l + jnp.sum(p, axis=0, keepdims=True)
            acc = acc + jnp.dot(vt_ref[:, c * tk:(c + 1) * tk], p.astype(BF16), preferred_element_type=F32)
            mx_next = score_chunk(q_next, c, mx_next)
        o = acc / l
        o = o[:, :tq] - lam * o[:, tq:]
        ms = jnp.mean(o * o, axis=0, keepdims=True)
        on = (o * lax.rsqrt(ms + EPS) * sg_ref[...]) * (1.0 - lam_init)
        o_ref[pl.ds(pl.multiple_of(j * tq, tq), tq), :] = on.T.astype(o_ref.dtype)
        return mx_next

    lax.fori_loop(0, n_sub, sub_tile, mx0)


def _diff_attention(q_all, k_all, vt_all, lambda_qk, subln_g, lam_init, q_row0, n_q, key_row0, n_keys, tq, n_sub, tk):
    bsz = q_all.shape[0]
    d = D_MODEL
    tstep = tq * n_sub
    nq = n_q // tstep
    qb0 = q_row0 // tstep
    kb0 = key_row0 // n_keys
    s_bytes = n_keys * 2 * tq * 4
    return pl.pallas_call(
        functools.partial(_attn_kernel, n_keys=n_keys, tk=tk, tq=tq, lam_init=lam_init),
        grid=(bsz, N_HEADS, nq),
        in_specs=[
            pl.BlockSpec((4, HEAD_DIM), lambda b, h, i: (0, 0)),
            pl.BlockSpec((V_DIM, 1), lambda b, h, i: (0, 0)),
            pl.BlockSpec((None, tstep, LANES), lambda b, h, i: (b, qb0 + i, 2 * h)),
            pl.BlockSpec((None, tstep, LANES), lambda b, h, i: (b, qb0 + i, 2 * h + 1)),
            pl.BlockSpec((None, n_keys, LANES), lambda b, h, i: (b, kb0, h)),
            pl.BlockSpec((None, V_DIM, n_keys), lambda b, h, i: (b, h, kb0)),
        ],
        out_specs=pl.BlockSpec((tstep, LANES), lambda b, h, i: (b * nq + i, h)),
        out_shape=jax.ShapeDtypeStruct((bsz * n_q, d), BF16),
        scratch_shapes=[pltpu.VMEM((n_keys, 2 * tq), F32)],
        compiler_params=_cparams(("parallel", "parallel", "arbitrary"), 28 + s_bytes / 2 ** 20),
        name="diff_attention",
    )(lambda_qk, subln_g.reshape(V_DIM, 1), q_all, q_all, k_all, vt_all)


def _lru_kernel(*refs, rev, n_c, n_s, tm):
    if rev:
        (cur_ref, prev_ref, next_ref, cw_ref, cb_ref, wa_ref, ba_ref, wi_ref, bi_ref, lam_ref, fwd_ref,
         o_ref, a_scr, b_scr, h_scr) = refs
    else:
        (cur_ref, prev_ref, next_ref, cw_ref, cb_ref, wa_ref, ba_ref, wi_ref, bi_ref, lam_ref,
         o_ref, a_scr, b_scr, h_scr) = refs
    s = pl.program_id(1)
    is_ctx = s < n_c
    if rev:
        tile = jnp.where(is_ctx, n_c - 1 - s, n_s - 1 - (s - n_c))
    else:
        tile = jnp.where(is_ctx, s, s - n_c)
    n_tile = jnp.where(is_ctx, n_c, n_s)

    @pl.when(s == 0)
    def _():
        h_scr[...] = jnp.zeros_like(h_scr)

    x = cur_ref[...].astype(F32)
    pv = jnp.where(tile == 0, 0.0, prev_ref[...].astype(F32)[SUBLANES:])
    nx = jnp.where(tile == n_tile - 1, 0.0, next_ref[...].astype(F32)[:SUBLANES])
    r8 = lax.broadcasted_iota(jnp.int32, (SUBLANES, D_MODEL), 0)

    def earlier(k):
        rolled = pltpu.roll(x, k, 0)
        top = jnp.where(r8 < k, pltpu.roll(pv, k, 0), rolled[:SUBLANES])
        return jnp.concatenate([top, rolled[SUBLANES:]], axis=0)

    rolled = pltpu.roll(x, tm - 1, 0)
    bot = jnp.where(r8 >= SUBLANES - 1, pltpu.roll(nx, SUBLANES - 1, 0), rolled[tm - SUBLANES:])
    later = jnp.concatenate([rolled[:tm - SUBLANES], bot], axis=0)
    cw = cw_ref[...]
    conv = earlier(2) * cw[0:1] + earlier(1) * cw[1:2] + x * cw[2:3] + later * cw[3:4] + cb_ref[...]

    conv16 = conv.astype(BF16)

    def block_diag(w_ref, b_ref):
        parts = [jnp.dot(conv16[:, j * MXU_DIM:(j + 1) * MXU_DIM], w_ref[j], preferred_element_type=F32)
                 for j in range(D_MODEL // MXU_DIM)]
        return jnp.concatenate(parts, axis=1) + b_ref[...]

    r = jax.nn.sigmoid(block_diag(wa_ref, ba_ref))
    gate_i = jax.nn.sigmoid(block_diag(wi_ref, bi_ref))
    neg_lam = -lam_ref[...]
    softplus = jnp.maximum(neg_lam, 0.0) + jnp.log1p(jnp.exp(-jnp.abs(neg_lam)))
    log_a = -LRU_C * r * softplus
    a = jnp.exp(log_a)
    mult = jnp.sqrt(1.0 - a * a)
    a_scr[...] = a
    b_scr[...] = mult * gate_i * conv

    def body(t, h):
        row = tm - 1 - t if rev else t
        h = a_scr[pl.ds(row, 1), :] * h + b_scr[pl.ds(row, 1), :]
        if rev:
            o_ref[pl.ds(row, 1), :] = fwd_ref[pl.ds(row, 1), :] + h
        else:
            o_ref[pl.ds(row, 1), :] = h
        return h

    h_scr[...] = lax.fori_loop(0, tm, body, h_scr[...], unroll=8)


def _rglru(u, conv_w, conv_b, wa_bd, ba, wi_bd, bi, lam, bsz, n_lat, n_ctx):
    tm = TM
    d = D_MODEL
    n_tok = u.shape[0]
    n_s, n_c = n_lat // tm, n_ctx // tm
    lat_tiles = bsz * n_s
    halo = 2 * SUBLANES
    per_halo = tm // halo
    last_halo = n_tok // halo - 1

    def make(rev, fwd):
        def blk(b, s):
            is_ctx = s < n_c
            if rev:
                tile = jnp.where(is_ctx, n_c - 1 - s, n_s - 1 - (s - n_c))
            else:
                tile = jnp.where(is_ctx, s, s - n_c)
            return jnp.where(is_ctx, lat_tiles + b * n_c + tile, b * n_s + tile)

        vec = pl.BlockSpec((None, 1, d), lambda b, s: (int(rev), 0, 0))
        wspec = pl.BlockSpec((None, d // MXU_DIM, MXU_DIM, MXU_DIM), lambda b, s: (int(rev), 0, 0, 0))
        row_spec = pl.BlockSpec((tm, d), lambda b, s: (blk(b, s), 0))
        in_specs = [
            pl.BlockSpec((tm, d), lambda b, s: (blk(b, s), CB_LX)),
            pl.BlockSpec((halo, d), lambda b, s: (jnp.maximum(blk(b, s) * per_halo - 1, 0), CB_LX)),
            pl.BlockSpec((halo, d), lambda b, s: (jnp.minimum((blk(b, s) + 1) * per_halo, last_halo), CB_LX)),
            pl.BlockSpec((CONV_W, d), lambda b, s: (0, 0)),
            pl.BlockSpec((1, d), lambda b, s: (0, 0)),
            wspec, vec, wspec, vec, vec,
        ]
        args = [u, u, u, conv_w, conv_b.reshape(1, d), wa_bd, ba.reshape(2, 1, d), wi_bd, bi.reshape(2, 1, d),
                lam.reshape(2, 1, d)]
        if rev:
            in_specs.append(row_spec)
            args.append(fwd)
        return pl.pallas_call(
            functools.partial(_lru_kernel, rev=rev, n_c=n_c, n_s=n_s, tm=tm),
            grid=(bsz, n_c + n_s),
            in_specs=in_specs,
            out_specs=row_spec,
            out_shape=jax.ShapeDtypeStruct((n_tok, d), F32),
            scratch_shapes=[pltpu.VMEM((tm, d), F32), pltpu.VMEM((tm, d), F32), pltpu.VMEM((1, d), F32)],
            input_output_aliases={10: 0} if rev else {},
            compiler_params=_cparams(("arbitrary", "arbitrary"), 40),
            name="rglru_bwd" if rev else "rglru_fwd",
        )(*args)

    fwd = make(False, None)
    return make(True, fwd)


def _chan_dft_kernel(u_ref, cs_ref, o_ref):
    x = u_ref[...].astype(BF16)
    cs = cs_ref[...]
    gd = FOUR_GROUP_DIM
    for g in range(FOUR_GROUPS):
        y = jnp.dot(x[:, g * gd:(g + 1) * gd], cs, preferred_element_type=F32)
        o_ref[:, g * gd:(g + 1) * gd] = y[:, :gd].astype(BF16)
        o_ref[:, D_MODEL + g * gd:D_MODEL + (g + 1) * gd] = y[:, gd:].astype(BF16)


def _chan_dft(u, cs_mat, row0, n_rows):
    tm = TM
    d = D_MODEL
    tile0 = row0 // tm
    return pl.pallas_call(
        _chan_dft_kernel,
        grid=(n_rows // tm,),
        in_specs=[pl.BlockSpec((tm, d), lambda i: (tile0 + i, CB_F)),
                  pl.BlockSpec((FOUR_GROUP_DIM, 2 * FOUR_GROUP_DIM), lambda i: (0, 0))],
        out_specs=pl.BlockSpec((tm, 2 * d), lambda i: (i, 0)),
        out_shape=jax.ShapeDtypeStruct((n_rows, 2 * d), BF16),
        compiler_params=_cparams(("parallel",), 32),
        name="fourier_channels",
    )(u, cs_mat)


def _chan_dft_split_kernel(u_ref, cs_ref, o_ref, y_scr, *, radix):
    x = u_ref[...].astype(BF16)
    cs = cs_ref[...]
    gd = FOUR_GROUP_DIM
    per_g = gd // LANES
    n_cos = D_MODEL // LANES
    for g in range(FOUR_GROUPS):
        y = jnp.dot(x[:, g * gd:(g + 1) * gd], cs, preferred_element_type=F32)
        for j in range(per_g):
            y_scr[g * per_g + j] = y[:, j * LANES:(j + 1) * LANES]
            y_scr[n_cos + g * per_g + j] = y[:, gd + j * LANES:gd + (j + 1) * LANES]
    rows = u_ref.shape[0] // radix
    for r in range(radix):
        for cb in range(2 * n_cos):
            o_ref[r, :, cb * LANES:(cb + 1) * LANES] = y_scr[cb, pl.ds(r, rows, stride=radix), :].astype(BF16)


def _chan_dft_split(u, cs_mat, bsz, n_lat, radix):
    tm = TM
    d = D_MODEL
    n_s = n_lat // tm
    return pl.pallas_call(
        functools.partial(_chan_dft_split_kernel, radix=radix),
        grid=(bsz, n_s),
        in_specs=[pl.BlockSpec((tm, d), lambda b, i: (b * n_s + i, CB_F)),
                  pl.BlockSpec((FOUR_GROUP_DIM, 2 * FOUR_GROUP_DIM), lambda b, i: (0, 0))],
        out_specs=pl.BlockSpec((None, radix, tm // radix, 2 * d), lambda b, i: (b, 0, i, 0)),
        out_shape=jax.ShapeDtypeStruct((bsz, radix, n_lat // radix, 2 * d), BF16),
        scratch_shapes=[pltpu.VMEM((2 * d // LANES, tm, LANES), F32)],
        compiler_params=_cparams(("parallel", "parallel"), 32),
        name="fourier_channels_split",
    )(u, cs_mat)


def _time_fft_kernel(zr_ref, zi_ref, cc_ref, pc_ref, ps_ref, o_ref, *, radix):
    cc = cc_ref[...]
    tk1 = cc.shape[0] // 2
    for r in range(radix):
        pr = jnp.dot(cc, zr_ref[r], preferred_element_type=F32)
        pi = jnp.dot(cc, zi_ref[r], preferred_element_type=F32)
        a_re = pr[:tk1] - pi[tk1:]
        a_im = pr[tk1:] + pi[:tk1]
        for k2 in range(radix):
            term = pc_ref[k2][:, r:r + 1] * a_re - ps_ref[k2][:, r:r + 1] * a_im
            if r == 0:
                o_ref[k2] = term
            else:
                o_ref[k2] += term


def _time_fft(yp, cc, pc, ps, radix, tk1, tn):
    bsz, _, t1, two_d = yp.shape
    d = two_d // 2
    nj = d // tn
    return pl.pallas_call(
        functools.partial(_time_fft_kernel, radix=radix),
        grid=(bsz, nj, t1 // tk1),
        in_specs=[
            pl.BlockSpec((None, radix, t1, tn), lambda b, j, i: (b, 0, 0, j)),
            pl.BlockSpec((None, radix, t1, tn), lambda b, j, i: (b, 0, 0, nj + j)),
            pl.BlockSpec((None, 2 * tk1, t1), lambda b, j, i: (i, 0, 0)),
            pl.BlockSpec((radix, tk1, radix), lambda b, j, i: (0, i, 0)),
            pl.BlockSpec((radix, tk1, radix), lambda b, j, i: (0, i, 0)),
        ],
        out_specs=pl.BlockSpec((None, radix, tk1, tn), lambda b, j, i: (b, 0, i, j)),
        out_shape=jax.ShapeDtypeStruct((bsz, radix, t1, d), F32),
        compiler_params=_cparams(("parallel", "parallel", "arbitrary"), 48),
        name="fourier_positions_fft",
    )(yp, yp, cc, pc, ps)


def _fft_tables(n, radix, tk1):
    t1 = n // radix
    k1 = jnp.arange(t1, dtype=jnp.int32)
    ang = ((k1[:, None] * k1[None, :]) % t1).astype(F32) * (2.0 * math.pi / t1)
    c = jnp.cos(ang).reshape(t1 // tk1, tk1, t1)
    s = jnp.sin(ang).reshape(t1 // tk1, tk1, t1)
    cc = jnp.concatenate([c, s], axis=1).astype(BF16)
    k2 = jnp.arange(radix, dtype=jnp.int32)
    r = jnp.arange(radix, dtype=jnp.int32)
    k = k1[None, :, None] + t1 * k2[:, None, None]
    ph = ((k * r[None, None, :]) % n).astype(F32) * (2.0 * math.pi / n)
    scale = float(n) ** -0.5
    return cc, jnp.cos(ph) * scale, jnp.sin(ph) * scale


def _time_dft_kernel(ct_ref, st_ref, yc_ref, ys_ref, o_ref, acc, *, scale):
    kk = pl.program_id(2)

    @pl.when(kk == 0)
    def _():
        acc[...] = jnp.zeros_like(acc)

    acc[...] += (jnp.dot(ct_ref[...], yc_ref[...], preferred_element_type=F32)
                 - jnp.dot(st_ref[...], ys_ref[...], preferred_element_type=F32))

    @pl.when(kk == pl.num_programs(2) - 1)
    def _():
        o_ref[...] = acc[...] * scale


def _time_dft(y, ct, st, bsz, n_pos, row0, tmk):
    d = D_MODEL
    nt = n_pos // tmk
    rb0 = row0 // tmk
    return pl.pallas_call(
        functools.partial(_time_dft_kernel, scale=float(n_pos) ** -0.5),
        grid=(bsz, nt, nt),
        in_specs=[
            pl.BlockSpec((tmk, tmk), lambda b, i, k: (i, k)),
            pl.BlockSpec((tmk, tmk), lambda b, i, k: (i, k)),
            pl.BlockSpec((tmk, d), lambda b, i, k: (rb0 + b * nt + k, 0)),
            pl.BlockSpec((tmk, d), lambda b, i, k: (rb0 + b * nt + k, 1)),
        ],
        out_specs=pl.BlockSpec((tmk, d), lambda b, i, k: (b * nt + i, 0)),
        out_shape=jax.ShapeDtypeStruct((bsz * n_pos, d), F32),
        scratch_shapes=[pltpu.VMEM((tmk, d), F32)],
        compiler_params=_cparams(("parallel", "parallel", "arbitrary"), 48),
        name="fourier_positions",
    )(ct, st, y, y)


def _dft_tables(n):
    k = jnp.arange(n, dtype=jnp.int32)
    ang = ((k[:, None] * k[None, :]) % n).astype(F32) * (2.0 * math.pi / n)
    return jnp.cos(ang).astype(BF16), jnp.sin(ang).astype(BF16)


def _merge_kernel(x_ref, attn_l_ref, attn_c_ref, rec_ref, four_l_ref, four_c_ref, lg_ref, g0_ref, g1_ref, g2_ref,
                  wa_ref, wl_ref, wf_ref, wo_ref, gate_ref, n2_ref, sh_ref, sc_ref,
                  xo_ref, h_ref, hlo_ref, hhi_ref, *, lat_tiles):
    def mm(a, w_ref):
        return jnp.dot(a.astype(BF16), w_ref[...], preferred_element_type=F32)

    is_lat = pl.program_id(0) < lat_tiles
    attn_o = mm(jnp.where(is_lat, attn_l_ref[...], attn_c_ref[...]), wa_ref)
    rec_o = mm(rec_ref[...] * jax.nn.gelu(lg_ref[...].astype(F32), approximate=True), wl_ref)
    four_o = mm(jnp.where(is_lat, four_l_ref[...], four_c_ref[...]), wf_ref)
    mixed = (jax.nn.sigmoid(g0_ref[...].astype(F32)) * attn_o + jax.nn.sigmoid(g1_ref[...].astype(F32)) * rec_o
             + jax.nn.sigmoid(g2_ref[...].astype(F32)) * four_o)
    x = x_ref[...] + gate_ref[...] * mm(mixed, wo_ref)
    xo_ref[...] = x
    y = x * lax.rsqrt(jnp.mean(x * x, axis=-1, keepdims=True) + EPS)
    h = (y * n2_ref[...]) * (1.0 + sc_ref[...]) + sh_ref[...]
    h_ref[...] = h.astype(BF16)
    hlo_ref[...] = _pack_halves(h[:, :D_MODEL // 2])
    hhi_ref[...] = _pack_halves(h[:, D_MODEL // 2:])


def _merge(x_tok, attn_l, attn_c, rec, four_l, four_c, u, w_attn_o, w_lru_o, w_four_o, w_out, mod4, norm2_g,
           row_of_tile, n_rows):
    tm = TM
    d = D_MODEL
    lat_tiles = attn_l.shape[0] // tm
    ctx_tiles = attn_c.shape[0] // tm
    row = pl.BlockSpec((tm, d), lambda i: (i, 0))
    lat_row = pl.BlockSpec((tm, d), lambda i: (jnp.minimum(i, lat_tiles - 1), 0))
    ctx_row = pl.BlockSpec((tm, d), lambda i: (jnp.clip(i - lat_tiles, 0, ctx_tiles - 1), 0))
    wspec = pl.BlockSpec((d, d), lambda i: (0, 0))

    def ucol(cb):
        return pl.BlockSpec((tm, d), lambda i: (i, cb))

    return pl.pallas_call(
        functools.partial(_merge_kernel, lat_tiles=lat_tiles),
        grid=(n_rows // tm,),
        in_specs=[row, lat_row, ctx_row, row, lat_row, ctx_row, ucol(CB_LG), ucol(CB_G), ucol(CB_G + 1),
                  ucol(CB_G + 2), wspec, wspec, wspec, wspec,
                  _mod_spec(2, row_of_tile), pl.BlockSpec((1, d), lambda i: (0, 0)),
                  _mod_spec(3, row_of_tile), _mod_spec(4, row_of_tile)],
        out_specs=[row, row, pl.BlockSpec((tm, d // 4), lambda i: (i, 0)),
                   pl.BlockSpec((tm, d // 4), lambda i: (i, 0))],
        out_shape=[jax.ShapeDtypeStruct((n_rows, d), F32), jax.ShapeDtypeStruct((n_rows, d), BF16),
                   jax.ShapeDtypeStruct((n_rows, d // 4), jnp.uint32),
                   jax.ShapeDtypeStruct((n_rows, d // 4), jnp.uint32)],
        compiler_params=_cparams(("parallel",), 56),
        name="merge_branches",
    )(x_tok, attn_l, attn_c, rec, four_l, four_c, u, u, u, u, w_attn_o, w_lru_o, w_four_o, w_out,
      mod4, norm2_g.reshape(1, d), mod4, mod4)


def _route_kernel(h_ref, rw_ref, rb_ref, tri_ref, idx_ref, wgt_ref, pos_ref, cnt_ref, base_scr):
    tm = h_ref.shape[0]

    @pl.when(pl.program_id(0) == 0)
    def _():
        base_scr[...] = jnp.zeros_like(base_scr)

    logits = lax.dot_general(rw_ref[...], h_ref[...], (((1,), (1,)), ((), ())), preferred_element_type=F32)
    scores = jax.nn.sigmoid(logits)
    biased = scores + rb_ref[...]
    neg = -jnp.inf
    e_iota = lax.broadcasted_iota(jnp.int32, (N_EXPERTS, tm), 0)
    g_iota = lax.broadcasted_iota(jnp.int32, (N_GROUPS, tm), 0)

    rows = []
    for g in range(N_GROUPS):
        xg = biased[g * GROUP_SIZE:(g + 1) * GROUP_SIZE]
        m1 = jnp.max(xg, axis=0, keepdims=True)
        is_max = xg == m1
        n_max = jnp.sum(is_max.astype(F32), axis=0, keepdims=True)
        m2 = jnp.max(jnp.where(is_max, neg, xg), axis=0, keepdims=True)
        rows.append(m1 + jnp.where(n_max >= 2.0, m1, m2))
    grp = jnp.concatenate(rows, axis=0)

    g_sel = jnp.zeros((N_GROUPS, tm), jnp.bool_)
    work = grp
    for _ in range(TOPK_GROUPS):
        m = jnp.max(work, axis=0, keepdims=True)
        first = jnp.min(jnp.where(work == m, g_iota, N_GROUPS), axis=0, keepdims=True)
        hit = g_iota == first
        g_sel = jnp.logical_or(g_sel, hit)
        work = jnp.where(hit, neg, work)
    g_sel_f = g_sel.astype(F32)
    mask_rows = [jnp.broadcast_to(g_sel_f[g:g + 1], (GROUP_SIZE, tm)) for g in range(N_GROUPS)]
    e_mask = jnp.concatenate(mask_rows, axis=0) > 0.5

    work = jnp.where(e_mask, biased, neg)
    sel = jnp.zeros((N_EXPERTS, tm), F32)
    idx_rows, w_rows, hits = [], [], []
    for _ in range(TOP_K):
        m = jnp.max(work, axis=0, keepdims=True)
        first = jnp.min(jnp.where(work == m, e_iota, N_EXPERTS), axis=0, keepdims=True)
        hit = e_iota == first
        hits.append(hit)
        idx_rows.append(first)
        w_rows.append(jnp.sum(jnp.where(hit, scores, 0.0), axis=0, keepdims=True))
        sel = jnp.where(hit, 1.0, sel)
        work = jnp.where(hit, neg, work)
    w = jnp.concatenate(w_rows, axis=0)
    wgt_ref[...] = w / jnp.sum(w, axis=0, keepdims=True) * ROUTED_SCALE
    idx_ref[...] = jnp.concatenate(idx_rows, axis=0)

    before = jnp.dot(sel.astype(BF16), tri_ref[...], preferred_element_type=F32)
    rank = base_scr[...] + before
    pos_rows = [jnp.sum(jnp.where(hit, rank, 0.0), axis=0, keepdims=True) for hit in hits]
    pos_ref[...] = jnp.concatenate(pos_rows, axis=0).astype(jnp.int32)
    base_scr[...] = base_scr[...] + jnp.sum(sel, axis=1, keepdims=True)
    cnt_ref[...] = jnp.broadcast_to(base_scr[...], cnt_ref.shape)


def _route(h16, rw_t, router_b, tm):
    n_tok, d = h16.shape
    tri = (jnp.arange(tm)[:, None] < jnp.arange(tm)[None, :]).astype(BF16)
    out_col = pl.BlockSpec((TOP_K, tm), lambda i: (0, i))
    return pl.pallas_call(
        _route_kernel,
        grid=(n_tok // tm,),
        in_specs=[pl.BlockSpec((tm, d), lambda i: (i, 0)),
                  pl.BlockSpec((N_EXPERTS, d), lambda i: (0, 0)),
                  pl.BlockSpec((N_EXPERTS, 1), lambda i: (0, 0)),
                  pl.BlockSpec((tm, tm), lambda i: (0, 0))],
        out_specs=[out_col, out_col, out_col, pl.BlockSpec((N_EXPERTS, LANES), lambda i: (0, 0))],
        out_shape=[jax.ShapeDtypeStruct((TOP_K, n_tok), jnp.int32), jax.ShapeDtypeStruct((TOP_K, n_tok), F32),
                   jax.ShapeDtypeStruct((TOP_K, n_tok), jnp.int32),
                   jax.ShapeDtypeStruct((N_EXPERTS, LANES), F32)],
        scratch_shapes=[pltpu.VMEM((N_EXPERTS, 1), F32)],
        compiler_params=_cparams(("arbitrary",), 32),
        name="moe_router",
    )(h16, rw_t, router_b.reshape(N_EXPERTS, 1), tri)


def _slot_kernel(idx_ref, pos_ref, offs_ref, dest_ref):
    tm = idx_ref.shape[1]
    e_iota = lax.broadcasted_iota(jnp.int32, (N_EXPERTS, tm), 0)
    idx = idx_ref[...]
    offs = offs_ref[...]
    rows = [jnp.sum(jnp.where(e_iota == idx[r:r + 1], offs, 0), axis=0, keepdims=True) for r in range(TOP_K)]
    dest_ref[...] = jnp.concatenate(rows, axis=0) + pos_ref[...]


def _slots(top_idx, pos, offs, tm):
    n_tok = top_idx.shape[1]
    col = pl.BlockSpec((TOP_K, tm), lambda i: (0, i))
    return pl.pallas_call(
        _slot_kernel,
        grid=(n_tok // tm,),
        in_specs=[col, col, pl.BlockSpec((N_EXPERTS, 1), lambda i: (0, 0))],
        out_specs=col,
        out_shape=jax.ShapeDtypeStruct((TOP_K, n_tok), jnp.int32),
        compiler_params=_cparams(("parallel",), 32),
        name="moe_slots",
    )(top_idx, pos, offs.reshape(N_EXPERTS, 1))


def _sc_scatter_rows(src, idx, n_out):
    n_src, width = src.shape
    n_idx = idx.shape[0]
    n_win = n_src // SC_WINDOW
    mesh = plsc.VectorSubcoreMesh(core_axis_name="core", subcore_axis_name="subcore")

    @pl.kernel(out_type=jax.ShapeDtypeStruct((n_out, width), src.dtype), mesh=mesh, scratch_types=[],
               name="moe_scatter_sc")
    def scatter(src_hbm, idx_hbm, out_hbm):
        def body(src_vmem, idx_vmem):
            pltpu.sync_copy(src_vmem, out_hbm.at[idx_vmem.at[0]])

        pltpu.emit_pipeline(
            body,
            grid=(n_idx // SC_WINDOW,),
            in_specs=[pl.BlockSpec((SC_WINDOW, width), index_map=lambda g: (g % n_win, 0)),
                      pl.BlockSpec((1, SC_WINDOW), index_map=lambda g: (0, g))],
            out_specs=[],
            core_axis_name=("core", "subcore"),
            dimension_semantics=(pltpu.PARALLEL,),
        )(src_hbm, idx_hbm)

    return scatter(src, idx.reshape(1, n_idx))


def _expert_kernel(be_ref, nu_ref, nv_ref, xlo_ref, xhi_ref, wg_ref, wu_ref, wd_ref, ylo_ref, yhi_ref,
                   wg_s, wu_s, wd_s):
    i = pl.program_id(0)
    used = i < nu_ref[0]
    prev_e = be_ref[jnp.maximum(i - 1, 0)]
    fresh = jnp.logical_or(i == 0, be_ref[i] != prev_e)

    @pl.when(jnp.logical_and(used, fresh))
    def _():
        wg_s[...] = wg_ref[0].astype(BF16)
        wu_s[...] = wu_ref[0].astype(BF16)
        wd_s[...] = wd_ref[0].astype(BF16)

    @pl.when(used)
    def _():
        row = lax.broadcasted_iota(jnp.int32, xlo_ref.shape, 0)
        live = row < nv_ref[i]
        half = D_MODEL // 2
        quarter = D_MODEL // 4
        parts = []
        for ref in (xlo_ref, xhi_ref):
            parts.extend(p.astype(BF16) for p in _unpack_halves(jnp.where(live, ref[...], jnp.uint32(0))))

        def proj(w_s):
            acc = jnp.dot(parts[0], w_s[:quarter, :], preferred_element_type=F32)
            for k in range(1, 4):
                acc = acc + jnp.dot(parts[k], w_s[k * quarter:(k + 1) * quarter, :], preferred_element_type=F32)
            return acc

        g = proj(wg_s)
        act = (_silu(g) * proj(wu_s)).astype(BF16)
        y = jnp.dot(act, wd_s[...], preferred_element_type=F32)
        ylo_ref[...] = _pack_halves(y[:, :half])
        yhi_ref[...] = _pack_halves(y[:, half:])


def _experts(xs_lo, xs_hi, block_e, n_used, n_valid, w_gate, w_up, w_down, layer, bm):
    n_slots, quarter = xs_lo.shape
    half = 2 * quarter
    d = D_MODEL

    def blk(i, be, nu, nv):
        return (jnp.minimum(i, nu[0] - 1), 0)

    def wsel(i, be, nu, nv):
        return (layer, be[i], 0, 0)

    grid_spec = pltpu.PrefetchScalarGridSpec(
        num_scalar_prefetch=3,
        grid=(n_slots // bm,),
        in_specs=[pl.BlockSpec((bm, quarter), blk),
                  pl.BlockSpec((bm, quarter), blk),
                  pl.BlockSpec((None, 1, d, EXPERT_FF), wsel),
                  pl.BlockSpec((None, 1, d, EXPERT_FF), wsel),
                  pl.BlockSpec((None, 1, EXPERT_FF, d), wsel)],
        out_specs=[pl.BlockSpec((bm, half // 2), blk), pl.BlockSpec((bm, half // 2), blk)],
        scratch_shapes=[pltpu.VMEM((d, EXPERT_FF), BF16), pltpu.VMEM((d, EXPERT_FF), BF16),
                        pltpu.VMEM((EXPERT_FF, d), BF16)],
    )
    y_shape = jax.ShapeDtypeStruct((n_slots, half // 2), jnp.uint32)
    return pl.pallas_call(
        _expert_kernel,
        grid_spec=grid_spec,
        out_shape=[y_shape, y_shape],
        compiler_params=_cparams(("arbitrary",), 40),
        name="moe_experts",
    )(block_e, n_used, n_valid, xs_lo, xs_hi, w_gate, w_up, w_down)


def _sc_gather_rows(table, idx):
    n_idx = idx.shape[0]
    width = table.shape[1]
    mesh = plsc.VectorSubcoreMesh(core_axis_name="core", subcore_axis_name="subcore")

    @pl.kernel(out_type=jax.ShapeDtypeStruct((n_idx, width), table.dtype), mesh=mesh, scratch_types=[],
               name="moe_gather_sc")
    def gather(table_hbm, idx_hbm, out_hbm):
        def body(idx_vmem, out_vmem):
            pltpu.sync_copy(table_hbm.at[idx_vmem.at[0]], out_vmem)

        pltpu.emit_pipeline(
            body,
            grid=(n_idx // SC_WINDOW,),
            in_specs=[pl.BlockSpec((1, SC_WINDOW), index_map=lambda i: (0, i))],
            out_specs=[pl.BlockSpec((SC_WINDOW, width), index_map=lambda i: (i, 0))],
            core_axis_name=("core", "subcore"),
            dimension_semantics=(pltpu.PARALLEL,),
        )(idx_hbm, out_hbm)

    return gather(table, idx.reshape(1, n_idx))


def _combine_kernel(x_ref, h_ref, tw_ref, gate_ref, wsg_ref, wsu_ref, wsd_ref, lo_ref, hi_ref, o_ref):
    quarter = D_MODEL // 4
    h = h_ref[...]
    act = (_silu(jnp.dot(h, wsg_ref[...], preferred_element_type=F32))
           * jnp.dot(h, wsu_ref[...], preferred_element_type=F32)).astype(BF16)
    shared = jnp.dot(act, wsd_ref[...], preferred_element_type=F32)

    tw = tw_ref[...]
    gate = gate_ref[...]
    for q, buf in enumerate((lo_ref, hi_ref)):
        moe_a = moe_b = None
        for r in range(TOP_K):
            ya, yb = _unpack_halves(buf[r])
            wr = tw[:, r:r + 1]
            moe_a = ya * wr if moe_a is None else moe_a + ya * wr
            moe_b = yb * wr if moe_b is None else moe_b + yb * wr
        for part, moe in ((2 * q, moe_a), (2 * q + 1, moe_b)):
            cols = slice(part * quarter, (part + 1) * quarter)
            o_ref[:, cols] = x_ref[:, cols] + gate[:, cols] * (moe + shared[:, cols])


def _combine(x_tok, h16, g_lo, g_hi, top_w, mod4, ws_gate, ws_up, ws_down, row_of_tile):
    n_tok, d = x_tok.shape
    tm = TM
    ff = ws_gate.shape[1]
    row = pl.BlockSpec((tm, d), lambda i: (i, 0))
    gat = pl.BlockSpec((TOP_K, tm, d // 4), lambda i: (0, i, 0))
    return pl.pallas_call(
        _combine_kernel,
        grid=(n_tok // tm,),
        in_specs=[row, row,
                  pl.BlockSpec((tm, TOP_K), lambda i: (i, 0)),
                  _mod_spec(5, row_of_tile),
                  pl.BlockSpec((d, ff), lambda i: (0, 0)),
                  pl.BlockSpec((d, ff), lambda i: (0, 0)),
                  pl.BlockSpec((ff, d), lambda i: (0, 0)),
                  gat, gat],
        out_specs=row,
        out_shape=jax.ShapeDtypeStruct((n_tok, d), F32),
        compiler_params=_cparams(("parallel",), 48),
        name="moe_combine",
    )(x_tok, h16, top_w, mod4, ws_gate, ws_up, ws_down, g_lo, g_hi)


def _moe(x_tok, h16, h_lo, h_hi, mod4, row_of_tile, rw_t, router_b, w_gate, w_up, w_down, layer, ws_gate, ws_up, ws_down):
    n_tok = x_tok.shape[0]
    bm = MOE_BM
    tm_r = TM_ROUTE if n_tok % TM_ROUTE == 0 else TM
    top_idx, top_w, pos, cnt = _route(h16, rw_t, router_b, tm_r)

    counts = cnt[:, 0].astype(jnp.int32)
    padded = (counts + bm - 1) // bm * bm
    ends = jnp.cumsum(padded)
    n_blocks = -(-(n_tok * TOP_K + N_EXPERTS * (bm - 1)) // bm)
    offs = ends - padded
    dest = _slots(top_idx, pos, offs, tm_r)
    block_start = jnp.arange(n_blocks, dtype=jnp.int32) * bm
    block_e = jnp.minimum(jnp.sum((ends[None, :] <= block_start[:, None]).astype(jnp.int32), axis=1),
                          N_EXPERTS - 1)
    n_valid = jnp.clip(counts[block_e] - (block_start - offs[block_e]), 0, bm).astype(jnp.int32)
    n_used = (ends[-1] // bm).astype(jnp.int32).reshape(1)
    dest_flat = dest.reshape(TOP_K * n_tok)
    xs_lo = _sc_scatter_rows(h_lo, dest_flat, n_blocks * bm)
    xs_hi = _sc_scatter_rows(h_hi, dest_flat, n_blocks * bm)
    ys_lo, ys_hi = _experts(xs_lo, xs_hi, block_e, n_used, n_valid, w_gate, w_up, w_down, layer, bm)
    g_lo = _sc_gather_rows(ys_lo, dest_flat).reshape(TOP_K, n_tok, D_MODEL // 4)
    g_hi = _sc_gather_rows(ys_hi, dest_flat).reshape(TOP_K, n_tok, D_MODEL // 4)
    return _combine(x_tok, h16, g_lo, g_hi, top_w.T, mod4, ws_gate, ws_up, ws_down, row_of_tile)


def _block_diag_tiles(w):
    per = MXU_DIM // LRU_BLOCK_DIM
    w = w.reshape(2, D_MODEL // MXU_DIM, per, LRU_BLOCK_DIM, LRU_BLOCK_DIM)
    eye = jnp.eye(per, dtype=w.dtype)
    t = jnp.einsum('ztpde,pq->ztpdqe', w, eye)
    return t.reshape(2, D_MODEL // MXU_DIM, MXU_DIM, MXU_DIM).astype(BF16)


def _rope_tables(n_lat, n_ctx):
    rows = n_lat // GRID_W
    row = jnp.repeat(jnp.arange(rows), GRID_W)
    col = jnp.tile(jnp.arange(GRID_W), rows)
    inv = ROPE_THETA ** (-jnp.arange(ROPE_FREQS, dtype=F32) / ROPE_FREQS)
    ang = jnp.stack([row[:, None] * inv, col[:, None] * inv], axis=1)
    ang = jnp.concatenate([ang, ang], axis=2).reshape(n_lat, HEAD_DIM)
    ang = jnp.tile(ang, (1, LANES // HEAD_DIM))
    ang = jnp.concatenate([ang, jnp.zeros((n_ctx, LANES), F32)], axis=0)
    lane = jnp.arange(LANES)
    sign = jnp.where((lane % (2 * ROPE_FREQS)) < ROPE_FREQS, -1.0, 1.0).astype(F32)
    return jnp.cos(ang), jnp.sin(ang) * sign


def kernel(x, c, ctx, c_ctx, w_mod, b_mod, norm1_g, w_in, q_norm_g, k_norm_g, lambda_qk, subln_g, w_attn_o, conv_w, conv_b, lru_wa, lru_ba, lru_wi, lru_bi, lru_lambda, w_lru_o, w_four_o, w_out, norm2_g, router_w, router_b, exp_w_gate, exp_w_up, exp_w_down, sh_w_gate, sh_w_up, sh_w_down):
    bsz, n_lat, d = x.shape
    n_ctx = ctx.shape[1]
    depth = w_mod.shape[0]
    n_latent_rows = bsz * n_lat
    n_tok = n_latent_rows + bsz * n_ctx
    assert d == D_MODEL and bsz + 1 <= MOD_ROWS
    assert n_lat % TM == 0 and n_ctx % TM == 0 and n_lat % GRID_W == 0
    tm_in = TM_IN if (n_lat % TM_IN == 0 and (bsz * n_ctx) % TM_IN == 0) else TM

    cvec = jnp.zeros((MOD_ROWS, d), F32).at[:bsz].set(c).at[bsz].set(c_ctx)
    mods = _modulation(cvec, w_mod, b_mod)

    def row_of_tile_fn(tm):
        per_batch = n_lat // tm
        return lambda i: jnp.minimum(i // per_batch, bsz)

    assert n_lat % n_ctx == 0
    cos_t, sin_t = _rope_tables(n_lat, n_ctx)
    group_mat = jnp.kron(jnp.eye(d // HEAD_DIM, dtype=F32), jnp.ones((HEAD_DIM, HEAD_DIM), F32)).astype(BF16)
    m = jnp.arange(FOUR_GROUP_DIM, dtype=jnp.int32)
    ang_c = ((m[:, None] * m[None, :]) % FOUR_GROUP_DIM).astype(F32) * (2.0 * math.pi / FOUR_GROUP_DIM)
    inv_sqrt_c = FOUR_GROUP_DIM ** -0.5
    cs_mat = jnp.concatenate([jnp.cos(ang_c) * inv_sqrt_c, jnp.sin(ang_c) * inv_sqrt_c], axis=1).astype(BF16)
    assert n_lat % (FFT_RADIX * MXU_DIM) == 0 or n_lat // FFT_RADIX < MXU_DIM
    fft_tk1 = min(MXU_DIM, n_lat // FFT_RADIX)
    fft_cc, fft_pc, fft_ps = _fft_tables(n_lat, FFT_RADIX, fft_tk1)
    ct_ctx, st_ctx = _dft_tables(n_ctx)
    q_sub = math.gcd(Q_SUB, n_lat // TQ)
    n_keys = n_ctx + n_lat
    tk = TK if n_keys % TK == 0 else TM

    x_tok = jnp.concatenate([x.reshape(n_latent_rows, d), ctx.reshape(bsz * n_ctx, d)], axis=0)
    for l in range(depth):
        last = l == depth - 1
        lam_init = 0.8 - 0.6 * math.exp(-0.3 * l)
        mod4 = mods[l].reshape(MOD_ROWS, 6, 1, d)

        u = _in_projection(x_tok, mod4, norm1_g[l], w_in[l].astype(BF16), row_of_tile_fn(tm_in), tm_in)

        k_all, q_all, vt_all = _qkv_prepare(u, k_norm_g[l], q_norm_g[l], group_mat, cos_t, sin_t,
                                            bsz, n_lat, n_ctx)
        attn_l = _diff_attention(q_all, k_all, vt_all, lambda_qk[l], subln_g[l], lam_init,
                                 0, n_lat, 0, n_keys, TQ, q_sub, tk)
        attn_c = attn_l
        if not last:
            attn_c = _diff_attention(q_all, k_all, vt_all, lambda_qk[l], subln_g[l], lam_init,
                                     n_lat, n_ctx, n_lat, n_ctx, TM, 1, TM)

        rec = _rglru(u, conv_w[l], conv_b[l], _block_diag_tiles(lru_wa[l]), lru_ba[l],
                     _block_diag_tiles(lru_wi[l]), lru_bi[l], lru_lambda[l], bsz, n_lat, n_ctx)

        y_lat = _chan_dft_split(u, cs_mat, bsz, n_lat, FFT_RADIX)
        four_l = _time_fft(y_lat, fft_cc, fft_pc, fft_ps, FFT_RADIX, fft_tk1, MXU_DIM).reshape(n_latent_rows, d)
        four_c = four_l
        if not last:
            y_ctx = _chan_dft(u, cs_mat, n_latent_rows, bsz * n_ctx)
            four_c = _time_dft(y_ctx, ct_ctx, st_ctx, bsz, n_ctx, 0, n_ctx)

        n_rows = n_latent_rows if last else n_tok
        row_of_tile = row_of_tile_fn(TM)
        x_mid, h16, h_lo, h_hi = _merge(x_tok, attn_l, attn_c, rec, four_l, four_c, u, w_attn_o[l].astype(BF16),
                                w_lru_o[l].astype(BF16), w_four_o[l].astype(BF16), w_out[l].astype(BF16),
                                mod4, norm2_g[l], row_of_tile, n_rows)
        x_tok = _moe(x_mid, h16, h_lo, h_hi, mod4, row_of_tile, router_w[l].T.astype(BF16), router_b[l],
                     exp_w_gate, exp_w_up, exp_w_down, l,
                     sh_w_gate[l].astype(BF16), sh_w_up[l].astype(BF16), sh_w_down[l].astype(BF16))
    return x_tok[:n_latent_rows].reshape(bsz, n_lat, d)
```

```python
import functools
import math

import jax
import jax.numpy as jnp
from jax import lax
from jax.experimental import pallas as pl
from jax.experimental.pallas import tpu as pltpu
from jax.experimental.pallas import tpu_sc as plsc

F32 = jnp.float32
BF16 = jnp.bfloat16

D_MODEL = 1024
EPS = 1e-6
GRID_W = 64

N_HEADS = 8
HEAD_DIM = 64
V_DIM = 2 * HEAD_DIM
ATTN_SCALE = HEAD_DIM ** -0.5
ROPE_THETA = 10000.0
ROPE_FREQS = HEAD_DIM // 4

LRU_BLOCKS = 16
LRU_BLOCK_DIM = D_MODEL // LRU_BLOCKS
LRU_C = 8.0
CONV_W = 4

FOUR_GROUPS = 4
FOUR_GROUP_DIM = D_MODEL // FOUR_GROUPS

CB_K, CB_V, CB_LX, CB_Q, CB_LG, CB_F, CB_G = 0, 1, 2, 3, 4, 5, 6

N_EXPERTS = 256
TOP_K = 8
N_GROUPS = 8
GROUP_SIZE = N_EXPERTS // N_GROUPS
TOPK_GROUPS = 4
EXPERT_FF = 256
ROUTED_SCALE = 2.5

LANES = 128
SUBLANES = 8
MXU_DIM = 256
VMEM_BYTES_V7X = 64 * 1024 * 1024

MOD_ROWS = 8
TM = 256
TM_IN = 512
IN_PANEL = 3
TQ = 256
Q_SUB = 32
TK = 768
TM_ROUTE = 512
MOE_BM = 512
SC_WINDOW = 128
FFT_RADIX = 8
LOG2E = 1.4426950408889634


def _cparams(sem, vmem_mb):
    vmem_bytes = int(vmem_mb * 1024 * 1024)
    assert vmem_bytes < VMEM_BYTES_V7X
    return pltpu.CompilerParams(dimension_semantics=sem, vmem_limit_bytes=vmem_bytes)


def _silu(x):
    return x * jax.nn.sigmoid(x)


def _pack_halves(x):
    half = x.shape[1] // 2
    bits = pltpu.bitcast(x.astype(BF16).astype(F32), jnp.uint32)
    return (bits[:, :half] & jnp.uint32(0xFFFF0000)) | (bits[:, half:] >> 16)


def _unpack_halves(w):
    return pltpu.bitcast(w & jnp.uint32(0xFFFF0000), F32), pltpu.bitcast(w << 16, F32)


def _mod_kernel(c_ref, w_ref, b_ref, o_ref):
    c = c_ref[...]
    s = _silu(c).astype(BF16)
    o_ref[0] = jnp.dot(s, w_ref[0].astype(BF16), preferred_element_type=F32) + b_ref[0]


def _modulation(cvec, w_mod, b_mod):
    n_layers, d, six_d = w_mod.shape
    nj = six_d // d
    return pl.pallas_call(
        _mod_kernel,
        grid=(n_layers, nj),
        in_specs=[
            pl.BlockSpec((MOD_ROWS, d), lambda l, j: (0, 0)),
            pl.BlockSpec((1, d, d), lambda l, j: (l, 0, j)),
            pl.BlockSpec((1, 1, d), lambda l, j: (l, 0, j)),
        ],
        out_specs=pl.BlockSpec((1, MOD_ROWS, d), lambda l, j: (l, 0, j)),
        out_shape=jax.ShapeDtypeStruct((n_layers, MOD_ROWS, six_d), F32),
        compiler_params=_cparams(("parallel", "parallel"), 32),
        name="modulation",
    )(cvec, w_mod, b_mod.reshape(n_layers, 1, six_d))


def _mod_spec(part, row_of_tile):
    return pl.BlockSpec((None, None, 1, D_MODEL), lambda i: (row_of_tile(i), part, 0, 0))


def _inproj_kernel(x_ref, g_ref, sh_ref, sc_ref, w_ref, o_ref):
    x = x_ref[...]
    y = x * lax.rsqrt(jnp.mean(x * x, axis=-1, keepdims=True) + EPS)
    h = (y * g_ref[...]) * (1.0 + sc_ref[...]) + sh_ref[...]
    o_ref[...] = jnp.dot(h.astype(BF16), w_ref[...], preferred_element_type=F32).astype(o_ref.dtype)


def _in_projection(x_tok, mod4, norm_g, w_in_bf16, row_of_tile, tm):
    n_tok, d = x_tok.shape
    n_cols = w_in_bf16.shape[1]
    tn = IN_PANEL * d
    row_of = lambda j, i: row_of_tile(i)
    return pl.pallas_call(
        _inproj_kernel,
        grid=(n_cols // tn, n_tok // tm),
        in_specs=[
            pl.BlockSpec((tm, d), lambda j, i: (i, 0)),
            pl.BlockSpec((1, d), lambda j, i: (0, 0)),
            pl.BlockSpec((None, None, 1, d), lambda j, i: (row_of(j, i), 0, 0, 0)),
            pl.BlockSpec((None, None, 1, d), lambda j, i: (row_of(j, i), 1, 0, 0)),
            pl.BlockSpec((d, tn), lambda j, i: (0, j)),
        ],
        out_specs=pl.BlockSpec((tm, tn), lambda j, i: (i, j)),
        out_shape=jax.ShapeDtypeStruct((n_tok, n_cols), BF16),
        compiler_params=_cparams(("parallel", "parallel"), 48),
        name="in_projection",
    )(x_tok, norm_g.reshape(1, d), mod4, mod4, w_in_bf16)


def _prep_kernel(uk_ref, uq_ref, uv_ref, kg_ref, qg_ref, gm_ref, cos_ref, sin_ref, k_out, q_out, vt_out, *, tm):
    gm = gm_ref[...]
    lane = lax.broadcasted_iota(jnp.int32, (tm, LANES), 1)
    first_half = (lane % (2 * ROPE_FREQS)) < ROPE_FREQS

    def norm_rope(x, g):
        ss = jnp.dot((x * x).astype(BF16), gm, preferred_element_type=F32) * (1.0 / HEAD_DIM)
        y = x * lax.rsqrt(ss + EPS) * g
        c = cos_ref[...]
        s = sin_ref[...]
        outs = []
        for j in range(D_MODEL // LANES):
            yt = y[:, j * LANES:(j + 1) * LANES]
            rot = jnp.where(first_half, pltpu.roll(yt, LANES - ROPE_FREQS, 1), pltpu.roll(yt, ROPE_FREQS, 1))
            outs.append(yt * c + rot * s)
        return jnp.concatenate(outs, axis=1)

    k_out[...] = norm_rope(uk_ref[...].astype(F32), kg_ref[...]).astype(BF16)
    qn = norm_rope(uq_ref[...].astype(F32), qg_ref[...]) * (ATTN_SCALE * LOG2E)
    lo = lane < HEAD_DIM
    for h in range(N_HEADS):
        qt = qn[:, h * LANES:(h + 1) * LANES]
        q_out[:, (2 * h) * LANES:(2 * h + 1) * LANES] = jnp.where(lo, qt, 0.0).astype(BF16)
        q_out[:, (2 * h + 1) * LANES:(2 * h + 2) * LANES] = jnp.where(lo, 0.0, qt).astype(BF16)
    vt_out[...] = uv_ref[...].astype(F32).T.astype(BF16)


def _qkv_prepare(u, k_gain, q_gain, group_mat, cos_t, sin_t, bsz, n_lat, n_ctx):
    tm = TM
    d = D_MODEL
    n_s, n_c = n_lat // tm, n_ctx // tm
    n_rows = n_lat + n_ctx
    lat_tiles = bsz * n_s
    kg = jnp.tile(k_gain, d // HEAD_DIM).reshape(1, d)
    qg = jnp.tile(q_gain, d // HEAD_DIM).reshape(1, d)

    def tok_tile(b, j):
        return jnp.where(j < n_s, b * n_s + j, lat_tiles + b * n_c + (j - n_s))

    def u_spec(cb):
        return pl.BlockSpec((tm, d), lambda b, j: (tok_tile(b, j), cb))

    vec = pl.BlockSpec((1, d), lambda b, j: (0, 0))
    rope_spec = pl.BlockSpec((tm, LANES), lambda b, j: (j, 0))
    return pl.pallas_call(
        functools.partial(_prep_kernel, tm=tm),
        grid=(bsz, n_s + n_c),
        in_specs=[u_spec(CB_K), u_spec(CB_Q), u_spec(CB_V), vec, vec,
                  pl.BlockSpec((d, d), lambda b, j: (0, 0)), rope_spec, rope_spec],
        out_specs=[
            pl.BlockSpec((None, tm, d), lambda b, j: (b, j, 0)),
            pl.BlockSpec((None, tm, 2 * d), lambda b, j: (b, j, 0)),
            pl.BlockSpec((None, d, tm), lambda b, j: (b, 0, j)),
        ],
        out_shape=[jax.ShapeDtypeStruct((bsz, n_rows, d), BF16),
                   jax.ShapeDtypeStruct((bsz, n_rows, 2 * d), BF16),
                   jax.ShapeDtypeStruct((bsz, d, n_rows), BF16)],
        compiler_params=_cparams(("parallel", "parallel"), 40),
        name="qkv_prepare",
    )(u, u, u, kg, qg, group_mat, cos_t, sin_t)


def _attn_kernel(lq_ref, sg_ref, q1_ref, q2_ref, k_ref, vt_ref, o_ref, s_buf, *, n_keys, tk, tq, lam_init):
    lq = lq_ref[...]
    lam = (jnp.exp(jnp.sum(lq[0:1] * lq[1:2], axis=1, keepdims=True))
           - jnp.exp(jnp.sum(lq[2:3] * lq[3:4], axis=1, keepdims=True)) + lam_init)
    n_chunks = n_keys // tk
    n_sub = q1_ref.shape[0] // tq

    def load_q(j):
        rows = pl.ds(pl.multiple_of(j * tq, tq), tq)
        return jnp.concatenate([q1_ref[rows, :], q2_ref[rows, :]], axis=0)

    def score_chunk(q, c, mx):
        s = lax.dot_general(k_ref[c * tk:(c + 1) * tk, :], q, (((1,), (1,)), ((), ())),
                            preferred_element_type=F32)
        s_buf[c * tk:(c + 1) * tk, :] = s
        cm = jnp.max(s, axis=0, keepdims=True)
        return cm if mx is None else jnp.maximum(mx, cm)

    q0 = load_q(0)
    mx0 = None
    for c in range(n_chunks):
        mx0 = score_chunk(q0, c, mx0)

    def sub_tile(j, mx):
        q_next = load_q(jnp.minimum(j + 1, n_sub - 1))
        mx_next = None
        l = jnp.zeros((1, 2 * tq), F32)
        acc = jnp.zeros((V_DIM, 2 * tq), F32)
        for c in range(n_chunks):
            p = jnp.exp2(s_buf[c * tk:(c + 1) * tk, :] - mx)
            l = l + jnp.sum(p, axis=0, keepdims=True)
            acc = acc + jnp.dot(vt_ref[:, c * tk:(c + 1) * tk], p.astype(BF16), preferred_element_type=F32)
            mx_next = score_chunk(q_next, c, mx_next)
        o = acc / l
        o = o[:, :tq] - lam * o[:, tq:]
        ms = jnp.mean(o * o, axis=0, keepdims=True)
        on = (o * lax.rsqrt(ms + EPS) * sg_ref[...]) * (1.0 - lam_init)
        o_ref[pl.ds(pl.multiple_of(j * tq, tq), tq), :] = on.T.astype(o_ref.dtype)
        return mx_next

    lax.fori_loop(0, n_sub, sub_tile, mx0)


def _diff_attention(q_all, k_all, vt_all, lambda_qk, subln_g, lam_init, q_row0, n_q, key_row0, n_keys, tq, n_sub, tk):
    bsz = q_all.shape[0]
    d = D_MODEL
    tstep = tq * n_sub
    nq = n_q // tstep
    qb0 = q_row0 // tstep
    kb0 = key_row0 // n_keys
    s_bytes = n_keys * 2 * tq * 4
    return pl.pallas_call(
        functools.partial(_attn_kernel, n_keys=n_keys, tk=tk, tq=tq, lam_init=lam_init),
        grid=(bsz, N_HEADS, nq),
        in_specs=[
            pl.BlockSpec((4, HEAD_DIM), lambda b, h, i: (0, 0)),
            pl.BlockSpec((V_DIM, 1), lambda b, h, i: (0, 0)),
            pl.BlockSpec((None, tstep, LANES), lambda b, h, i: (b, qb0 + i, 2 * h)),
            pl.BlockSpec((None, tstep, LANES), lambda b, h, i: (b, qb0 + i, 2 * h + 1)),
            pl.BlockSpec((None, n_keys, LANES), lambda b, h, i: (b, kb0, h)),
            pl.BlockSpec((None, V_DIM, n_keys), lambda b, h, i: (b, h, kb0)),
        ],
        out_specs=pl.BlockSpec((tstep, LANES), lambda b, h, i: (b * nq + i, h)),
        out_shape=jax.ShapeDtypeStruct((bsz * n_q, d), BF16),
        scratch_shapes=[pltpu.VMEM((n_keys, 2 * tq), F32)],
        compiler_params=_cparams(("parallel", "parallel", "arbitrary"), 28 + s_bytes / 2 ** 20),
        name="diff_attention",
    )(lambda_qk, subln_g.reshape(V_DIM, 1), q_all, q_all, k_all, vt_all)


def _lru_kernel(*refs, rev, n_c, n_s, tm):
    if rev:
        (cur_ref, prev_ref, next_ref, cw_ref, cb_ref, wa_ref, ba_ref, wi_ref, bi_ref, lam_ref, fwd_ref,
         o_ref, a_scr, b_scr, h_scr) = refs
    else:
        (cur_ref, prev_ref, next_ref, cw_ref, cb_ref, wa_ref, ba_ref, wi_ref, bi_ref, lam_ref,
         o_ref, a_scr, b_scr, h_scr) = refs
    s = pl.program_id(1)
    is_ctx = s < n_c
    if rev:
        tile = jnp.where(is_ctx, n_c - 1 - s, n_s - 1 - (s - n_c))
    else:
        tile = jnp.where(is_ctx, s, s - n_c)
    n_tile = jnp.where(is_ctx, n_c, n_s)

    @pl.when(s == 0)
    def _():
        h_scr[...] = jnp.zeros_like(h_scr)

    x = cur_ref[...].astype(F32)
    pv = jnp.where(tile == 0, 0.0, prev_ref[...].astype(F32)[SUBLANES:])
    nx = jnp.where(tile == n_tile - 1, 0.0, next_ref[...].astype(F32)[:SUBLANES])
    r8 = lax.broadcasted_iota(jnp.int32, (SUBLANES, D_MODEL), 0)

    def earlier(k):
        rolled = pltpu.roll(x, k, 0)
        top = jnp.where(r8 < k, pltpu.roll(pv, k, 0), rolled[:SUBLANES])
        return jnp.concatenate([top, rolled[SUBLANES:]], axis=0)

    rolled = pltpu.roll(x, tm - 1, 0)
    bot = jnp.where(r8 >= SUBLANES - 1, pltpu.roll(nx, SUBLANES - 1, 0), rolled[tm - SUBLANES:])
    later = jnp.concatenate([rolled[:tm - SUBLANES], bot], axis=0)
    cw = cw_ref[...]
    conv = earlier(2) * cw[0:1] + earlier(1) * cw[1:2] + x * cw[2:3] + later * cw[3:4] + cb_ref[...]

    conv16 = conv.astype(BF16)

    def block_diag(w_ref, b_ref):
        parts = [jnp.dot(conv16[:, j * MXU_DIM:(j + 1) * MXU_DIM], w_ref[j], preferred_element_type=F32)
                 for j in range(D_MODEL // MXU_DIM)]
        return jnp.concatenate(parts, axis=1) + b_ref[...]

    r = jax.nn.sigmoid(block_diag(wa_ref, ba_ref))
    gate_i = jax.nn.sigmoid(block_diag(wi_ref, bi_ref))
    neg_lam = -lam_ref[...]
    softplus = jnp.maximum(neg_lam, 0.0) + jnp.log1p(jnp.exp(-jnp.abs(neg_lam)))
    log_a = -LRU_C * r * softplus
    a = jnp.exp(log_a)
    mult = jnp.sqrt(1.0 - a * a)
    a_scr[...] = a
    b_scr[...] = mult * gate_i * conv

    def body(t, h):
        row = tm - 1 - t if rev else t
        h = a_scr[pl.ds(row, 1), :] * h + b_scr[pl.ds(row, 1), :]
        if rev:
            o_ref[pl.ds(row, 1), :] = fwd_ref[pl.ds(row, 1), :] + h
        else:
            o_ref[pl.ds(row, 1), :] = h
        return h

    h_scr[...] = lax.fori_loop(0, tm, body, h_scr[...], unroll=8)


def _rglru(u, conv_w, conv_b, wa_bd, ba, wi_bd, bi, lam, bsz, n_lat, n_ctx):
    tm = TM
    d = D_MODEL
    n_tok = u.shape[0]
    n_s, n_c = n_lat // tm, n_ctx // tm
    lat_tiles = bsz * n_s
    halo = 2 * SUBLANES
    per_halo = tm // halo
    last_halo = n_tok // halo - 1

    def make(rev, fwd):
        def blk(b, s):
            is_ctx = s < n_c
            if rev:
                tile = jnp.where(is_ctx, n_c - 1 - s, n_s - 1 - (s - n_c))
            else:
                tile = jnp.where(is_ctx, s, s - n_c)
            return jnp.where(is_ctx, lat_tiles + b * n_c + tile, b * n_s + tile)

        vec = pl.BlockSpec((None, 1, d), lambda b, s: (int(rev), 0, 0))
        wspec = pl.BlockSpec((None, d // MXU_DIM, MXU_DIM, MXU_DIM), lambda b, s: (int(rev), 0, 0, 0))
        row_spec = pl.BlockSpec((tm, d), lambda b, s: (blk(b, s), 0))
        in_specs = [
            pl.BlockSpec((tm, d), lambda b, s: (blk(b, s), CB_LX)),
            pl.BlockSpec((halo, d), lambda b, s: (jnp.maximum(blk(b, s) * per_halo - 1, 0), CB_LX)),
            pl.BlockSpec((halo, d), lambda b, s: (jnp.minimum((blk(b, s) + 1) * per_halo, last_halo), CB_LX)),
            pl.BlockSpec((CONV_W, d), lambda b, s: (0, 0)),
            pl.BlockSpec((1, d), lambda b, s: (0, 0)),
            wspec, vec, wspec, vec, vec,
        ]
        args = [u, u, u, conv_w, conv_b.reshape(1, d), wa_bd, ba.reshape(2, 1, d), wi_bd, bi.reshape(2, 1, d),
                lam.reshape(2, 1, d)]
        if rev:
            in_specs.append(row_spec)
            args.append(fwd)
        return pl.pallas_call(
            functools.partial(_lru_kernel, rev=rev, n_c=n_c, n_s=n_s, tm=tm),
            grid=(bsz, n_c + n_s),
            in_specs=in_specs,
            out_specs=row_spec,
            out_shape=jax.ShapeDtypeStruct((n_tok, d), F32),
            scratch_shapes=[pltpu.VMEM((tm, d), F32), pltpu.VMEM((tm, d), F32), pltpu.VMEM((1, d), F32)],
            input_output_aliases={10: 0} if rev else {},
            compiler_params=_cparams(("arbitrary", "arbitrary"), 40),
            name="rglru_bwd" if rev else "rglru_fwd",
        )(*args)

    fwd = make(False, None)
    return make(True, fwd)


def _chan_dft_kernel(u_ref, cs_ref, o_ref):
    x = u_ref[...].astype(BF16)
    cs = cs_ref[...]
    gd = FOUR_GROUP_DIM
    for g in range(FOUR_GROUPS):
        y = jnp.dot(x[:, g * gd:(g + 1) * gd], cs, preferred_element_type=F32)
        o_ref[:, g * gd:(g + 1) * gd] = y[:, :gd].astype(BF16)
        o_ref[:, D_MODEL + g * gd:D_MODEL + (g + 1) * gd] = y[:, gd:].astype(BF16)


def _chan_dft(u, cs_mat, row0, n_rows):
    tm = TM
    d = D_MODEL
    tile0 = row0 // tm
    return pl.pallas_call(
        _chan_dft_kernel,
        grid=(n_rows // tm,),
        in_specs=[pl.BlockSpec((tm, d), lambda i: (tile0 + i, CB_F)),
                  pl.BlockSpec((FOUR_GROUP_DIM, 2 * FOUR_GROUP_DIM), lambda i: (0, 0))],
        out_specs=pl.BlockSpec((tm, 2 * d), lambda i: (i, 0)),
        out_shape=jax.ShapeDtypeStruct((n_rows, 2 * d), BF16),
        compiler_params=_cparams(("parallel",), 32),
        name="fourier_channels",
    )(u, cs_mat)


def _chan_dft_split_kernel(u_ref, cs_ref, o_ref, y_scr, *, radix):
    x = u_ref[...].astype(BF16)
    cs = cs_ref[...]
    gd = FOUR_GROUP_DIM
    per_g = gd // LANES
    n_cos = D_MODEL // LANES
    for g in range(FOUR_GROUPS):
        y = jnp.dot(x[:, g * gd:(g + 1) * gd], cs, preferred_element_type=F32)
        for j in range(per_g):
            y_scr[g * per_g + j] = y[:, j * LANES:(j + 1) * LANES]
            y_scr[n_cos + g * per_g + j] = y[:, gd + j * LANES:gd + (j + 1) * LANES]
    rows = u_ref.shape[0] // radix
    for r in range(radix):
        for cb in range(2 * n_cos):
            o_ref[r, :, cb * LANES:(cb + 1) * LANES] = y_scr[cb, pl.ds(r, rows, stride=radix), :].astype(BF16)


def _chan_dft_split(u, cs_mat, bsz, n_lat, radix):
    tm = TM
    d = D_MODEL
    n_s = n_lat // tm
    return pl.pallas_call(
        functools.partial(_chan_dft_split_kernel, radix=radix),
        grid=(bsz, n_s),
        in_specs=[pl.BlockSpec((tm, d), lambda b, i: (b * n_s + i, CB_F)),
                  pl.BlockSpec((FOUR_GROUP_DIM, 2 * FOUR_GROUP_DIM), lambda b, i: (0, 0))],
        out_specs=pl.BlockSpec((None, radix, tm // radix, 2 * d), lambda b, i: (b, 0, i, 0)),
        out_shape=jax.ShapeDtypeStruct((bsz, radix, n_lat // radix, 2 * d), BF16),
        scratch_shapes=[pltpu.VMEM((2 * d // LANES, tm, LANES), F32)],
        compiler_params=_cparams(("parallel", "parallel"), 32),
        name="fourier_channels_split",
    )(u, cs_mat)


def _time_fft_kernel(zr_ref, zi_ref, cc_ref, pc_ref, ps_ref, o_ref, *, radix):
    cc = cc_ref[...]
    tk1 = cc.shape[0] // 2
    for r in range(radix):
        pr = jnp.dot(cc, zr_ref[r], preferred_element_type=F32)
        pi = jnp.dot(cc, zi_ref[r], preferred_element_type=F32)
        a_re = pr[:tk1] - pi[tk1:]
        a_im = pr[tk1:] + pi[:tk1]
        for k2 in range(radix):
            term = pc_ref[k2][:, r:r + 1] * a_re - ps_ref[k2][:, r:r + 1] * a_im
            if r == 0:
                o_ref[k2] = term
            else:
                o_ref[k2] += term


def _time_fft(yp, cc, pc, ps, radix, tk1, tn):
    bsz, _, t1, two_d = yp.shape
    d = two_d // 2
    nj = d // tn
    return pl.pallas_call(
        functools.partial(_time_fft_kernel, radix=radix),
        grid=(bsz, nj, t1 // tk1),
        in_specs=[
            pl.BlockSpec((None, radix, t1, tn), lambda b, j, i: (b, 0, 0, j)),
            pl.BlockSpec((None, radix, t1, tn), lambda b, j, i: (b, 0, 0, nj + j)),
            pl.BlockSpec((None, 2 * tk1, t1), lambda b, j, i: (i, 0, 0)),
            pl.BlockSpec((radix, tk1, radix), lambda b, j, i: (0, i, 0)),
            pl.BlockSpec((radix, tk1, radix), lambda b, j, i: (0, i, 0)),
        ],
        out_specs=pl.BlockSpec((None, radix, tk1, tn), lambda b, j, i: (b, 0, i, j)),
        out_shape=jax.ShapeDtypeStruct((bsz, radix, t1, d), F32),
        compiler_params=_cparams(("parallel", "parallel", "arbitrary"), 48),
        name="fourier_positions_fft",
    )(yp, yp, cc, pc, ps)


def _fft_tables(n, radix, tk1):
    t1 = n // radix
    k1 = jnp.arange(t1, dtype=jnp.int32)
    ang = ((k1[:, None] * k1[None, :]) % t1).astype(F32) * (2.0 * math.pi / t1)
    c = jnp.cos(ang).reshape(t1 // tk1, tk1, t1)
    s = jnp.sin(ang).reshape(t1 // tk1, tk1, t1)
    cc = jnp.concatenate([c, s], axis=1).astype(BF16)
    k2 = jnp.arange(radix, dtype=jnp.int32)
    r = jnp.arange(radix, dtype=jnp.int32)
    k = k1[None, :, None] + t1 * k2[:, None, None]
    ph = ((k * r[None, None, :]) % n).astype(F32) * (2.0 * math.pi / n)
    scale = float(n) ** -0.5
    return cc, jnp.cos(ph) * scale, jnp.sin(ph) * scale


def _time_dft_kernel(ct_ref, st_ref, yc_ref, ys_ref, o_ref, acc, *, scale):
    kk = pl.program_id(2)

    @pl.when(kk == 0)
    def _():
        acc[...] = jnp.zeros_like(acc)

    acc[...] += (jnp.dot(ct_ref[...], yc_ref[...], preferred_element_type=F32)
                 - jnp.dot(st_ref[...], ys_ref[...], preferred_element_type=F32))

    @pl.when(kk == pl.num_programs(2) - 1)
    def _():
        o_ref[...] = acc[...] * scale


def _time_dft(y, ct, st, bsz, n_pos, row0, tmk):
    d = D_MODEL
    nt = n_pos // tmk
    rb0 = row0 // tmk
    return pl.pallas_call(
        functools.partial(_time_dft_kernel, scale=float(n_pos) ** -0.5),
        grid=(bsz, nt, nt),
        in_specs=[
            pl.BlockSpec((tmk, tmk), lambda b, i, k: (i, k)),
            pl.BlockSpec((tmk, tmk), lambda b, i, k: (i, k)),
            pl.BlockSpec((tmk, d), lambda b, i, k: (rb0 + b * nt + k, 0)),
            pl.BlockSpec((tmk, d), lambda b, i, k: (rb0 + b * nt + k, 1)),
        ],
        out_specs=pl.BlockSpec((tmk, d), lambda b, i, k: (b * nt + i, 0)),
        out_shape=jax.ShapeDtypeStruct((bsz * n_pos, d), F32),
        scratch_shapes=[pltpu.VMEM((tmk, d), F32)],
        compiler_params=_cparams(("parallel", "parallel", "arbitrary"), 48),
        name="fourier_positions",
    )(ct, st, y, y)


def _dft_tables(n):
    k = jnp.arange(n, dtype=jnp.int32)
    ang = ((k[:, None] * k[None, :]) % n).astype(F32) * (2.0 * math.pi / n)
    return jnp.cos(ang).astype(BF16), jnp.sin(ang).astype(BF16)


def _merge_kernel(x_ref, attn_l_ref, attn_c_ref, rec_ref, four_l_ref, four_c_ref, lg_ref, g0_ref, g1_ref, g2_ref,
                  wa_ref, wl_ref, wf_ref, wo_ref, gate_ref, n2_ref, sh_ref, sc_ref,
                  xo_ref, h_ref, hlo_ref, hhi_ref, *, lat_tiles):
    def mm(a, w_ref):
        return jnp.dot(a.astype(BF16), w_ref[...], preferred_element_type=F32)

    is_lat = pl.program_id(0) < lat_tiles
    attn_o = mm(jnp.where(is_lat, attn_l_ref[...], attn_c_ref[...]), wa_ref)
    rec_o = mm(rec_ref[...] * jax.nn.gelu(lg_ref[...].astype(F32), approximate=True), wl_ref)
    four_o = mm(jnp.where(is_lat, four_l_ref[...], four_c_ref[...]), wf_ref)
    mixed = (jax.nn.sigmoid(g0_ref[...].astype(F32)) * attn_o + jax.nn.sigmoid(g1_ref[...].astype(F32)) * rec_o
             + jax.nn.sigmoid(g2_ref[...].astype(F32)) * four_o)
    x = x_ref[...] + gate_ref[...] * mm(mixed, wo_ref)
    xo_ref[...] = x
    y = x * lax.rsqrt(jnp.mean(x * x, axis=-1, keepdims=True) + EPS)
    h = (y * n2_ref[...]) * (1.0 + sc_ref[...]) + sh_ref[...]
    h_ref[...] = h.astype(BF16)
    hlo_ref[...] = _pack_halves(h[:, :D_MODEL // 2])
    hhi_ref[...] = _pack_halves(h[:, D_MODEL // 2:])


def _merge(x_tok, attn_l, attn_c, rec, four_l, four_c, u, w_attn_o, w_lru_o, w_four_o, w_out, mod4, norm2_g,
           row_of_tile, n_rows):
    tm = TM
    d = D_MODEL
    lat_tiles = attn_l.shape[0] // tm
    ctx_tiles = attn_c.shape[0] // tm
    row = pl.BlockSpec((tm, d), lambda i: (i, 0))
    lat_row = pl.BlockSpec((tm, d), lambda i: (jnp.minimum(i, lat_tiles - 1), 0))
    ctx_row = pl.BlockSpec((tm, d), lambda i: (jnp.clip(i - lat_tiles, 0, ctx_tiles - 1), 0))
    wspec = pl.BlockSpec((d, d), lambda i: (0, 0))

    def ucol(cb):
        return pl.BlockSpec((tm, d), lambda i: (i, cb))

    return pl.pallas_call(
        functools.partial(_merge_kernel, lat_tiles=lat_tiles),
        grid=(n_rows // tm,),
        in_specs=[row, lat_row, ctx_row, row, lat_row, ctx_row, ucol(CB_LG), ucol(CB_G), ucol(CB_G + 1),
                  ucol(CB_G + 2), wspec, wspec, wspec, wspec,
                  _mod_spec(2, row_of_tile), pl.BlockSpec((1, d), lambda i: (0, 0)),
                  _mod_spec(3, row_of_tile), _mod_spec(4, row_of_tile)],
        out_specs=[row, row, pl.BlockSpec((tm, d // 4), lambda i: (i, 0)),
                   pl.BlockSpec((tm, d // 4), lambda i: (i, 0))],
        out_shape=[jax.ShapeDtypeStruct((n_rows, d), F32), jax.ShapeDtypeStruct((n_rows, d), BF16),
                   jax.ShapeDtypeStruct((n_rows, d // 4), jnp.uint32),
                   jax.ShapeDtypeStruct((n_rows, d // 4), jnp.uint32)],
        compiler_params=_cparams(("parallel",), 56),
        name="merge_branches",
    )(x_tok, attn_l, attn_c, rec, four_l, four_c, u, u, u, u, w_attn_o, w_lru_o, w_four_o, w_out,
      mod4, norm2_g.reshape(1, d), mod4, mod4)


def _route_kernel(h_ref, rw_ref, rb_ref, tri_ref, idx_ref, wgt_ref, pos_ref, cnt_ref, base_scr):
    tm = h_ref.shape[0]

    @pl.when(pl.program_id(0) == 0)
    def _():
        base_scr[...] = jnp.zeros_like(base_scr)

    logits = lax.dot_general(rw_ref[...], h_ref[...], (((1,), (1,)), ((), ())), preferred_element_type=F32)
    scores = jax.nn.sigmoid(logits)
    biased = scores + rb_ref[...]
    neg = -jnp.inf
    e_iota = lax.broadcasted_iota(jnp.int32, (N_EXPERTS, tm), 0)
    g_iota = lax.broadcasted_iota(jnp.int32, (N_GROUPS, tm), 0)

    rows = []
    for g in range(N_GROUPS):
        xg = biased[g * GROUP_SIZE:(g + 1) * GROUP_SIZE]
        m1 = jnp.max(xg, axis=0, keepdims=True)
        is_max = xg == m1
        n_max = jnp.sum(is_max.astype(F32), axis=0, keepdims=True)
        m2 = jnp.max(jnp.where(is_max, neg, xg), axis=0, keepdims=True)
        rows.append(m1 + jnp.where(n_max >= 2.0, m1, m2))
    grp = jnp.concatenate(rows, axis=0)

    g_sel = jnp.zeros((N_GROUPS, tm), jnp.bool_)
    work = grp
    for _ in range(TOPK_GROUPS):
        m = jnp.max(work, axis=0, keepdims=True)
        first = jnp.min(jnp.where(work == m, g_iota, N_GROUPS), axis=0, keepdims=True)
        hit = g_iota == first
        g_sel = jnp.logical_or(g_sel, hit)
        work = jnp.where(hit, neg, work)
    g_sel_f = g_sel.astype(F32)
    mask_rows = [jnp.broadcast_to(g_sel_f[g:g + 1], (GROUP_SIZE, tm)) for g in range(N_GROUPS)]
    e_mask = jnp.concatenate(mask_rows, axis=0) > 0.5

    work = jnp.where(e_mask, biased, neg)
    sel = jnp.zeros((N_EXPERTS, tm), F32)
    idx_rows, w_rows, hits = [], [], []
    for _ in range(TOP_K):
        m = jnp.max(work, axis=0, keepdims=True)
        first = jnp.min(jnp.where(work == m, e_iota, N_EXPERTS), axis=0, keepdims=True)
        hit = e_iota == first
        hits.append(hit)
        idx_rows.append(first)
        w_rows.append(jnp.sum(jnp.where(hit, scores, 0.0), axis=0, keepdims=True))
        sel = jnp.where(hit, 1.0, sel)
        work = jnp.where(hit, neg, work)
    w = jnp.concatenate(w_rows, axis=0)
    wgt_ref[...] = w / jnp.sum(w, axis=0, keepdims=True) * ROUTED_SCALE
    idx_ref[...] = jnp.concatenate(idx_rows, axis=0)

    before = jnp.dot(sel.astype(BF16), tri_ref[...], preferred_element_type=F32)
    rank = base_scr[...] + before
    pos_rows = [jnp.sum(jnp.where(hit, rank, 0.0), axis=0, keepdims=True) for hit in hits]
    pos_ref[...] = jnp.concatenate(pos_rows, axis=0).astype(jnp.int32)
    base_scr[...] = base_scr[...] + jnp.sum(sel, axis=1, keepdims=True)
    cnt_ref[...] = jnp.broadcast_to(base_scr[...], cnt_ref.shape)


def _route(h16, rw_t, router_b, tm):
    n_tok, d = h16.shape
    tri = (jnp.arange(tm)[:, None] < jnp.arange(tm)[None, :]).astype(BF16)
    out_col = pl.BlockSpec((TOP_K, tm), lambda i: (0, i))
    return pl.pallas_call(
        _route_kernel,
        grid=(n_tok // tm,),
        in_specs=[pl.BlockSpec((tm, d), lambda i: (i, 0)),
                  pl.BlockSpec((N_EXPERTS, d), lambda i: (0, 0)),
                  pl.BlockSpec((N_EXPERTS, 1), lambda i: (0, 0)),
                  pl.BlockSpec((tm, tm), lambda i: (0, 0))],
        out_specs=[out_col, out_col, out_col, pl.BlockSpec((N_EXPERTS, LANES), lambda i: (0, 0))],
        out_shape=[jax.ShapeDtypeStruct((TOP_K, n_tok), jnp.int32), jax.ShapeDtypeStruct((TOP_K, n_tok), F32),
                   jax.ShapeDtypeStruct((TOP_K, n_tok), jnp.int32),
                   jax.ShapeDtypeStruct((N_EXPERTS, LANES), F32)],
        scratch_shapes=[pltpu.VMEM((N_EXPERTS, 1), F32)],
        compiler_params=_cparams(("arbitrary",), 32),
        name="moe_router",
    )(h16, rw_t, router_b.reshape(N_EXPERTS, 1), tri)


def _slot_kernel(idx_ref, pos_ref, offs_ref, dest_ref):
    tm = idx_ref.shape[1]
    e_iota = lax.broadcasted_iota(jnp.int32, (N_EXPERTS, tm), 0)
    idx = idx_ref[...]
    offs = offs_ref[...]
    rows = [jnp.sum(jnp.where(e_iota == idx[r:r + 1], offs, 0), axis=0, keepdims=True) for r in range(TOP_K)]
    dest_ref[...] = jnp.concatenate(rows, axis=0) + pos_ref[...]


def _slots(top_idx, pos, offs, tm):
    n_tok = top_idx.shape[1]
    col = pl.BlockSpec((TOP_K, tm), lambda i: (0, i))
    return pl.pallas_call(
        _slot_kernel,
        grid=(n_tok // tm,),
        in_specs=[col, col, pl.BlockSpec((N_EXPERTS, 1), lambda i: (0, 0))],
        out_specs=col,
        out_shape=jax.ShapeDtypeStruct((TOP_K, n_tok), jnp.int32),
        compiler_params=_cparams(("parallel",), 32),
        name="moe_slots",
    )(top_idx, pos, offs.reshape(N_EXPERTS, 1))


def _sc_scatter_rows(src, idx, n_out):
    n_src, width = src.shape
    n_idx = idx.shape[0]
    n_win = n_src // SC_WINDOW
    mesh = plsc.VectorSubcoreMesh(core_axis_name="core", subcore_axis_name="subcore")

    @pl.kernel(out_type=jax.ShapeDtypeStruct((n_out, width), src.dtype), mesh=mesh, scratch_types=[],
               name="moe_scatter_sc")
    def scatter(src_hbm, idx_hbm, out_hbm):
        def body(src_vmem, idx_vmem):
            pltpu.sync_copy(src_vmem, out_hbm.at[idx_vmem.at[0]])

        pltpu.emit_pipeline(
            body,
            grid=(n_idx // SC_WINDOW,),
            in_specs=[pl.BlockSpec((SC_WINDOW, width), index_map=lambda g: (g % n_win, 0)),
                      pl.BlockSpec((1, SC_WINDOW), index_map=lambda g: (0, g))],
            out_specs=[],
            core_axis_name=("core", "subcore"),
            dimension_semantics=(pltpu.PARALLEL,),
        )(src_hbm, idx_hbm)

    return scatter(src, idx.reshape(1, n_idx))


def _expert_kernel(be_ref, nu_ref, nv_ref, xlo_ref, xhi_ref, wg_ref, wu_ref, wd_ref, ylo_ref, yhi_ref,
                   wg_s, wu_s, wd_s):
    i = pl.program_id(0)
    used = i < nu_ref[0]
    prev_e = be_ref[jnp.maximum(i - 1, 0)]
    fresh = jnp.logical_or(i == 0, be_ref[i] != prev_e)

    @pl.when(jnp.logical_and(used, fresh))
    def _():
        wg_s[...] = wg_ref[0].astype(BF16)
        wu_s[...] = wu_ref[0].astype(BF16)
        wd_s[...] = wd_ref[0].astype(BF16)

    @pl.when(used)
    def _():
        row = lax.broadcasted_iota(jnp.int32, xlo_ref.shape, 0)
        live = row < nv_ref[i]
        half = D_MODEL // 2
        quarter = D_MODEL // 4
        parts = []
        for ref in (xlo_ref, xhi_ref):
            parts.extend(p.astype(BF16) for p in _unpack_halves(jnp.where(live, ref[...], jnp.uint32(0))))

        def proj(w_s):
            acc = jnp.dot(parts[0], w_s[:quarter, :], preferred_element_type=F32)
            for k in range(1, 4):
                acc = acc + jnp.dot(parts[k], w_s[k * quarter:(k + 1) * quarter, :], preferred_element_type=F32)
            return acc

        g = proj(wg_s)
        act = (_silu(g) * proj(wu_s)).astype(BF16)
        y = jnp.dot(act, wd_s[...], preferred_element_type=F32)
        ylo_ref[...] = _pack_halves(y[:, :half])
        yhi_ref[...] = _pack_halves(y[:, half:])


def _experts(xs_lo, xs_hi, block_e, n_used, n_valid, w_gate, w_up, w_down, layer, bm):
    n_slots, quarter = xs_lo.shape
    half = 2 * quarter
    d = D_MODEL

    def blk(i, be, nu, nv):
        return (jnp.minimum(i, nu[0] - 1), 0)

    def wsel(i, be, nu, nv):
        return (layer, be[i], 0, 0)

    grid_spec = pltpu.PrefetchScalarGridSpec(
        num_scalar_prefetch=3,
        grid=(n_slots // bm,),
        in_specs=[pl.BlockSpec((bm, quarter), blk),
                  pl.BlockSpec((bm, quarter), blk),
                  pl.BlockSpec((None, 1, d, EXPERT_FF), wsel),
                  pl.BlockSpec((None, 1, d, EXPERT_FF), wsel),
                  pl.BlockSpec((None, 1, EXPERT_FF, d), wsel)],
        out_specs=[pl.BlockSpec((bm, half // 2), blk), pl.BlockSpec((bm, half // 2), blk)],
        scratch_shapes=[pltpu.VMEM((d, EXPERT_FF), BF16), pltpu.VMEM((d, EXPERT_FF), BF16),
                        pltpu.VMEM((EXPERT_FF, d), BF16)],
    )
    y_shape = jax.ShapeDtypeStruct((n_slots, half // 2), jnp.uint32)
    return pl.pallas_call(
        _expert_kernel,
        grid_spec=grid_spec,
        out_shape=[y_shape, y_shape],
        compiler_params=_cparams(("arbitrary",), 40),
        name="moe_experts",
    )(block_e, n_used, n_valid, xs_lo, xs_hi, w_gate, w_up, w_down)


def _sc_gather_rows(table, idx):
    n_idx = idx.shape[0]
    width = table.shape[1]
    mesh = plsc.VectorSubcoreMesh(core_axis_name="core", subcore_axis_name="subcore")

    @pl.kernel(out_type=jax.ShapeDtypeStruct((n_idx, width), table.dtype), mesh=mesh, scratch_types=[],
               name="moe_gather_sc")
    def gather(table_hbm, idx_hbm, out_hbm):
        def body(idx_vmem, out_vmem):
            pltpu.sync_copy(table_hbm.at[idx_vmem.at[0]], out_vmem)

        pltpu.emit_pipeline(
            body,
            grid=(n_idx // SC_WINDOW,),
            in_specs=[pl.BlockSpec((1, SC_WINDOW), index_map=lambda i: (0, i))],
            out_specs=[pl.BlockSpec((SC_WINDOW, width), index_map=lambda i: (i, 0))],
            core_axis_name=("core", "subcore"),
            dimension_semantics=(pltpu.PARALLEL,),
        )(idx_hbm, out_hbm)

    return gather(table, idx.reshape(1, n_idx))


def _combine_kernel(x_ref, h_ref, tw_ref, gate_ref, wsg_ref, wsu_ref, wsd_ref, lo_ref, hi_ref, o_ref):
    quarter = D_MODEL // 4
    h = h_ref[...]
    act = (_silu(jnp.dot(h, wsg_ref[...], preferred_element_type=F32))
           * jnp.dot(h, wsu_ref[...], preferred_element_type=F32)).astype(BF16)
    shared = jnp.dot(act, wsd_ref[...], preferred_element_type=F32)

    tw = tw_ref[...]
    gate = gate_ref[...]
    for q, buf in enumerate((lo_ref, hi_ref)):
        moe_a = moe_b = None
        for r in range(TOP_K):
            ya, yb = _unpack_halves(buf[r])
            wr = tw[:, r:r + 1]
            moe_a = ya * wr if moe_a is None else moe_a + ya * wr
            moe_b = yb * wr if moe_b is None else moe_b + yb * wr
        for part, moe in ((2 * q, moe_a), (2 * q + 1, moe_b)):
            cols = slice(part * quarter, (part + 1) * quarter)
            o_ref[:, cols] = x_ref[:, cols] + gate[:, cols] * (moe + shared[:, cols])


def _combine(x_tok, h16, g_lo, g_hi, top_w, mod4, ws_gate, ws_up, ws_down, row_of_tile):
    n_tok, d = x_tok.shape
    tm = TM
    ff = ws_gate.shape[1]
    row = pl.BlockSpec((tm, d), lambda i: (i, 0))
    gat = pl.BlockSpec((TOP_K, tm, d // 4), lambda i: (0, i, 0))
    return pl.pallas_call(
        _combine_kernel,
        grid=(n_tok // tm,),
        in_specs=[row, row,
                  pl.BlockSpec((tm, TOP_K), lambda i: (i, 0)),
                  _mod_spec(5, row_of_tile),
                  pl.BlockSpec((d, ff), lambda i: (0, 0)),
                  pl.BlockSpec((d, ff), lambda i: (0, 0)),
                  pl.BlockSpec((ff, d), lambda i: (0, 0)),
                  gat, gat],
        out_specs=row,
        out_shape=jax.ShapeDtypeStruct((n_tok, d), F32),
        compiler_params=_cparams(("parallel",), 48),
        name="moe_combine",
    )(x_tok, h16, top_w, mod4, ws_gate, ws_up, ws_down, g_lo, g_hi)


def _moe(x_tok, h16, h_lo, h_hi, mod4, row_of_tile, rw_t, router_b, w_gate, w_up, w_down, layer, ws_gate, ws_up, ws_down):
    n_tok = x_tok.shape[0]
    bm = MOE_BM
    tm_r = TM_ROUTE if n_tok % TM_ROUTE == 0 else TM
    top_idx, top_w, pos, cnt = _route(h16, rw_t, router_b, tm_r)

    counts = cnt[:, 0].astype(jnp.int32)
    padded = (counts + bm - 1) // bm * bm
    ends = jnp.cumsum(padded)
    n_blocks = -(-(n_tok * TOP_K + N_EXPERTS * (bm - 1)) // bm)
    offs = ends - padded
    dest = _slots(top_idx, pos, offs, tm_r)
    block_start = jnp.arange(n_blocks, dtype=jnp.int32) * bm
    block_e = jnp.minimum(jnp.sum((ends[None, :] <= block_start[:, None]).astype(jnp.int32), axis=1),
                          N_EXPERTS - 1)
    n_valid = jnp.clip(counts[block_e] - (block_start - offs[block_e]), 0, bm).astype(jnp.int32)
    n_used = (ends[-1] // bm).astype(jnp.int32).reshape(1)
    dest_flat = dest.reshape(TOP_K * n_tok)
    xs_lo = _sc_scatter_rows(h_lo, dest_flat, n_blocks * bm)
    xs_hi = _sc_scatter_rows(h_hi, dest_flat, n_blocks * bm)
    ys_lo, ys_hi = _experts(xs_lo, xs_hi, block_e, n_used, n_valid, w_gate, w_up, w_down, layer, bm)
    g_lo = _sc_gather_rows(ys_lo, dest_flat).reshape(TOP_K, n_tok, D_MODEL // 4)
    g_hi = _sc_gather_rows(ys_hi, dest_flat).reshape(TOP_K, n_tok, D_MODEL // 4)
    return _combine(x_tok, h16, g_lo, g_hi, top_w.T, mod4, ws_gate, ws_up, ws_down, row_of_tile)


def _block_diag_tiles(w):
    per = MXU_DIM // LRU_BLOCK_DIM
    w = w.reshape(2, D_MODEL // MXU_DIM, per, LRU_BLOCK_DIM, LRU_BLOCK_DIM)
    eye = jnp.eye(per, dtype=w.dtype)
    t = jnp.einsum('ztpde,pq->ztpdqe', w, eye)
    return t.reshape(2, D_MODEL // MXU_DIM, MXU_DIM, MXU_DIM).astype(BF16)


def _rope_tables(n_lat, n_ctx):
    rows = n_lat // GRID_W
    row = jnp.repeat(jnp.arange(rows), GRID_W)
    col = jnp.tile(jnp.arange(GRID_W), rows)
    inv = ROPE_THETA ** (-jnp.arange(ROPE_FREQS, dtype=F32) / ROPE_FREQS)
    ang = jnp.stack([row[:, None] * inv, col[:, None] * inv], axis=1)
    ang = jnp.concatenate([ang, ang], axis=2).reshape(n_lat, HEAD_DIM)
    ang = jnp.tile(ang, (1, LANES // HEAD_DIM))
    ang = jnp.concatenate([ang, jnp.zeros((n_ctx, LANES), F32)], axis=0)
    lane = jnp.arange(LANES)
    sign = jnp.where((lane % (2 * ROPE_FREQS)) < ROPE_FREQS, -1.0, 1.0).astype(F32)
    return jnp.cos(ang), jnp.sin(ang) * sign


def kernel(x, c, ctx, c_ctx, w_mod, b_mod, norm1_g, w_in, q_norm_g, k_norm_g, lambda_qk, subln_g, w_attn_o, conv_w, conv_b, lru_wa, lru_ba, lru_wi, lru_bi, lru_lambda, w_lru_o, w_four_o, w_out, norm2_g, router_w, router_b, exp_w_gate, exp_w_up, exp_w_down, sh_w_gate, sh_w_up, sh_w_down):
    bsz, n_lat, d = x.shape
    n_ctx = ctx.shape[1]
    depth = w_mod.shape[0]
    n_latent_rows = bsz * n_lat
    n_tok = n_latent_rows + bsz * n_ctx
    assert d == D_MODEL and bsz + 1 <= MOD_ROWS
    assert n_lat % TM == 0 and n_ctx % TM == 0 and n_lat % GRID_W == 0
    tm_in = TM_IN if (n_lat % TM_IN == 0 and (bsz * n_ctx) % TM_IN == 0) else TM

    cvec = jnp.zeros((MOD_ROWS, d), F32).at[:bsz].set(c).at[bsz].set(c_ctx)
    mods = _modulation(cvec, w_mod, b_mod)

    def row_of_tile_fn(tm):
        per_batch = n_lat // tm
        return lambda i: jnp.minimum(i // per_batch, bsz)

    assert n_lat % n_ctx == 0
    cos_t, sin_t = _rope_tables(n_lat, n_ctx)
    group_mat = jnp.kron(jnp.eye(d // HEAD_DIM, dtype=F32), jnp.ones((HEAD_DIM, HEAD_DIM), F32)).astype(BF16)
    m = jnp.arange(FOUR_GROUP_DIM, dtype=jnp.int32)
    ang_c = ((m[:, None] * m[None, :]) % FOUR_GROUP_DIM).astype(F32) * (2.0 * math.pi / FOUR_GROUP_DIM)
    inv_sqrt_c = FOUR_GROUP_DIM ** -0.5
    cs_mat = jnp.concatenate([jnp.cos(ang_c) * inv_sqrt_c, jnp.sin(ang_c) * inv_sqrt_c], axis=1).astype(BF16)
    assert n_lat % (FFT_RADIX * MXU_DIM) == 0 or n_lat // FFT_RADIX < MXU_DIM
    fft_tk1 = min(MXU_DIM, n_lat // FFT_RADIX)
    fft_cc, fft_pc, fft_ps = _fft_tables(n_lat, FFT_RADIX, fft_tk1)
    ct_ctx, st_ctx = _dft_tables(n_ctx)
    q_sub = math.gcd(Q_SUB, n_lat // TQ)
    n_keys = n_ctx + n_lat
    tk = TK if n_keys % TK == 0 else TM

    x_tok = jnp.concatenate([x.reshape(n_latent_rows, d), ctx.reshape(bsz * n_ctx, d)], axis=0)
    for l in range(depth):
        last = l == depth - 1
        lam_init = 0.8 - 0.6 * math.exp(-0.3 * l)
        mod4 = mods[l].reshape(MOD_ROWS, 6, 1, d)

        u = _in_projection(x_tok, mod4, norm1_g[l], w_in[l].astype(BF16), row_of_tile_fn(tm_in), tm_in)

        k_all, q_all, vt_all = _qkv_prepare(u, k_norm_g[l], q_norm_g[l], group_mat, cos_t, sin_t,
                                            bsz, n_lat, n_ctx)
        attn_l = _diff_attention(q_all, k_all, vt_all, lambda_qk[l], subln_g[l], lam_init,
                                 0, n_lat, 0, n_keys, TQ, q_sub, tk)
        attn_c = attn_l
        if not last:
            attn_c = _diff_attention(q_all, k_all, vt_all, lambda_qk[l], subln_g[l], lam_init,
                                     n_lat, n_ctx, n_lat, n_ctx, TM, 1, TM)

        rec = _rglru(u, conv_w[l], conv_b[l], _block_diag_tiles(lru_wa[l]), lru_ba[l],
                     _block_diag_tiles(lru_wi[l]), lru_bi[l], lru_lambda[l], bsz, n_lat, n_ctx)

        y_lat = _chan_dft_split(u, cs_mat, bsz, n_lat, FFT_RADIX)
        four_l = _time_fft(y_lat, fft_cc, fft_pc, fft_ps, FFT_RADIX, fft_tk1, MXU_DIM).reshape(n_latent_rows, d)
        four_c = four_l
        if not last:
            y_ctx = _chan_dft(u, cs_mat, n_latent_rows, bsz * n_ctx)
            four_c = _time_dft(y_ctx, ct_ctx, st_ctx, bsz, n_ctx, 0, n_ctx)

        n_rows = n_latent_rows if last else n_tok
        row_of_tile = row_of_tile_fn(TM)
        x_mid, h16, h_lo, h_hi = _merge(x_tok, attn_l, attn_c, rec, four_l, four_c, u, w_attn_o[l].astype(BF16),
                                w_lru_o[l].astype(BF16), w_four_o[l].astype(BF16), w_out[l].astype(BF16),
                                mod4, norm2_g[l], row_of_tile, n_rows)
        x_tok = _moe(x_mid, h16, h_lo, h_hi, mod4, row_of_tile, router_w[l].T.astype(BF16), router_b[l],
                     exp_w_gate, exp_w_up, exp_w_down, l,
                     sh_w_gate[l].astype(BF16), sh_w_up[l].astype(BF16), sh_w_down[l].astype(BF16))
    return x_tok[:n_latent_rows].reshape(bsz, n_lat, d)
```

```python
import functools
import math

import jax
import jax.numpy as jnp
from jax import lax
from jax.experimental import pallas as pl
from jax.experimental.pallas import tpu as pltpu
from jax.experimental.pallas import tpu_sc as plsc

F32 = jnp.float32
BF16 = jnp.bfloat16

D_MODEL = 1024
EPS = 1e-6
GRID_W = 64

N_HEADS = 8
HEAD_DIM = 64
V_DIM = 2 * HEAD_DIM
ATTN_SCALE = HEAD_DIM ** -0.5
ROPE_THETA = 10000.0
ROPE_FREQS = HEAD_DIM // 4

LRU_BLOCKS = 16
LRU_BLOCK_DIM = D_MODEL // LRU_BLOCKS
LRU_C = 8.0
CONV_W = 4

FOUR_GROUPS = 4
FOUR_GROUP_DIM = D_MODEL // FOUR_GROUPS

CB_K, CB_V, CB_LX, CB_Q, CB_LG, CB_F, CB_G = 0, 1, 2, 3, 4, 5, 6

N_EXPERTS = 256
TOP_K = 8
N_GROUPS = 8
GROUP_SIZE = N_EXPERTS // N_GROUPS
TOPK_GROUPS = 4
EXPERT_FF = 256
ROUTED_SCALE = 2.5

LANES = 128
SUBLANES = 8
MXU_DIM = 256
VMEM_BYTES_V7X = 64 * 1024 * 1024

MOD_ROWS = 8
TM = 256
TM_IN = 512
IN_PANEL = 3
TQ = 256
Q_SUB = 32
TK = 768
TM_ROUTE = 512
MOE_BM = 512
W_PIECES = 4
SC_WINDOW = 128
FFT_RADIX = 8
LOG2E = 1.4426950408889634


def _cparams(sem, vmem_mb):
    vmem_bytes = int(vmem_mb * 1024 * 1024)
    assert vmem_bytes < VMEM_BYTES_V7X
    return pltpu.CompilerParams(dimension_semantics=sem, vmem_limit_bytes=vmem_bytes)


def _silu(x):
    return x * jax.nn.sigmoid(x)


def _pack_halves(x):
    half = x.shape[1] // 2
    bits = pltpu.bitcast(x.astype(BF16).astype(F32), jnp.uint32)
    return (bits[:, :half] & jnp.uint32(0xFFFF0000)) | (bits[:, half:] >> 16)


def _unpack_halves(w):
    return pltpu.bitcast(w & jnp.uint32(0xFFFF0000), F32), pltpu.bitcast(w << 16, F32)


def _mod_kernel(c_ref, w_ref, b_ref, o_ref):
    c = c_ref[...]
    s = _silu(c).astype(BF16)
    o_ref[0] = jnp.dot(s, w_ref[0].astype(BF16), preferred_element_type=F32) + b_ref[0]


def _modulation(cvec, w_mod, b_mod):
    n_layers, d, six_d = w_mod.shape
    nj = six_d // d
    return pl.pallas_call(
        _mod_kernel,
        grid=(n_layers, nj),
        in_specs=[
            pl.BlockSpec((MOD_ROWS, d), lambda l, j: (0, 0)),
            pl.BlockSpec((1, d, d), lambda l, j: (l, 0, j)),
            pl.BlockSpec((1, 1, d), lambda l, j: (l, 0, j)),
        ],
        out_specs=pl.BlockSpec((1, MOD_ROWS, d), lambda l, j: (l, 0, j)),
        out_shape=jax.ShapeDtypeStruct((n_layers, MOD_ROWS, six_d), F32),
        compiler_params=_cparams(("parallel", "parallel"), 32),
        name="modulation",
    )(cvec, w_mod, b_mod.reshape(n_layers, 1, six_d))


def _mod_spec(part, row_of_tile):
    return pl.BlockSpec((None, None, 1, D_MODEL), lambda i: (row_of_tile(i), part, 0, 0))


def _inproj_kernel(x_ref, g_ref, sh_ref, sc_ref, w_ref, o_ref):
    x = x_ref[...]
    y = x * lax.rsqrt(jnp.mean(x * x, axis=-1, keepdims=True) + EPS)
    h = (y * g_ref[...]) * (1.0 + sc_ref[...]) + sh_ref[...]
    o_ref[...] = jnp.dot(h.astype(BF16), w_ref[...], preferred_element_type=F32).astype(o_ref.dtype)


def _in_projection(x_tok, mod4, norm_g, w_in_bf16, row_of_tile, tm):
    n_tok, d = x_tok.shape
    n_cols = w_in_bf16.shape[1]
    tn = IN_PANEL * d
    row_of = lambda j, i: row_of_tile(i)
    return pl.pallas_call(
        _inproj_kernel,
        grid=(n_cols // tn, n_tok // tm),
        in_specs=[
            pl.BlockSpec((tm, d), lambda j, i: (i, 0)),
            pl.BlockSpec((1, d), lambda j, i: (0, 0)),
            pl.BlockSpec((None, None, 1, d), lambda j, i: (row_of(j, i), 0, 0, 0)),
            pl.BlockSpec((None, None, 1, d), lambda j, i: (row_of(j, i), 1, 0, 0)),
            pl.BlockSpec((d, tn), lambda j, i: (0, j)),
        ],
        out_specs=pl.BlockSpec((tm, tn), lambda j, i: (i, j)),
        out_shape=jax.ShapeDtypeStruct((n_tok, n_cols), BF16),
        compiler_params=_cparams(("parallel", "parallel"), 48),
        name="in_projection",
    )(x_tok, norm_g.reshape(1, d), mod4, mod4, w_in_bf16)


def _prep_kernel(uk_ref, uq_ref, uv_ref, kg_ref, qg_ref, gm_ref, cos_ref, sin_ref, k_out, q_out, vt_out, *, tm):
    gm = gm_ref[...]
    lane = lax.broadcasted_iota(jnp.int32, (tm, LANES), 1)
    first_half = (lane % (2 * ROPE_FREQS)) < ROPE_FREQS

    def norm_rope(x, g):
        ss = jnp.dot((x * x).astype(BF16), gm, preferred_element_type=F32) * (1.0 / HEAD_DIM)
        y = x * lax.rsqrt(ss + EPS) * g
        c = cos_ref[...]
        s = sin_ref[...]
        outs = []
        for j in range(D_MODEL // LANES):
            yt = y[:, j * LANES:(j + 1) * LANES]
            rot = jnp.where(first_half, pltpu.roll(yt, LANES - ROPE_FREQS, 1), pltpu.roll(yt, ROPE_FREQS, 1))
            outs.append(yt * c + rot * s)
        return jnp.concatenate(outs, axis=1)

    k_out[...] = norm_rope(uk_ref[...].astype(F32), kg_ref[...]).astype(BF16)
    qn = norm_rope(uq_ref[...].astype(F32), qg_ref[...]) * (ATTN_SCALE * LOG2E)
    lo = lane < HEAD_DIM
    for h in range(N_HEADS):
        qt = qn[:, h * LANES:(h + 1) * LANES]
        q_out[:, (2 * h) * LANES:(2 * h + 1) * LANES] = jnp.where(lo, qt, 0.0).astype(BF16)
        q_out[:, (2 * h + 1) * LANES:(2 * h + 2) * LANES] = jnp.where(lo, 0.0, qt).astype(BF16)
    vt_out[...] = uv_ref[...].astype(F32).T.astype(BF16)


def _qkv_prepare(u, k_gain, q_gain, group_mat, cos_t, sin_t, bsz, n_lat, n_ctx):
    tm = TM
    d = D_MODEL
    n_s, n_c = n_lat // tm, n_ctx // tm
    n_rows = n_lat + n_ctx
    lat_tiles = bsz * n_s
    kg = jnp.tile(k_gain, d // HEAD_DIM).reshape(1, d)
    qg = jnp.tile(q_gain, d // HEAD_DIM).reshape(1, d)

    def tok_tile(b, j):
        return jnp.where(j < n_s, b * n_s + j, lat_tiles + b * n_c + (j - n_s))

    def u_spec(cb):
        return pl.BlockSpec((tm, d), lambda b, j: (tok_tile(b, j), cb))

    vec = pl.BlockSpec((1, d), lambda b, j: (0, 0))
    rope_spec = pl.BlockSpec((tm, LANES), lambda b, j: (j, 0))
    return pl.pallas_call(
        functools.partial(_prep_kernel, tm=tm),
        grid=(bsz, n_s + n_c),
        in_specs=[u_spec(CB_K), u_spec(CB_Q), u_spec(CB_V), vec, vec,
                  pl.BlockSpec((d, d), lambda b, j: (0, 0)), rope_spec, rope_spec],
        out_specs=[
            pl.BlockSpec((None, tm, d), lambda b, j: (b, j, 0)),
            pl.BlockSpec((None, tm, 2 * d), lambda b, j: (b, j, 0)),
            pl.BlockSpec((None, d, tm), lambda b, j: (b, 0, j)),
        ],
        out_shape=[jax.ShapeDtypeStruct((bsz, n_rows, d), BF16),
                   jax.ShapeDtypeStruct((bsz, n_rows, 2 * d), BF16),
                   jax.ShapeDtypeStruct((bsz, d, n_rows), BF16)],
        compiler_params=_cparams(("parallel", "parallel"), 40),
        name="qkv_prepare",
    )(u, u, u, kg, qg, group_mat, cos_t, sin_t)


def _attn_kernel(lq_ref, sg_ref, q1_ref, q2_ref, k_ref, vt_ref, o_ref, s_buf, *, n_keys, tk, tq, lam_init):
    lq = lq_ref[...]
    lam = (jnp.exp(jnp.sum(lq[0:1] * lq[1:2], axis=1, keepdims=True))
           - jnp.exp(jnp.sum(lq[2:3] * lq[3:4], axis=1, keepdims=True)) + lam_init)
    n_chunks = n_keys // tk
    n_sub = q1_ref.shape[0] // tq

    def load_q(j):
        rows = pl.ds(pl.multiple_of(j * tq, tq), tq)
        return jnp.concatenate([q1_ref[rows, :], q2_ref[rows, :]], axis=0)

    def score_chunk(q, c, mx):
        s = lax.dot_general(k_ref[c * tk:(c + 1) * tk, :], q, (((1,), (1,)), ((), ())),
                            preferred_element_type=F32)
        s_buf[c * tk:(c + 1) * tk, :] = s
        cm = jnp.max(s, axis=0, keepdims=True)
        return cm if mx is None else jnp.maximum(mx, cm)

    q0 = load_q(0)
    mx0 = None
    for c in range(n_chunks):
        mx0 = score_chunk(q0, c, mx0)

    def sub_tile(j, mx):
        q_next = load_q(jnp.minimum(j + 1, n_sub - 1))
        mx_next = None
        l = jnp.zeros((1, 2 * tq), F32)
        acc = jnp.zeros((V_DIM, 2 * tq), F32)
        for c in range(n_chunks):
            p = jnp.exp2(s_buf[c * tk:(c + 1) * tk, :] - mx)
            l = l + jnp.sum(p, axis=0, keepdims=True)
            acc = acc + jnp.dot(vt_ref[:, c * tk:(c + 1) * tk], p.astype(BF16), preferred_element_type=F32)
            mx_next = score_chunk(q_next, c, mx_next)
        o = acc / l
        o = o[:, :tq] - lam * o[:, tq:]
        ms = jnp.mean(o * o, axis=0, keepdims=True)
        on = (o * lax.rsqrt(ms + EPS) * sg_ref[...]) * (1.0 - lam_init)
        o_ref[pl.ds(pl.multiple_of(j * tq, tq), tq), :] = on.T.astype(o_ref.dtype)
        return mx_next

    lax.fori_loop(0, n_sub, sub_tile, mx0)


def _diff_attention(q_all, k_all, vt_all, lambda_qk, subln_g, lam_init, q_row0, n_q, key_row0, n_keys, tq, n_sub, tk):
    bsz = q_all.shape[0]
    d = D_MODEL
    tstep = tq * n_sub
    nq = n_q // tstep
    qb0 = q_row0 // tstep
    kb0 = key_row0 // n_keys
    s_bytes = n_keys * 2 * tq * 4
    return pl.pallas_call(
        functools.partial(_attn_kernel, n_keys=n_keys, tk=tk, tq=tq, lam_init=lam_init),
        grid=(bsz, N_HEADS, nq),
        in_specs=[
            pl.BlockSpec((4, HEAD_DIM), lambda b, h, i: (0, 0)),
            pl.BlockSpec((V_DIM, 1), lambda b, h, i: (0, 0)),
            pl.BlockSpec((None, tstep, LANES), lambda b, h, i: (b, qb0 + i, 2 * h)),
            pl.BlockSpec((None, tstep, LANES), lambda b, h, i: (b, qb0 + i, 2 * h + 1)),
            pl.BlockSpec((None, n_keys, LANES), lambda b, h, i: (b, kb0, h)),
            pl.BlockSpec((None, V_DIM, n_keys), lambda b, h, i: (b, h, kb0)),
        ],
        out_specs=pl.BlockSpec((tstep, LANES), lambda b, h, i: (b * nq + i, h)),
        out_shape=jax.ShapeDtypeStruct((bsz * n_q, d), BF16),
        scratch_shapes=[pltpu.VMEM((n_keys, 2 * tq), F32)],
        compiler_params=_cparams(("parallel", "parallel", "arbitrary"), 28 + s_bytes / 2 ** 20),
        name="diff_attention",
    )(lambda_qk, subln_g.reshape(V_DIM, 1), q_all, q_all, k_all, vt_all)


def _lru_kernel(*refs, rev, n_c, n_s, tm):
    if rev:
        (cur_ref, prev_ref, next_ref, cw_ref, cb_ref, wa_ref, ba_ref, wi_ref, bi_ref, lam_ref, fwd_ref,
         o_ref, a_scr, b_scr, h_scr) = refs
    else:
        (cur_ref, prev_ref, next_ref, cw_ref, cb_ref, wa_ref, ba_ref, wi_ref, bi_ref, lam_ref,
         o_ref, a_scr, b_scr, h_scr) = refs
    s = pl.program_id(1)
    is_ctx = s < n_c
    if rev:
        tile = jnp.where(is_ctx, n_c - 1 - s, n_s - 1 - (s - n_c))
    else:
        tile = jnp.where(is_ctx, s, s - n_c)
    n_tile = jnp.where(is_ctx, n_c, n_s)

    @pl.when(s == 0)
    def _():
        h_scr[...] = jnp.zeros_like(h_scr)

    x = cur_ref[...].astype(F32)
    pv = jnp.where(tile == 0, 0.0, prev_ref[...].astype(F32)[SUBLANES:])
    nx = jnp.where(tile == n_tile - 1, 0.0, next_ref[...].astype(F32)[:SUBLANES])
    r8 = lax.broadcasted_iota(jnp.int32, (SUBLANES, D_MODEL), 0)

    def earlier(k):
        rolled = pltpu.roll(x, k, 0)
        top = jnp.where(r8 < k, pltpu.roll(pv, k, 0), rolled[:SUBLANES])
        return jnp.concatenate([top, rolled[SUBLANES:]], axis=0)

    rolled = pltpu.roll(x, tm - 1, 0)
    bot = jnp.where(r8 >= SUBLANES - 1, pltpu.roll(nx, SUBLANES - 1, 0), rolled[tm - SUBLANES:])
    later = jnp.concatenate([rolled[:tm - SUBLANES], bot], axis=0)
    cw = cw_ref[...]
    conv = earlier(2) * cw[0:1] + earlier(1) * cw[1:2] + x * cw[2:3] + later * cw[3:4] + cb_ref[...]

    conv16 = conv.astype(BF16)

    def block_diag(w_ref, b_ref):
        parts = [jnp.dot(conv16[:, j * MXU_DIM:(j + 1) * MXU_DIM], w_ref[j], preferred_element_type=F32)
                 for j in range(D_MODEL // MXU_DIM)]
        return jnp.concatenate(parts, axis=1) + b_ref[...]

    r = jax.nn.sigmoid(block_diag(wa_ref, ba_ref))
    gate_i = jax.nn.sigmoid(block_diag(wi_ref, bi_ref))
    neg_lam = -lam_ref[...]
    softplus = jnp.maximum(neg_lam, 0.0) + jnp.log1p(jnp.exp(-jnp.abs(neg_lam)))
    log_a = -LRU_C * r * softplus
    a = jnp.exp(log_a)
    mult = jnp.sqrt(1.0 - a * a)
    a_scr[...] = a
    b_scr[...] = mult * gate_i * conv

    def body(t, h):
        row = tm - 1 - t if rev else t
        h = a_scr[pl.ds(row, 1), :] * h + b_scr[pl.ds(row, 1), :]
        if rev:
            o_ref[pl.ds(row, 1), :] = fwd_ref[pl.ds(row, 1), :] + h
        else:
            o_ref[pl.ds(row, 1), :] = h
        return h

    h_scr[...] = lax.fori_loop(0, tm, body, h_scr[...], unroll=8)


def _rglru(u, conv_w, conv_b, wa_bd, ba, wi_bd, bi, lam, bsz, n_lat, n_ctx):
    tm = TM
    d = D_MODEL
    n_tok = u.shape[0]
    n_s, n_c = n_lat // tm, n_ctx // tm
    lat_tiles = bsz * n_s
    halo = 2 * SUBLANES
    per_halo = tm // halo
    last_halo = n_tok // halo - 1

    def make(rev, fwd):
        def blk(b, s):
            is_ctx = s < n_c
            if rev:
                tile = jnp.where(is_ctx, n_c - 1 - s, n_s - 1 - (s - n_c))
            else:
                tile = jnp.where(is_ctx, s, s - n_c)
            return jnp.where(is_ctx, lat_tiles + b * n_c + tile, b * n_s + tile)

        vec = pl.BlockSpec((None, 1, d), lambda b, s: (int(rev), 0, 0))
        wspec = pl.BlockSpec((None, d // MXU_DIM, MXU_DIM, MXU_DIM), lambda b, s: (int(rev), 0, 0, 0))
        row_spec = pl.BlockSpec((tm, d), lambda b, s: (blk(b, s), 0))
        in_specs = [
            pl.BlockSpec((tm, d), lambda b, s: (blk(b, s), CB_LX)),
            pl.BlockSpec((halo, d), lambda b, s: (jnp.maximum(blk(b, s) * per_halo - 1, 0), CB_LX)),
            pl.BlockSpec((halo, d), lambda b, s: (jnp.minimum((blk(b, s) + 1) * per_halo, last_halo), CB_LX)),
            pl.BlockSpec((CONV_W, d), lambda b, s: (0, 0)),
            pl.BlockSpec((1, d), lambda b, s: (0, 0)),
            wspec, vec, wspec, vec, vec,
        ]
        args = [u, u, u, conv_w, conv_b.reshape(1, d), wa_bd, ba.reshape(2, 1, d), wi_bd, bi.reshape(2, 1, d),
                lam.reshape(2, 1, d)]
        if rev:
            in_specs.append(row_spec)
            args.append(fwd)
        return pl.pallas_call(
            functools.partial(_lru_kernel, rev=rev, n_c=n_c, n_s=n_s, tm=tm),
            grid=(bsz, n_c + n_s),
            in_specs=in_specs,
            out_specs=row_spec,
            out_shape=jax.ShapeDtypeStruct((n_tok, d), F32),
            scratch_shapes=[pltpu.VMEM((tm, d), F32), pltpu.VMEM((tm, d), F32), pltpu.VMEM((1, d), F32)],
            input_output_aliases={10: 0} if rev else {},
            compiler_params=_cparams(("arbitrary", "arbitrary"), 40),
            name="rglru_bwd" if rev else "rglru_fwd",
        )(*args)

    fwd = make(False, None)
    return make(True, fwd)


def _chan_dft_kernel(u_ref, cs_ref, o_ref):
    x = u_ref[...].astype(BF16)
    cs = cs_ref[...]
    gd = FOUR_GROUP_DIM
    for g in range(FOUR_GROUPS):
        y = jnp.dot(x[:, g * gd:(g + 1) * gd], cs, preferred_element_type=F32)
        o_ref[:, g * gd:(g + 1) * gd] = y[:, :gd].astype(BF16)
        o_ref[:, D_MODEL + g * gd:D_MODEL + (g + 1) * gd] = y[:, gd:].astype(BF16)


def _chan_dft(u, cs_mat, row0, n_rows):
    tm = TM
    d = D_MODEL
    tile0 = row0 // tm
    return pl.pallas_call(
        _chan_dft_kernel,
        grid=(n_rows // tm,),
        in_specs=[pl.BlockSpec((tm, d), lambda i: (tile0 + i, CB_F)),
                  pl.BlockSpec((FOUR_GROUP_DIM, 2 * FOUR_GROUP_DIM), lambda i: (0, 0))],
        out_specs=pl.BlockSpec((tm, 2 * d), lambda i: (i, 0)),
        out_shape=jax.ShapeDtypeStruct((n_rows, 2 * d), BF16),
        compiler_params=_cparams(("parallel",), 32),
        name="fourier_channels",
    )(u, cs_mat)


def _chan_dft_split_kernel(u_ref, cs_ref, o_ref, y_scr, *, radix):
    x = u_ref[...].astype(BF16)
    cs = cs_ref[...]
    gd = FOUR_GROUP_DIM
    per_g = gd // LANES
    n_cos = D_MODEL // LANES
    for g in range(FOUR_GROUPS):
        y = jnp.dot(x[:, g * gd:(g + 1) * gd], cs, preferred_element_type=F32)
        for j in range(per_g):
            y_scr[g * per_g + j] = y[:, j * LANES:(j + 1) * LANES]
            y_scr[n_cos + g * per_g + j] = y[:, gd + j * LANES:gd + (j + 1) * LANES]
    rows = u_ref.shape[0] // radix
    for r in range(radix):
        for cb in range(2 * n_cos):
            o_ref[r, :, cb * LANES:(cb + 1) * LANES] = y_scr[cb, pl.ds(r, rows, stride=radix), :].astype(BF16)


def _chan_dft_split(u, cs_mat, bsz, n_lat, radix):
    tm = TM
    d = D_MODEL
    n_s = n_lat // tm
    return pl.pallas_call(
        functools.partial(_chan_dft_split_kernel, radix=radix),
        grid=(bsz, n_s),
        in_specs=[pl.BlockSpec((tm, d), lambda b, i: (b * n_s + i, CB_F)),
                  pl.BlockSpec((FOUR_GROUP_DIM, 2 * FOUR_GROUP_DIM), lambda b, i: (0, 0))],
        out_specs=pl.BlockSpec((None, radix, tm // radix, 2 * d), lambda b, i: (b, 0, i, 0)),
        out_shape=jax.ShapeDtypeStruct((bsz, radix, n_lat // radix, 2 * d), BF16),
        scratch_shapes=[pltpu.VMEM((2 * d // LANES, tm, LANES), F32)],
        compiler_params=_cparams(("parallel", "parallel"), 32),
        name="fourier_channels_split",
    )(u, cs_mat)


def _time_fft_kernel(zr_ref, zi_ref, cc_ref, pc_ref, ps_ref, o_ref, *, radix):
    cc = cc_ref[...]
    tk1 = cc.shape[0] // 2
    for r in range(radix):
        pr = jnp.dot(cc, zr_ref[r], preferred_element_type=F32)
        pi = jnp.dot(cc, zi_ref[r], preferred_element_type=F32)
        a_re = pr[:tk1] - pi[tk1:]
        a_im = pr[tk1:] + pi[:tk1]
        for k2 in range(radix):
            term = pc_ref[k2][:, r:r + 1] * a_re - ps_ref[k2][:, r:r + 1] * a_im
            if r == 0:
                o_ref[k2] = term
            else:
                o_ref[k2] += term


def _time_fft(yp, cc, pc, ps, radix, tk1, tn):
    bsz, _, t1, two_d = yp.shape
    d = two_d // 2
    nj = d // tn
    return pl.pallas_call(
        functools.partial(_time_fft_kernel, radix=radix),
        grid=(bsz, nj, t1 // tk1),
        in_specs=[
            pl.BlockSpec((None, radix, t1, tn), lambda b, j, i: (b, 0, 0, j)),
            pl.BlockSpec((None, radix, t1, tn), lambda b, j, i: (b, 0, 0, nj + j)),
            pl.BlockSpec((None, 2 * tk1, t1), lambda b, j, i: (i, 0, 0)),
            pl.BlockSpec((radix, tk1, radix), lambda b, j, i: (0, i, 0)),
            pl.BlockSpec((radix, tk1, radix), lambda b, j, i: (0, i, 0)),
        ],
        out_specs=pl.BlockSpec((None, radix, tk1, tn), lambda b, j, i: (b, 0, i, j)),
        out_shape=jax.ShapeDtypeStruct((bsz, radix, t1, d), F32),
        compiler_params=_cparams(("parallel", "parallel", "arbitrary"), 48),
        name="fourier_positions_fft",
    )(yp, yp, cc, pc, ps)


def _fft_tables(n, radix, tk1):
    t1 = n // radix
    k1 = jnp.arange(t1, dtype=jnp.int32)
    ang = ((k1[:, None] * k1[None, :]) % t1).astype(F32) * (2.0 * math.pi / t1)
    c = jnp.cos(ang).reshape(t1 // tk1, tk1, t1)
    s = jnp.sin(ang).reshape(t1 // tk1, tk1, t1)
    cc = jnp.concatenate([c, s], axis=1).astype(BF16)
    k2 = jnp.arange(radix, dtype=jnp.int32)
    r = jnp.arange(radix, dtype=jnp.int32)
    k = k1[None, :, None] + t1 * k2[:, None, None]
    ph = ((k * r[None, None, :]) % n).astype(F32) * (2.0 * math.pi / n)
    scale = float(n) ** -0.5
    return cc, jnp.cos(ph) * scale, jnp.sin(ph) * scale


def _time_dft_kernel(ct_ref, st_ref, yc_ref, ys_ref, o_ref, acc, *, scale):
    kk = pl.program_id(2)

    @pl.when(kk == 0)
    def _():
        acc[...] = jnp.zeros_like(acc)

    acc[...] += (jnp.dot(ct_ref[...], yc_ref[...], preferred_element_type=F32)
                 - jnp.dot(st_ref[...], ys_ref[...], preferred_element_type=F32))

    @pl.when(kk == pl.num_programs(2) - 1)
    def _():
        o_ref[...] = acc[...] * scale


def _time_dft(y, ct, st, bsz, n_pos, row0, tmk):
    d = D_MODEL
    nt = n_pos // tmk
    rb0 = row0 // tmk
    return pl.pallas_call(
        functools.partial(_time_dft_kernel, scale=float(n_pos) ** -0.5),
        grid=(bsz, nt, nt),
        in_specs=[
            pl.BlockSpec((tmk, tmk), lambda b, i, k: (i, k)),
            pl.BlockSpec((tmk, tmk), lambda b, i, k: (i, k)),
            pl.BlockSpec((tmk, d), lambda b, i, k: (rb0 + b * nt + k, 0)),
            pl.BlockSpec((tmk, d), lambda b, i, k: (rb0 + b * nt + k, 1)),
        ],
        out_specs=pl.BlockSpec((tmk, d), lambda b, i, k: (b * nt + i, 0)),
        out_shape=jax.ShapeDtypeStruct((bsz * n_pos, d), F32),
        scratch_shapes=[pltpu.VMEM((tmk, d), F32)],
        compiler_params=_cparams(("parallel", "parallel", "arbitrary"), 48),
        name="fourier_positions",
    )(ct, st, y, y)


def _dft_tables(n):
    k = jnp.arange(n, dtype=jnp.int32)
    ang = ((k[:, None] * k[None, :]) % n).astype(F32) * (2.0 * math.pi / n)
    return jnp.cos(ang).astype(BF16), jnp.sin(ang).astype(BF16)


def _merge_kernel(x_ref, attn_l_ref, attn_c_ref, rec_ref, four_l_ref, four_c_ref, lg_ref, g0_ref, g1_ref, g2_ref,
                  wa_ref, wl_ref, wf_ref, wo_ref, gate_ref, n2_ref, sh_ref, sc_ref,
                  xo_ref, h_ref, hlo_ref, hhi_ref, *, lat_tiles):
    def mm(a, w_ref):
        return jnp.dot(a.astype(BF16), w_ref[...], preferred_element_type=F32)

    is_lat = pl.program_id(0) < lat_tiles
    attn_o = mm(jnp.where(is_lat, attn_l_ref[...], attn_c_ref[...]), wa_ref)
    rec_o = mm(rec_ref[...] * jax.nn.gelu(lg_ref[...].astype(F32), approximate=True), wl_ref)
    four_o = mm(jnp.where(is_lat, four_l_ref[...], four_c_ref[...]), wf_ref)
    mixed = (jax.nn.sigmoid(g0_ref[...].astype(F32)) * attn_o + jax.nn.sigmoid(g1_ref[...].astype(F32)) * rec_o
             + jax.nn.sigmoid(g2_ref[...].astype(F32)) * four_o)
    x = x_ref[...] + gate_ref[...] * mm(mixed, wo_ref)
    xo_ref[...] = x
    y = x * lax.rsqrt(jnp.mean(x * x, axis=-1, keepdims=True) + EPS)
    h = (y * n2_ref[...]) * (1.0 + sc_ref[...]) + sh_ref[...]
    h_ref[...] = h.astype(BF16)
    hlo_ref[...] = _pack_halves(h[:, :D_MODEL // 2])
    hhi_ref[...] = _pack_halves(h[:, D_MODEL // 2:])


def _merge(x_tok, attn_l, attn_c, rec, four_l, four_c, u, w_attn_o, w_lru_o, w_four_o, w_out, mod4, norm2_g,
           row_of_tile, n_rows):
    tm = TM
    d = D_MODEL
    lat_tiles = attn_l.shape[0] // tm
    ctx_tiles = attn_c.shape[0] // tm
    row = pl.BlockSpec((tm, d), lambda i: (i, 0))
    lat_row = pl.BlockSpec((tm, d), lambda i: (jnp.minimum(i, lat_tiles - 1), 0))
    ctx_row = pl.BlockSpec((tm, d), lambda i: (jnp.clip(i - lat_tiles, 0, ctx_tiles - 1), 0))
    wspec = pl.BlockSpec((d, d), lambda i: (0, 0))

    def ucol(cb):
        return pl.BlockSpec((tm, d), lambda i: (i, cb))

    return pl.pallas_call(
        functools.partial(_merge_kernel, lat_tiles=lat_tiles),
        grid=(n_rows // tm,),
        in_specs=[row, lat_row, ctx_row, row, lat_row, ctx_row, ucol(CB_LG), ucol(CB_G), ucol(CB_G + 1),
                  ucol(CB_G + 2), wspec, wspec, wspec, wspec,
                  _mod_spec(2, row_of_tile), pl.BlockSpec((1, d), lambda i: (0, 0)),
                  _mod_spec(3, row_of_tile), _mod_spec(4, row_of_tile)],
        out_specs=[row, row, pl.BlockSpec((tm, d // 4), lambda i: (i, 0)),
                   pl.BlockSpec((tm, d // 4), lambda i: (i, 0))],
        out_shape=[jax.ShapeDtypeStruct((n_rows, d), F32), jax.ShapeDtypeStruct((n_rows, d), BF16),
                   jax.ShapeDtypeStruct((n_rows, d // 4), jnp.uint32),
                   jax.ShapeDtypeStruct((n_rows, d // 4), jnp.uint32)],
        compiler_params=_cparams(("parallel",), 56),
        name="merge_branches",
    )(x_tok, attn_l, attn_c, rec, four_l, four_c, u, u, u, u, w_attn_o, w_lru_o, w_four_o, w_out,
      mod4, norm2_g.reshape(1, d), mod4, mod4)


def _route_kernel(h_ref, rw_ref, rb_ref, tri_ref, idx_ref, wgt_ref, pos_ref, cnt_ref, base_scr):
    tm = h_ref.shape[0]

    @pl.when(pl.program_id(0) == 0)
    def _():
        base_scr[...] = jnp.zeros_like(base_scr)

    logits = lax.dot_general(rw_ref[...], h_ref[...], (((1,), (1,)), ((), ())), preferred_element_type=F32)
    scores = jax.nn.sigmoid(logits)
    biased = scores + rb_ref[...]
    neg = -jnp.inf
    e_iota = lax.broadcasted_iota(jnp.int32, (N_EXPERTS, tm), 0)
    g_iota = lax.broadcasted_iota(jnp.int32, (N_GROUPS, tm), 0)

    rows = []
    for g in range(N_GROUPS):
        xg = biased[g * GROUP_SIZE:(g + 1) * GROUP_SIZE]
        m1 = jnp.max(xg, axis=0, keepdims=True)
        is_max = xg == m1
        n_max = jnp.sum(is_max.astype(F32), axis=0, keepdims=True)
        m2 = jnp.max(jnp.where(is_max, neg, xg), axis=0, keepdims=True)
        rows.append(m1 + jnp.where(n_max >= 2.0, m1, m2))
    grp = jnp.concatenate(rows, axis=0)

    g_sel = jnp.zeros((N_GROUPS, tm), jnp.bool_)
    work = grp
    for _ in range(TOPK_GROUPS):
        m = jnp.max(work, axis=0, keepdims=True)
        first = jnp.min(jnp.where(work == m, g_iota, N_GROUPS), axis=0, keepdims=True)
        hit = g_iota == first
        g_sel = jnp.logical_or(g_sel, hit)
        work = jnp.where(hit, neg, work)
    g_sel_f = g_sel.astype(F32)
    mask_rows = [jnp.broadcast_to(g_sel_f[g:g + 1], (GROUP_SIZE, tm)) for g in range(N_GROUPS)]
    e_mask = jnp.concatenate(mask_rows, axis=0) > 0.5

    work = jnp.where(e_mask, biased, neg)
    sel = jnp.zeros((N_EXPERTS, tm), F32)
    idx_rows, w_rows, hits = [], [], []
    for _ in range(TOP_K):
        m = jnp.max(work, axis=0, keepdims=True)
        first = jnp.min(jnp.where(work == m, e_iota, N_EXPERTS), axis=0, keepdims=True)
        hit = e_iota == first
        hits.append(hit)
        idx_rows.append(first)
        w_rows.append(jnp.sum(jnp.where(hit, scores, 0.0), axis=0, keepdims=True))
        sel = jnp.where(hit, 1.0, sel)
        work = jnp.where(hit, neg, work)
    w = jnp.concatenate(w_rows, axis=0)
    wgt_ref[...] = w / jnp.sum(w, axis=0, keepdims=True) * ROUTED_SCALE
    idx_ref[...] = jnp.concatenate(idx_rows, axis=0)

    before = jnp.dot(sel.astype(BF16), tri_ref[...], preferred_element_type=F32)
    rank = base_scr[...] + before
    pos_rows = [jnp.sum(jnp.where(hit, rank, 0.0), axis=0, keepdims=True) for hit in hits]
    pos_ref[...] = jnp.concatenate(pos_rows, axis=0).astype(jnp.int32)
    base_scr[...] = base_scr[...] + jnp.sum(sel, axis=1, keepdims=True)
    cnt_ref[...] = jnp.broadcast_to(base_scr[...], cnt_ref.shape)


def _route(h16, rw_t, router_b, tm):
    n_tok, d = h16.shape
    tri = (jnp.arange(tm)[:, None] < jnp.arange(tm)[None, :]).astype(BF16)
    out_col = pl.BlockSpec((TOP_K, tm), lambda i: (0, i))
    return pl.pallas_call(
        _route_kernel,
        grid=(n_tok // tm,),
        in_specs=[pl.BlockSpec((tm, d), lambda i: (i, 0)),
                  pl.BlockSpec((N_EXPERTS, d), lambda i: (0, 0)),
                  pl.BlockSpec((N_EXPERTS, 1), lambda i: (0, 0)),
                  pl.BlockSpec((tm, tm), lambda i: (0, 0))],
        out_specs=[out_col, out_col, out_col, pl.BlockSpec((N_EXPERTS, LANES), lambda i: (0, 0))],
        out_shape=[jax.ShapeDtypeStruct((TOP_K, n_tok), jnp.int32), jax.ShapeDtypeStruct((TOP_K, n_tok), F32),
                   jax.ShapeDtypeStruct((TOP_K, n_tok), jnp.int32),
                   jax.ShapeDtypeStruct((N_EXPERTS, LANES), F32)],
        scratch_shapes=[pltpu.VMEM((N_EXPERTS, 1), F32)],
        compiler_params=_cparams(("arbitrary",), 32),
        name="moe_router",
    )(h16, rw_t, router_b.reshape(N_EXPERTS, 1), tri)


def _slot_kernel(idx_ref, pos_ref, offs_ref, dest_ref):
    tm = idx_ref.shape[1]
    e_iota = lax.broadcasted_iota(jnp.int32, (N_EXPERTS, tm), 0)
    idx = idx_ref[...]
    offs = offs_ref[...]
    rows = [jnp.sum(jnp.where(e_iota == idx[r:r + 1], offs, 0), axis=0, keepdims=True) for r in range(TOP_K)]
    dest_ref[...] = jnp.concatenate(rows, axis=0) + pos_ref[...]


def _slots(top_idx, pos, offs, tm):
    n_tok = top_idx.shape[1]
    col = pl.BlockSpec((TOP_K, tm), lambda i: (0, i))
    return pl.pallas_call(
        _slot_kernel,
        grid=(n_tok // tm,),
        in_specs=[col, col, pl.BlockSpec((N_EXPERTS, 1), lambda i: (0, 0))],
        out_specs=col,
        out_shape=jax.ShapeDtypeStruct((TOP_K, n_tok), jnp.int32),
        compiler_params=_cparams(("parallel",), 32),
        name="moe_slots",
    )(top_idx, pos, offs.reshape(N_EXPERTS, 1))


def _sc_scatter_rows(src, idx, n_out):
    n_src, width = src.shape
    n_idx = idx.shape[0]
    n_win = n_src // SC_WINDOW
    mesh = plsc.VectorSubcoreMesh(core_axis_name="core", subcore_axis_name="subcore")

    @pl.kernel(out_type=jax.ShapeDtypeStruct((n_out, width), src.dtype), mesh=mesh, scratch_types=[],
               name="moe_scatter_sc")
    def scatter(src_hbm, idx_hbm, out_hbm):
        def body(src_vmem, idx_vmem):
            pltpu.sync_copy(src_vmem, out_hbm.at[idx_vmem.at[0]])

        pltpu.emit_pipeline(
            body,
            grid=(n_idx // SC_WINDOW,),
            in_specs=[pl.BlockSpec((SC_WINDOW, width), index_map=lambda g: (g % n_win, 0)),
                      pl.BlockSpec((1, SC_WINDOW), index_map=lambda g: (0, g))],
            out_specs=[],
            core_axis_name=("core", "subcore"),
            dimension_semantics=(pltpu.PARALLEL,),
        )(src_hbm, idx_hbm)

    return scatter(src, idx.reshape(1, n_idx))


def _expert_kernel(be_ref, nu_ref, nv_ref, xlo_ref, xhi_ref, *refs):
    wg_refs, wu_refs, wd_refs = (refs[k * W_PIECES:(k + 1) * W_PIECES] for k in range(3))
    ylo_ref, yhi_ref, wg_s, wu_s, wd_s = refs[3 * W_PIECES:]
    i = pl.program_id(0)
    used = i < nu_ref[0]
    prev_e = be_ref[jnp.maximum(i - 1, 0)]
    fresh = jnp.logical_or(i == 0, be_ref[i] != prev_e)

    @pl.when(jnp.logical_and(used, fresh))
    def _():
        rows = D_MODEL // W_PIECES
        for k in range(W_PIECES):
            wg_s[k * rows:(k + 1) * rows, :] = wg_refs[k][0].astype(BF16)
            wu_s[k * rows:(k + 1) * rows, :] = wu_refs[k][0].astype(BF16)
            wd_s[:, k * rows:(k + 1) * rows] = wd_refs[k][0].astype(BF16)

    @pl.when(used)
    def _():
        row = lax.broadcasted_iota(jnp.int32, xlo_ref.shape, 0)
        live = row < nv_ref[i]
        half = D_MODEL // 2
        quarter = D_MODEL // 4
        parts = []
        for ref in (xlo_ref, xhi_ref):
            parts.extend(p.astype(BF16) for p in _unpack_halves(jnp.where(live, ref[...], jnp.uint32(0))))

        def proj(w_s):
            acc = jnp.dot(parts[0], w_s[:quarter, :], preferred_element_type=F32)
            for k in range(1, 4):
                acc = acc + jnp.dot(parts[k], w_s[k * quarter:(k + 1) * quarter, :], preferred_element_type=F32)
            return acc

        g = proj(wg_s)
        act = (_silu(g) * proj(wu_s)).astype(BF16)
        y = jnp.dot(act, wd_s[...], preferred_element_type=F32)
        ylo_ref[...] = _pack_halves(y[:, :half])
        yhi_ref[...] = _pack_halves(y[:, half:])


def _experts(xs_lo, xs_hi, block_e, n_used, n_valid, w_gate, w_up, w_down, layer, bm):
    n_slots, quarter = xs_lo.shape
    half = 2 * quarter
    d = D_MODEL

    def blk(i, be, nu, nv):
        return (jnp.minimum(i, nu[0] - 1), 0)

    piece = d // W_PIECES
    row_pieces = [pl.BlockSpec((None, 1, piece, EXPERT_FF), functools.partial(
        lambda i, be, nu, nv, k: (layer, be[i], k, 0), k=k)) for k in range(W_PIECES)]
    col_pieces = [pl.BlockSpec((None, 1, EXPERT_FF, piece), functools.partial(
        lambda i, be, nu, nv, k: (layer, be[i], 0, k), k=k)) for k in range(W_PIECES)]

    grid_spec = pltpu.PrefetchScalarGridSpec(
        num_scalar_prefetch=3,
        grid=(n_slots // bm,),
        in_specs=[pl.BlockSpec((bm, quarter), blk),
                  pl.BlockSpec((bm, quarter), blk),
                  *row_pieces, *row_pieces, *col_pieces],
        out_specs=[pl.BlockSpec((bm, half // 2), blk), pl.BlockSpec((bm, half // 2), blk)],
        scratch_shapes=[pltpu.VMEM((d, EXPERT_FF), BF16), pltpu.VMEM((d, EXPERT_FF), BF16),
                        pltpu.VMEM((EXPERT_FF, d), BF16)],
    )
    y_shape = jax.ShapeDtypeStruct((n_slots, half // 2), jnp.uint32)
    return pl.pallas_call(
        _expert_kernel,
        grid_spec=grid_spec,
        out_shape=[y_shape, y_shape],
        compiler_params=_cparams(("arbitrary",), 40),
        name="moe_experts",
    )(block_e, n_used, n_valid, xs_lo, xs_hi, *([w_gate] * W_PIECES), *([w_up] * W_PIECES),
      *([w_down] * W_PIECES))


def _sc_gather_rows(table, idx):
    n_idx = idx.shape[0]
    width = table.shape[1]
    mesh = plsc.VectorSubcoreMesh(core_axis_name="core", subcore_axis_name="subcore")

    @pl.kernel(out_type=jax.ShapeDtypeStruct((n_idx, width), table.dtype), mesh=mesh, scratch_types=[],
               name="moe_gather_sc")
    def gather(table_hbm, idx_hbm, out_hbm):
        def body(idx_vmem, out_vmem):
            pltpu.sync_copy(table_hbm.at[idx_vmem.at[0]], out_vmem)

        pltpu.emit_pipeline(
            body,
            grid=(n_idx // SC_WINDOW,),
            in_specs=[pl.BlockSpec((1, SC_WINDOW), index_map=lambda i: (0, i))],
            out_specs=[pl.BlockSpec((SC_WINDOW, width), index_map=lambda i: (i, 0))],
            core_axis_name=("core", "subcore"),
            dimension_semantics=(pltpu.PARALLEL,),
        )(idx_hbm, out_hbm)

    return gather(table, idx.reshape(1, n_idx))


def _combine_kernel(x_ref, h_ref, tw_ref, gate_ref, wsg_ref, wsu_ref, wsd_ref, lo_ref, hi_ref, o_ref):
    quarter = D_MODEL // 4
    h = h_ref[...]
    act = (_silu(jnp.dot(h, wsg_ref[...], preferred_element_type=F32))
           * jnp.dot(h, wsu_ref[...], preferred_element_type=F32)).astype(BF16)
    shared = jnp.dot(act, wsd_ref[...], preferred_element_type=F32)

    tw = tw_ref[...]
    gate = gate_ref[...]
    for q, buf in enumerate((lo_ref, hi_ref)):
        moe_a = moe_b = None
        for r in range(TOP_K):
            ya, yb = _unpack_halves(buf[r])
            wr = tw[:, r:r + 1]
            moe_a = ya * wr if moe_a is None else moe_a + ya * wr
            moe_b = yb * wr if moe_b is None else moe_b + yb * wr
        for part, moe in ((2 * q, moe_a), (2 * q + 1, moe_b)):
            cols = slice(part * quarter, (part + 1) * quarter)
            o_ref[:, cols] = x_ref[:, cols] + gate[:, cols] * (moe + shared[:, cols])


def _combine(x_tok, h16, g_lo, g_hi, top_w, mod4, ws_gate, ws_up, ws_down, row_of_tile):
    n_tok, d = x_tok.shape
    tm = TM
    ff = ws_gate.shape[1]
    row = pl.BlockSpec((tm, d), lambda i: (i, 0))
    gat = pl.BlockSpec((TOP_K, tm, d // 4), lambda i: (0, i, 0))
    return pl.pallas_call(
        _combine_kernel,
        grid=(n_tok // tm,),
        in_specs=[row, row,
                  pl.BlockSpec((tm, TOP_K), lambda i: (i, 0)),
                  _mod_spec(5, row_of_tile),
                  pl.BlockSpec((d, ff), lambda i: (0, 0)),
                  pl.BlockSpec((d, ff), lambda i: (0, 0)),
                  pl.BlockSpec((ff, d), lambda i: (0, 0)),
                  gat, gat],
        out_specs=row,
        out_shape=jax.ShapeDtypeStruct((n_tok, d), F32),
        compiler_params=_cparams(("parallel",), 48),
        name="moe_combine",
    )(x_tok, h16, top_w, mod4, ws_gate, ws_up, ws_down, g_lo, g_hi)


def _moe(x_tok, h16, h_lo, h_hi, mod4, row_of_tile, rw_t, router_b, w_gate, w_up, w_down, layer, ws_gate, ws_up, ws_down):
    n_tok = x_tok.shape[0]
    bm = MOE_BM
    tm_r = TM_ROUTE if n_tok % TM_ROUTE == 0 else TM
    top_idx, top_w, pos, cnt = _route(h16, rw_t, router_b, tm_r)

    counts = cnt[:, 0].astype(jnp.int32)
    padded = (counts + bm - 1) // bm * bm
    ends = jnp.cumsum(padded)
    n_blocks = -(-(n_tok * TOP_K + N_EXPERTS * (bm - 1)) // bm)
    offs = ends - padded
    dest = _slots(top_idx, pos, offs, tm_r)
    block_start = jnp.arange(n_blocks, dtype=jnp.int32) * bm
    block_e = jnp.minimum(jnp.sum((ends[None, :] <= block_start[:, None]).astype(jnp.int32), axis=1),
                          N_EXPERTS - 1)
    n_valid = jnp.clip(counts[block_e] - (block_start - offs[block_e]), 0, bm).astype(jnp.int32)
    n_used = (ends[-1] // bm).astype(jnp.int32).reshape(1)
    dest_flat = dest.reshape(TOP_K * n_tok)
    xs_lo = _sc_scatter_rows(h_lo, dest_flat, n_blocks * bm)
    xs_hi = _sc_scatter_rows(h_hi, dest_flat, n_blocks * bm)
    ys_lo, ys_hi = _experts(xs_lo, xs_hi, block_e, n_used, n_valid, w_gate, w_up, w_down, layer, bm)
    g_lo = _sc_gather_rows(ys_lo, dest_flat).reshape(TOP_K, n_tok, D_MODEL // 4)
    g_hi = _sc_gather_rows(ys_hi, dest_flat).reshape(TOP_K, n_tok, D_MODEL // 4)
    return _combine(x_tok, h16, g_lo, g_hi, top_w.T, mod4, ws_gate, ws_up, ws_down, row_of_tile)


def _block_diag_tiles(w):
    per = MXU_DIM // LRU_BLOCK_DIM
    w = w.reshape(2, D_MODEL // MXU_DIM, per, LRU_BLOCK_DIM, LRU_BLOCK_DIM)
    eye = jnp.eye(per, dtype=w.dtype)
    t = jnp.einsum('ztpde,pq->ztpdqe', w, eye)
    return t.reshape(2, D_MODEL // MXU_DIM, MXU_DIM, MXU_DIM).astype(BF16)


def _rope_tables(n_lat, n_ctx):
    rows = n_lat // GRID_W
    row = jnp.repeat(jnp.arange(rows), GRID_W)
    col = jnp.tile(jnp.arange(GRID_W), rows)
    inv = ROPE_THETA ** (-jnp.arange(ROPE_FREQS, dtype=F32) / ROPE_FREQS)
    ang = jnp.stack([row[:, None] * inv, col[:, None] * inv], axis=1)
    ang = jnp.concatenate([ang, ang], axis=2).reshape(n_lat, HEAD_DIM)
    ang = jnp.tile(ang, (1, LANES // HEAD_DIM))
    ang = jnp.concatenate([ang, jnp.zeros((n_ctx, LANES), F32)], axis=0)
    lane = jnp.arange(LANES)
    sign = jnp.where((lane % (2 * ROPE_FREQS)) < ROPE_FREQS, -1.0, 1.0).astype(F32)
    return jnp.cos(ang), jnp.sin(ang) * sign


def kernel(x, c, ctx, c_ctx, w_mod, b_mod, norm1_g, w_in, q_norm_g, k_norm_g, lambda_qk, subln_g, w_attn_o, conv_w, conv_b, lru_wa, lru_ba, lru_wi, lru_bi, lru_lambda, w_lru_o, w_four_o, w_out, norm2_g, router_w, router_b, exp_w_gate, exp_w_up, exp_w_down, sh_w_gate, sh_w_up, sh_w_down):
    bsz, n_lat, d = x.shape
    n_ctx = ctx.shape[1]
    depth = w_mod.shape[0]
    n_latent_rows = bsz * n_lat
    n_tok = n_latent_rows + bsz * n_ctx
    assert d == D_MODEL and bsz + 1 <= MOD_ROWS
    assert n_lat % TM == 0 and n_ctx % TM == 0 and n_lat % GRID_W == 0
    tm_in = TM_IN if (n_lat % TM_IN == 0 and (bsz * n_ctx) % TM_IN == 0) else TM

    cvec = jnp.zeros((MOD_ROWS, d), F32).at[:bsz].set(c).at[bsz].set(c_ctx)
    mods = _modulation(cvec, w_mod, b_mod)

    def row_of_tile_fn(tm):
        per_batch = n_lat // tm
        return lambda i: jnp.minimum(i // per_batch, bsz)

    assert n_lat % n_ctx == 0
    cos_t, sin_t = _rope_tables(n_lat, n_ctx)
    group_mat = jnp.kron(jnp.eye(d // HEAD_DIM, dtype=F32), jnp.ones((HEAD_DIM, HEAD_DIM), F32)).astype(BF16)
    m = jnp.arange(FOUR_GROUP_DIM, dtype=jnp.int32)
    ang_c = ((m[:, None] * m[None, :]) % FOUR_GROUP_DIM).astype(F32) * (2.0 * math.pi / FOUR_GROUP_DIM)
    inv_sqrt_c = FOUR_GROUP_DIM ** -0.5
    cs_mat = jnp.concatenate([jnp.cos(ang_c) * inv_sqrt_c, jnp.sin(ang_c) * inv_sqrt_c], axis=1).astype(BF16)
    assert n_lat % (FFT_RADIX * MXU_DIM) == 0 or n_lat // FFT_RADIX < MXU_DIM
    fft_tk1 = min(MXU_DIM, n_lat // FFT_RADIX)
    fft_cc, fft_pc, fft_ps = _fft_tables(n_lat, FFT_RADIX, fft_tk1)
    ct_ctx, st_ctx = _dft_tables(n_ctx)
    q_sub = math.gcd(Q_SUB, n_lat // TQ)
    n_keys = n_ctx + n_lat
    tk = TK if n_keys % TK == 0 else TM

    x_tok = jnp.concatenate([x.reshape(n_latent_rows, d), ctx.reshape(bsz * n_ctx, d)], axis=0)
    for l in range(depth):
        last = l == depth - 1
        lam_init = 0.8 - 0.6 * math.exp(-0.3 * l)
        mod4 = mods[l].reshape(MOD_ROWS, 6, 1, d)

        u = _in_projection(x_tok, mod4, norm1_g[l], w_in[l].astype(BF16), row_of_tile_fn(tm_in), tm_in)

        k_all, q_all, vt_all = _qkv_prepare(u, k_norm_g[l], q_norm_g[l], group_mat, cos_t, sin_t,
                                            bsz, n_lat, n_ctx)
        attn_l = _diff_attention(q_all, k_all, vt_all, lambda_qk[l], subln_g[l], lam_init,
                                 0, n_lat, 0, n_keys, TQ, q_sub, tk)
        attn_c = attn_l
        if not last:
            attn_c = _diff_attention(q_all, k_all, vt_all, lambda_qk[l], subln_g[l], lam_init,
                                     n_lat, n_ctx, n_lat, n_ctx, TM, 1, TM)

        rec = _rglru(u, conv_w[l], conv_b[l], _block_diag_tiles(lru_wa[l]), lru_ba[l],
                     _block_diag_tiles(lru_wi[l]), lru_bi[l], lru_lambda[l], bsz, n_lat, n_ctx)

        y_lat = _chan_dft_split(u, cs_mat, bsz, n_lat, FFT_RADIX)
        four_l = _time_fft(y_lat, fft_cc, fft_pc, fft_ps, FFT_RADIX, fft_tk1, MXU_DIM).reshape(n_latent_rows, d)
        four_c = four_l
        if not last:
            y_ctx = _chan_dft(u, cs_mat, n_latent_rows, bsz * n_ctx)
            four_c = _time_dft(y_ctx, ct_ctx, st_ctx, bsz, n_ctx, 0, n_ctx)

        n_rows = n_latent_rows if last else n_tok
        row_of_tile = row_of_tile_fn(TM)
        x_mid, h16, h_lo, h_hi = _merge(x_tok, attn_l, attn_c, rec, four_l, four_c, u, w_attn_o[l].astype(BF16),
                                w_lru_o[l].astype(BF16), w_four_o[l].astype(BF16), w_out[l].astype(BF16),
                                mod4, norm2_g[l], row_of_tile, n_rows)
        x_tok = _moe(x_mid, h16, h_lo, h_hi, mod4, row_of_tile, router_w[l].T.astype(BF16), router_b[l],
                     exp_w_gate, exp_w_up, exp_w_down, l,
                     sh_w_gate[l].astype(BF16), sh_w_up[l].astype(BF16), sh_w_down[l].astype(BF16))
    return x_tok[:n_latent_rows].reshape(bsz, n_lat, d)
```

```python
import functools
import math

import jax
import jax.numpy as jnp
from jax import lax
from jax.experimental import pallas as pl
from jax.experimental.pallas import tpu as pltpu
from jax.experimental.pallas import tpu_sc as plsc

F32 = jnp.float32
BF16 = jnp.bfloat16

D_MODEL = 1024
EPS = 1e-6
GRID_W = 64

N_HEADS = 8
HEAD_DIM = 64
V_DIM = 2 * HEAD_DIM
ATTN_SCALE = HEAD_DIM ** -0.5
ROPE_THETA = 10000.0
ROPE_FREQS = HEAD_DIM // 4

LRU_BLOCKS = 16
LRU_BLOCK_DIM = D_MODEL // LRU_BLOCKS
LRU_C = 8.0
CONV_W = 4

FOUR_GROUPS = 4
FOUR_GROUP_DIM = D_MODEL // FOUR_GROUPS

CB_K, CB_V, CB_LX, CB_Q, CB_LG, CB_F, CB_G = 0, 1, 2, 3, 4, 5, 6

N_EXPERTS = 256
TOP_K = 8
N_GROUPS = 8
GROUP_SIZE = N_EXPERTS // N_GROUPS
TOPK_GROUPS = 4
EXPERT_FF = 256
ROUTED_SCALE = 2.5

LANES = 128
SUBLANES = 8
MXU_DIM = 256
VMEM_BYTES_V7X = 64 * 1024 * 1024

MOD_ROWS = 8
TM = 256
TM_IN = 512
IN_PANEL = 3
TQ = 256
Q_SUB = 32
TK = 768
TM_ROUTE = 512
MOE_BM = 512
SC_WINDOW = 128
FFT_RADIX = 8
LOG2E = 1.4426950408889634


def _cparams(sem, vmem_mb):
    vmem_bytes = int(vmem_mb * 1024 * 1024)
    assert vmem_bytes < VMEM_BYTES_V7X
    return pltpu.CompilerParams(dimension_semantics=sem, vmem_limit_bytes=vmem_bytes)


def _silu(x):
    return x * jax.nn.sigmoid(x)


def _pack_halves(x):
    half = x.shape[1] // 2
    bits = pltpu.bitcast(x.astype(BF16).astype(F32), jnp.uint32)
    return (bits[:, :half] & jnp.uint32(0xFFFF0000)) | (bits[:, half:] >> 16)


def _unpack_halves(w):
    return pltpu.bitcast(w & jnp.uint32(0xFFFF0000), F32), pltpu.bitcast(w << 16, F32)


def _mod_kernel(c_ref, w_ref, b_ref, o_ref):
    c = c_ref[...]
    s = _silu(c).astype(BF16)
    o_ref[0] = jnp.dot(s, w_ref[0].astype(BF16), preferred_element_type=F32) + b_ref[0]


def _modulation(cvec, w_mod, b_mod):
    n_layers, d, six_d = w_mod.shape
    nj = six_d // d
    return pl.pallas_call(
        _mod_kernel,
        grid=(n_layers, nj),
        in_specs=[
            pl.BlockSpec((MOD_ROWS, d), lambda l, j: (0, 0)),
            pl.BlockSpec((1, d, d), lambda l, j: (l, 0, j)),
            pl.BlockSpec((1, 1, d), lambda l, j: (l, 0, j)),
        ],
        out_specs=pl.BlockSpec((1, MOD_ROWS, d), lambda l, j: (l, 0, j)),
        out_shape=jax.ShapeDtypeStruct((n_layers, MOD_ROWS, six_d), F32),
        compiler_params=_cparams(("parallel", "parallel"), 32),
        name="modulation",
    )(cvec, w_mod, b_mod.reshape(n_layers, 1, six_d))


def _mod_spec(part, row_of_tile):
    return pl.BlockSpec((None, None, 1, D_MODEL), lambda i: (row_of_tile(i), part, 0, 0))


def _inproj_kernel(x_ref, g_ref, sh_ref, sc_ref, w_ref, o_ref):
    x = x_ref[...]
    y = x * lax.rsqrt(jnp.mean(x * x, axis=-1, keepdims=True) + EPS)
    h = (y * g_ref[...]) * (1.0 + sc_ref[...]) + sh_ref[...]
    o_ref[...] = jnp.dot(h.astype(BF16), w_ref[...], preferred_element_type=F32).astype(o_ref.dtype)


def _in_projection(x_tok, mod4, norm_g, w_in_bf16, row_of_tile, tm):
    n_tok, d = x_tok.shape
    n_cols = w_in_bf16.shape[1]
    tn = IN_PANEL * d
    row_of = lambda j, i: row_of_tile(i)
    return pl.pallas_call(
        _inproj_kernel,
        grid=(n_cols // tn, n_tok // tm),
        in_specs=[
            pl.BlockSpec((tm, d), lambda j, i: (i, 0)),
            pl.BlockSpec((1, d), lambda j, i: (0, 0)),
            pl.BlockSpec((None, None, 1, d), lambda j, i: (row_of(j, i), 0, 0, 0)),
            pl.BlockSpec((None, None, 1, d), lambda j, i: (row_of(j, i), 1, 0, 0)),
            pl.BlockSpec((d, tn), lambda j, i: (0, j)),
        ],
        out_specs=pl.BlockSpec((tm, tn), lambda j, i: (i, j)),
        out_shape=jax.ShapeDtypeStruct((n_tok, n_cols), BF16),
        compiler_params=_cparams(("parallel", "parallel"), 48),
        name="in_projection",
    )(x_tok, norm_g.reshape(1, d), mod4, mod4, w_in_bf16)


def _prep_kernel(uk_ref, uq_ref, uv_ref, kg_ref, qg_ref, gm_ref, cos_ref, sin_ref, k_out, q_out, vt_out, *, tm):
    gm = gm_ref[...]
    lane = lax.broadcasted_iota(jnp.int32, (tm, LANES), 1)
    first_half = (lane % (2 * ROPE_FREQS)) < ROPE_FREQS

    def norm_rope(x, g):
        ss = jnp.dot((x * x).astype(BF16), gm, preferred_element_type=F32) * (1.0 / HEAD_DIM)
        y = x * lax.rsqrt(ss + EPS) * g
        c = cos_ref[...]
        s = sin_ref[...]
        outs = []
        for j in range(D_MODEL // LANES):
            yt = y[:, j * LANES:(j + 1) * LANES]
            rot = jnp.where(first_half, pltpu.roll(yt, LANES - ROPE_FREQS, 1), pltpu.roll(yt, ROPE_FREQS, 1))
            outs.append(yt * c + rot * s)
        return jnp.concatenate(outs, axis=1)

    k_out[...] = norm_rope(uk_ref[...].astype(F32), kg_ref[...]).astype(BF16)
    qn = norm_rope(uq_ref[...].astype(F32), qg_ref[...]) * (ATTN_SCALE * LOG2E)
    lo = lane < HEAD_DIM
    for h in range(N_HEADS):
        qt = qn[:, h * LANES:(h + 1) * LANES]
        q_out[:, (2 * h) * LANES:(2 * h + 1) * LANES] = jnp.where(lo, qt, 0.0).astype(BF16)
        q_out[:, (2 * h + 1) * LANES:(2 * h + 2) * LANES] = jnp.where(lo, 0.0, qt).astype(BF16)
    vt_out[...] = uv_ref[...].astype(F32).T.astype(BF16)


def _qkv_prepare(u, k_gain, q_gain, group_mat, cos_t, sin_t, bsz, n_lat, n_ctx):
    tm = TM
    d = D_MODEL
    n_s, n_c = n_lat // tm, n_ctx // tm
    n_rows = n_lat + n_ctx
    lat_tiles = bsz * n_s
    kg = jnp.tile(k_gain, d // HEAD_DIM).reshape(1, d)
    qg = jnp.tile(q_gain, d // HEAD_DIM).reshape(1, d)

    def tok_tile(b, j):
        return jnp.where(j < n_s, b * n_s + j, lat_tiles + b * n_c + (j - n_s))

    def u_spec(cb):
        return pl.BlockSpec((tm, d), lambda b, j: (tok_tile(b, j), cb))

    vec = pl.BlockSpec((1, d), lambda b, j: (0, 0))
    rope_spec = pl.BlockSpec((tm, LANES), lambda b, j: (j, 0))
    return pl.pallas_call(
        functools.partial(_prep_kernel, tm=tm),
        grid=(bsz, n_s + n_c),
        in_specs=[u_spec(CB_K), u_spec(CB_Q), u_spec(CB_V), vec, vec,
                  pl.BlockSpec((d, d), lambda b, j: (0, 0)), rope_spec, rope_spec],
        out_specs=[
            pl.BlockSpec((None, tm, d), lambda b, j: (b, j, 0)),
            pl.BlockSpec((None, tm, 2 * d), lambda b, j: (b, j, 0)),
            pl.BlockSpec((None, d, tm), lambda b, j: (b, 0, j)),
        ],
        out_shape=[jax.ShapeDtypeStruct((bsz, n_rows, d), BF16),
                   jax.ShapeDtypeStruct((bsz, n_rows, 2 * d), BF16),
                   jax.ShapeDtypeStruct((bsz, d, n_rows), BF16)],
        compiler_params=_cparams(("parallel", "parallel"), 40),
        name="qkv_prepare",
    )(u, u, u, kg, qg, group_mat, cos_t, sin_t)


def _attn_kernel(lq_ref, sg_ref, q1_ref, q2_ref, k_ref, vt_ref, o_ref, s_buf, *, n_keys, tk, tq, lam_init):
    lq = lq_ref[...]
    lam = (jnp.exp(jnp.sum(lq[0:1] * lq[1:2], axis=1, keepdims=True))
           - jnp.exp(jnp.sum(lq[2:3] * lq[3:4], axis=1, keepdims=True)) + lam_init)
    n_chunks = n_keys // tk
    n_sub = q1_ref.shape[0] // tq

    def load_q(j):
        rows = pl.ds(pl.multiple_of(j * tq, tq), tq)
        return jnp.concatenate([q1_ref[rows, :], q2_ref[rows, :]], axis=0)

    def score_chunk(q, c, mx):
        s = lax.dot_general(k_ref[c * tk:(c + 1) * tk, :], q, (((1,), (1,)), ((), ())),
                            preferred_element_type=F32)
        s_buf[c * tk:(c + 1) * tk, :] = s
        cm = jnp.max(s, axis=0, keepdims=True)
        return cm if mx is None else jnp.maximum(mx, cm)

    q0 = load_q(0)
    mx0 = None
    for c in range(n_chunks):
        mx0 = score_chunk(q0, c, mx0)

    def sub_tile(j, mx):
        q_next = load_q(jnp.minimum(j + 1, n_sub - 1))
        mx_next = None
        l = jnp.zeros((1, 2 * tq), F32)
        acc = jnp.zeros((V_DIM, 2 * tq), F32)
        for c in range(n_chunks):
            p = jnp.exp2(s_buf[c * tk:(c + 1) * tk, :] - mx)
            l = l + jnp.sum(p, axis=0, keepdims=True)
            acc = acc + jnp.dot(vt_ref[:, c * tk:(c + 1) * tk], p.astype(BF16), preferred_element_type=F32)
            mx_next = score_chunk(q_next, c, mx_next)
        o = acc / l
        o = o[:, :tq] - lam * o[:, tq:]
        ms = jnp.mean(o * o, axis=0, keepdims=True)
        on = (o * lax.rsqrt(ms + EPS) * sg_ref[...]) * (1.0 - lam_init)
        o_ref[pl.ds(pl.multiple_of(j * tq, tq), tq), :] = on.T.astype(o_ref.dtype)
        return mx_next

    lax.fori_loop(0, n_sub, sub_tile, mx0)


def _diff_attention(q_all, k_all, vt_all, lambda_qk, subln_g, lam_init, q_row0, n_q, key_row0, n_keys, tq, n_sub, tk):
    bsz = q_all.shape[0]
    d = D_MODEL
    tstep = tq * n_sub
    nq = n_q // tstep
    qb0 = q_row0 // tstep
    kb0 = key_row0 // n_keys
    s_bytes = n_keys * 2 * tq * 4
    return pl.pallas_call(
        functools.partial(_attn_kernel, n_keys=n_keys, tk=tk, tq=tq, lam_init=lam_init),
        grid=(bsz, N_HEADS, nq),
        in_specs=[
            pl.BlockSpec((4, HEAD_DIM), lambda b, h, i: (0, 0)),
            pl.BlockSpec((V_DIM, 1), lambda b, h, i: (0, 0)),
            pl.BlockSpec((None, tstep, LANES), lambda b, h, i: (b, qb0 + i, 2 * h)),
            pl.BlockSpec((None, tstep, LANES), lambda b, h, i: (b, qb0 + i, 2 * h + 1)),
            pl.BlockSpec((None, n_keys, LANES), lambda b, h, i: (b, kb0, h)),
            pl.BlockSpec((None, V_DIM, n_keys), lambda b, h, i: (b, h, kb0)),
        ],
        out_specs=pl.BlockSpec((tstep, LANES), lambda b, h, i: (b * nq + i, h)),
        out_shape=jax.ShapeDtypeStruct((bsz * n_q, d), BF16),
        scratch_shapes=[pltpu.VMEM((n_keys, 2 * tq), F32)],
        compiler_params=_cparams(("parallel", "parallel", "arbitrary"), 28 + s_bytes / 2 ** 20),
        name="diff_attention",
    )(lambda_qk, subln_g.reshape(V_DIM, 1), q_all, q_all, k_all, vt_all)


def _lru_kernel(*refs, rev, n_c, n_s, tm):
    if rev:
        (cur_ref, prev_ref, next_ref, cw_ref, cb_ref, wa_ref, ba_ref, wi_ref, bi_ref, lam_ref, fwd_ref,
         o_ref, a_scr, b_scr, h_scr) = refs
    else:
        (cur_ref, prev_ref, next_ref, cw_ref, cb_ref, wa_ref, ba_ref, wi_ref, bi_ref, lam_ref,
         o_ref, a_scr, b_scr, h_scr) = refs
    s = pl.program_id(1)
    is_ctx = s < n_c
    if rev:
        tile = jnp.where(is_ctx, n_c - 1 - s, n_s - 1 - (s - n_c))
    else:
        tile = jnp.where(is_ctx, s, s - n_c)
    n_tile = jnp.where(is_ctx, n_c, n_s)

    @pl.when(s == 0)
    def _():
        h_scr[...] = jnp.zeros_like(h_scr)

    x = cur_ref[...].astype(F32)
    pv = jnp.where(tile == 0, 0.0, prev_ref[...].astype(F32)[SUBLANES:])
    nx = jnp.where(tile == n_tile - 1, 0.0, next_ref[...].astype(F32)[:SUBLANES])
    r8 = lax.broadcasted_iota(jnp.int32, (SUBLANES, D_MODEL), 0)

    def earlier(k):
        rolled = pltpu.roll(x, k, 0)
        top = jnp.where(r8 < k, pltpu.roll(pv, k, 0), rolled[:SUBLANES])
        return jnp.concatenate([top, rolled[SUBLANES:]], axis=0)

    rolled = pltpu.roll(x, tm - 1, 0)
    bot = jnp.where(r8 >= SUBLANES - 1, pltpu.roll(nx, SUBLANES - 1, 0), rolled[tm - SUBLANES:])
    later = jnp.concatenate([rolled[:tm - SUBLANES], bot], axis=0)
    cw = cw_ref[...]
    conv = earlier(2) * cw[0:1] + earlier(1) * cw[1:2] + x * cw[2:3] + later * cw[3:4] + cb_ref[...]

    conv16 = conv.astype(BF16)

    def block_diag(w_ref, b_ref):
        parts = [jnp.dot(conv16[:, j * MXU_DIM:(j + 1) * MXU_DIM], w_ref[j], preferred_element_type=F32)
                 for j in range(D_MODEL // MXU_DIM)]
        return jnp.concatenate(parts, axis=1) + b_ref[...]

    r = jax.nn.sigmoid(block_diag(wa_ref, ba_ref))
    gate_i = jax.nn.sigmoid(block_diag(wi_ref, bi_ref))
    neg_lam = -lam_ref[...]
    softplus = jnp.maximum(neg_lam, 0.0) + jnp.log1p(jnp.exp(-jnp.abs(neg_lam)))
    log_a = -LRU_C * r * softplus
    a = jnp.exp(log_a)
    mult = jnp.sqrt(1.0 - a * a)
    a_scr[...] = a
    b_scr[...] = mult * gate_i * conv

    def body(t, h):
        row = tm - 1 - t if rev else t
        h = a_scr[pl.ds(row, 1), :] * h + b_scr[pl.ds(row, 1), :]
        if rev:
            o_ref[pl.ds(row, 1), :] = fwd_ref[pl.ds(row, 1), :] + h
        else:
            o_ref[pl.ds(row, 1), :] = h
        return h

    h_scr[...] = lax.fori_loop(0, tm, body, h_scr[...], unroll=8)


def _rglru(u, conv_w, conv_b, wa_bd, ba, wi_bd, bi, lam, bsz, n_lat, n_ctx):
    tm = TM
    d = D_MODEL
    n_tok = u.shape[0]
    n_s, n_c = n_lat // tm, n_ctx // tm
    lat_tiles = bsz * n_s
    halo = 2 * SUBLANES
    per_halo = tm // halo
    last_halo = n_tok // halo - 1

    def make(rev, fwd):
        def blk(b, s):
            is_ctx = s < n_c
            if rev:
                tile = jnp.where(is_ctx, n_c - 1 - s, n_s - 1 - (s - n_c))
            else:
                tile = jnp.where(is_ctx, s, s - n_c)
            return jnp.where(is_ctx, lat_tiles + b * n_c + tile, b * n_s + tile)

        vec = pl.BlockSpec((None, 1, d), lambda b, s: (int(rev), 0, 0))
        wspec = pl.BlockSpec((None, d // MXU_DIM, MXU_DIM, MXU_DIM), lambda b, s: (int(rev), 0, 0, 0))
        row_spec = pl.BlockSpec((tm, d), lambda b, s: (blk(b, s), 0))
        in_specs = [
            pl.BlockSpec((tm, d), lambda b, s: (blk(b, s), CB_LX)),
            pl.BlockSpec((halo, d), lambda b, s: (jnp.maximum(blk(b, s) * per_halo - 1, 0), CB_LX)),
            pl.BlockSpec((halo, d), lambda b, s: (jnp.minimum((blk(b, s) + 1) * per_halo, last_halo), CB_LX)),
            pl.BlockSpec((CONV_W, d), lambda b, s: (0, 0)),
            pl.BlockSpec((1, d), lambda b, s: (0, 0)),
            wspec, vec, wspec, vec, vec,
        ]
        args = [u, u, u, conv_w, conv_b.reshape(1, d), wa_bd, ba.reshape(2, 1, d), wi_bd, bi.reshape(2, 1, d),
                lam.reshape(2, 1, d)]
        if rev:
            in_specs.append(row_spec)
            args.append(fwd)
        return pl.pallas_call(
            functools.partial(_lru_kernel, rev=rev, n_c=n_c, n_s=n_s, tm=tm),
            grid=(bsz, n_c + n_s),
            in_specs=in_specs,
            out_specs=row_spec,
            out_shape=jax.ShapeDtypeStruct((n_tok, d), F32),
            scratch_shapes=[pltpu.VMEM((tm, d), F32), pltpu.VMEM((tm, d), F32), pltpu.VMEM((1, d), F32)],
            input_output_aliases={10: 0} if rev else {},
            compiler_params=_cparams(("arbitrary", "arbitrary"), 40),
            name="rglru_bwd" if rev else "rglru_fwd",
        )(*args)

    fwd = make(False, None)
    return make(True, fwd)


def _chan_dft_kernel(u_ref, cs_ref, o_ref):
    x = u_ref[...].astype(BF16)
    cs = cs_ref[...]
    gd = FOUR_GROUP_DIM
    for g in range(FOUR_GROUPS):
        y = jnp.dot(x[:, g * gd:(g + 1) * gd], cs, preferred_element_type=F32)
        o_ref[:, g * gd:(g + 1) * gd] = y[:, :gd].astype(BF16)
        o_ref[:, D_MODEL + g * gd:D_MODEL + (g + 1) * gd] = y[:, gd:].astype(BF16)


def _chan_dft(u, cs_mat, row0, n_rows):
    tm = TM
    d = D_MODEL
    tile0 = row0 // tm
    return pl.pallas_call(
        _chan_dft_kernel,
        grid=(n_rows // tm,),
        in_specs=[pl.BlockSpec((tm, d), lambda i: (tile0 + i, CB_F)),
                  pl.BlockSpec((FOUR_GROUP_DIM, 2 * FOUR_GROUP_DIM), lambda i: (0, 0))],
        out_specs=pl.BlockSpec((tm, 2 * d), lambda i: (i, 0)),
        out_shape=jax.ShapeDtypeStruct((n_rows, 2 * d), BF16),
        compiler_params=_cparams(("parallel",), 32),
        name="fourier_channels",
    )(u, cs_mat)


def _chan_dft_split_kernel(u_ref, cs_ref, o_ref, y_scr, *, radix):
    x = u_ref[...].astype(BF16)
    cs = cs_ref[...]
    gd = FOUR_GROUP_DIM
    per_g = gd // LANES
    n_cos = D_MODEL // LANES
    for g in range(FOUR_GROUPS):
        y = jnp.dot(x[:, g * gd:(g + 1) * gd], cs, preferred_element_type=F32)
        for j in range(per_g):
            y_scr[g * per_g + j] = y[:, j * LANES:(j + 1) * LANES]
            y_scr[n_cos + g * per_g + j] = y[:, gd + j * LANES:gd + (j + 1) * LANES]
    rows = u_ref.shape[0] // radix
    for r in range(radix):
        for cb in range(2 * n_cos):
            o_ref[r, :, cb * LANES:(cb + 1) * LANES] = y_scr[cb, pl.ds(r, rows, stride=radix), :].astype(BF16)


def _chan_dft_split(u, cs_mat, bsz, n_lat, radix):
    tm = TM
    d = D_MODEL
    n_s = n_lat // tm
    return pl.pallas_call(
        functools.partial(_chan_dft_split_kernel, radix=radix),
        grid=(bsz, n_s),
        in_specs=[pl.BlockSpec((tm, d), lambda b, i: (b * n_s + i, CB_F)),
                  pl.BlockSpec((FOUR_GROUP_DIM, 2 * FOUR_GROUP_DIM), lambda b, i: (0, 0))],
        out_specs=pl.BlockSpec((None, radix, tm // radix, 2 * d), lambda b, i: (b, 0, i, 0)),
        out_shape=jax.ShapeDtypeStruct((bsz, radix, n_lat // radix, 2 * d), BF16),
        scratch_shapes=[pltpu.VMEM((2 * d // LANES, tm, LANES), F32)],
        compiler_params=_cparams(("parallel", "parallel"), 32),
        name="fourier_channels_split",
    )(u, cs_mat)


def _time_fft_kernel(zr_ref, zi_ref, cc_ref, pc_ref, ps_ref, o_ref, acc, *, radix):
    cc = cc_ref[...]
    tk1 = cc.shape[0] // 2
    for r in range(radix):
        pr = jnp.dot(cc, zr_ref[r], preferred_element_type=F32)
        pi = jnp.dot(cc, zi_ref[r], preferred_element_type=F32)
        a_re = pr[:tk1] - pi[tk1:]
        a_im = pr[tk1:] + pi[:tk1]
        for k2 in range(radix):
            term = pc_ref[k2][:, r:r + 1] * a_re - ps_ref[k2][:, r:r + 1] * a_im
            if r == 0:
                acc[k2] = term
            else:
                acc[k2] += term
    o_ref[...] = acc[...].astype(o_ref.dtype)


def _time_fft(yp, cc, pc, ps, radix, tk1, tn):
    bsz, _, t1, two_d = yp.shape
    d = two_d // 2
    nj = d // tn
    return pl.pallas_call(
        functools.partial(_time_fft_kernel, radix=radix),
        grid=(bsz, nj, t1 // tk1),
        in_specs=[
            pl.BlockSpec((None, radix, t1, tn), lambda b, j, i: (b, 0, 0, j)),
            pl.BlockSpec((None, radix, t1, tn), lambda b, j, i: (b, 0, 0, nj + j)),
            pl.BlockSpec((None, 2 * tk1, t1), lambda b, j, i: (i, 0, 0)),
            pl.BlockSpec((radix, tk1, radix), lambda b, j, i: (0, i, 0)),
            pl.BlockSpec((radix, tk1, radix), lambda b, j, i: (0, i, 0)),
        ],
        out_specs=pl.BlockSpec((None, radix, tk1, tn), lambda b, j, i: (b, 0, i, j)),
        out_shape=jax.ShapeDtypeStruct((bsz, radix, t1, d), BF16),
        scratch_shapes=[pltpu.VMEM((radix, tk1, tn), F32)],
        compiler_params=_cparams(("parallel", "parallel", "arbitrary"), 48),
        name="fourier_positions_fft",
    )(yp, yp, cc, pc, ps)


def _fft_tables(n, radix, tk1):
    t1 = n // radix
    k1 = jnp.arange(t1, dtype=jnp.int32)
    ang = ((k1[:, None] * k1[None, :]) % t1).astype(F32) * (2.0 * math.pi / t1)
    c = jnp.cos(ang).reshape(t1 // tk1, tk1, t1)
    s = jnp.sin(ang).reshape(t1 // tk1, tk1, t1)
    cc = jnp.concatenate([c, s], axis=1).astype(BF16)
    k2 = jnp.arange(radix, dtype=jnp.int32)
    r = jnp.arange(radix, dtype=jnp.int32)
    k = k1[None, :, None] + t1 * k2[:, None, None]
    ph = ((k * r[None, None, :]) % n).astype(F32) * (2.0 * math.pi / n)
    scale = float(n) ** -0.5
    return cc, jnp.cos(ph) * scale, jnp.sin(ph) * scale


def _time_dft_kernel(ct_ref, st_ref, yc_ref, ys_ref, o_ref, acc, *, scale):
    kk = pl.program_id(2)

    @pl.when(kk == 0)
    def _():
        acc[...] = jnp.zeros_like(acc)

    acc[...] += (jnp.dot(ct_ref[...], yc_ref[...], preferred_element_type=F32)
                 - jnp.dot(st_ref[...], ys_ref[...], preferred_element_type=F32))

    @pl.when(kk == pl.num_programs(2) - 1)
    def _():
        o_ref[...] = (acc[...] * scale).astype(o_ref.dtype)


def _time_dft(y, ct, st, bsz, n_pos, row0, tmk):
    d = D_MODEL
    nt = n_pos // tmk
    rb0 = row0 // tmk
    return pl.pallas_call(
        functools.partial(_time_dft_kernel, scale=float(n_pos) ** -0.5),
        grid=(bsz, nt, nt),
        in_specs=[
            pl.BlockSpec((tmk, tmk), lambda b, i, k: (i, k)),
            pl.BlockSpec((tmk, tmk), lambda b, i, k: (i, k)),
            pl.BlockSpec((tmk, d), lambda b, i, k: (rb0 + b * nt + k, 0)),
            pl.BlockSpec((tmk, d), lambda b, i, k: (rb0 + b * nt + k, 1)),
        ],
        out_specs=pl.BlockSpec((tmk, d), lambda b, i, k: (b * nt + i, 0)),
        out_shape=jax.ShapeDtypeStruct((bsz * n_pos, d), BF16),
        scratch_shapes=[pltpu.VMEM((tmk, d), F32)],
        compiler_params=_cparams(("parallel", "parallel", "arbitrary"), 48),
        name="fourier_positions",
    )(ct, st, y, y)


def _dft_tables(n):
    k = jnp.arange(n, dtype=jnp.int32)
    ang = ((k[:, None] * k[None, :]) % n).astype(F32) * (2.0 * math.pi / n)
    return jnp.cos(ang).astype(BF16), jnp.sin(ang).astype(BF16)


def _merge_kernel(x_ref, attn_l_ref, attn_c_ref, rec_ref, four_l_ref, four_c_ref, lg_ref, g0_ref, g1_ref, g2_ref,
                  wa_ref, wl_ref, wf_ref, wo_ref, gate_ref, n2_ref, sh_ref, sc_ref,
                  xo_ref, h_ref, hlo_ref, hhi_ref, *, lat_tiles):
    def mm(a, w_ref):
        return jnp.dot(a.astype(BF16), w_ref[...], preferred_element_type=F32)

    is_lat = pl.program_id(0) < lat_tiles
    attn_o = mm(jnp.where(is_lat, attn_l_ref[...], attn_c_ref[...]), wa_ref)
    rec_o = mm(rec_ref[...] * jax.nn.gelu(lg_ref[...].astype(F32), approximate=True), wl_ref)
    four_o = mm(jnp.where(is_lat, four_l_ref[...], four_c_ref[...]), wf_ref)
    mixed = (jax.nn.sigmoid(g0_ref[...].astype(F32)) * attn_o + jax.nn.sigmoid(g1_ref[...].astype(F32)) * rec_o
             + jax.nn.sigmoid(g2_ref[...].astype(F32)) * four_o)
    x = x_ref[...] + gate_ref[...] * mm(mixed, wo_ref)
    xo_ref[...] = x
    y = x * lax.rsqrt(jnp.mean(x * x, axis=-1, keepdims=True) + EPS)
    h = (y * n2_ref[...]) * (1.0 + sc_ref[...]) + sh_ref[...]
    h_ref[...] = h.astype(BF16)
    hlo_ref[...] = _pack_halves(h[:, :D_MODEL // 2])
    hhi_ref[...] = _pack_halves(h[:, D_MODEL // 2:])


def _merge(x_tok, attn_l, attn_c, rec, four_l, four_c, u, w_attn_o, w_lru_o, w_four_o, w_out, mod4, norm2_g,
           row_of_tile, n_rows):
    tm = TM
    d = D_MODEL
    lat_tiles = attn_l.shape[0] // tm
    ctx_tiles = attn_c.shape[0] // tm
    row = pl.BlockSpec((tm, d), lambda i: (i, 0))
    lat_row = pl.BlockSpec((tm, d), lambda i: (jnp.minimum(i, lat_tiles - 1), 0))
    ctx_row = pl.BlockSpec((tm, d), lambda i: (jnp.clip(i - lat_tiles, 0, ctx_tiles - 1), 0))
    wspec = pl.BlockSpec((d, d), lambda i: (0, 0))

    def ucol(cb):
        return pl.BlockSpec((tm, d), lambda i: (i, cb))

    return pl.pallas_call(
        functools.partial(_merge_kernel, lat_tiles=lat_tiles),
        grid=(n_rows // tm,),
        in_specs=[row, lat_row, ctx_row, row, lat_row, ctx_row, ucol(CB_LG), ucol(CB_G), ucol(CB_G + 1),
                  ucol(CB_G + 2), wspec, wspec, wspec, wspec,
                  _mod_spec(2, row_of_tile), pl.BlockSpec((1, d), lambda i: (0, 0)),
                  _mod_spec(3, row_of_tile), _mod_spec(4, row_of_tile)],
        out_specs=[row, row, pl.BlockSpec((tm, d // 4), lambda i: (i, 0)),
                   pl.BlockSpec((tm, d // 4), lambda i: (i, 0))],
        out_shape=[jax.ShapeDtypeStruct((n_rows, d), F32), jax.ShapeDtypeStruct((n_rows, d), BF16),
                   jax.ShapeDtypeStruct((n_rows, d // 4), jnp.uint32),
                   jax.ShapeDtypeStruct((n_rows, d // 4), jnp.uint32)],
        compiler_params=_cparams(("parallel",), 56),
        name="merge_branches",
    )(x_tok, attn_l, attn_c, rec, four_l, four_c, u, u, u, u, w_attn_o, w_lru_o, w_four_o, w_out,
      mod4, norm2_g.reshape(1, d), mod4, mod4)


def _route_kernel(h_ref, rw_ref, rb_ref, tri_ref, idx_ref, wgt_ref, pos_ref, cnt_ref, base_scr):
    tm = h_ref.shape[0]

    @pl.when(pl.program_id(0) == 0)
    def _():
        base_scr[...] = jnp.zeros_like(base_scr)

    logits = lax.dot_general(rw_ref[...], h_ref[...], (((1,), (1,)), ((), ())), preferred_element_type=F32)
    scores = jax.nn.sigmoid(logits)
    biased = scores + rb_ref[...]
    neg = -jnp.inf
    e_iota = lax.broadcasted_iota(jnp.int32, (N_EXPERTS, tm), 0)
    g_iota = lax.broadcasted_iota(jnp.int32, (N_GROUPS, tm), 0)

    rows = []
    for g in range(N_GROUPS):
        xg = biased[g * GROUP_SIZE:(g + 1) * GROUP_SIZE]
        m1 = jnp.max(xg, axis=0, keepdims=True)
        is_max = xg == m1
        n_max = jnp.sum(is_max.astype(F32), axis=0, keepdims=True)
        m2 = jnp.max(jnp.where(is_max, neg, xg), axis=0, keepdims=True)
        rows.append(m1 + jnp.where(n_max >= 2.0, m1, m2))
    grp = jnp.concatenate(rows, axis=0)

    g_sel = jnp.zeros((N_GROUPS, tm), jnp.bool_)
    work = grp
    for _ in range(TOPK_GROUPS):
        m = jnp.max(work, axis=0, keepdims=True)
        first = jnp.min(jnp.where(work == m, g_iota, N_GROUPS), axis=0, keepdims=True)
        hit = g_iota == first
        g_sel = jnp.logical_or(g_sel, hit)
        work = jnp.where(hit, neg, work)
    g_sel_f = g_sel.astype(F32)
    mask_rows = [jnp.broadcast_to(g_sel_f[g:g + 1], (GROUP_SIZE, tm)) for g in range(N_GROUPS)]
    e_mask = jnp.concatenate(mask_rows, axis=0) > 0.5

    work = jnp.where(e_mask, biased, neg)
    sel = jnp.zeros((N_EXPERTS, tm), F32)
    idx_rows, w_rows, hits = [], [], []
    for _ in range(TOP_K):
        m = jnp.max(work, axis=0, keepdims=True)
        first = jnp.min(jnp.where(work == m, e_iota, N_EXPERTS), axis=0, keepdims=True)
        hit = e_iota == first
        hits.append(hit)
        idx_rows.append(first)
        w_rows.append(jnp.sum(jnp.where(hit, scores, 0.0), axis=0, keepdims=True))
        sel = jnp.where(hit, 1.0, sel)
        work = jnp.where(hit, neg, work)
    w = jnp.concatenate(w_rows, axis=0)
    wgt_ref[...] = w / jnp.sum(w, axis=0, keepdims=True) * ROUTED_SCALE
    idx_ref[...] = jnp.concatenate(idx_rows, axis=0)

    before = jnp.dot(sel.astype(BF16), tri_ref[...], preferred_element_type=F32)
    rank = base_scr[...] + before
    pos_rows = [jnp.sum(jnp.where(hit, rank, 0.0), axis=0, keepdims=True) for hit in hits]
    pos_ref[...] = jnp.concatenate(pos_rows, axis=0).astype(jnp.int32)
    base_scr[...] = base_scr[...] + jnp.sum(sel, axis=1, keepdims=True)
    cnt_ref[...] = jnp.broadcast_to(base_scr[...], cnt_ref.shape)


def _route(h16, rw_t, router_b, tm):
    n_tok, d = h16.shape
    tri = (jnp.arange(tm)[:, None] < jnp.arange(tm)[None, :]).astype(BF16)
    out_col = pl.BlockSpec((TOP_K, tm), lambda i: (0, i))
    return pl.pallas_call(
        _route_kernel,
        grid=(n_tok // tm,),
        in_specs=[pl.BlockSpec((tm, d), lambda i: (i, 0)),
                  pl.BlockSpec((N_EXPERTS, d), lambda i: (0, 0)),
                  pl.BlockSpec((N_EXPERTS, 1), lambda i: (0, 0)),
                  pl.BlockSpec((tm, tm), lambda i: (0, 0))],
        out_specs=[out_col, out_col, out_col, pl.BlockSpec((N_EXPERTS, LANES), lambda i: (0, 0))],
        out_shape=[jax.ShapeDtypeStruct((TOP_K, n_tok), jnp.int32), jax.ShapeDtypeStruct((TOP_K, n_tok), F32),
                   jax.ShapeDtypeStruct((TOP_K, n_tok), jnp.int32),
                   jax.ShapeDtypeStruct((N_EXPERTS, LANES), F32)],
        scratch_shapes=[pltpu.VMEM((N_EXPERTS, 1), F32)],
        compiler_params=_cparams(("arbitrary",), 32),
        name="moe_router",
    )(h16, rw_t, router_b.reshape(N_EXPERTS, 1), tri)


def _slot_kernel(idx_ref, pos_ref, offs_ref, dest_ref):
    tm = idx_ref.shape[1]
    e_iota = lax.broadcasted_iota(jnp.int32, (N_EXPERTS, tm), 0)
    idx = idx_ref[...]
    offs = offs_ref[...]
    rows = [jnp.sum(jnp.where(e_iota == idx[r:r + 1], offs, 0), axis=0, keepdims=True) for r in range(TOP_K)]
    dest_ref[...] = jnp.concatenate(rows, axis=0) + pos_ref[...]


def _slots(top_idx, pos, offs, tm):
    n_tok = top_idx.shape[1]
    col = pl.BlockSpec((TOP_K, tm), lambda i: (0, i))
    return pl.pallas_call(
        _slot_kernel,
        grid=(n_tok // tm,),
        in_specs=[col, col, pl.BlockSpec((N_EXPERTS, 1), lambda i: (0, 0))],
        out_specs=col,
        out_shape=jax.ShapeDtypeStruct((TOP_K, n_tok), jnp.int32),
        compiler_params=_cparams(("parallel",), 32),
        name="moe_slots",
    )(top_idx, pos, offs.reshape(N_EXPERTS, 1))


def _sc_scatter_rows(src, idx, n_out):
    n_src, width = src.shape
    n_idx = idx.shape[0]
    n_win = n_src // SC_WINDOW
    mesh = plsc.VectorSubcoreMesh(core_axis_name="core", subcore_axis_name="subcore")

    @pl.kernel(out_type=jax.ShapeDtypeStruct((n_out, width), src.dtype), mesh=mesh, scratch_types=[],
               name="moe_scatter_sc")
    def scatter(src_hbm, idx_hbm, out_hbm):
        def body(src_vmem, idx_vmem):
            pltpu.sync_copy(src_vmem, out_hbm.at[idx_vmem.at[0]])

        pltpu.emit_pipeline(
            body,
            grid=(n_idx // SC_WINDOW,),
            in_specs=[pl.BlockSpec((SC_WINDOW, width), index_map=lambda g: (g % n_win, 0)),
                      pl.BlockSpec((1, SC_WINDOW), index_map=lambda g: (0, g))],
            out_specs=[],
            core_axis_name=("core", "subcore"),
            dimension_semantics=(pltpu.PARALLEL,),
        )(src_hbm, idx_hbm)

    return scatter(src, idx.reshape(1, n_idx))


def _expert_kernel(be_ref, nu_ref, nv_ref, xlo_ref, xhi_ref, wg_ref, wu_ref, wd_ref, ylo_ref, yhi_ref,
                   wg_s, wu_s, wd_s):
    i = pl.program_id(0)
    used = i < nu_ref[0]
    prev_e = be_ref[jnp.maximum(i - 1, 0)]
    fresh = jnp.logical_or(i == 0, be_ref[i] != prev_e)

    @pl.when(jnp.logical_and(used, fresh))
    def _():
        wg_s[...] = wg_ref[0].astype(BF16)
        wu_s[...] = wu_ref[0].astype(BF16)
        wd_s[...] = wd_ref[0].astype(BF16)

    @pl.when(used)
    def _():
        row = lax.broadcasted_iota(jnp.int32, xlo_ref.shape, 0)
        live = row < nv_ref[i]
        half = D_MODEL // 2
        quarter = D_MODEL // 4
        parts = []
        for ref in (xlo_ref, xhi_ref):
            parts.extend(p.astype(BF16) for p in _unpack_halves(jnp.where(live, ref[...], jnp.uint32(0))))

        def proj(w_s):
            acc = jnp.dot(parts[0], w_s[:quarter, :], preferred_element_type=F32)
            for k in range(1, 4):
                acc = acc + jnp.dot(parts[k], w_s[k * quarter:(k + 1) * quarter, :], preferred_element_type=F32)
            return acc

        g = proj(wg_s)
        act = (_silu(g) * proj(wu_s)).astype(BF16)
        y = jnp.dot(act, wd_s[...], preferred_element_type=F32)
        ylo_ref[...] = _pack_halves(y[:, :half])
        yhi_ref[...] = _pack_halves(y[:, half:])


def _experts(xs_lo, xs_hi, block_e, n_used, n_valid, w_gate, w_up, w_down, layer, bm):
    n_slots, quarter = xs_lo.shape
    half = 2 * quarter
    d = D_MODEL

    def blk(i, be, nu, nv):
        return (jnp.minimum(i, nu[0] - 1), 0)

    def wsel(i, be, nu, nv):
        return (layer, be[i], 0, 0)

    grid_spec = pltpu.PrefetchScalarGridSpec(
        num_scalar_prefetch=3,
        grid=(n_slots // bm,),
        in_specs=[pl.BlockSpec((bm, quarter), blk),
                  pl.BlockSpec((bm, quarter), blk),
                  pl.BlockSpec((None, 1, d, EXPERT_FF), wsel),
                  pl.BlockSpec((None, 1, d, EXPERT_FF), wsel),
                  pl.BlockSpec((None, 1, EXPERT_FF, d), wsel)],
        out_specs=[pl.BlockSpec((bm, half // 2), blk), pl.BlockSpec((bm, half // 2), blk)],
        scratch_shapes=[pltpu.VMEM((d, EXPERT_FF), BF16), pltpu.VMEM((d, EXPERT_FF), BF16),
                        pltpu.VMEM((EXPERT_FF, d), BF16)],
    )
    y_shape = jax.ShapeDtypeStruct((n_slots, half // 2), jnp.uint32)
    return pl.pallas_call(
        _expert_kernel,
        grid_spec=grid_spec,
        out_shape=[y_shape, y_shape],
        compiler_params=_cparams(("arbitrary",), 40),
        name="moe_experts",
    )(block_e, n_used, n_valid, xs_lo, xs_hi, w_gate, w_up, w_down)


def _sc_gather_rows(table, idx):
    n_idx = idx.shape[0]
    width = table.shape[1]
    mesh = plsc.VectorSubcoreMesh(core_axis_name="core", subcore_axis_name="subcore")

    @pl.kernel(out_type=jax.ShapeDtypeStruct((n_idx, width), table.dtype), mesh=mesh, scratch_types=[],
               name="moe_gather_sc")
    def gather(table_hbm, idx_hbm, out_hbm):
        def body(idx_vmem, out_vmem):
            pltpu.sync_copy(table_hbm.at[idx_vmem.at[0]], out_vmem)

        pltpu.emit_pipeline(
            body,
            grid=(n_idx // SC_WINDOW,),
            in_specs=[pl.BlockSpec((1, SC_WINDOW), index_map=lambda i: (0, i))],
            out_specs=[pl.BlockSpec((SC_WINDOW, width), index_map=lambda i: (i, 0))],
            core_axis_name=("core", "subcore"),
            dimension_semantics=(pltpu.PARALLEL,),
        )(idx_hbm, out_hbm)

    return gather(table, idx.reshape(1, n_idx))


def _combine_kernel(x_ref, h_ref, tw_ref, gate_ref, wsg_ref, wsu_ref, wsd_ref, lo_ref, hi_ref, o_ref):
    quarter = D_MODEL // 4
    h = h_ref[...]
    act = (_silu(jnp.dot(h, wsg_ref[...], preferred_element_type=F32))
           * jnp.dot(h, wsu_ref[...], preferred_element_type=F32)).astype(BF16)
    shared = jnp.dot(act, wsd_ref[...], preferred_element_type=F32)

    tw = tw_ref[...]
    gate = gate_ref[...]
    for q, buf in enumerate((lo_ref, hi_ref)):
        moe_a = moe_b = None
        for r in range(TOP_K):
            ya, yb = _unpack_halves(buf[r])
            wr = tw[:, r:r + 1]
            moe_a = ya * wr if moe_a is None else moe_a + ya * wr
            moe_b = yb * wr if moe_b is None else moe_b + yb * wr
        for part, moe in ((2 * q, moe_a), (2 * q + 1, moe_b)):
            cols = slice(part * quarter, (part + 1) * quarter)
            o_ref[:, cols] = x_ref[:, cols] + gate[:, cols] * (moe + shared[:, cols])


def _combine(x_tok, h16, g_lo, g_hi, top_w, mod4, ws_gate, ws_up, ws_down, row_of_tile):
    n_tok, d = x_tok.shape
    tm = TM
    ff = ws_gate.shape[1]
    row = pl.BlockSpec((tm, d), lambda i: (i, 0))
    gat = pl.BlockSpec((TOP_K, tm, d // 4), lambda i: (0, i, 0))
    return pl.pallas_call(
        _combine_kernel,
        grid=(n_tok // tm,),
        in_specs=[row, row,
                  pl.BlockSpec((tm, TOP_K), lambda i: (i, 0)),
                  _mod_spec(5, row_of_tile),
                  pl.BlockSpec((d, ff), lambda i: (0, 0)),
                  pl.BlockSpec((d, ff), lambda i: (0, 0)),
                  pl.BlockSpec((ff, d), lambda i: (0, 0)),
                  gat, gat],
        out_specs=row,
        out_shape=jax.ShapeDtypeStruct((n_tok, d), F32),
        compiler_params=_cparams(("parallel",), 48),
        name="moe_combine",
    )(x_tok, h16, top_w, mod4, ws_gate, ws_up, ws_down, g_lo, g_hi)


def _moe(x_tok, h16, h_lo, h_hi, mod4, row_of_tile, rw_t, router_b, w_gate, w_up, w_down, layer, ws_gate, ws_up, ws_down):
    n_tok = x_tok.shape[0]
    bm = MOE_BM
    tm_r = TM_ROUTE if n_tok % TM_ROUTE == 0 else TM
    top_idx, top_w, pos, cnt = _route(h16, rw_t, router_b, tm_r)

    counts = cnt[:, 0].astype(jnp.int32)
    padded = (counts + bm - 1) // bm * bm
    ends = jnp.cumsum(padded)
    n_blocks = -(-(n_tok * TOP_K + N_EXPERTS * (bm - 1)) // bm)
    offs = ends - padded
    dest = _slots(top_idx, pos, offs, tm_r)
    block_start = jnp.arange(n_blocks, dtype=jnp.int32) * bm
    block_e = jnp.minimum(jnp.sum((ends[None, :] <= block_start[:, None]).astype(jnp.int32), axis=1),
                          N_EXPERTS - 1)
    n_valid = jnp.clip(counts[block_e] - (block_start - offs[block_e]), 0, bm).astype(jnp.int32)
    n_used = (ends[-1] // bm).astype(jnp.int32).reshape(1)
    dest_flat = dest.reshape(TOP_K * n_tok)
    xs_lo = _sc_scatter_rows(h_lo, dest_flat, n_blocks * bm)
    xs_hi = _sc_scatter_rows(h_hi, dest_flat, n_blocks * bm)
    ys_lo, ys_hi = _experts(xs_lo, xs_hi, block_e, n_used, n_valid, w_gate, w_up, w_down, layer, bm)
    g_lo = _sc_gather_rows(ys_lo, dest_flat).reshape(TOP_K, n_tok, D_MODEL // 4)
    g_hi = _sc_gather_rows(ys_hi, dest_flat).reshape(TOP_K, n_tok, D_MODEL // 4)
    return _combine(x_tok, h16, g_lo, g_hi, top_w.T, mod4, ws_gate, ws_up, ws_down, row_of_tile)


def _block_diag_tiles(w):
    per = MXU_DIM // LRU_BLOCK_DIM
    w = w.reshape(2, D_MODEL // MXU_DIM, per, LRU_BLOCK_DIM, LRU_BLOCK_DIM)
    eye = jnp.eye(per, dtype=w.dtype)
    t = jnp.einsum('ztpde,pq->ztpdqe', w, eye)
    return t.reshape(2, D_MODEL // MXU_DIM, MXU_DIM, MXU_DIM).astype(BF16)


def _rope_tables(n_lat, n_ctx):
    rows = n_lat // GRID_W
    row = jnp.repeat(jnp.arange(rows), GRID_W)
    col = jnp.tile(jnp.arange(GRID_W), rows)
    inv = ROPE_THETA ** (-jnp.arange(ROPE_FREQS, dtype=F32) / ROPE_FREQS)
    ang = jnp.stack([row[:, None] * inv, col[:, None] * inv], axis=1)
    ang = jnp.concatenate([ang, ang], axis=2).reshape(n_lat, HEAD_DIM)
    ang = jnp.tile(ang, (1, LANES // HEAD_DIM))
    ang = jnp.concatenate([ang, jnp.zeros((n_ctx, LANES), F32)], axis=0)
    lane = jnp.arange(LANES)
    sign = jnp.where((lane % (2 * ROPE_FREQS)) < ROPE_FREQS, -1.0, 1.0).astype(F32)
    return jnp.cos(ang), jnp.sin(ang) * sign


def kernel(x, c, ctx, c_ctx, w_mod, b_mod, norm1_g, w_in, q_norm_g, k_norm_g, lambda_qk, subln_g, w_attn_o, conv_w, conv_b, lru_wa, lru_ba, lru_wi, lru_bi, lru_lambda, w_lru_o, w_four_o, w_out, norm2_g, router_w, router_b, exp_w_gate, exp_w_up, exp_w_down, sh_w_gate, sh_w_up, sh_w_down):
    bsz, n_lat, d = x.shape
    n_ctx = ctx.shape[1]
    depth = w_mod.shape[0]
    n_latent_rows = bsz * n_lat
    n_tok = n_latent_rows + bsz * n_ctx
    assert d == D_MODEL and bsz + 1 <= MOD_ROWS
    assert n_lat % TM == 0 and n_ctx % TM == 0 and n_lat % GRID_W == 0
    tm_in = TM_IN if (n_lat % TM_IN == 0 and (bsz * n_ctx) % TM_IN == 0) else TM

    cvec = jnp.zeros((MOD_ROWS, d), F32).at[:bsz].set(c).at[bsz].set(c_ctx)
    mods = _modulation(cvec, w_mod, b_mod)

    def row_of_tile_fn(tm):
        per_batch = n_lat // tm
        return lambda i: jnp.minimum(i // per_batch, bsz)

    assert n_lat % n_ctx == 0
    cos_t, sin_t = _rope_tables(n_lat, n_ctx)
    group_mat = jnp.kron(jnp.eye(d // HEAD_DIM, dtype=F32), jnp.ones((HEAD_DIM, HEAD_DIM), F32)).astype(BF16)
    m = jnp.arange(FOUR_GROUP_DIM, dtype=jnp.int32)
    ang_c = ((m[:, None] * m[None, :]) % FOUR_GROUP_DIM).astype(F32) * (2.0 * math.pi / FOUR_GROUP_DIM)
    inv_sqrt_c = FOUR_GROUP_DIM ** -0.5
    cs_mat = jnp.concatenate([jnp.cos(ang_c) * inv_sqrt_c, jnp.sin(ang_c) * inv_sqrt_c], axis=1).astype(BF16)
    assert n_lat % (FFT_RADIX * MXU_DIM) == 0 or n_lat // FFT_RADIX < MXU_DIM
    fft_tk1 = min(MXU_DIM, n_lat // FFT_RADIX)
    fft_cc, fft_pc, fft_ps = _fft_tables(n_lat, FFT_RADIX, fft_tk1)
    ct_ctx, st_ctx = _dft_tables(n_ctx)
    q_sub = math.gcd(Q_SUB, n_lat // TQ)
    n_keys = n_ctx + n_lat
    tk = TK if n_keys % TK == 0 else TM

    x_tok = jnp.concatenate([x.reshape(n_latent_rows, d), ctx.reshape(bsz * n_ctx, d)], axis=0)
    for l in range(depth):
        last = l == depth - 1
        lam_init = 0.8 - 0.6 * math.exp(-0.3 * l)
        mod4 = mods[l].reshape(MOD_ROWS, 6, 1, d)

        u = _in_projection(x_tok, mod4, norm1_g[l], w_in[l].astype(BF16), row_of_tile_fn(tm_in), tm_in)

        k_all, q_all, vt_all = _qkv_prepare(u, k_norm_g[l], q_norm_g[l], group_mat, cos_t, sin_t,
                                            bsz, n_lat, n_ctx)
        attn_l = _diff_attention(q_all, k_all, vt_all, lambda_qk[l], subln_g[l], lam_init,
                                 0, n_lat, 0, n_keys, TQ, q_sub, tk)
        attn_c = attn_l
        if not last:
            attn_c = _diff_attention(q_all, k_all, vt_all, lambda_qk[l], subln_g[l], lam_init,
                                     n_lat, n_ctx, n_lat, n_ctx, TM, 1, TM)

        rec = _rglru(u, conv_w[l], conv_b[l], _block_diag_tiles(lru_wa[l]), lru_ba[l],
                     _block_diag_tiles(lru_wi[l]), lru_bi[l], lru_lambda[l], bsz, n_lat, n_ctx)

        y_lat = _chan_dft_split(u, cs_mat, bsz, n_lat, FFT_RADIX)
        four_l = _time_fft(y_lat, fft_cc, fft_pc, fft_ps, FFT_RADIX, fft_tk1, MXU_DIM).reshape(n_latent_rows, d)
        four_c = four_l
        if not last:
            y_ctx = _chan_dft(u, cs_mat, n_latent_rows, bsz * n_ctx)
            four_c = _time_dft(y_ctx, ct_ctx, st_ctx, bsz, n_ctx, 0, n_ctx)

        n_rows = n_latent_rows if last else n_tok
        row_of_tile = row_of_tile_fn(TM)
        x_mid, h16, h_lo, h_hi = _merge(x_tok, attn_l, attn_c, rec, four_l, four_c, u, w_attn_o[l].astype(BF16),
                                w_lru_o[l].astype(BF16), w_four_o[l].astype(BF16), w_out[l].astype(BF16),
                                mod4, norm2_g[l], row_of_tile, n_rows)
        x_tok = _moe(x_mid, h16, h_lo, h_hi, mod4, row_of_tile, router_w[l].T.astype(BF16), router_b[l],
                     exp_w_gate, exp_w_up, exp_w_down, l,
                     sh_w_gate[l].astype(BF16), sh_w_up[l].astype(BF16), sh_w_down[l].astype(BF16))
    return x_tok[:n_latent_rows].reshape(bsz, n_lat, d)
```

```python
import functools
import math

import jax
import jax.numpy as jnp
from jax import lax
from jax.experimental import pallas as pl
from jax.experimental.pallas import tpu as pltpu
from jax.experimental.pallas import tpu_sc as plsc

F32 = jnp.float32
BF16 = jnp.bfloat16

D_MODEL = 1024
EPS = 1e-6
GRID_W = 64

N_HEADS = 8
HEAD_DIM = 64
V_DIM = 2 * HEAD_DIM
ATTN_SCALE = HEAD_DIM ** -0.5
ROPE_THETA = 10000.0
ROPE_FREQS = HEAD_DIM // 4

LRU_BLOCKS = 16
LRU_BLOCK_DIM = D_MODEL // LRU_BLOCKS
LRU_C = 8.0
CONV_W = 4

FOUR_GROUPS = 4
FOUR_GROUP_DIM = D_MODEL // FOUR_GROUPS

CB_K, CB_V, CB_LX, CB_Q, CB_LG, CB_F, CB_G = 0, 1, 2, 3, 4, 5, 6

N_EXPERTS = 256
TOP_K = 8
N_GROUPS = 8
GROUP_SIZE = N_EXPERTS // N_GROUPS
TOPK_GROUPS = 4
EXPERT_FF = 256
ROUTED_SCALE = 2.5

LANES = 128
SUBLANES = 8
MXU_DIM = 256
VMEM_BYTES_V7X = 64 * 1024 * 1024

MOD_ROWS = 8
TM = 256
TM_IN = 512
IN_PANEL = 3
TQ = 256
Q_SUB = 32
TK = 768
TM_ROUTE = 512
MOE_BM = 512
W_RING = 4
SC_WINDOW = 128
FFT_RADIX = 8
LOG2E = 1.4426950408889634


def _cparams(sem, vmem_mb):
    vmem_bytes = int(vmem_mb * 1024 * 1024)
    assert vmem_bytes < VMEM_BYTES_V7X
    return pltpu.CompilerParams(dimension_semantics=sem, vmem_limit_bytes=vmem_bytes)


def _silu(x):
    return x * jax.nn.sigmoid(x)


def _pack_halves(x):
    half = x.shape[1] // 2
    bits = pltpu.bitcast(x.astype(BF16).astype(F32), jnp.uint32)
    return (bits[:, :half] & jnp.uint32(0xFFFF0000)) | (bits[:, half:] >> 16)


def _unpack_halves(w):
    return pltpu.bitcast(w & jnp.uint32(0xFFFF0000), F32), pltpu.bitcast(w << 16, F32)


def _mod_kernel(c_ref, w_ref, b_ref, o_ref):
    c = c_ref[...]
    s = _silu(c).astype(BF16)
    o_ref[0] = jnp.dot(s, w_ref[0].astype(BF16), preferred_element_type=F32) + b_ref[0]


def _modulation(cvec, w_mod, b_mod):
    n_layers, d, six_d = w_mod.shape
    nj = six_d // d
    return pl.pallas_call(
        _mod_kernel,
        grid=(n_layers, nj),
        in_specs=[
            pl.BlockSpec((MOD_ROWS, d), lambda l, j: (0, 0)),
            pl.BlockSpec((1, d, d), lambda l, j: (l, 0, j)),
            pl.BlockSpec((1, 1, d), lambda l, j: (l, 0, j)),
        ],
        out_specs=pl.BlockSpec((1, MOD_ROWS, d), lambda l, j: (l, 0, j)),
        out_shape=jax.ShapeDtypeStruct((n_layers, MOD_ROWS, six_d), F32),
        compiler_params=_cparams(("parallel", "parallel"), 32),
        name="modulation",
    )(cvec, w_mod, b_mod.reshape(n_layers, 1, six_d))


def _mod_spec(part, row_of_tile):
    return pl.BlockSpec((None, None, 1, D_MODEL), lambda i: (row_of_tile(i), part, 0, 0))


def _inproj_kernel(x_ref, g_ref, sh_ref, sc_ref, w_ref, o_ref):
    x = x_ref[...]
    y = x * lax.rsqrt(jnp.mean(x * x, axis=-1, keepdims=True) + EPS)
    h = (y * g_ref[...]) * (1.0 + sc_ref[...]) + sh_ref[...]
    o_ref[...] = jnp.dot(h.astype(BF16), w_ref[...], preferred_element_type=F32).astype(o_ref.dtype)


def _in_projection(x_tok, mod4, norm_g, w_in_bf16, row_of_tile, tm):
    n_tok, d = x_tok.shape
    n_cols = w_in_bf16.shape[1]
    tn = IN_PANEL * d
    row_of = lambda j, i: row_of_tile(i)
    return pl.pallas_call(
        _inproj_kernel,
        grid=(n_cols // tn, n_tok // tm),
        in_specs=[
            pl.BlockSpec((tm, d), lambda j, i: (i, 0)),
            pl.BlockSpec((1, d), lambda j, i: (0, 0)),
            pl.BlockSpec((None, None, 1, d), lambda j, i: (row_of(j, i), 0, 0, 0)),
            pl.BlockSpec((None, None, 1, d), lambda j, i: (row_of(j, i), 1, 0, 0)),
            pl.BlockSpec((d, tn), lambda j, i: (0, j)),
        ],
        out_specs=pl.BlockSpec((tm, tn), lambda j, i: (i, j)),
        out_shape=jax.ShapeDtypeStruct((n_tok, n_cols), BF16),
        compiler_params=_cparams(("parallel", "parallel"), 48),
        name="in_projection",
    )(x_tok, norm_g.reshape(1, d), mod4, mod4, w_in_bf16)


def _prep_kernel(uk_ref, uq_ref, uv_ref, kg_ref, qg_ref, gm_ref, cos_ref, sin_ref, k_out, q_out, vt_out, *, tm):
    gm = gm_ref[...]
    lane = lax.broadcasted_iota(jnp.int32, (tm, LANES), 1)
    first_half = (lane % (2 * ROPE_FREQS)) < ROPE_FREQS

    def norm_rope(x, g):
        ss = jnp.dot((x * x).astype(BF16), gm, preferred_element_type=F32) * (1.0 / HEAD_DIM)
        y = x * lax.rsqrt(ss + EPS) * g
        c = cos_ref[...]
        s = sin_ref[...]
        outs = []
        for j in range(D_MODEL // LANES):
            yt = y[:, j * LANES:(j + 1) * LANES]
            rot = jnp.where(first_half, pltpu.roll(yt, LANES - ROPE_FREQS, 1), pltpu.roll(yt, ROPE_FREQS, 1))
            outs.append(yt * c + rot * s)
        return jnp.concatenate(outs, axis=1)

    k_out[...] = norm_rope(uk_ref[...].astype(F32), kg_ref[...]).astype(BF16)
    qn = norm_rope(uq_ref[...].astype(F32), qg_ref[...]) * (ATTN_SCALE * LOG2E)
    lo = lane < HEAD_DIM
    for h in range(N_HEADS):
        qt = qn[:, h * LANES:(h + 1) * LANES]
        q_out[:, (2 * h) * LANES:(2 * h + 1) * LANES] = jnp.where(lo, qt, 0.0).astype(BF16)
        q_out[:, (2 * h + 1) * LANES:(2 * h + 2) * LANES] = jnp.where(lo, 0.0, qt).astype(BF16)
    vt_out[...] = uv_ref[...].astype(F32).T.astype(BF16)


def _qkv_prepare(u, k_gain, q_gain, group_mat, cos_t, sin_t, bsz, n_lat, n_ctx):
    tm = TM
    d = D_MODEL
    n_s, n_c = n_lat // tm, n_ctx // tm
    n_rows = n_lat + n_ctx
    lat_tiles = bsz * n_s
    kg = jnp.tile(k_gain, d // HEAD_DIM).reshape(1, d)
    qg = jnp.tile(q_gain, d // HEAD_DIM).reshape(1, d)

    def tok_tile(b, j):
        return jnp.where(j < n_s, b * n_s + j, lat_tiles + b * n_c + (j - n_s))

    def u_spec(cb):
        return pl.BlockSpec((tm, d), lambda b, j: (tok_tile(b, j), cb))

    vec = pl.BlockSpec((1, d), lambda b, j: (0, 0))
    rope_spec = pl.BlockSpec((tm, LANES), lambda b, j: (j, 0))
    return pl.pallas_call(
        functools.partial(_prep_kernel, tm=tm),
        grid=(bsz, n_s + n_c),
        in_specs=[u_spec(CB_K), u_spec(CB_Q), u_spec(CB_V), vec, vec,
                  pl.BlockSpec((d, d), lambda b, j: (0, 0)), rope_spec, rope_spec],
        out_specs=[
            pl.BlockSpec((None, tm, d), lambda b, j: (b, j, 0)),
            pl.BlockSpec((None, tm, 2 * d), lambda b, j: (b, j, 0)),
            pl.BlockSpec((None, d, tm), lambda b, j: (b, 0, j)),
        ],
        out_shape=[jax.ShapeDtypeStruct((bsz, n_rows, d), BF16),
                   jax.ShapeDtypeStruct((bsz, n_rows, 2 * d), BF16),
                   jax.ShapeDtypeStruct((bsz, d, n_rows), BF16)],
        compiler_params=_cparams(("parallel", "parallel"), 40),
        name="qkv_prepare",
    )(u, u, u, kg, qg, group_mat, cos_t, sin_t)


def _attn_kernel(lq_ref, sg_ref, q1_ref, q2_ref, k_ref, vt_ref, o_ref, s_buf, *, n_keys, tk, tq, lam_init):
    lq = lq_ref[...]
    lam = (jnp.exp(jnp.sum(lq[0:1] * lq[1:2], axis=1, keepdims=True))
           - jnp.exp(jnp.sum(lq[2:3] * lq[3:4], axis=1, keepdims=True)) + lam_init)
    n_chunks = n_keys // tk
    n_sub = q1_ref.shape[0] // tq

    def load_q(j):
        rows = pl.ds(pl.multiple_of(j * tq, tq), tq)
        return jnp.concatenate([q1_ref[rows, :], q2_ref[rows, :]], axis=0)

    def score_chunk(q, c, mx):
        s = lax.dot_general(k_ref[c * tk:(c + 1) * tk, :], q, (((1,), (1,)), ((), ())),
                            preferred_element_type=F32)
        s_buf[c * tk:(c + 1) * tk, :] = s
        cm = jnp.max(s, axis=0, keepdims=True)
        return cm if mx is None else jnp.maximum(mx, cm)

    q0 = load_q(0)
    mx0 = None
    for c in range(n_chunks):
        mx0 = score_chunk(q0, c, mx0)

    def sub_tile(j, mx):
        q_next = load_q(jnp.minimum(j + 1, n_sub - 1))
        mx_next = None
        l = jnp.zeros((1, 2 * tq), F32)
        acc = jnp.zeros((V_DIM, 2 * tq), F32)
        for c in range(n_chunks):
            p = jnp.exp2(s_buf[c * tk:(c + 1) * tk, :] - mx)
            l = l + jnp.sum(p, axis=0, keepdims=True)
            acc = acc + jnp.dot(vt_ref[:, c * tk:(c + 1) * tk], p.astype(BF16), preferred_element_type=F32)
            mx_next = score_chunk(q_next, c, mx_next)
        o = acc / l
        o = o[:, :tq] - lam * o[:, tq:]
        ms = jnp.mean(o * o, axis=0, keepdims=True)
        on = (o * lax.rsqrt(ms + EPS) * sg_ref[...]) * (1.0 - lam_init)
        o_ref[pl.ds(pl.multiple_of(j * tq, tq), tq), :] = on.T.astype(o_ref.dtype)
        return mx_next

    lax.fori_loop(0, n_sub, sub_tile, mx0)


def _diff_attention(q_all, k_all, vt_all, lambda_qk, subln_g, lam_init, q_row0, n_q, key_row0, n_keys, tq, n_sub, tk):
    bsz = q_all.shape[0]
    d = D_MODEL
    tstep = tq * n_sub
    nq = n_q // tstep
    qb0 = q_row0 // tstep
    kb0 = key_row0 // n_keys
    s_bytes = n_keys * 2 * tq * 4
    return pl.pallas_call(
        functools.partial(_attn_kernel, n_keys=n_keys, tk=tk, tq=tq, lam_init=lam_init),
        grid=(bsz, N_HEADS, nq),
        in_specs=[
            pl.BlockSpec((4, HEAD_DIM), lambda b, h, i: (0, 0)),
            pl.BlockSpec((V_DIM, 1), lambda b, h, i: (0, 0)),
            pl.BlockSpec((None, tstep, LANES), lambda b, h, i: (b, qb0 + i, 2 * h)),
            pl.BlockSpec((None, tstep, LANES), lambda b, h, i: (b, qb0 + i, 2 * h + 1)),
            pl.BlockSpec((None, n_keys, LANES), lambda b, h, i: (b, kb0, h)),
            pl.BlockSpec((None, V_DIM, n_keys), lambda b, h, i: (b, h, kb0)),
        ],
        out_specs=pl.BlockSpec((tstep, LANES), lambda b, h, i: (b * nq + i, h)),
        out_shape=jax.ShapeDtypeStruct((bsz * n_q, d), BF16),
        scratch_shapes=[pltpu.VMEM((n_keys, 2 * tq), F32)],
        compiler_params=_cparams(("parallel", "parallel", "arbitrary"), 28 + s_bytes / 2 ** 20),
        name="diff_attention",
    )(lambda_qk, subln_g.reshape(V_DIM, 1), q_all, q_all, k_all, vt_all)


def _lru_kernel(*refs, rev, n_c, n_s, tm):
    if rev:
        (cur_ref, prev_ref, next_ref, cw_ref, cb_ref, wa_ref, ba_ref, wi_ref, bi_ref, lam_ref, fwd_ref,
         o_ref, a_scr, b_scr, h_scr) = refs
    else:
        (cur_ref, prev_ref, next_ref, cw_ref, cb_ref, wa_ref, ba_ref, wi_ref, bi_ref, lam_ref,
         o_ref, a_scr, b_scr, h_scr) = refs
    s = pl.program_id(1)
    is_ctx = s < n_c
    if rev:
        tile = jnp.where(is_ctx, n_c - 1 - s, n_s - 1 - (s - n_c))
    else:
        tile = jnp.where(is_ctx, s, s - n_c)
    n_tile = jnp.where(is_ctx, n_c, n_s)

    @pl.when(s == 0)
    def _():
        h_scr[...] = jnp.zeros_like(h_scr)

    x = cur_ref[...].astype(F32)
    pv = jnp.where(tile == 0, 0.0, prev_ref[...].astype(F32)[SUBLANES:])
    nx = jnp.where(tile == n_tile - 1, 0.0, next_ref[...].astype(F32)[:SUBLANES])
    r8 = lax.broadcasted_iota(jnp.int32, (SUBLANES, D_MODEL), 0)

    def earlier(k):
        rolled = pltpu.roll(x, k, 0)
        top = jnp.where(r8 < k, pltpu.roll(pv, k, 0), rolled[:SUBLANES])
        return jnp.concatenate([top, rolled[SUBLANES:]], axis=0)

    rolled = pltpu.roll(x, tm - 1, 0)
    bot = jnp.where(r8 >= SUBLANES - 1, pltpu.roll(nx, SUBLANES - 1, 0), rolled[tm - SUBLANES:])
    later = jnp.concatenate([rolled[:tm - SUBLANES], bot], axis=0)
    cw = cw_ref[...]
    conv = earlier(2) * cw[0:1] + earlier(1) * cw[1:2] + x * cw[2:3] + later * cw[3:4] + cb_ref[...]

    conv16 = conv.astype(BF16)

    def block_diag(w_ref, b_ref):
        parts = [jnp.dot(conv16[:, j * MXU_DIM:(j + 1) * MXU_DIM], w_ref[j], preferred_element_type=F32)
                 for j in range(D_MODEL // MXU_DIM)]
        return jnp.concatenate(parts, axis=1) + b_ref[...]

    r = jax.nn.sigmoid(block_diag(wa_ref, ba_ref))
    gate_i = jax.nn.sigmoid(block_diag(wi_ref, bi_ref))
    neg_lam = -lam_ref[...]
    softplus = jnp.maximum(neg_lam, 0.0) + jnp.log1p(jnp.exp(-jnp.abs(neg_lam)))
    log_a = -LRU_C * r * softplus
    a = jnp.exp(log_a)
    mult = jnp.sqrt(1.0 - a * a)
    a_scr[...] = a
    b_scr[...] = mult * gate_i * conv

    def body(t, h):
        row = tm - 1 - t if rev else t
        h = a_scr[pl.ds(row, 1), :] * h + b_scr[pl.ds(row, 1), :]
        if rev:
            o_ref[pl.ds(row, 1), :] = fwd_ref[pl.ds(row, 1), :] + h
        else:
            o_ref[pl.ds(row, 1), :] = h
        return h

    h_scr[...] = lax.fori_loop(0, tm, body, h_scr[...], unroll=8)


def _rglru(u, conv_w, conv_b, wa_bd, ba, wi_bd, bi, lam, bsz, n_lat, n_ctx):
    tm = TM
    d = D_MODEL
    n_tok = u.shape[0]
    n_s, n_c = n_lat // tm, n_ctx // tm
    lat_tiles = bsz * n_s
    halo = 2 * SUBLANES
    per_halo = tm // halo
    last_halo = n_tok // halo - 1

    def make(rev, fwd):
        def blk(b, s):
            is_ctx = s < n_c
            if rev:
                tile = jnp.where(is_ctx, n_c - 1 - s, n_s - 1 - (s - n_c))
            else:
                tile = jnp.where(is_ctx, s, s - n_c)
            return jnp.where(is_ctx, lat_tiles + b * n_c + tile, b * n_s + tile)

        vec = pl.BlockSpec((None, 1, d), lambda b, s: (int(rev), 0, 0))
        wspec = pl.BlockSpec((None, d // MXU_DIM, MXU_DIM, MXU_DIM), lambda b, s: (int(rev), 0, 0, 0))
        row_spec = pl.BlockSpec((tm, d), lambda b, s: (blk(b, s), 0))
        in_specs = [
            pl.BlockSpec((tm, d), lambda b, s: (blk(b, s), CB_LX)),
            pl.BlockSpec((halo, d), lambda b, s: (jnp.maximum(blk(b, s) * per_halo - 1, 0), CB_LX)),
            pl.BlockSpec((halo, d), lambda b, s: (jnp.minimum((blk(b, s) + 1) * per_halo, last_halo), CB_LX)),
            pl.BlockSpec((CONV_W, d), lambda b, s: (0, 0)),
            pl.BlockSpec((1, d), lambda b, s: (0, 0)),
            wspec, vec, wspec, vec, vec,
        ]
        args = [u, u, u, conv_w, conv_b.reshape(1, d), wa_bd, ba.reshape(2, 1, d), wi_bd, bi.reshape(2, 1, d),
                lam.reshape(2, 1, d)]
        if rev:
            in_specs.append(row_spec)
            args.append(fwd)
        return pl.pallas_call(
            functools.partial(_lru_kernel, rev=rev, n_c=n_c, n_s=n_s, tm=tm),
            grid=(bsz, n_c + n_s),
            in_specs=in_specs,
            out_specs=row_spec,
            out_shape=jax.ShapeDtypeStruct((n_tok, d), F32),
            scratch_shapes=[pltpu.VMEM((tm, d), F32), pltpu.VMEM((tm, d), F32), pltpu.VMEM((1, d), F32)],
            input_output_aliases={10: 0} if rev else {},
            compiler_params=_cparams(("arbitrary", "arbitrary"), 40),
            name="rglru_bwd" if rev else "rglru_fwd",
        )(*args)

    fwd = make(False, None)
    return make(True, fwd)


def _chan_dft_kernel(u_ref, cs_ref, o_ref):
    x = u_ref[...].astype(BF16)
    cs = cs_ref[...]
    gd = FOUR_GROUP_DIM
    for g in range(FOUR_GROUPS):
        y = jnp.dot(x[:, g * gd:(g + 1) * gd], cs, preferred_element_type=F32)
        o_ref[:, g * gd:(g + 1) * gd] = y[:, :gd].astype(BF16)
        o_ref[:, D_MODEL + g * gd:D_MODEL + (g + 1) * gd] = y[:, gd:].astype(BF16)


def _chan_dft(u, cs_mat, row0, n_rows):
    tm = TM
    d = D_MODEL
    tile0 = row0 // tm
    return pl.pallas_call(
        _chan_dft_kernel,
        grid=(n_rows // tm,),
        in_specs=[pl.BlockSpec((tm, d), lambda i: (tile0 + i, CB_F)),
                  pl.BlockSpec((FOUR_GROUP_DIM, 2 * FOUR_GROUP_DIM), lambda i: (0, 0))],
        out_specs=pl.BlockSpec((tm, 2 * d), lambda i: (i, 0)),
        out_shape=jax.ShapeDtypeStruct((n_rows, 2 * d), BF16),
        compiler_params=_cparams(("parallel",), 32),
        name="fourier_channels",
    )(u, cs_mat)


def _chan_dft_split_kernel(u_ref, cs_ref, o_ref, y_scr, *, radix):
    x = u_ref[...].astype(BF16)
    cs = cs_ref[...]
    gd = FOUR_GROUP_DIM
    per_g = gd // LANES
    n_cos = D_MODEL // LANES
    for g in range(FOUR_GROUPS):
        y = jnp.dot(x[:, g * gd:(g + 1) * gd], cs, preferred_element_type=F32)
        for j in range(per_g):
            y_scr[g * per_g + j] = y[:, j * LANES:(j + 1) * LANES]
            y_scr[n_cos + g * per_g + j] = y[:, gd + j * LANES:gd + (j + 1) * LANES]
    rows = u_ref.shape[0] // radix
    for r in range(radix):
        for cb in range(2 * n_cos):
            o_ref[r, :, cb * LANES:(cb + 1) * LANES] = y_scr[cb, pl.ds(r, rows, stride=radix), :].astype(BF16)


def _chan_dft_split(u, cs_mat, bsz, n_lat, radix):
    tm = TM
    d = D_MODEL
    n_s = n_lat // tm
    return pl.pallas_call(
        functools.partial(_chan_dft_split_kernel, radix=radix),
        grid=(bsz, n_s),
        in_specs=[pl.BlockSpec((tm, d), lambda b, i: (b * n_s + i, CB_F)),
                  pl.BlockSpec((FOUR_GROUP_DIM, 2 * FOUR_GROUP_DIM), lambda b, i: (0, 0))],
        out_specs=pl.BlockSpec((None, radix, tm // radix, 2 * d), lambda b, i: (b, 0, i, 0)),
        out_shape=jax.ShapeDtypeStruct((bsz, radix, n_lat // radix, 2 * d), BF16),
        scratch_shapes=[pltpu.VMEM((2 * d // LANES, tm, LANES), F32)],
        compiler_params=_cparams(("parallel", "parallel"), 32),
        name="fourier_channels_split",
    )(u, cs_mat)


def _time_fft_kernel(zr_ref, zi_ref, cc_ref, pc_ref, ps_ref, o_ref, acc, *, radix):
    cc = cc_ref[...]
    tk1 = cc.shape[0] // 2
    for r in range(radix):
        pr = jnp.dot(cc, zr_ref[r], preferred_element_type=F32)
        pi = jnp.dot(cc, zi_ref[r], preferred_element_type=F32)
        a_re = pr[:tk1] - pi[tk1:]
        a_im = pr[tk1:] + pi[:tk1]
        for k2 in range(radix):
            term = pc_ref[k2][:, r:r + 1] * a_re - ps_ref[k2][:, r:r + 1] * a_im
            if r == 0:
                acc[k2] = term
            else:
                acc[k2] += term
    o_ref[...] = acc[...].astype(o_ref.dtype)


def _time_fft(yp, cc, pc, ps, radix, tk1, tn):
    bsz, _, t1, two_d = yp.shape
    d = two_d // 2
    nj = d // tn
    return pl.pallas_call(
        functools.partial(_time_fft_kernel, radix=radix),
        grid=(bsz, nj, t1 // tk1),
        in_specs=[
            pl.BlockSpec((None, radix, t1, tn), lambda b, j, i: (b, 0, 0, j)),
            pl.BlockSpec((None, radix, t1, tn), lambda b, j, i: (b, 0, 0, nj + j)),
            pl.BlockSpec((None, 2 * tk1, t1), lambda b, j, i: (i, 0, 0)),
            pl.BlockSpec((radix, tk1, radix), lambda b, j, i: (0, i, 0)),
            pl.BlockSpec((radix, tk1, radix), lambda b, j, i: (0, i, 0)),
        ],
        out_specs=pl.BlockSpec((None, radix, tk1, tn), lambda b, j, i: (b, 0, i, j)),
        out_shape=jax.ShapeDtypeStruct((bsz, radix, t1, d), BF16),
        scratch_shapes=[pltpu.VMEM((radix, tk1, tn), F32)],
        compiler_params=_cparams(("parallel", "parallel", "arbitrary"), 48),
        name="fourier_positions_fft",
    )(yp, yp, cc, pc, ps)


def _fft_tables(n, radix, tk1):
    t1 = n // radix
    k1 = jnp.arange(t1, dtype=jnp.int32)
    ang = ((k1[:, None] * k1[None, :]) % t1).astype(F32) * (2.0 * math.pi / t1)
    c = jnp.cos(ang).reshape(t1 // tk1, tk1, t1)
    s = jnp.sin(ang).reshape(t1 // tk1, tk1, t1)
    cc = jnp.concatenate([c, s], axis=1).astype(BF16)
    k2 = jnp.arange(radix, dtype=jnp.int32)
    r = jnp.arange(radix, dtype=jnp.int32)
    k = k1[None, :, None] + t1 * k2[:, None, None]
    ph = ((k * r[None, None, :]) % n).astype(F32) * (2.0 * math.pi / n)
    scale = float(n) ** -0.5
    return cc, jnp.cos(ph) * scale, jnp.sin(ph) * scale


def _time_dft_kernel(ct_ref, st_ref, yc_ref, ys_ref, o_ref, acc, *, scale):
    kk = pl.program_id(2)

    @pl.when(kk == 0)
    def _():
        acc[...] = jnp.zeros_like(acc)

    acc[...] += (jnp.dot(ct_ref[...], yc_ref[...], preferred_element_type=F32)
                 - jnp.dot(st_ref[...], ys_ref[...], preferred_element_type=F32))

    @pl.when(kk == pl.num_programs(2) - 1)
    def _():
        o_ref[...] = (acc[...] * scale).astype(o_ref.dtype)


def _time_dft(y, ct, st, bsz, n_pos, row0, tmk):
    d = D_MODEL
    nt = n_pos // tmk
    rb0 = row0 // tmk
    return pl.pallas_call(
        functools.partial(_time_dft_kernel, scale=float(n_pos) ** -0.5),
        grid=(bsz, nt, nt),
        in_specs=[
            pl.BlockSpec((tmk, tmk), lambda b, i, k: (i, k)),
            pl.BlockSpec((tmk, tmk), lambda b, i, k: (i, k)),
            pl.BlockSpec((tmk, d), lambda b, i, k: (rb0 + b * nt + k, 0)),
            pl.BlockSpec((tmk, d), lambda b, i, k: (rb0 + b * nt + k, 1)),
        ],
        out_specs=pl.BlockSpec((tmk, d), lambda b, i, k: (b * nt + i, 0)),
        out_shape=jax.ShapeDtypeStruct((bsz * n_pos, d), BF16),
        scratch_shapes=[pltpu.VMEM((tmk, d), F32)],
        compiler_params=_cparams(("parallel", "parallel", "arbitrary"), 48),
        name="fourier_positions",
    )(ct, st, y, y)


def _dft_tables(n):
    k = jnp.arange(n, dtype=jnp.int32)
    ang = ((k[:, None] * k[None, :]) % n).astype(F32) * (2.0 * math.pi / n)
    return jnp.cos(ang).astype(BF16), jnp.sin(ang).astype(BF16)


def _merge_kernel(x_ref, attn_l_ref, attn_c_ref, rec_ref, four_l_ref, four_c_ref, lg_ref, g0_ref, g1_ref, g2_ref,
                  wa_ref, wl_ref, wf_ref, wo_ref, gate_ref, n2_ref, sh_ref, sc_ref,
                  xo_ref, h_ref, hlo_ref, hhi_ref, *, lat_tiles):
    def mm(a, w_ref):
        return jnp.dot(a.astype(BF16), w_ref[...], preferred_element_type=F32)

    is_lat = pl.program_id(0) < lat_tiles
    attn_o = mm(jnp.where(is_lat, attn_l_ref[...], attn_c_ref[...]), wa_ref)
    rec_o = mm(rec_ref[...] * jax.nn.gelu(lg_ref[...].astype(F32), approximate=True), wl_ref)
    four_o = mm(jnp.where(is_lat, four_l_ref[...], four_c_ref[...]), wf_ref)
    mixed = (jax.nn.sigmoid(g0_ref[...].astype(F32)) * attn_o + jax.nn.sigmoid(g1_ref[...].astype(F32)) * rec_o
             + jax.nn.sigmoid(g2_ref[...].astype(F32)) * four_o)
    x = x_ref[...] + gate_ref[...] * mm(mixed, wo_ref)
    xo_ref[...] = x
    y = x * lax.rsqrt(jnp.mean(x * x, axis=-1, keepdims=True) + EPS)
    h = (y * n2_ref[...]) * (1.0 + sc_ref[...]) + sh_ref[...]
    h_ref[...] = h.astype(BF16)
    hlo_ref[...] = _pack_halves(h[:, :D_MODEL // 2])
    hhi_ref[...] = _pack_halves(h[:, D_MODEL // 2:])


def _merge(x_tok, attn_l, attn_c, rec, four_l, four_c, u, w_attn_o, w_lru_o, w_four_o, w_out, mod4, norm2_g,
           row_of_tile, n_rows):
    tm = TM
    d = D_MODEL
    lat_tiles = attn_l.shape[0] // tm
    ctx_tiles = attn_c.shape[0] // tm
    row = pl.BlockSpec((tm, d), lambda i: (i, 0))
    lat_row = pl.BlockSpec((tm, d), lambda i: (jnp.minimum(i, lat_tiles - 1), 0))
    ctx_row = pl.BlockSpec((tm, d), lambda i: (jnp.clip(i - lat_tiles, 0, ctx_tiles - 1), 0))
    wspec = pl.BlockSpec((d, d), lambda i: (0, 0))

    def ucol(cb):
        return pl.BlockSpec((tm, d), lambda i: (i, cb))

    return pl.pallas_call(
        functools.partial(_merge_kernel, lat_tiles=lat_tiles),
        grid=(n_rows // tm,),
        in_specs=[row, lat_row, ctx_row, row, lat_row, ctx_row, ucol(CB_LG), ucol(CB_G), ucol(CB_G + 1),
                  ucol(CB_G + 2), wspec, wspec, wspec, wspec,
                  _mod_spec(2, row_of_tile), pl.BlockSpec((1, d), lambda i: (0, 0)),
                  _mod_spec(3, row_of_tile), _mod_spec(4, row_of_tile)],
        out_specs=[row, row, pl.BlockSpec((tm, d // 4), lambda i: (i, 0)),
                   pl.BlockSpec((tm, d // 4), lambda i: (i, 0))],
        out_shape=[jax.ShapeDtypeStruct((n_rows, d), F32), jax.ShapeDtypeStruct((n_rows, d), BF16),
                   jax.ShapeDtypeStruct((n_rows, d // 4), jnp.uint32),
                   jax.ShapeDtypeStruct((n_rows, d // 4), jnp.uint32)],
        compiler_params=_cparams(("parallel",), 56),
        name="merge_branches",
    )(x_tok, attn_l, attn_c, rec, four_l, four_c, u, u, u, u, w_attn_o, w_lru_o, w_four_o, w_out,
      mod4, norm2_g.reshape(1, d), mod4, mod4)


def _route_kernel(h_ref, rw_ref, rb_ref, tri_ref, idx_ref, wgt_ref, pos_ref, cnt_ref, base_scr):
    tm = h_ref.shape[0]

    @pl.when(pl.program_id(0) == 0)
    def _():
        base_scr[...] = jnp.zeros_like(base_scr)

    logits = lax.dot_general(rw_ref[...], h_ref[...], (((1,), (1,)), ((), ())), preferred_element_type=F32)
    scores = jax.nn.sigmoid(logits)
    biased = scores + rb_ref[...]
    neg = -jnp.inf
    e_iota = lax.broadcasted_iota(jnp.int32, (N_EXPERTS, tm), 0)
    g_iota = lax.broadcasted_iota(jnp.int32, (N_GROUPS, tm), 0)

    rows = []
    for g in range(N_GROUPS):
        xg = biased[g * GROUP_SIZE:(g + 1) * GROUP_SIZE]
        m1 = jnp.max(xg, axis=0, keepdims=True)
        is_max = xg == m1
        n_max = jnp.sum(is_max.astype(F32), axis=0, keepdims=True)
        m2 = jnp.max(jnp.where(is_max, neg, xg), axis=0, keepdims=True)
        rows.append(m1 + jnp.where(n_max >= 2.0, m1, m2))
    grp = jnp.concatenate(rows, axis=0)

    g_sel = jnp.zeros((N_GROUPS, tm), jnp.bool_)
    work = grp
    for _ in range(TOPK_GROUPS):
        m = jnp.max(work, axis=0, keepdims=True)
        first = jnp.min(jnp.where(work == m, g_iota, N_GROUPS), axis=0, keepdims=True)
        hit = g_iota == first
        g_sel = jnp.logical_or(g_sel, hit)
        work = jnp.where(hit, neg, work)
    g_sel_f = g_sel.astype(F32)
    mask_rows = [jnp.broadcast_to(g_sel_f[g:g + 1], (GROUP_SIZE, tm)) for g in range(N_GROUPS)]
    e_mask = jnp.concatenate(mask_rows, axis=0) > 0.5

    work = jnp.where(e_mask, biased, neg)
    sel = jnp.zeros((N_EXPERTS, tm), F32)
    idx_rows, w_rows, hits = [], [], []
    for _ in range(TOP_K):
        m = jnp.max(work, axis=0, keepdims=True)
        first = jnp.min(jnp.where(work == m, e_iota, N_EXPERTS), axis=0, keepdims=True)
        hit = e_iota == first
        hits.append(hit)
        idx_rows.append(first)
        w_rows.append(jnp.sum(jnp.where(hit, scores, 0.0), axis=0, keepdims=True))
        sel = jnp.where(hit, 1.0, sel)
        work = jnp.where(hit, neg, work)
    w = jnp.concatenate(w_rows, axis=0)
    wgt_ref[...] = w / jnp.sum(w, axis=0, keepdims=True) * ROUTED_SCALE
    idx_ref[...] = jnp.concatenate(idx_rows, axis=0)

    before = jnp.dot(sel.astype(BF16), tri_ref[...], preferred_element_type=F32)
    rank = base_scr[...] + before
    pos_rows = [jnp.sum(jnp.where(hit, rank, 0.0), axis=0, keepdims=True) for hit in hits]
    pos_ref[...] = jnp.concatenate(pos_rows, axis=0).astype(jnp.int32)
    base_scr[...] = base_scr[...] + jnp.sum(sel, axis=1, keepdims=True)
    cnt_ref[...] = jnp.broadcast_to(base_scr[...], cnt_ref.shape)


def _route(h16, rw_t, router_b, tm):
    n_tok, d = h16.shape
    tri = (jnp.arange(tm)[:, None] < jnp.arange(tm)[None, :]).astype(BF16)
    out_col = pl.BlockSpec((TOP_K, tm), lambda i: (0, i))
    return pl.pallas_call(
        _route_kernel,
        grid=(n_tok // tm,),
        in_specs=[pl.BlockSpec((tm, d), lambda i: (i, 0)),
                  pl.BlockSpec((N_EXPERTS, d), lambda i: (0, 0)),
                  pl.BlockSpec((N_EXPERTS, 1), lambda i: (0, 0)),
                  pl.BlockSpec((tm, tm), lambda i: (0, 0))],
        out_specs=[out_col, out_col, out_col, pl.BlockSpec((N_EXPERTS, LANES), lambda i: (0, 0))],
        out_shape=[jax.ShapeDtypeStruct((TOP_K, n_tok), jnp.int32), jax.ShapeDtypeStruct((TOP_K, n_tok), F32),
                   jax.ShapeDtypeStruct((TOP_K, n_tok), jnp.int32),
                   jax.ShapeDtypeStruct((N_EXPERTS, LANES), F32)],
        scratch_shapes=[pltpu.VMEM((N_EXPERTS, 1), F32)],
        compiler_params=_cparams(("arbitrary",), 32),
        name="moe_router",
    )(h16, rw_t, router_b.reshape(N_EXPERTS, 1), tri)


def _slot_kernel(idx_ref, pos_ref, offs_ref, dest_ref):
    tm = idx_ref.shape[1]
    e_iota = lax.broadcasted_iota(jnp.int32, (N_EXPERTS, tm), 0)
    idx = idx_ref[...]
    offs = offs_ref[...]
    rows = [jnp.sum(jnp.where(e_iota == idx[r:r + 1], offs, 0), axis=0, keepdims=True) for r in range(TOP_K)]
    dest_ref[...] = jnp.concatenate(rows, axis=0) + pos_ref[...]


def _slots(top_idx, pos, offs, tm):
    n_tok = top_idx.shape[1]
    col = pl.BlockSpec((TOP_K, tm), lambda i: (0, i))
    return pl.pallas_call(
        _slot_kernel,
        grid=(n_tok // tm,),
        in_specs=[col, col, pl.BlockSpec((N_EXPERTS, 1), lambda i: (0, 0))],
        out_specs=col,
        out_shape=jax.ShapeDtypeStruct((TOP_K, n_tok), jnp.int32),
        compiler_params=_cparams(("parallel",), 32),
        name="moe_slots",
    )(top_idx, pos, offs.reshape(N_EXPERTS, 1))


def _sc_scatter_rows(src, idx, n_out):
    n_src, width = src.shape
    n_idx = idx.shape[0]
    n_win = n_src // SC_WINDOW
    mesh = plsc.VectorSubcoreMesh(core_axis_name="core", subcore_axis_name="subcore")

    @pl.kernel(out_type=jax.ShapeDtypeStruct((n_out, width), src.dtype), mesh=mesh, scratch_types=[],
               name="moe_scatter_sc")
    def scatter(src_hbm, idx_hbm, out_hbm):
        def body(src_vmem, idx_vmem):
            pltpu.sync_copy(src_vmem, out_hbm.at[idx_vmem.at[0]])

        pltpu.emit_pipeline(
            body,
            grid=(n_idx // SC_WINDOW,),
            in_specs=[pl.BlockSpec((SC_WINDOW, width), index_map=lambda g: (g % n_win, 0)),
                      pl.BlockSpec((1, SC_WINDOW), index_map=lambda g: (0, g))],
            out_specs=[],
            core_axis_name=("core", "subcore"),
            dimension_semantics=(pltpu.PARALLEL,),
        )(src_hbm, idx_hbm)

    return scatter(src, idx.reshape(1, n_idx))


def _expert_kernel(be_ref, nu_ref, nv_ref, ord_ref, ue_ref, nue_ref, xlo_ref, xhi_ref, wg_hbm, wu_hbm, wd_hbm,
                   ylo_ref, yhi_ref, wg_s, wu_s, wd_s, rg, ru, rd, sem, *, layer):
    i = pl.program_id(0)
    used = i < nu_ref[0]
    prev_e = be_ref[jnp.maximum(i - 1, 0)]
    fresh = jnp.logical_or(i == 0, be_ref[i] != prev_e)

    def copies(k, slot):
        e = ue_ref[k]
        return (pltpu.make_async_copy(wg_hbm.at[layer, e], rg.at[slot], sem.at[0, slot]),
                pltpu.make_async_copy(wu_hbm.at[layer, e], ru.at[slot], sem.at[1, slot]),
                pltpu.make_async_copy(wd_hbm.at[layer, e], rd.at[slot], sem.at[2, slot]))

    @pl.when(i == 0)
    def _():
        for j in range(W_RING):
            @pl.when(j < nue_ref[0])
            def _():
                for c in copies(j, j):
                    c.start()

    @pl.when(jnp.logical_and(used, fresh))
    def _():
        k = ord_ref[i]
        slot = k % W_RING
        for c in copies(k, slot):
            c.wait()
        wg_s[...] = rg[slot].astype(BF16)
        wu_s[...] = ru[slot].astype(BF16)
        wd_s[...] = rd[slot].astype(BF16)

        @pl.when(k + W_RING < nue_ref[0])
        def _():
            for c in copies(k + W_RING, slot):
                c.start()

    @pl.when(used)
    def _():
        row = lax.broadcasted_iota(jnp.int32, xlo_ref.shape, 0)
        live = row < nv_ref[i]
        half = D_MODEL // 2
        quarter = D_MODEL // 4
        parts = []
        for ref in (xlo_ref, xhi_ref):
            parts.extend(p.astype(BF16) for p in _unpack_halves(jnp.where(live, ref[...], jnp.uint32(0))))

        def proj(w_s):
            acc = jnp.dot(parts[0], w_s[:quarter, :], preferred_element_type=F32)
            for k in range(1, 4):
                acc = acc + jnp.dot(parts[k], w_s[k * quarter:(k + 1) * quarter, :], preferred_element_type=F32)
            return acc

        g = proj(wg_s)
        act = (_silu(g) * proj(wu_s)).astype(BF16)
        y = jnp.dot(act, wd_s[...], preferred_element_type=F32)
        ylo_ref[...] = _pack_halves(y[:, :half])
        yhi_ref[...] = _pack_halves(y[:, half:])


def _experts(xs_lo, xs_hi, block_e, n_used, n_valid, ordinal, used_experts, n_used_experts,
             w_gate, w_up, w_down, layer, bm):
    n_slots, quarter = xs_lo.shape
    half = 2 * quarter
    d = D_MODEL

    def blk(i, be, nu, nv, od, ue, ne):
        return (jnp.minimum(i, nu[0] - 1), 0)

    hbm = pl.BlockSpec(memory_space=pl.ANY)
    grid_spec = pltpu.PrefetchScalarGridSpec(
        num_scalar_prefetch=6,
        grid=(n_slots // bm,),
        in_specs=[pl.BlockSpec((bm, quarter), blk), pl.BlockSpec((bm, quarter), blk), hbm, hbm, hbm],
        out_specs=[pl.BlockSpec((bm, half // 2), blk), pl.BlockSpec((bm, half // 2), blk)],
        scratch_shapes=[pltpu.VMEM((d, EXPERT_FF), BF16), pltpu.VMEM((d, EXPERT_FF), BF16),
                        pltpu.VMEM((EXPERT_FF, d), BF16),
                        pltpu.VMEM((W_RING, d, EXPERT_FF), F32), pltpu.VMEM((W_RING, d, EXPERT_FF), F32),
                        pltpu.VMEM((W_RING, EXPERT_FF, d), F32), pltpu.SemaphoreType.DMA((3, W_RING))],
    )
    y_shape = jax.ShapeDtypeStruct((n_slots, half // 2), jnp.uint32)
    return pl.pallas_call(
        functools.partial(_expert_kernel, layer=layer),
        grid_spec=grid_spec,
        out_shape=[y_shape, y_shape],
        compiler_params=_cparams(("arbitrary",), 40),
        name="moe_experts",
    )(block_e, n_used, n_valid, ordinal, used_experts, n_used_experts, xs_lo, xs_hi, w_gate, w_up, w_down)


def _sc_gather_rows(table, idx):
    n_idx = idx.shape[0]
    width = table.shape[1]
    mesh = plsc.VectorSubcoreMesh(core_axis_name="core", subcore_axis_name="subcore")

    @pl.kernel(out_type=jax.ShapeDtypeStruct((n_idx, width), table.dtype), mesh=mesh, scratch_types=[],
               name="moe_gather_sc")
    def gather(table_hbm, idx_hbm, out_hbm):
        def body(idx_vmem, out_vmem):
            pltpu.sync_copy(table_hbm.at[idx_vmem.at[0]], out_vmem)

        pltpu.emit_pipeline(
            body,
            grid=(n_idx // SC_WINDOW,),
            in_specs=[pl.BlockSpec((1, SC_WINDOW), index_map=lambda i: (0, i))],
            out_specs=[pl.BlockSpec((SC_WINDOW, width), index_map=lambda i: (i, 0))],
            core_axis_name=("core", "subcore"),
            dimension_semantics=(pltpu.PARALLEL,),
        )(idx_hbm, out_hbm)

    return gather(table, idx.reshape(1, n_idx))


def _combine_kernel(x_ref, h_ref, tw_ref, gate_ref, wsg_ref, wsu_ref, wsd_ref, lo_ref, hi_ref, o_ref):
    quarter = D_MODEL // 4
    h = h_ref[...]
    act = (_silu(jnp.dot(h, wsg_ref[...], preferred_element_type=F32))
           * jnp.dot(h, wsu_ref[...], preferred_element_type=F32)).astype(BF16)
    shared = jnp.dot(act, wsd_ref[...], preferred_element_type=F32)

    tw = tw_ref[...]
    gate = gate_ref[...]
    for q, buf in enumerate((lo_ref, hi_ref)):
        moe_a = moe_b = None
        for r in range(TOP_K):
            ya, yb = _unpack_halves(buf[r])
            wr = tw[:, r:r + 1]
            moe_a = ya * wr if moe_a is None else moe_a + ya * wr
            moe_b = yb * wr if moe_b is None else moe_b + yb * wr
        for part, moe in ((2 * q, moe_a), (2 * q + 1, moe_b)):
            cols = slice(part * quarter, (part + 1) * quarter)
            o_ref[:, cols] = x_ref[:, cols] + gate[:, cols] * (moe + shared[:, cols])


def _combine(x_tok, h16, g_lo, g_hi, top_w, mod4, ws_gate, ws_up, ws_down, row_of_tile):
    n_tok, d = x_tok.shape
    tm = TM
    ff = ws_gate.shape[1]
    row = pl.BlockSpec((tm, d), lambda i: (i, 0))
    gat = pl.BlockSpec((TOP_K, tm, d // 4), lambda i: (0, i, 0))
    return pl.pallas_call(
        _combine_kernel,
        grid=(n_tok // tm,),
        in_specs=[row, row,
                  pl.BlockSpec((tm, TOP_K), lambda i: (i, 0)),
                  _mod_spec(5, row_of_tile),
                  pl.BlockSpec((d, ff), lambda i: (0, 0)),
                  pl.BlockSpec((d, ff), lambda i: (0, 0)),
                  pl.BlockSpec((ff, d), lambda i: (0, 0)),
                  gat, gat],
        out_specs=row,
        out_shape=jax.ShapeDtypeStruct((n_tok, d), F32),
        compiler_params=_cparams(("parallel",), 48),
        name="moe_combine",
    )(x_tok, h16, top_w, mod4, ws_gate, ws_up, ws_down, g_lo, g_hi)


def _moe(x_tok, h16, h_lo, h_hi, mod4, row_of_tile, rw_t, router_b, w_gate, w_up, w_down, layer, ws_gate, ws_up, ws_down):
    n_tok = x_tok.shape[0]
    bm = MOE_BM
    tm_r = TM_ROUTE if n_tok % TM_ROUTE == 0 else TM
    top_idx, top_w, pos, cnt = _route(h16, rw_t, router_b, tm_r)

    counts = cnt[:, 0].astype(jnp.int32)
    padded = (counts + bm - 1) // bm * bm
    ends = jnp.cumsum(padded)
    n_blocks = -(-(n_tok * TOP_K + N_EXPERTS * (bm - 1)) // bm)
    offs = ends - padded
    dest = _slots(top_idx, pos, offs, tm_r)
    block_start = jnp.arange(n_blocks, dtype=jnp.int32) * bm
    block_e = jnp.minimum(jnp.sum((ends[None, :] <= block_start[:, None]).astype(jnp.int32), axis=1),
                          N_EXPERTS - 1)
    n_valid = jnp.clip(counts[block_e] - (block_start - offs[block_e]), 0, bm).astype(jnp.int32)
    n_used = (ends[-1] // bm).astype(jnp.int32).reshape(1)
    dest_flat = dest.reshape(TOP_K * n_tok)
    xs_lo = _sc_scatter_rows(h_lo, dest_flat, n_blocks * bm)
    xs_hi = _sc_scatter_rows(h_hi, dest_flat, n_blocks * bm)
    block_id = jnp.arange(n_blocks, dtype=jnp.int32)
    first_of_expert = ((block_id == 0) | (block_e != jnp.roll(block_e, 1))) & (block_id < n_used[0])
    ordinal = jnp.cumsum(first_of_expert.astype(jnp.int32)) - 1
    n_used_experts = jnp.sum(first_of_expert.astype(jnp.int32)).reshape(1)
    used_experts = jnp.zeros((N_EXPERTS,), jnp.int32).at[
        jnp.where(first_of_expert, ordinal, N_EXPERTS)].set(block_e, mode='drop')
    ys_lo, ys_hi = _experts(xs_lo, xs_hi, block_e, n_used, n_valid, ordinal, used_experts, n_used_experts,
                            w_gate, w_up, w_down, layer, bm)
    g_lo = _sc_gather_rows(ys_lo, dest_flat).reshape(TOP_K, n_tok, D_MODEL // 4)
    g_hi = _sc_gather_rows(ys_hi, dest_flat).reshape(TOP_K, n_tok, D_MODEL // 4)
    return _combine(x_tok, h16, g_lo, g_hi, top_w.T, mod4, ws_gate, ws_up, ws_down, row_of_tile)


def _block_diag_tiles(w):
    per = MXU_DIM // LRU_BLOCK_DIM
    w = w.reshape(2, D_MODEL // MXU_DIM, per, LRU_BLOCK_DIM, LRU_BLOCK_DIM)
    eye = jnp.eye(per, dtype=w.dtype)
    t = jnp.einsum('ztpde,pq->ztpdqe', w, eye)
    return t.reshape(2, D_MODEL // MXU_DIM, MXU_DIM, MXU_DIM).astype(BF16)


def _rope_tables(n_lat, n_ctx):
    rows = n_lat // GRID_W
    row = jnp.repeat(jnp.arange(rows), GRID_W)
    col = jnp.tile(jnp.arange(GRID_W), rows)
    inv = ROPE_THETA ** (-jnp.arange(ROPE_FREQS, dtype=F32) / ROPE_FREQS)
    ang = jnp.stack([row[:, None] * inv, col[:, None] * inv], axis=1)
    ang = jnp.concatenate([ang, ang], axis=2).reshape(n_lat, HEAD_DIM)
    ang = jnp.tile(ang, (1, LANES // HEAD_DIM))
    ang = jnp.concatenate([ang, jnp.zeros((n_ctx, LANES), F32)], axis=0)
    lane = jnp.arange(LANES)
    sign = jnp.where((lane % (2 * ROPE_FREQS)) < ROPE_FREQS, -1.0, 1.0).astype(F32)
    return jnp.cos(ang), jnp.sin(ang) * sign


def kernel(x, c, ctx, c_ctx, w_mod, b_mod, norm1_g, w_in, q_norm_g, k_norm_g, lambda_qk, subln_g, w_attn_o, conv_w, conv_b, lru_wa, lru_ba, lru_wi, lru_bi, lru_lambda, w_lru_o, w_four_o, w_out, norm2_g, router_w, router_b, exp_w_gate, exp_w_up, exp_w_down, sh_w_gate, sh_w_up, sh_w_down):
    bsz, n_lat, d = x.shape
    n_ctx = ctx.shape[1]
    depth = w_mod.shape[0]
    n_latent_rows = bsz * n_lat
    n_tok = n_latent_rows + bsz * n_ctx
    assert d == D_MODEL and bsz + 1 <= MOD_ROWS
    assert n_lat % TM == 0 and n_ctx % TM == 0 and n_lat % GRID_W == 0
    tm_in = TM_IN if (n_lat % TM_IN == 0 and (bsz * n_ctx) % TM_IN == 0) else TM

    cvec = jnp.zeros((MOD_ROWS, d), F32).at[:bsz].set(c).at[bsz].set(c_ctx)
    mods = _modulation(cvec, w_mod, b_mod)

    def row_of_tile_fn(tm):
        per_batch = n_lat // tm
        return lambda i: jnp.minimum(i // per_batch, bsz)

    assert n_lat % n_ctx == 0
    cos_t, sin_t = _rope_tables(n_lat, n_ctx)
    group_mat = jnp.kron(jnp.eye(d // HEAD_DIM, dtype=F32), jnp.ones((HEAD_DIM, HEAD_DIM), F32)).astype(BF16)
    m = jnp.arange(FOUR_GROUP_DIM, dtype=jnp.int32)
    ang_c = ((m[:, None] * m[None, :]) % FOUR_GROUP_DIM).astype(F32) * (2.0 * math.pi / FOUR_GROUP_DIM)
    inv_sqrt_c = FOUR_GROUP_DIM ** -0.5
    cs_mat = jnp.concatenate([jnp.cos(ang_c) * inv_sqrt_c, jnp.sin(ang_c) * inv_sqrt_c], axis=1).astype(BF16)
    assert n_lat % (FFT_RADIX * MXU_DIM) == 0 or n_lat // FFT_RADIX < MXU_DIM
    fft_tk1 = min(MXU_DIM, n_lat // FFT_RADIX)
    fft_cc, fft_pc, fft_ps = _fft_tables(n_lat, FFT_RADIX, fft_tk1)
    ct_ctx, st_ctx = _dft_tables(n_ctx)
    q_sub = math.gcd(Q_SUB, n_lat // TQ)
    n_keys = n_ctx + n_lat
    tk = TK if n_keys % TK == 0 else TM

    x_tok = jnp.concatenate([x.reshape(n_latent_rows, d), ctx.reshape(bsz * n_ctx, d)], axis=0)
    for l in range(depth):
        last = l == depth - 1
        lam_init = 0.8 - 0.6 * math.exp(-0.3 * l)
        mod4 = mods[l].reshape(MOD_ROWS, 6, 1, d)

        u = _in_projection(x_tok, mod4, norm1_g[l], w_in[l].astype(BF16), row_of_tile_fn(tm_in), tm_in)

        k_all, q_all, vt_all = _qkv_prepare(u, k_norm_g[l], q_norm_g[l], group_mat, cos_t, sin_t,
                                            bsz, n_lat, n_ctx)
        attn_l = _diff_attention(q_all, k_all, vt_all, lambda_qk[l], subln_g[l], lam_init,
                                 0, n_lat, 0, n_keys, TQ, q_sub, tk)
        attn_c = attn_l
        if not last:
            attn_c = _diff_attention(q_all, k_all, vt_all, lambda_qk[l], subln_g[l], lam_init,
                                     n_lat, n_ctx, n_lat, n_ctx, TM, 1, TM)

        rec = _rglru(u, conv_w[l], conv_b[l], _block_diag_tiles(lru_wa[l]), lru_ba[l],
                     _block_diag_tiles(lru_wi[l]), lru_bi[l], lru_lambda[l], bsz, n_lat, n_ctx)

        y_lat = _chan_dft_split(u, cs_mat, bsz, n_lat, FFT_RADIX)
        four_l = _time_fft(y_lat, fft_cc, fft_pc, fft_ps, FFT_RADIX, fft_tk1, MXU_DIM).reshape(n_latent_rows, d)
        four_c = four_l
        if not last:
            y_ctx = _chan_dft(u, cs_mat, n_latent_rows, bsz * n_ctx)
            four_c = _time_dft(y_ctx, ct_ctx, st_ctx, bsz, n_ctx, 0, n_ctx)

        n_rows = n_latent_rows if last else n_tok
        row_of_tile = row_of_tile_fn(TM)
        x_mid, h16, h_lo, h_hi = _merge(x_tok, attn_l, attn_c, rec, four_l, four_c, u, w_attn_o[l].astype(BF16),
                                w_lru_o[l].astype(BF16), w_four_o[l].astype(BF16), w_out[l].astype(BF16),
                                mod4, norm2_g[l], row_of_tile, n_rows)
        x_tok = _moe(x_mid, h16, h_lo, h_hi, mod4, row_of_tile, router_w[l].T.astype(BF16), router_b[l],
                     exp_w_gate, exp_w_up, exp_w_down, l,
                     sh_w_gate[l].astype(BF16), sh_w_up[l].astype(BF16), sh_w_down[l].astype(BF16))
    return x_tok[:n_latent_rows].reshape(bsz, n_lat, d)
```
